```python
import math
import jax, jax.numpy as jnp
from jax import lax
import numpy as np

D_MODEL = 1024
BATCH = 4
SEQ = 4096
DEPTH = 4

N_MIXERS = 3
EPS = 1e-6
CONV_WIDTH = 3
N_HEADS = 16
N_KV_HEADS = 4
HEAD_DIM = D_MODEL // N_HEADS
IDX_HEADS = 8
IDX_DIM = 64
TOPK_MAX = 256
Q_BLOCK = 128
REL_BUCKETS = 32
REL_MAX_DIST = 128
ATTN_IN_SIZES = [N_HEADS * HEAD_DIM, N_KV_HEADS * HEAD_DIM, N_KV_HEADS * HEAD_DIM, IDX_HEADS * IDX_DIM, IDX_DIM, IDX_HEADS]
ATTN_IN_DIM = sum(ATTN_IN_SIZES)
SSM_GROUP = 16
SSM_GROUPS = D_MODEL // SSM_GROUP
SSM_STATE = 64
D_FF = ((8 * D_MODEL // 3 + 127) // 128) * 128
N_EXPERTS = 8
TOP_K_EXPERTS = 2
N_CONV_LAYERS = (DEPTH + 2) // 3
N_ATTN_LAYERS = (DEPTH + 1) // 3
N_SSM_LAYERS = DEPTH // 3
N_DENSE_LAYERS = (DEPTH + 1) // 2
N_MOE_LAYERS = DEPTH // 2

kernel_name = 'hybrid_conv_dsa_s5_moe_trunk'


def rms_norm(x, g):
    x32 = x.astype(jnp.float32)
    y = x32 * lax.rsqrt(jnp.mean(x32 * x32, axis=-1, keepdims=True) + EPS)
    return (y * g.astype(jnp.float32)).astype(x.dtype)


def modulate(h, shift, scale):
    return h * (1 + scale[:, None, :]) + shift[:, None, :]


def short_conv_mixer(h, w_in, w_conv, w_out):
    b_gate, c_gate, v = jnp.split(h @ w_in, 3, axis=-1)
    u = c_gate * v
    conv = lax.conv_general_dilated(
        u, w_conv[:, None, :], window_strides=(1,), padding=[(CONV_WIDTH - 1, 0)],
        dimension_numbers=('NWC', 'WIO', 'NWC'), feature_group_count=u.shape[-1])
    return (b_gate * conv) @ w_out


def rel_bucket(dist):
    max_exact = REL_BUCKETS // 2
    d = jnp.maximum(dist, 1).astype(jnp.float32)
    large = max_exact + (jnp.log(d / max_exact) / math.log(REL_MAX_DIST / max_exact)
                         * (REL_BUCKETS - max_exact)).astype(jnp.int32)
    large = jnp.minimum(large, REL_BUCKETS - 1)
    return jnp.where(dist < max_exact, dist, large)


def dsa_attention(h, w_in, q_gain, k_gain, w_out, rel_bias):
    bsz, seq_len, _ = h.shape
    f32 = jnp.float32
    top_k = min(TOPK_MAX, seq_len // 4)
    n_rep = N_HEADS // N_KV_HEADS
    splits = [int(s) for s in np.cumsum(ATTN_IN_SIZES)[:-1]]
    q, k, v, qi, ki, wi = jnp.split(h @ w_in, splits, axis=-1)
    q = rms_norm(q.reshape(bsz, seq_len, N_KV_HEADS, n_rep, HEAD_DIM), q_gain)
    k = rms_norm(k.reshape(bsz, seq_len, N_KV_HEADS, HEAD_DIM), k_gain)
    v = v.reshape(bsz, seq_len, N_KV_HEADS, HEAD_DIM)
    qi = qi.reshape(bsz, seq_len, IDX_HEADS, IDX_DIM)
    wi = wi * (IDX_HEADS ** -0.5 * IDX_DIM ** -0.5)
    key_pos = jnp.arange(seq_len)
    gather = jax.vmap(lambda arr, idx: arr[idx])
    bias_table = rel_bias.astype(f32)

    def block(start):
        qpos = start + jnp.arange(Q_BLOCK)
        qb = lax.dynamic_slice_in_dim(q, start, Q_BLOCK, axis=1)
        qib = lax.dynamic_slice_in_dim(qi, start, Q_BLOCK, axis=1)
        wib = lax.dynamic_slice_in_dim(wi, start, Q_BLOCK, axis=1)
        dots = jax.nn.relu(jnp.einsum('bthd,bsd->bths', qib, ki).astype(f32))
        score = jnp.einsum('bths,bth->bts', dots, wib.astype(f32))
        score = jnp.where(key_pos[None, None, :] <= qpos[None, :, None], score, -jnp.inf)
        _, sel = lax.top_k(score, top_k)
        valid = sel <= qpos[None, :, None]
        kg = gather(k, sel)
        vg = gather(v, sel)
        logits = jnp.einsum('btngd,btknd->btngk', qb, kg).astype(f32) * (HEAD_DIM ** -0.5)
        bias = bias_table[rel_bucket(jnp.maximum(qpos[None, :, None] - sel, 0))]
        bias = bias.reshape(bsz, Q_BLOCK, top_k, N_KV_HEADS, n_rep).transpose(0, 1, 3, 4, 2)
        logits = jnp.where(valid[:, :, None, None, :], logits + bias, -jnp.inf)
        p = jax.nn.softmax(logits, axis=-1).astype(v.dtype)
        o = jnp.einsum('btngk,btknd->btngd', p, vg)
        return o.reshape(bsz, Q_BLOCK, N_HEADS * HEAD_DIM)

    starts = jnp.arange(seq_len // Q_BLOCK, dtype=jnp.int32) * Q_BLOCK
    out = lax.map(block, starts)
    out = out.transpose(1, 0, 2, 3).reshape(bsz, seq_len, N_HEADS * HEAD_DIM)
    return out @ w_out


def s5_mixer(h, lam_re, lam_im, log_step, b_re, b_im, c_re, c_im, d_skip, w_glu):
    bsz, seq_len, _ = h.shape
    f32 = jnp.float32
    lam = lax.complex(jnp.minimum(lam_re.astype(f32), -1e-4), lam_im.astype(f32))
    step = jnp.exp(log_step.astype(f32))[:, None]
    lam_bar = jnp.exp(lam * step)
    b_bar = ((lam_bar - 1.0) / lam)[..., None] * lax.complex(b_re.astype(f32), b_im.astype(f32))
    c_mat = lax.complex(c_re.astype(f32), c_im.astype(f32))
    u = h.astype(f32).reshape(bsz, seq_len, SSM_GROUPS, SSM_GROUP)
    bu = jnp.einsum('blgc,gpc->blgp', u.astype(jnp.complex64), b_bar)
    a = jnp.broadcast_to(lam_bar, (1, seq_len, SSM_GROUPS, SSM_STATE))

    def combine(left, right):
        a_l, b_l = left
        a_r, b_r = right
        return a_r * a_l, a_r * b_l + b_r

    _, states = lax.associative_scan(combine, (a, bu), axis=1)
    y = jnp.einsum('blgp,gcp->blgc', states, c_mat).real.reshape(bsz, seq_len, D_MODEL)
    y = y + d_skip.astype(f32) * h.astype(f32)
    g = jax.nn.gelu(y).astype(h.dtype)
    lin, gate = jnp.split(g @ w_glu, 2, axis=-1)
    return lin * jax.nn.sigmoid(gate)


def swiglu(h, w_gu, w_down):
    g, u = jnp.split(h @ w_gu, 2, axis=-1)
    return (jax.nn.silu(g) * u) @ w_down


def moe_swiglu(h, router_w, router_b, w_gu, w_down):
    bsz, seq_len, d = h.shape
    f32 = jnp.float32
    t = h.reshape(-1, d)
    logits = t.astype(f32) @ router_w.astype(f32) + router_b.astype(f32)
    probs = jax.nn.softmax(logits, axis=-1)
    top_p, top_i = lax.top_k(probs, TOP_K_EXPERTS)
    top_p = top_p / jnp.sum(top_p, axis=-1, keepdims=True)
    gates = jnp.sum(jax.nn.one_hot(top_i, N_EXPERTS, dtype=f32) * top_p[..., None], axis=1)
    out = jnp.zeros_like(t)
    for e in range(N_EXPERTS):
        out = out + gates[:, e:e + 1].astype(t.dtype) * swiglu(t, w_gu[e], w_down[e])
    return out.reshape(bsz, seq_len, d)


def setup_inputs(seed: int = 0) -> dict:
    key = jax.random.key(seed)
    ks = jax.random.split(key, 32)
    f32 = jnp.float32

    def nrm(k, shape, scale):
        return jax.random.normal(k, shape, f32) * scale

    n_idx = jnp.arange(SSM_STATE, dtype=f32)
    return {
        'x': nrm(ks[0], (BATCH, SEQ, D_MODEL), 1.0),
        'c': nrm(ks[1], (BATCH, D_MODEL), 1.0),
        'ada_w': nrm(ks[2], (DEPTH, D_MODEL, 6 * D_MODEL), 0.5 * D_MODEL ** -0.5),
        'ada_b': nrm(ks[3], (DEPTH, 6 * D_MODEL), 0.02),
        'norm_g': 1.0 + nrm(ks[4], (DEPTH, 2, D_MODEL), 0.02),
        'conv_w_in': nrm(ks[5], (N_CONV_LAYERS, D_MODEL, 3 * D_MODEL), D_MODEL ** -0.5),
        'conv_w': nrm(ks[6], (N_CONV_LAYERS, CONV_WIDTH, D_MODEL), CONV_WIDTH ** -0.5),
        'conv_w_out': nrm(ks[7], (N_CONV_LAYERS, D_MODEL, D_MODEL), D_MODEL ** -0.5),
        'attn_w_in': nrm(ks[8], (N_ATTN_LAYERS, D_MODEL, ATTN_IN_DIM), D_MODEL ** -0.5),
        'attn_q_gain': 1.0 + nrm(ks[9], (N_ATTN_LAYERS, HEAD_DIM), 0.02),
        'attn_k_gain': 1.0 + nrm(ks[10], (N_ATTN_LAYERS, HEAD_DIM), 0.02),
        'attn_w_out': nrm(ks[11], (N_ATTN_LAYERS, N_HEADS * HEAD_DIM, D_MODEL), (N_HEADS * HEAD_DIM) ** -0.5),
        'rel_bias': nrm(ks[12], (REL_BUCKETS, N_HEADS), 0.5),
        'ssm_lambda_re': -0.5 + nrm(ks[13], (N_SSM_LAYERS, SSM_GROUPS, SSM_STATE), 0.01),
        'ssm_lambda_im': math.pi * n_idx + nrm(ks[14], (N_SSM_LAYERS, SSM_GROUPS, SSM_STATE), 0.01),
        'ssm_log_step': jax.random.uniform(ks[15], (N_SSM_LAYERS, SSM_GROUPS), f32, math.log(1e-3), math.log(1e-1)),
        'ssm_b_re': nrm(ks[16], (N_SSM_LAYERS, SSM_GROUPS, SSM_STATE, SSM_GROUP), SSM_GROUP ** -0.5),
        'ssm_b_im': nrm(ks[17], (N_SSM_LAYERS, SSM_GROUPS, SSM_STATE, SSM_GROUP), SSM_GROUP ** -0.5),
        'ssm_c_re': nrm(ks[18], (N_SSM_LAYERS, SSM_GROUPS, SSM_GROUP, SSM_STATE), SSM_STATE ** -0.5),
        'ssm_c_im': nrm(ks[19], (N_SSM_LAYERS, SSM_GROUPS, SSM_GROUP, SSM_STATE), SSM_STATE ** -0.5),
        'ssm_d': nrm(ks[20], (N_SSM_LAYERS, D_MODEL), 1.0),
        'ssm_w_glu': nrm(ks[21], (N_SSM_LAYERS, D_MODEL, 2 * D_MODEL), D_MODEL ** -0.5),
        'ffn_w_gu': nrm(ks[22], (N_DENSE_LAYERS, D_MODEL, 2 * D_FF), D_MODEL ** -0.5),
        'ffn_w_down': nrm(ks[23], (N_DENSE_LAYERS, D_FF, D_MODEL), D_FF ** -0.5),
        'moe_router_w': nrm(ks[24], (N_MOE_LAYERS, D_MODEL, N_EXPERTS), D_MODEL ** -0.5),
        'moe_router_b': nrm(ks[25], (N_MOE_LAYERS, N_EXPERTS), 0.01),
        'moe_w_gu': nrm(ks[26], (N_MOE_LAYERS, N_EXPERTS, D_MODEL, 2 * D_FF), D_MODEL ** -0.5),
        'moe_w_down': nrm(ks[27], (N_MOE_LAYERS, N_EXPERTS, D_FF, D_MODEL), D_FF ** -0.5),
    }


def reference(x, c, ada_w, ada_b, norm_g, conv_w_in, conv_w, conv_w_out, attn_w_in, attn_q_gain,
              attn_k_gain, attn_w_out, rel_bias, ssm_lambda_re, ssm_lambda_im, ssm_log_step, ssm_b_re,
              ssm_b_im, ssm_c_re, ssm_c_im, ssm_d, ssm_w_glu, ffn_w_gu, ffn_w_down, moe_router_w,
              moe_router_b, moe_w_gu, moe_w_down):
    cond = jax.nn.silu(c)
    for i in range(DEPTH):
        mod = cond @ ada_w[i] + ada_b[i]
        sh1, sc1, g1, sh2, sc2, g2 = jnp.split(mod, 6, axis=-1)
        h = modulate(rms_norm(x, norm_g[i, 0]), sh1, sc1)
        j = i // N_MIXERS
        if i % N_MIXERS == 0:
            y = short_conv_mixer(h, conv_w_in[j], conv_w[j], conv_w_out[j])
        elif i % N_MIXERS == 1:
            y = dsa_attention(h, attn_w_in[j], attn_q_gain[j], attn_k_gain[j], attn_w_out[j], rel_bias)
        else:
            y = s5_mixer(h, ssm_lambda_re[j], ssm_lambda_im[j], ssm_log_step[j], ssm_b_re[j], ssm_b_im[j],
                         ssm_c_re[j], ssm_c_im[j], ssm_d[j], ssm_w_glu[j])
        x = x + g1[:, None, :] * y
        h = modulate(rms_norm(x, norm_g[i, 1]), sh2, sc2)
        if i % 2 == 0:
            y = swiglu(h, ffn_w_gu[i // 2], ffn_w_down[i // 2])
        else:
            y = moe_swiglu(h, moe_router_w[i // 2], moe_router_b[i // 2], moe_w_gu[i // 2], moe_w_down[i // 2])
        x = x + g2[:, None, :] * y
    return x
```

```python
import functools
import math

import numpy as np
import jax
import jax.numpy as jnp
from jax import lax
from jax.experimental import pallas as pl
from jax.experimental.pallas import tpu as pltpu

F32 = jnp.float32
BF16 = jnp.bfloat16
I32 = jnp.int32
HIGHEST = lax.Precision.HIGHEST

DEPTH = 4
N_MIXERS = 3
EPS = 1e-6
CONV_WIDTH = 3
N_HEADS = 16
N_KV_HEADS = 4
N_REP = N_HEADS // N_KV_HEADS
HEAD_DIM = 64
IDX_HEADS = 8
IDX_DIM = 64
TOPK_MAX = 256
REL_BUCKETS = 32
REL_MAX_DIST = 128
SSM_GROUP = 16
SSM_STATE = 64
N_EXPERTS = 8
TOP_K_EXPERTS = 2

VMEM_LIMIT_BYTES = 56 * 1024 * 1024
LANES = 128
SUBLANES = 8

INT_MIN = -(2 ** 31)
INT_MAX = 2 ** 31 - 1
NEG_BIG = -1e30

ATT_TQ = 128
ATT_KC = 256
BIAS_C = 384
BIAS_W = BIAS_C + ATT_KC

SSM_CHUNK = 16


def _cparams(*sem):
    return pltpu.CompilerParams(dimension_semantics=sem, vmem_limit_bytes=VMEM_LIMIT_BYTES)


def _norm_mod(x, g, shift, scale):
    ms = jnp.mean(x * x, axis=-1, keepdims=True)
    y = x * lax.rsqrt(ms + EPS)
    return (y * g) * (1.0 + scale) + shift


def _silu(x):
    return x * jax.nn.sigmoid(x)


def _ada_kernel(c_ref, w_ref, b_ref, o_ref):
    c = c_ref[...]
    cond = _silu(c)
    o_ref[0] = jnp.dot(cond, w_ref[0], preferred_element_type=F32, precision=HIGHEST) + b_ref[0]


def _ada_mod(c, ada_w, ada_b):
    depth, d, d6 = ada_w.shape
    bsz = c.shape[0]
    tn = d
    return pl.pallas_call(
        _ada_kernel,
        out_shape=jax.ShapeDtypeStruct((depth, bsz, d6), F32),
        grid=(depth, d6 // tn),
        in_specs=[
            pl.BlockSpec((bsz, d), lambda i, j: (0, 0)),
            pl.BlockSpec((1, d, tn), lambda i, j: (i, 0, j)),
            pl.BlockSpec((1, 1, tn), lambda i, j: (i, 0, j)),
        ],
        out_specs=pl.BlockSpec((1, bsz, tn), lambda i, j: (i, 0, j)),
        compiler_params=_cparams("parallel", "parallel"),
        name="ada_mod",
    )(c, ada_w, ada_b.reshape(depth, 1, d6))


def _conv_kernel(x_ref, xh_ref, g_ref, sh_ref, sc_ref, gate_ref, win_ref, wc_ref, wout_ref,
                 o_ref, u_scr, *, tm, d):
    i = pl.program_id(1)
    g = g_ref[...]
    sh = sh_ref[0]
    sc = sc_ref[0]
    x = x_ref[0]
    h = _norm_mod(x, g, sh, sc).astype(BF16)
    z = jnp.dot(h, win_ref[...], preferred_element_type=F32)
    b_gate = z[:, :d]
    u = z[:, d:2 * d] * z[:, 2 * d:]
    hh = _norm_mod(xh_ref[0], g, sh, sc).astype(BF16)
    zh = jnp.dot(hh, win_ref[:, d:], preferred_element_type=F32)
    uh = zh[:, :d] * zh[:, d:]
    uh = jnp.where(i > 0, uh, 0.0)
    u_scr[0:SUBLANES, :] = uh
    u_scr[SUBLANES:SUBLANES + tm, :] = u
    wc = wc_ref[...]
    conv = (wc[0:1, :] * u_scr[SUBLANES - 2:SUBLANES - 2 + tm, :]
            + wc[1:2, :] * u_scr[SUBLANES - 1:SUBLANES - 1 + tm, :]
            + wc[2:3, :] * u)
    y = jnp.dot((b_gate * conv).astype(BF16), wout_ref[...], preferred_element_type=F32)
    o_ref[0] = x + gate_ref[0] * y


def _conv_layer(x, g, sh, sc, gate, w_in, w_conv, w_out, tm=512):
    bsz, seq, d = x.shape
    tm = min(tm, seq)
    nt = seq // tm
    hb = tm // SUBLANES
    kern = functools.partial(_conv_kernel, tm=tm, d=d)
    mod_spec = pl.BlockSpec((1, 1, d), lambda b, i: (b, 0, 0))
    return pl.pallas_call(
        kern,
        out_shape=jax.ShapeDtypeStruct((bsz, seq, d), F32),
        grid=(bsz, nt),
        in_specs=[
            pl.BlockSpec((1, tm, d), lambda b, i: (b, i, 0)),
            pl.BlockSpec((1, SUBLANES, d), lambda b, i: (b, jnp.maximum(i * hb - 1, 0), 0)),
            pl.BlockSpec((1, d), lambda b, i: (0, 0)),
            mod_spec, mod_spec, mod_spec,
            pl.BlockSpec((d, 3 * d), lambda b, i: (0, 0)),
            pl.BlockSpec((CONV_WIDTH, d), lambda b, i: (0, 0)),
            pl.BlockSpec((d, d), lambda b, i: (0, 0)),
        ],
        out_specs=pl.BlockSpec((1, tm, d), lambda b, i: (b, i, 0)),
        scratch_shapes=[pltpu.VMEM((tm + SUBLANES, d), F32)],
        compiler_params=_cparams("parallel", "parallel"),
        name="conv_mixer",
    )(x, x, g, sh, sc, gate, w_in, w_conv, w_out)


def _ffn_kernel(x_ref, g_ref, sh_ref, sc_ref, gate_ref, wgu_ref, wd_ref, o_ref, *, dff, nchunk):
    x = x_ref[0]
    h = _norm_mod(x, g_ref[...], sh_ref[0], sc_ref[0]).astype(BF16)
    cols = dff // nchunk
    acc = jnp.zeros(x.shape, F32)
    for c in range(nchunk):
        gg = jnp.dot(h, wgu_ref[:, c * cols:(c + 1) * cols], preferred_element_type=F32)
        uu = jnp.dot(h, wgu_ref[:, dff + c * cols:dff + (c + 1) * cols], preferred_element_type=F32)
        a = (_silu(gg) * uu).astype(BF16)
        acc = acc + jnp.dot(a, wd_ref[c * cols:(c + 1) * cols, :], preferred_element_type=F32)
    o_ref[0] = x + gate_ref[0] * acc


def _ffn_layer(x, g, sh, sc, gate, w_gu, w_down, tm=512):
    bsz, seq, d = x.shape
    dff = w_down.shape[0]
    tm = min(tm, seq)
    kern = functools.partial(_ffn_kernel, dff=dff, nchunk=2)
    mod_spec = pl.BlockSpec((1, 1, d), lambda b, i: (b, 0, 0))
    return pl.pallas_call(
        kern,
        out_shape=jax.ShapeDtypeStruct((bsz, seq, d), F32),
        grid=(bsz, seq // tm),
        in_specs=[
            pl.BlockSpec((1, tm, d), lambda b, i: (b, i, 0)),
            pl.BlockSpec((1, d), lambda b, i: (0, 0)),
            mod_spec, mod_spec, mod_spec,
            pl.BlockSpec((d, 2 * dff), lambda b, i: (0, 0), pipeline_mode=pl.Buffered(1)),
            pl.BlockSpec((dff, d), lambda b, i: (0, 0), pipeline_mode=pl.Buffered(1)),
        ],
        out_specs=pl.BlockSpec((1, tm, d), lambda b, i: (b, i, 0)),
        compiler_params=_cparams("parallel", "parallel"),
        name="ffn_dense",
    )(x, g, sh, sc, gate, w_gu, w_down)


def _moe_kernel(x_ref, g_ref, sh_ref, sc_ref, gate_ref, rw_ref, rb_ref, wg_ref, wu_ref, wd_ref,
                o_ref, h_scr, gates_scr, acc_scr):
    e = pl.program_id(1)
    c = pl.program_id(2)
    ne = pl.num_programs(1)
    nc = pl.num_programs(2)

    @pl.when(jnp.logical_and(e == 0, c == 0))
    def _():
        hf = _norm_mod(x_ref[...], g_ref[...], sh_ref[0], sc_ref[0])
        h_scr[...] = hf.astype(BF16)
        logits = jnp.dot(hf, rw_ref[...], preferred_element_type=F32, precision=HIGHEST) + rb_ref[...]
        mx = jnp.max(logits, axis=-1, keepdims=True)
        ex = jnp.exp(logits - mx)
        probs = ex / jnp.sum(ex, axis=-1, keepdims=True)
        lane = lax.broadcasted_iota(I32, probs.shape, 1)
        m1 = jnp.max(probs, axis=-1, keepdims=True)
        i1 = jnp.min(jnp.where(probs == m1, lane, N_EXPERTS), axis=-1, keepdims=True)
        rest = jnp.where(lane == i1, -1.0, probs)
        m2 = jnp.max(rest, axis=-1, keepdims=True)
        i2 = jnp.min(jnp.where(rest == m2, lane, N_EXPERTS), axis=-1, keepdims=True)
        den = m1 + m2
        gates_scr[...] = (jnp.where(lane == i1, m1 / den, 0.0) + jnp.where(lane == i2, m2 / den, 0.0))
        acc_scr[...] = jnp.zeros(acc_scr.shape, F32)

    h = h_scr[...]
    gg = jnp.dot(h, wg_ref[0], preferred_element_type=F32)
    uu = jnp.dot(h, wu_ref[0], preferred_element_type=F32)
    a = (_silu(gg) * uu).astype(BF16)
    y = jnp.dot(a, wd_ref[0], preferred_element_type=F32)
    gates = gates_scr[...]
    lane = lax.broadcasted_iota(I32, gates.shape, 1)
    ge = jnp.sum(jnp.where(lane == e, gates, 0.0), axis=-1, keepdims=True)
    acc_scr[...] += ge * y

    @pl.when(jnp.logical_and(e == ne - 1, c == nc - 1))
    def _():
        o_ref[...] = x_ref[...] + gate_ref[0] * acc_scr[...]


def _moe_layer(x, g, sh, sc, gate, router_w, router_b, w_gu, w_down, tm=512, nchunk=2):
    bsz, seq, d = x.shape
    ne, dff, _ = w_down.shape
    tm = min(tm, seq)
    tpb = seq // tm
    cols = dff // nchunk
    xt = x.reshape(bsz * seq, d)
    mod_spec = pl.BlockSpec((1, 1, d), lambda i, e, c: (i // tpb, 0, 0))
    out = pl.pallas_call(
        _moe_kernel,
        out_shape=jax.ShapeDtypeStruct((bsz * seq, d), F32),
        grid=(bsz * tpb, ne, nchunk),
        in_specs=[
            pl.BlockSpec((tm, d), lambda i, e, c: (i, 0)),
            pl.BlockSpec((1, d), lambda i, e, c: (0, 0)),
            mod_spec, mod_spec, mod_spec,
            pl.BlockSpec((d, ne), lambda i, e, c: (0, 0)),
            pl.BlockSpec((1, ne), lambda i, e, c: (0, 0)),
            pl.BlockSpec((1, d, cols), lambda i, e, c: (e, 0, c)),
            pl.BlockSpec((1, d, cols), lambda i, e, c: (e, 0, nchunk + c)),
            pl.BlockSpec((1, cols, d), lambda i, e, c: (e, c, 0)),
        ],
        out_specs=pl.BlockSpec((tm, d), lambda i, e, c: (i, 0)),
        scratch_shapes=[pltpu.VMEM((tm, d), BF16), pltpu.VMEM((tm, ne), F32), pltpu.VMEM((tm, d), F32)],
        compiler_params=_cparams("parallel", "arbitrary", "arbitrary"),
        name="moe_dense",
    )(xt, g, sh, sc, gate, router_w, router_b.reshape(1, ne), w_gu, w_gu, w_down)
    return out.reshape(bsz, seq, d)


def _head_norm(q, hsum_ref, hexp_ref, gain, scale):
    ms = jnp.dot((q * q).astype(BF16), hsum_ref[...], preferred_element_type=F32)
    r = lax.rsqrt(ms + EPS)
    r_hi = r.astype(BF16)
    r_lo = (r - r_hi.astype(F32)).astype(BF16)
    rexp = (jnp.dot(r_hi, hexp_ref[...], preferred_element_type=F32)
            + jnp.dot(r_lo, hexp_ref[...], preferred_element_type=F32))
    return q * rexp * (gain * scale)


def _qkv_kernel(x_ref, g_ref, sh_ref, sc_ref, wm_ref, ws_ref, qg_ref, kg_ref,
                hsq_ref, heq_ref, hsk_ref, hek_ref,
                q_ref, k_ref, v_ref, qi_ref, kiwi_ref, *, dq, dk, dqi):
    x = x_ref[0]
    h = _norm_mod(x, g_ref[...], sh_ref[0], sc_ref[0]).astype(BF16)
    z = jnp.dot(h, wm_ref[...], preferred_element_type=F32)
    q = z[:, :dq]
    k = z[:, dq:dq + dk]
    v = z[:, dq + dk:dq + 2 * dk]
    qi = z[:, dq + 2 * dk:dq + 2 * dk + dqi]
    q_ref[0] = _head_norm(q, hsq_ref, heq_ref, qg_ref[...], HEAD_DIM ** -0.5).astype(BF16)
    k_ref[0] = _head_norm(k, hsk_ref, hek_ref, kg_ref[...], 1.0).astype(BF16)
    v_ref[0] = v.astype(BF16)
    qi_ref[0] = qi.astype(BF16)
    kiwi_ref[0] = jnp.dot(h, ws_ref[...], preferred_element_type=F32)


def _head_indicators(nheads):
    hs = np.zeros((nheads * HEAD_DIM, LANES), np.float32)
    he = np.zeros((LANES, nheads * HEAD_DIM), np.float32)
    for hd in range(nheads):
        hs[hd * HEAD_DIM:(hd + 1) * HEAD_DIM, hd] = 1.0 / HEAD_DIM
        he[hd, hd * HEAD_DIM:(hd + 1) * HEAD_DIM] = 1.0
    return jnp.asarray(hs, BF16), jnp.asarray(he, BF16)


def _qkv_proj(x, g, sh, sc, w_main, w_small, q_gain, k_gain, tm=512):
    bsz, seq, d = x.shape
    tm = min(tm, seq)
    dq = N_HEADS * HEAD_DIM
    dk = N_KV_HEADS * HEAD_DIM
    dqi = IDX_HEADS * IDX_DIM
    hsq, heq = _head_indicators(N_HEADS)
    hsk, hek = _head_indicators(N_KV_HEADS)
    qg = jnp.tile(q_gain.reshape(1, HEAD_DIM), (1, N_HEADS)).astype(F32)
    kg = jnp.tile(k_gain.reshape(1, HEAD_DIM), (1, N_KV_HEADS)).astype(F32)
    kern = functools.partial(_qkv_kernel, dq=dq, dk=dk, dqi=dqi)
    mod_spec = pl.BlockSpec((1, 1, d), lambda b, i: (b, 0, 0))

    def full(a):
        return pl.BlockSpec(a.shape, lambda b, i: (0,) * a.ndim)

    def out(n):
        return pl.BlockSpec((1, tm, n), lambda b, i: (b, i, 0))

    return pl.pallas_call(
        kern,
        out_shape=[
            jax.ShapeDtypeStruct((bsz, seq, dq), BF16),
            jax.ShapeDtypeStruct((bsz, seq, dk), BF16),
            jax.ShapeDtypeStruct((bsz, seq, dk), BF16),
            jax.ShapeDtypeStruct((bsz, seq, dqi), BF16),
            jax.ShapeDtypeStruct((bsz, seq, LANES), F32),
        ],
        grid=(bsz, seq // tm),
        in_specs=[
            pl.BlockSpec((1, tm, d), lambda b, i: (b, i, 0)),
            pl.BlockSpec((1, d), lambda b, i: (0, 0)),
            mod_spec, mod_spec,
            full(w_main), full(w_small), full(qg), full(kg),
            full(hsq), full(heq), full(hsk), full(hek),
        ],
        out_specs=[out(dq), out(dk), out(dk), out(dqi), out(LANES)],
        compiler_params=_cparams("parallel", "parallel"),
        name="attn_qkv",
    )(x, g, sh, sc, w_main, w_small, qg, kg, hsq, heq, hsk, hek)


def _rel_bucket_np(dist):
    max_exact = REL_BUCKETS // 2
    d = np.maximum(dist, 1).astype(np.float64)
    large = max_exact + (np.log(d / max_exact) / math.log(REL_MAX_DIST / max_exact)
                         * (REL_BUCKETS - max_exact)).astype(np.int32)
    large = np.minimum(large, REL_BUCKETS - 1)
    return np.where(dist < max_exact, dist, large).astype(np.int32)


def _bias_table_kernel(bucket_ref, rb_ref, o_ref):
    hd = pl.program_id(0)
    bucket = bucket_ref[...]
    acc = jnp.zeros(bucket.shape, F32)
    for b in range(REL_BUCKETS):
        acc = jnp.where(bucket == b, rb_ref[b, hd], acc)
    o_ref[0] = acc


def _bias_table(rel_bias):
    w = np.arange(BIAS_W)[:, None]
    i = np.arange(ATT_TQ)[None, :]
    bucket = jnp.asarray(_rel_bucket_np(np.maximum(i - w + BIAS_C, 0)))
    return pl.pallas_call(
        _bias_table_kernel,
        out_shape=jax.ShapeDtypeStruct((N_HEADS, BIAS_W, ATT_TQ), F32),
        grid=(N_HEADS,),
        in_specs=[
            pl.BlockSpec((BIAS_W, ATT_TQ), lambda hd: (0, 0)),
            pl.BlockSpec(memory_space=pltpu.SMEM),
        ],
        out_specs=pl.BlockSpec((1, BIAS_W, ATT_TQ), lambda hd: (hd, 0, 0)),
        compiler_params=_cparams("arbitrary"),
        name="attn_bias_table",
    )(bucket, rel_bias.astype(F32))


def _attn_kernel(qT_ref, qiT_ref, wiT_ref, k_ref, vT_ref, ki_ref, pt_ref, o_ref,
                 keys_scr, negm_scr, pidx_scr, oT_scr, *, top_k):
    tq, kc = ATT_TQ, ATT_KC
    qt = pl.program_id(1)
    q0 = qt * tq
    nch = (q0 + tq + kc - 1) // kc
    tpos = q0 + lax.broadcasted_iota(I32, (kc, tq), 1)
    srow = lax.broadcasted_iota(I32, (kc, tq), 0)

    qiT = qiT_ref[0]
    qi_all = jnp.concatenate([qiT[hd * IDX_DIM:(hd + 1) * IDX_DIM, :] for hd in range(IDX_HEADS)], axis=1)
    wiT = wiT_ref[0]

    def score_chunk(c, carry):
        ks = pl.multiple_of(c * kc, kc)
        kic = ki_ref[0, pl.ds(ks, kc), :]
        dots = jnp.dot(kic, qi_all, preferred_element_type=F32)
        acc = jnp.zeros((kc, tq), F32)
        for hd in range(IDX_HEADS):
            acc = acc + jnp.maximum(dots[:, hd * tq:(hd + 1) * tq], 0.0) * wiT[hd:hd + 1, :]
        acc = jnp.where(acc == 0.0, 0.0, acc)
        bits = pltpu.bitcast(acc, I32)
        key = jnp.where(bits < 0, bits ^ INT_MAX, bits)
        key = jnp.where(ks + srow <= tpos, key, INT_MIN)
        keys_scr[pl.ds(ks, kc), :] = key
        return carry

    lax.fori_loop(0, nch, score_chunk, 0)

    def count(pred):
        def body(c, acc):
            ks = pl.multiple_of(c * kc, kc)
            m = pred(keys_scr[pl.ds(ks, kc), :], ks + srow).astype(I32)
            return acc + jnp.sum(m.reshape(kc // SUBLANES, SUBLANES, tq), axis=0)
        acc = lax.fori_loop(0, nch, body, jnp.zeros((SUBLANES, tq), I32))
        return jnp.sum(acc, axis=0, keepdims=True)

    def bit_body(it, p):
        cand_p = p | lax.shift_left(jnp.int32(1), 31 - it)
        cand = cand_p ^ INT_MIN
        cnt = count(lambda k, s: k >= cand)
        return jnp.where(cnt >= top_k, cand_p, p)

    p_fin = lax.fori_loop(0, 32, bit_body, jnp.zeros((1, tq), I32))
    v = p_fin ^ INT_MIN

    cnt_gt = count(lambda k, s: k > v)
    cnt_eq = count(lambda k, s: k == v)
    need = top_k - cnt_gt
    pidx_scr[...] = jnp.full((1, tq), INT_MAX, I32)

    @pl.when(jnp.max(cnt_eq - need) > 0)
    def _():
        def ibit(it, p):
            cand = p | lax.shift_left(jnp.int32(1), 11 - it)
            cnt = count(lambda k, s: jnp.logical_and(k == v, s < cand))
            return jnp.where(cnt < need, cand, p)
        pidx_scr[...] = lax.fori_loop(0, 12, ibit, jnp.zeros((1, tq), I32))

    pidx = pidx_scr[...]

    def mask_chunk(c, carry):
        ks = pl.multiple_of(c * kc, kc)
        k = keys_scr[pl.ds(ks, kc), :]
        spos = ks + srow
        sel = jnp.logical_or(k > v, jnp.logical_and(k == v, spos <= pidx))
        sel = jnp.logical_and(sel, spos <= tpos)
        negm_scr[pl.ds(ks, kc), :] = jnp.where(sel, 0.0, NEG_BIG)
        return carry

    lax.fori_loop(0, nch, mask_chunk, 0)

    for n in range(N_KV_HEADS):
        qn = qT_ref[0, n * N_REP * HEAD_DIM:(n + 1) * N_REP * HEAD_DIM, :]
        q_all = jnp.concatenate([qn[g * HEAD_DIM:(g + 1) * HEAD_DIM, :] for g in range(N_REP)], axis=1)

        def att_chunk(c, carry, n=n, q_all=q_all):
            m, l, acc = carry
            ks = pl.multiple_of(c * kc, kc)
            kcn = k_ref[0, n, pl.ds(ks, kc), :]
            s = jnp.dot(kcn, q_all, preferred_element_type=F32)
            w0 = pl.multiple_of(jnp.maximum(BIAS_C - (q0 - ks), 0), LANES)
            negm = negm_scr[pl.ds(ks, kc), :]
            lg = jnp.concatenate(
                [s[:, g * tq:(g + 1) * tq] + pt_ref[n * N_REP + g, pl.ds(w0, kc), :] + negm
                 for g in range(N_REP)], axis=1)
            m_new = jnp.maximum(m, jnp.max(lg, axis=0, keepdims=True))
            alpha = jnp.exp(m - m_new)
            p = jnp.exp(lg - m_new)
            l_new = alpha * l + jnp.sum(p, axis=0, keepdims=True)
            vtc = vT_ref[0, c, n * HEAD_DIM:(n + 1) * HEAD_DIM, :]
            acc_new = alpha * acc + jnp.dot(vtc, p.astype(BF16), preferred_element_type=F32)
            return m_new, l_new, acc_new

        m0 = jnp.full((1, N_REP * tq), NEG_BIG, F32)
        l0 = jnp.zeros((1, N_REP * tq), F32)
        a0 = jnp.zeros((HEAD_DIM, N_REP * tq), F32)
        _, l_f, acc_f = lax.fori_loop(0, nch, att_chunk, (m0, l0, a0))
        o_t = acc_f / l_f
        for g in range(N_REP):
            r0 = (n * N_REP + g) * HEAD_DIM
            oT_scr[r0:r0 + HEAD_DIM, :] = o_t[:, g * tq:(g + 1) * tq]

    o_ref[0] = oT_scr[...].T.astype(BF16)


def _attention(qT, qiT, wiT, k4, vT, ki, ptab, top_k):
    bsz, dq, seq = qT.shape
    tq = ATT_TQ
    kern = functools.partial(_attn_kernel, top_k=top_k)
    return pl.pallas_call(
        kern,
        out_shape=jax.ShapeDtypeStruct((bsz, seq, dq), BF16),
        grid=(bsz, seq // tq),
        in_specs=[
            pl.BlockSpec((1, dq, tq), lambda b, i: (b, 0, i)),
            pl.BlockSpec((1, qiT.shape[1], tq), lambda b, i: (b, 0, i)),
            pl.BlockSpec((1, wiT.shape[1], tq), lambda b, i: (b, 0, i)),
            pl.BlockSpec((1,) + k4.shape[1:], lambda b, i: (b, 0, 0, 0)),
            pl.BlockSpec((1,) + vT.shape[1:], lambda b, i: (b, 0, 0, 0)),
            pl.BlockSpec((1,) + ki.shape[1:], lambda b, i: (b, 0, 0)),
            pl.BlockSpec(ptab.shape, lambda b, i: (0, 0, 0)),
        ],
        out_specs=pl.BlockSpec((1, tq, dq), lambda b, i: (b, i, 0)),
        scratch_shapes=[
            pltpu.VMEM((seq, tq), I32),
            pltpu.VMEM((seq, tq), F32),
            pltpu.VMEM((1, tq), I32),
            pltpu.VMEM((dq, tq), F32),
        ],
        compiler_params=_cparams("parallel", "arbitrary"),
        name="attn_core",
    )(qT, qiT, wiT, k4, vT, ki, ptab)


def _proj_res_kernel(a_ref, x_ref, gate_ref, w_ref, o_ref):
    y = jnp.dot(a_ref[0], w_ref[...], preferred_element_type=F32)
    o_ref[0] = x_ref[0] + gate_ref[0] * y


def _proj_residual(a, x, gate, w, tm=512):
    bsz, seq, d = x.shape
    tm = min(tm, seq)
    return pl.pallas_call(
        _proj_res_kernel,
        out_shape=jax.ShapeDtypeStruct((bsz, seq, d), F32),
        grid=(bsz, seq // tm),
        in_specs=[
            pl.BlockSpec((1, tm, a.shape[2]), lambda b, i: (b, i, 0)),
            pl.BlockSpec((1, tm, d), lambda b, i: (b, i, 0)),
            pl.BlockSpec((1, 1, d), lambda b, i: (b, 0, 0)),
            pl.BlockSpec(w.shape, lambda b, i: (0, 0)),
        ],
        out_specs=pl.BlockSpec((1, tm, d), lambda b, i: (b, i, 0)),
        compiler_params=_cparams("parallel", "parallel"),
        name="proj_residual",
    )(a, x, gate, w)


def _attn_layer(x, g, sh, sc, gate, w_in, q_gain, k_gain, w_out, rel_bias):
    bsz, seq, d = x.shape
    top_k = min(TOPK_MAX, seq // 4)
    dq = N_HEADS * HEAD_DIM
    dk = N_KV_HEADS * HEAD_DIM
    dqi = IDX_HEADS * IDX_DIM
    nmain = dq + 2 * dk + dqi
    w_main = w_in[:, :nmain].astype(BF16)
    w_small = jnp.pad(w_in[:, nmain:], ((0, 0), (0, LANES - (IDX_DIM + IDX_HEADS)))).astype(BF16)
    q, k, v, qi, kiwi = _qkv_proj(x, g, sh, sc, w_main, w_small, q_gain, k_gain)
    ki = kiwi[:, :, :IDX_DIM].astype(BF16)
    wi = kiwi[:, :, IDX_DIM:IDX_DIM + IDX_HEADS] * (IDX_HEADS ** -0.5 * IDX_DIM ** -0.5)
    qT = jnp.swapaxes(q, 1, 2)
    qiT = jnp.swapaxes(qi, 1, 2)
    wiT = jnp.swapaxes(wi, 1, 2)
    vT = jnp.swapaxes(v.reshape(bsz, seq // ATT_KC, ATT_KC, dk), 2, 3)
    k4 = jnp.swapaxes(k.reshape(bsz, seq, N_KV_HEADS, HEAD_DIM), 1, 2)
    ptab = _bias_table(rel_bias)
    attn = _attention(qT, qiT, wiT, k4, vT, ki, ptab, top_k)
    return _proj_residual(attn, x, gate, w_out.astype(BF16))


def _s5_prep_kernel(lre_ref, lim_ref, ls_ref, bre_ref, bim_ref, cre_ref, cim_ref,
                    kmat_ref, bcre_ref, bcim_ref, ccre_ref, ccim_ref, are_ref, aim_ref):
    t_len = SSM_CHUNK
    lre = jnp.minimum(lre_ref[0], -1e-4)
    lim = lim_ref[0]
    step = jnp.exp(ls_ref[0])
    ar = lre * step
    ai = lim * step

    def powers(jv):
        mag = jnp.exp(jv * ar)
        return mag * jnp.cos(jv * ai), mag * jnp.sin(jv * ai)

    lb_re, lb_im = powers(1.0)
    nr = lb_re - 1.0
    ni = lb_im
    den = lre * lre + lim * lim
    cf_re = (nr * lre + ni * lim) / den
    cf_im = (ni * lre - nr * lim) / den
    bre = bre_ref[0]
    bim = bim_ref[0]
    bb_re = cf_re * bre - cf_im * bim
    bb_im = cf_re * bim + cf_im * bre
    cre = cre_ref[0]
    cim = cim_ref[0]
    nst = lre.shape[-1]
    jv = lax.broadcasted_iota(I32, (t_len, 1, nst), 0).astype(F32)
    pj_re, pj_im = powers(jv)
    a_re = (cre[None] * pj_re - cim[None] * pj_im).reshape(t_len * SSM_GROUP, nst)
    a_im = (cre[None] * pj_im + cim[None] * pj_re).reshape(t_len * SSM_GROUP, nst)
    dn = (((1,), (1,)), ((), ()))
    kmat_ref[0] = (lax.dot_general(a_re, bb_re, dn, preferred_element_type=F32, precision=HIGHEST)
                   - lax.dot_general(a_im, bb_im, dn, preferred_element_type=F32, precision=HIGHEST))
    pr_re, pr_im = powers((t_len - 1.0) - jv)
    bcre_ref[0] = (bb_re[None] * pr_re - bb_im[None] * pr_im).reshape(t_len * SSM_GROUP, nst)
    bcim_ref[0] = (bb_re[None] * pr_im + bb_im[None] * pr_re).reshape(t_len * SSM_GROUP, nst)
    pn_re, pn_im = powers(jv + 1.0)
    ccre_ref[0] = (cre[None] * pn_re - cim[None] * pn_im).reshape(t_len * SSM_GROUP, nst)
    ccim_ref[0] = (cre[None] * pn_im + cim[None] * pn_re).reshape(t_len * SSM_GROUP, nst)
    at_re, at_im = powers(float(t_len))
    are_ref[0] = at_re
    aim_ref[0] = at_im


def _s5_prep(lam_re, lam_im, log_step, b_re, b_im, c_re, c_im):
    ng, nst = lam_re.shape
    tc = SSM_CHUNK * SSM_GROUP
    vec = pl.BlockSpec((1, 1, nst), lambda gi: (gi, 0, 0))
    mat = pl.BlockSpec((1, SSM_GROUP, nst), lambda gi: (gi, 0, 0))
    big = pl.BlockSpec((1, tc, nst), lambda gi: (gi, 0, 0))
    return pl.pallas_call(
        _s5_prep_kernel,
        out_shape=[
            jax.ShapeDtypeStruct((ng, tc, SSM_GROUP), F32),
            jax.ShapeDtypeStruct((ng, tc, nst), F32),
            jax.ShapeDtypeStruct((ng, tc, nst), F32),
            jax.ShapeDtypeStruct((ng, tc, nst), F32),
            jax.ShapeDtypeStruct((ng, tc, nst), F32),
            jax.ShapeDtypeStruct((ng, 1, nst), F32),
            jax.ShapeDtypeStruct((ng, 1, nst), F32),
        ],
        grid=(ng,),
        in_specs=[vec, vec, pl.BlockSpec((1, 1, 1), lambda gi: (gi, 0, 0)), mat, mat, mat, mat],
        out_specs=[pl.BlockSpec((1, tc, SSM_GROUP), lambda gi: (gi, 0, 0)), big, big, big, big, vec, vec],
        compiler_params=_cparams("parallel"),
        name="s5_prep",
    )(lam_re.reshape(ng, 1, nst), lam_im.reshape(ng, 1, nst), log_step.reshape(ng, 1, 1),
      jnp.swapaxes(b_re, 1, 2), jnp.swapaxes(b_im, 1, 2), c_re, c_im)


def _s5_pre_kernel(x_ref, g_ref, sh_ref, sc_ref, o_ref):
    o_ref[0] = _norm_mod(x_ref[0], g_ref[...], sh_ref[0], sc_ref[0]).astype(BF16)


def _s5_pre(x, g, sh, sc, tm=512):
    bsz, seq, d = x.shape
    tm = min(tm, seq)
    mod_spec = pl.BlockSpec((1, 1, d), lambda b, i: (b, 0, 0))
    return pl.pallas_call(
        _s5_pre_kernel,
        out_shape=jax.ShapeDtypeStruct((bsz, seq, d), BF16),
        grid=(bsz, seq // tm),
        in_specs=[pl.BlockSpec((1, tm, d), lambda b, i: (b, i, 0)),
                  pl.BlockSpec((1, d), lambda b, i: (0, 0)), mod_spec, mod_spec],
        out_specs=pl.BlockSpec((1, tm, d), lambda b, i: (b, i, 0)),
        compiler_params=_cparams("parallel", "parallel"),
        name="s5_pre",
    )(x, g, sh, sc)


def _s5_scan_kernel(x_ref, bc_ref, mt_ref, cc_ref, are_ref, aim_ref, y_ref, v_scr, sp_scr, *, nseg, bsz):
    half = x_ref.shape[2] // 2
    nsteps = x_ref.shape[1] // SUBLANES
    x = x_ref[0]
    v_scr[...] = jnp.dot(x, bc_ref[0], preferred_element_type=F32)
    a_re = are_ref[0]
    a_im = aim_ref[0]

    def step(j, carry):
        s_re, s_im = carry
        r0 = pl.multiple_of(j * SUBLANES, SUBLANES)
        sp_scr[pl.ds(r0, SUBLANES), 0:LANES] = s_re
        sp_scr[pl.ds(r0, SUBLANES), LANES:2 * LANES] = s_im
        v_re = v_scr[pl.ds(r0, SUBLANES), 0:LANES]
        v_im = v_scr[pl.ds(r0, SUBLANES), LANES:2 * LANES]
        n_re = a_re * s_re - a_im * s_im + v_re
        n_im = a_re * s_im + a_im * s_re + v_im
        return n_re, n_im

    z = jnp.zeros((SUBLANES, LANES), F32)
    f_re, f_im = lax.fori_loop(0, nsteps, step, (z, z))

    def fix(j, carry):
        c_re, c_im = carry
        r0 = pl.multiple_of(j * SUBLANES, SUBLANES)
        sp_scr[pl.ds(r0, SUBLANES), 0:LANES] += c_re
        sp_scr[pl.ds(r0, SUBLANES), LANES:2 * LANES] += c_im
        return a_re * c_re - a_im * c_im, a_re * c_im + a_im * c_re

    row = lax.broadcasted_iota(I32, (SUBLANES, LANES), 0)
    for k in range(1, nseg):
        in_seg = jnp.logical_and(row >= k * bsz, row < (k + 1) * bsz)
        c_re = jnp.where(in_seg, pltpu.roll(f_re, bsz, 0), 0.0)
        c_im = jnp.where(in_seg, pltpu.roll(f_im, bsz, 0), 0.0)
        e_re, e_im = lax.fori_loop(0, nsteps, fix, (c_re, c_im))
        f_re = f_re + e_re
        f_im = f_im + e_im

    y0 = jnp.dot(x[:, :half], mt_ref[0, 0], preferred_element_type=F32)
    y1 = jnp.dot(x[:, half:], mt_ref[0, 1], preferred_element_type=F32)
    yc = jnp.dot(sp_scr[...].astype(BF16), cc_ref[0], preferred_element_type=F32)
    y_ref[0] = jnp.concatenate([y0, y1], axis=1) + yc


def _s5_scan(xp, bcp, mt, ccp, a_re, a_im, bsz):
    npair, n, width = xp.shape
    kern = functools.partial(_s5_scan_kernel, nseg=SUBLANES // bsz, bsz=bsz)
    return pl.pallas_call(
        kern,
        out_shape=jax.ShapeDtypeStruct((npair, n, width), F32),
        grid=(npair,),
        in_specs=[
            pl.BlockSpec((1, n, width), lambda p: (p, 0, 0)),
            pl.BlockSpec((1,) + bcp.shape[1:], lambda p: (p, 0, 0)),
            pl.BlockSpec((1,) + mt.shape[1:], lambda p: (p, 0, 0, 0)),
            pl.BlockSpec((1,) + ccp.shape[1:], lambda p: (p, 0, 0)),
            pl.BlockSpec((1, 1, LANES), lambda p: (p, 0, 0)),
            pl.BlockSpec((1, 1, LANES), lambda p: (p, 0, 0)),
        ],
        out_specs=pl.BlockSpec((1, n, width), lambda p: (p, 0, 0)),
        scratch_shapes=[pltpu.VMEM((n, 2 * LANES), F32), pltpu.VMEM((n, 2 * LANES), F32)],
        compiler_params=_cparams("parallel"),
        name="s5_scan",
    )(xp, bcp, mt, ccp, a_re, a_im)


def _s5_post_kernel(x_ref, y_ref, g_ref, sh_ref, sc_ref, gate_ref, dsk_ref, w_ref, o_ref, *, d):
    x = x_ref[0]
    h = _norm_mod(x, g_ref[...], sh_ref[0], sc_ref[0])
    yy = y_ref[0] + dsk_ref[...] * h
    gl = jax.nn.gelu(yy).astype(BF16)
    z = jnp.dot(gl, w_ref[...], preferred_element_type=F32)
    o_ref[0] = x + gate_ref[0] * (z[:, :d] * jax.nn.sigmoid(z[:, d:]))


def _s5_post(x, y, g, sh, sc, gate, d_skip, w_glu, tm=512):
    bsz, seq, d = x.shape
    tm = min(tm, seq)
    kern = functools.partial(_s5_post_kernel, d=d)
    mod_spec = pl.BlockSpec((1, 1, d), lambda b, i: (b, 0, 0))
    tile = pl.BlockSpec((1, tm, d), lambda b, i: (b, i, 0))
    return pl.pallas_call(
        kern,
        out_shape=jax.ShapeDtypeStruct((bsz, seq, d), F32),
        grid=(bsz, seq // tm),
        in_specs=[tile, tile, pl.BlockSpec((1, d), lambda b, i: (0, 0)), mod_spec, mod_spec, mod_spec,
                  pl.BlockSpec((1, d), lambda b, i: (0, 0)),
                  pl.BlockSpec(w_glu.shape, lambda b, i: (0, 0))],
        out_specs=tile,
        compiler_params=_cparams("parallel", "parallel"),
        name="s5_post",
    )(x, y, g, sh, sc, gate, d_skip, w_glu)


def _s5_layer(x, g, sh, sc, gate, lam_re, lam_im, log_step, b_re, b_im, c_re, c_im, d_skip, w_glu):
    bsz, seq, d = x.shape
    ng, nst = lam_re.shape
    t_len, cg = SSM_CHUNK, SSM_GROUP
    tc = t_len * cg
    npair = ng // 2
    nj = seq // t_len
    kmat, bc_re, bc_im, cc_re, cc_im, a_re, a_im = _s5_prep(lam_re, lam_im, log_step, b_re, b_im, c_re, c_im)

    tt = np.arange(t_len)
    lag = tt[:, None] - tt[None, :]
    k4 = kmat.reshape(ng, t_len, cg, cg)[:, np.maximum(lag, 0)]
    k4 = jnp.where(jnp.asarray(lag >= 0)[None, :, :, None, None], k4, 0.0)
    mt = jnp.transpose(k4, (0, 2, 4, 1, 3)).reshape(npair, 2, tc, tc).astype(BF16)

    zb = jnp.zeros((npair, tc, nst), F32)
    bre = bc_re.reshape(npair, 2, tc, nst)
    bim = bc_im.reshape(npair, 2, tc, nst)
    bcp = jnp.concatenate([
        jnp.concatenate([bre[:, 0], zb, bim[:, 0], zb], axis=2),
        jnp.concatenate([zb, bre[:, 1], zb, bim[:, 1]], axis=2)], axis=1).astype(BF16)
    cre = jnp.swapaxes(cc_re, 1, 2).reshape(npair, 2, nst, tc)
    cim = jnp.swapaxes(cc_im, 1, 2).reshape(npair, 2, nst, tc)
    zc = jnp.zeros((npair, nst, tc), F32)
    ccp = jnp.concatenate([
        jnp.concatenate([cre[:, 0], zc], axis=2),
        jnp.concatenate([zc, cre[:, 1]], axis=2),
        jnp.concatenate([-cim[:, 0], zc], axis=2),
        jnp.concatenate([zc, -cim[:, 1]], axis=2)], axis=1).astype(BF16)
    ap_re = a_re.reshape(npair, 1, 2 * nst)
    ap_im = a_im.reshape(npair, 1, 2 * nst)

    h = _s5_pre(x, g, sh, sc)
    assert SUBLANES % bsz == 0 and nj % (SUBLANES // bsz) == 0
    nseg = SUBLANES // bsz
    xp = h.reshape(bsz, nseg, nj // nseg, t_len, npair, 2, cg)
    xp = jnp.transpose(xp, (4, 2, 1, 0, 5, 3, 6)).reshape(npair, nj * bsz, 2 * tc)
    yp = _s5_scan(xp, bcp, mt, ccp, ap_re, ap_im, bsz)
    y = yp.reshape(npair, nj // nseg, nseg, bsz, 2, t_len, cg)
    y = jnp.transpose(y, (3, 2, 1, 5, 0, 4, 6)).reshape(bsz, seq, d)
    return _s5_post(x, y, g, sh, sc, gate, d_skip.reshape(1, d).astype(F32), w_glu.astype(BF16))


def kernel(x, c, ada_w, ada_b, norm_g, conv_w_in, conv_w, conv_w_out, attn_w_in, attn_q_gain, attn_k_gain, attn_w_out, rel_bias, ssm_lambda_re, ssm_lambda_im, ssm_log_step, ssm_b_re, ssm_b_im, ssm_c_re, ssm_c_im, ssm_d, ssm_w_glu, ffn_w_gu, ffn_w_down, moe_router_w, moe_router_b, moe_w_gu, moe_w_down):
    bsz, seq, d = x.shape
    depth = ada_w.shape[0]
    mod = _ada_mod(c, ada_w, ada_b).reshape(depth, bsz, 6, 1, d)
    for i in range(depth):
        sh1, sc1, g1, sh2, sc2, g2 = (mod[i, :, r] for r in range(6))
        gn1 = norm_g[i, 0].reshape(1, d)
        gn2 = norm_g[i, 1].reshape(1, d)
        j = i // N_MIXERS
        if i % N_MIXERS == 0:
            x = _conv_layer(x, gn1, sh1, sc1, g1, conv_w_in[j].astype(BF16), conv_w[j],
                            conv_w_out[j].astype(BF16))
        elif i % N_MIXERS == 1:
            x = _attn_layer(x, gn1, sh1, sc1, g1, attn_w_in[j], attn_q_gain[j], attn_k_gain[j],
                            attn_w_out[j], rel_bias)
        else:
            x = _s5_layer(x, gn1, sh1, sc1, g1, ssm_lambda_re[j], ssm_lambda_im[j], ssm_log_step[j],
                          ssm_b_re[j], ssm_b_im[j], ssm_c_re[j], ssm_c_im[j], ssm_d[j], ssm_w_glu[j])
        if i % 2 == 0:
            x = _ffn_layer(x, gn2, sh2, sc2, g2, ffn_w_gu[i // 2].astype(BF16), ffn_w_down[i // 2].astype(BF16))
        else:
            x = _moe_layer(x, gn2, sh2, sc2, g2, moe_router_w[i // 2], moe_router_b[i // 2],
                           moe_w_gu[i // 2].astype(BF16), moe_w_down[i // 2].astype(BF16))
    return x
```

```python
import functools
import math

import numpy as np
import jax
import jax.numpy as jnp
from jax import lax
from jax.experimental import pallas as pl
from jax.experimental.pallas import tpu as pltpu

F32 = jnp.float32
BF16 = jnp.bfloat16
I32 = jnp.int32
HIGHEST = lax.Precision.HIGHEST

DEPTH = 4
N_MIXERS = 3
EPS = 1e-6
CONV_WIDTH = 3
N_HEADS = 16
N_KV_HEADS = 4
N_REP = N_HEADS // N_KV_HEADS
HEAD_DIM = 64
IDX_HEADS = 8
IDX_DIM = 64
TOPK_MAX = 256
REL_BUCKETS = 32
REL_MAX_DIST = 128
SSM_GROUP = 16
SSM_STATE = 64
N_EXPERTS = 8
TOP_K_EXPERTS = 2

VMEM_LIMIT_BYTES = 56 * 1024 * 1024
LANES = 128
SUBLANES = 8

INT_MIN = -(2 ** 31)
INT_MAX = 2 ** 31 - 1
NEG_BIG = -1e30

ATT_TQ = 128
ATT_KC = 256
BIAS_C = 384
BIAS_W = BIAS_C + ATT_KC

SSM_CHUNK = 8


def _cparams(*sem):
    return pltpu.CompilerParams(dimension_semantics=sem, vmem_limit_bytes=VMEM_LIMIT_BYTES)


def _norm_mod(x, g, shift, scale):
    ms = jnp.mean(x * x, axis=-1, keepdims=True)
    y = x * lax.rsqrt(ms + EPS)
    return (y * g) * (1.0 + scale) + shift


def _silu(x):
    return x * jax.nn.sigmoid(x)


def _ada_kernel(c_ref, w_ref, b_ref, o_ref):
    c = c_ref[...]
    cond = _silu(c)
    o_ref[0] = jnp.dot(cond, w_ref[0], preferred_element_type=F32, precision=HIGHEST) + b_ref[0]


def _ada_mod(c, ada_w, ada_b):
    depth, d, d6 = ada_w.shape
    bsz = c.shape[0]
    tn = d
    return pl.pallas_call(
        _ada_kernel,
        out_shape=jax.ShapeDtypeStruct((depth, bsz, d6), F32),
        grid=(depth, d6 // tn),
        in_specs=[
            pl.BlockSpec((bsz, d), lambda i, j: (0, 0)),
            pl.BlockSpec((1, d, tn), lambda i, j: (i, 0, j)),
            pl.BlockSpec((1, 1, tn), lambda i, j: (i, 0, j)),
        ],
        out_specs=pl.BlockSpec((1, bsz, tn), lambda i, j: (i, 0, j)),
        compiler_params=_cparams("parallel", "parallel"),
        name="ada_mod",
    )(c, ada_w, ada_b.reshape(depth, 1, d6))


def _conv_kernel(x_ref, xh_ref, g_ref, sh_ref, sc_ref, gate_ref, win_ref, wc_ref, wout_ref,
                 o_ref, u_scr, *, tm, d):
    i = pl.program_id(1)
    g = g_ref[...]
    sh = sh_ref[0]
    sc = sc_ref[0]
    x = x_ref[0]
    h = _norm_mod(x, g, sh, sc).astype(BF16)
    z = jnp.dot(h, win_ref[...], preferred_element_type=F32)
    b_gate = z[:, :d]
    u = z[:, d:2 * d] * z[:, 2 * d:]
    hh = _norm_mod(xh_ref[0], g, sh, sc).astype(BF16)
    zh = jnp.dot(hh, win_ref[:, d:], preferred_element_type=F32)
    uh = zh[:, :d] * zh[:, d:]
    uh = jnp.where(i > 0, uh, 0.0)
    u_scr[0:SUBLANES, :] = uh
    u_scr[SUBLANES:SUBLANES + tm, :] = u
    wc = wc_ref[...]
    conv = (wc[0:1, :] * u_scr[SUBLANES - 2:SUBLANES - 2 + tm, :]
            + wc[1:2, :] * u_scr[SUBLANES - 1:SUBLANES - 1 + tm, :]
            + wc[2:3, :] * u)
    y = jnp.dot((b_gate * conv).astype(BF16), wout_ref[...], preferred_element_type=F32)
    o_ref[0] = x + gate_ref[0] * y


def _conv_layer(x, g, sh, sc, gate, w_in, w_conv, w_out, tm=512):
    bsz, seq, d = x.shape
    tm = min(tm, seq)
    nt = seq // tm
    hb = tm // SUBLANES
    kern = functools.partial(_conv_kernel, tm=tm, d=d)
    mod_spec = pl.BlockSpec((1, 1, d), lambda b, i: (b, 0, 0))
    return pl.pallas_call(
        kern,
        out_shape=jax.ShapeDtypeStruct((bsz, seq, d), F32),
        grid=(bsz, nt),
        in_specs=[
            pl.BlockSpec((1, tm, d), lambda b, i: (b, i, 0)),
            pl.BlockSpec((1, SUBLANES, d), lambda b, i: (b, jnp.maximum(i * hb - 1, 0), 0)),
            pl.BlockSpec((1, d), lambda b, i: (0, 0)),
            mod_spec, mod_spec, mod_spec,
            pl.BlockSpec((d, 3 * d), lambda b, i: (0, 0)),
            pl.BlockSpec((CONV_WIDTH, d), lambda b, i: (0, 0)),
            pl.BlockSpec((d, d), lambda b, i: (0, 0)),
        ],
        out_specs=pl.BlockSpec((1, tm, d), lambda b, i: (b, i, 0)),
        scratch_shapes=[pltpu.VMEM((tm + SUBLANES, d), F32)],
        compiler_params=_cparams("parallel", "parallel"),
        name="conv_mixer",
    )(x, x, g, sh, sc, gate, w_in, w_conv, w_out)


def _ffn_kernel(x_ref, g_ref, sh_ref, sc_ref, gate_ref, wgu_ref, wd_ref, o_ref, *, dff, nchunk):
    x = x_ref[0]
    h = _norm_mod(x, g_ref[...], sh_ref[0], sc_ref[0]).astype(BF16)
    cols = dff // nchunk
    acc = jnp.zeros(x.shape, F32)
    for c in range(nchunk):
        gg = jnp.dot(h, wgu_ref[:, c * cols:(c + 1) * cols], preferred_element_type=F32)
        uu = jnp.dot(h, wgu_ref[:, dff + c * cols:dff + (c + 1) * cols], preferred_element_type=F32)
        a = (_silu(gg) * uu).astype(BF16)
        acc = acc + jnp.dot(a, wd_ref[c * cols:(c + 1) * cols, :], preferred_element_type=F32)
    o_ref[0] = x + gate_ref[0] * acc


def _ffn_layer(x, g, sh, sc, gate, w_gu, w_down, tm=512):
    bsz, seq, d = x.shape
    dff = w_down.shape[0]
    tm = min(tm, seq)
    kern = functools.partial(_ffn_kernel, dff=dff, nchunk=2)
    mod_spec = pl.BlockSpec((1, 1, d), lambda b, i: (b, 0, 0))
    return pl.pallas_call(
        kern,
        out_shape=jax.ShapeDtypeStruct((bsz, seq, d), F32),
        grid=(bsz, seq // tm),
        in_specs=[
            pl.BlockSpec((1, tm, d), lambda b, i: (b, i, 0)),
            pl.BlockSpec((1, d), lambda b, i: (0, 0)),
            mod_spec, mod_spec, mod_spec,
            pl.BlockSpec((d, 2 * dff), lambda b, i: (0, 0), pipeline_mode=pl.Buffered(1)),
            pl.BlockSpec((dff, d), lambda b, i: (0, 0), pipeline_mode=pl.Buffered(1)),
        ],
        out_specs=pl.BlockSpec((1, tm, d), lambda b, i: (b, i, 0)),
        compiler_params=_cparams("parallel", "parallel"),
        name="ffn_dense",
    )(x, g, sh, sc, gate, w_gu, w_down)


def _moe_kernel(x_ref, g_ref, sh_ref, sc_ref, gate_ref, rw_ref, rb_ref, wg_ref, wu_ref, wd_ref,
                o_ref, h_scr, gates_scr, acc_scr):
    e = pl.program_id(1)
    c = pl.program_id(2)
    ne = pl.num_programs(1)
    nc = pl.num_programs(2)

    @pl.when(jnp.logical_and(e == 0, c == 0))
    def _():
        hf = _norm_mod(x_ref[...], g_ref[...], sh_ref[0], sc_ref[0])
        h_scr[...] = hf.astype(BF16)
        logits = jnp.dot(hf, rw_ref[...], preferred_element_type=F32, precision=HIGHEST) + rb_ref[...]
        mx = jnp.max(logits, axis=-1, keepdims=True)
        ex = jnp.exp(logits - mx)
        probs = ex / jnp.sum(ex, axis=-1, keepdims=True)
        lane = lax.broadcasted_iota(I32, probs.shape, 1)
        m1 = jnp.max(probs, axis=-1, keepdims=True)
        i1 = jnp.min(jnp.where(probs == m1, lane, N_EXPERTS), axis=-1, keepdims=True)
        rest = jnp.where(lane == i1, -1.0, probs)
        m2 = jnp.max(rest, axis=-1, keepdims=True)
        i2 = jnp.min(jnp.where(rest == m2, lane, N_EXPERTS), axis=-1, keepdims=True)
        den = m1 + m2
        gates_scr[...] = (jnp.where(lane == i1, m1 / den, 0.0) + jnp.where(lane == i2, m2 / den, 0.0))
        acc_scr[...] = jnp.zeros(acc_scr.shape, F32)

    h = h_scr[...]
    gg = jnp.dot(h, wg_ref[0], preferred_element_type=F32)
    uu = jnp.dot(h, wu_ref[0], preferred_element_type=F32)
    a = (_silu(gg) * uu).astype(BF16)
    y = jnp.dot(a, wd_ref[0], preferred_element_type=F32)
    gates = gates_scr[...]
    lane = lax.broadcasted_iota(I32, gates.shape, 1)
    ge = jnp.sum(jnp.where(lane == e, gates, 0.0), axis=-1, keepdims=True)
    acc_scr[...] += ge * y

    @pl.when(jnp.logical_and(e == ne - 1, c == nc - 1))
    def _():
        o_ref[...] = x_ref[...] + gate_ref[0] * acc_scr[...]


def _moe_layer(x, g, sh, sc, gate, router_w, router_b, w_gu, w_down, tm=512, nchunk=2):
    bsz, seq, d = x.shape
    ne, dff, _ = w_down.shape
    tm = min(tm, seq)
    tpb = seq // tm
    cols = dff // nchunk
    xt = x.reshape(bsz * seq, d)
    mod_spec = pl.BlockSpec((1, 1, d), lambda i, e, c: (i // tpb, 0, 0))
    out = pl.pallas_call(
        _moe_kernel,
        out_shape=jax.ShapeDtypeStruct((bsz * seq, d), F32),
        grid=(bsz * tpb, ne, nchunk),
        in_specs=[
            pl.BlockSpec((tm, d), lambda i, e, c: (i, 0)),
            pl.BlockSpec((1, d), lambda i, e, c: (0, 0)),
            mod_spec, mod_spec, mod_spec,
            pl.BlockSpec((d, ne), lambda i, e, c: (0, 0)),
            pl.BlockSpec((1, ne), lambda i, e, c: (0, 0)),
            pl.BlockSpec((1, d, cols), lambda i, e, c: (e, 0, c)),
            pl.BlockSpec((1, d, cols), lambda i, e, c: (e, 0, nchunk + c)),
            pl.BlockSpec((1, cols, d), lambda i, e, c: (e, c, 0)),
        ],
        out_specs=pl.BlockSpec((tm, d), lambda i, e, c: (i, 0)),
        scratch_shapes=[pltpu.VMEM((tm, d), BF16), pltpu.VMEM((tm, ne), F32), pltpu.VMEM((tm, d), F32)],
        compiler_params=_cparams("parallel", "arbitrary", "arbitrary"),
        name="moe_dense",
    )(xt, g, sh, sc, gate, router_w, router_b.reshape(1, ne), w_gu, w_gu, w_down)
    return out.reshape(bsz, seq, d)


def _head_norm(q, hsum_ref, hexp_ref, gain, scale):
    ms = jnp.dot((q * q).astype(BF16), hsum_ref[...], preferred_element_type=F32)
    r = lax.rsqrt(ms + EPS)
    r_hi = r.astype(BF16)
    r_lo = (r - r_hi.astype(F32)).astype(BF16)
    rexp = (jnp.dot(r_hi, hexp_ref[...], preferred_element_type=F32)
            + jnp.dot(r_lo, hexp_ref[...], preferred_element_type=F32))
    return q * rexp * (gain * scale)


def _qkv_kernel(x_ref, g_ref, sh_ref, sc_ref, wm_ref, ws_ref, qg_ref, kg_ref,
                hsq_ref, heq_ref, hsk_ref, hek_ref,
                q_ref, k_ref, v_ref, qi_ref, kiwi_ref, *, dq, dk, dqi):
    x = x_ref[0]
    h = _norm_mod(x, g_ref[...], sh_ref[0], sc_ref[0]).astype(BF16)
    z = jnp.dot(h, wm_ref[...], preferred_element_type=F32)
    q = z[:, :dq]
    k = z[:, dq:dq + dk]
    v = z[:, dq + dk:dq + 2 * dk]
    qi = z[:, dq + 2 * dk:dq + 2 * dk + dqi]
    q_ref[0] = _head_norm(q, hsq_ref, heq_ref, qg_ref[...], HEAD_DIM ** -0.5).astype(BF16)
    k_ref[0] = _head_norm(k, hsk_ref, hek_ref, kg_ref[...], 1.0).astype(BF16)
    v_ref[0] = v.astype(BF16)
    qi_ref[0] = qi.astype(BF16)
    kiwi_ref[0] = jnp.dot(h, ws_ref[...], preferred_element_type=F32)


def _head_indicators(nheads):
    hs = np.zeros((nheads * HEAD_DIM, LANES), np.float32)
    he = np.zeros((LANES, nheads * HEAD_DIM), np.float32)
    for hd in range(nheads):
        hs[hd * HEAD_DIM:(hd + 1) * HEAD_DIM, hd] = 1.0 / HEAD_DIM
        he[hd, hd * HEAD_DIM:(hd + 1) * HEAD_DIM] = 1.0
    return jnp.asarray(hs, BF16), jnp.asarray(he, BF16)


def _qkv_proj(x, g, sh, sc, w_main, w_small, q_gain, k_gain, tm=512):
    bsz, seq, d = x.shape
    tm = min(tm, seq)
    dq = N_HEADS * HEAD_DIM
    dk = N_KV_HEADS * HEAD_DIM
    dqi = IDX_HEADS * IDX_DIM
    hsq, heq = _head_indicators(N_HEADS)
    hsk, hek = _head_indicators(N_KV_HEADS)
    qg = jnp.tile(q_gain.reshape(1, HEAD_DIM), (1, N_HEADS)).astype(F32)
    kg = jnp.tile(k_gain.reshape(1, HEAD_DIM), (1, N_KV_HEADS)).astype(F32)
    kern = functools.partial(_qkv_kernel, dq=dq, dk=dk, dqi=dqi)
    mod_spec = pl.BlockSpec((1, 1, d), lambda b, i: (b, 0, 0))

    def full(a):
        return pl.BlockSpec(a.shape, lambda b, i: (0,) * a.ndim)

    def out(n):
        return pl.BlockSpec((1, tm, n), lambda b, i: (b, i, 0))

    return pl.pallas_call(
        kern,
        out_shape=[
            jax.ShapeDtypeStruct((bsz, seq, dq), BF16),
            jax.ShapeDtypeStruct((bsz, seq, dk), BF16),
            jax.ShapeDtypeStruct((bsz, seq, dk), BF16),
            jax.ShapeDtypeStruct((bsz, seq, dqi), BF16),
            jax.ShapeDtypeStruct((bsz, seq, LANES), F32),
        ],
        grid=(bsz, seq // tm),
        in_specs=[
            pl.BlockSpec((1, tm, d), lambda b, i: (b, i, 0)),
            pl.BlockSpec((1, d), lambda b, i: (0, 0)),
            mod_spec, mod_spec,
            full(w_main), full(w_small), full(qg), full(kg),
            full(hsq), full(heq), full(hsk), full(hek),
        ],
        out_specs=[out(dq), out(dk), out(dk), out(dqi), out(LANES)],
        compiler_params=_cparams("parallel", "parallel"),
        name="attn_qkv",
    )(x, g, sh, sc, w_main, w_small, qg, kg, hsq, heq, hsk, hek)


def _rel_bucket_np(dist):
    max_exact = REL_BUCKETS // 2
    d = np.maximum(dist, 1).astype(np.float64)
    large = max_exact + (np.log(d / max_exact) / math.log(REL_MAX_DIST / max_exact)
                         * (REL_BUCKETS - max_exact)).astype(np.int32)
    large = np.minimum(large, REL_BUCKETS - 1)
    return np.where(dist < max_exact, dist, large).astype(np.int32)


def _bias_table_kernel(bucket_ref, rb_ref, o_ref):
    hd = pl.program_id(0)
    bucket = bucket_ref[...]
    acc = jnp.zeros(bucket.shape, F32)
    for b in range(REL_BUCKETS):
        acc = jnp.where(bucket == b, rb_ref[b, hd], acc)
    o_ref[0] = acc


def _bias_table(rel_bias):
    w = np.arange(BIAS_W)[:, None]
    i = np.arange(ATT_TQ)[None, :]
    bucket = jnp.asarray(_rel_bucket_np(np.maximum(i - w + BIAS_C, 0)))
    return pl.pallas_call(
        _bias_table_kernel,
        out_shape=jax.ShapeDtypeStruct((N_HEADS, BIAS_W, ATT_TQ), F32),
        grid=(N_HEADS,),
        in_specs=[
            pl.BlockSpec((BIAS_W, ATT_TQ), lambda hd: (0, 0)),
            pl.BlockSpec(memory_space=pltpu.SMEM),
        ],
        out_specs=pl.BlockSpec((1, BIAS_W, ATT_TQ), lambda hd: (hd, 0, 0)),
        compiler_params=_cparams("arbitrary"),
        name="attn_bias_table",
    )(bucket, rel_bias.astype(F32))


def _attn_kernel(qT_ref, qiT_ref, wiT_ref, k_ref, vT_ref, ki_ref, pt_ref, o_ref,
                 keys_scr, negm_scr, pidx_scr, oT_scr, *, top_k):
    tq, kc = ATT_TQ, ATT_KC
    qt = pl.program_id(1)
    q0 = qt * tq
    nch = (q0 + tq + kc - 1) // kc
    tpos = q0 + lax.broadcasted_iota(I32, (kc, tq), 1)
    srow = lax.broadcasted_iota(I32, (kc, tq), 0)

    qiT = qiT_ref[0]
    qi_all = jnp.concatenate([qiT[hd * IDX_DIM:(hd + 1) * IDX_DIM, :] for hd in range(IDX_HEADS)], axis=1)
    wiT = wiT_ref[0]

    def score_chunk(c, carry):
        ks = pl.multiple_of(c * kc, kc)
        kic = ki_ref[0, pl.ds(ks, kc), :]
        dots = jnp.dot(kic, qi_all, preferred_element_type=F32)
        acc = jnp.zeros((kc, tq), F32)
        for hd in range(IDX_HEADS):
            acc = acc + jnp.maximum(dots[:, hd * tq:(hd + 1) * tq], 0.0) * wiT[hd:hd + 1, :]
        acc = jnp.where(acc == 0.0, 0.0, acc)
        bits = pltpu.bitcast(acc, I32)
        key = jnp.where(bits < 0, bits ^ INT_MAX, bits)
        key = jnp.where(ks + srow <= tpos, key, INT_MIN)
        keys_scr[pl.ds(ks, kc), :] = key
        return carry

    lax.fori_loop(0, nch, score_chunk, 0)

    def count(pred):
        def body(c, acc):
            ks = pl.multiple_of(c * kc, kc)
            m = pred(keys_scr[pl.ds(ks, kc), :], ks + srow).astype(I32)
            return acc + jnp.sum(m.reshape(kc // SUBLANES, SUBLANES, tq), axis=0)
        acc = lax.fori_loop(0, nch, body, jnp.zeros((SUBLANES, tq), I32))
        return jnp.sum(acc, axis=0, keepdims=True)

    def bit_body(it, p):
        cand_p = p | lax.shift_left(jnp.int32(1), 31 - it)
        cand = cand_p ^ INT_MIN
        cnt = count(lambda k, s: k >= cand)
        return jnp.where(cnt >= top_k, cand_p, p)

    p_fin = lax.fori_loop(0, 32, bit_body, jnp.zeros((1, tq), I32))
    v = p_fin ^ INT_MIN

    cnt_gt = count(lambda k, s: k > v)
    cnt_eq = count(lambda k, s: k == v)
    need = top_k - cnt_gt
    pidx_scr[...] = jnp.full((1, tq), INT_MAX, I32)

    @pl.when(jnp.max(cnt_eq - need) > 0)
    def _():
        def ibit(it, p):
            cand = p | lax.shift_left(jnp.int32(1), 11 - it)
            cnt = count(lambda k, s: jnp.logical_and(k == v, s < cand))
            return jnp.where(cnt < need, cand, p)
        pidx_scr[...] = lax.fori_loop(0, 12, ibit, jnp.zeros((1, tq), I32))

    pidx = pidx_scr[...]

    def mask_chunk(c, carry):
        ks = pl.multiple_of(c * kc, kc)
        k = keys_scr[pl.ds(ks, kc), :]
        spos = ks + srow
        sel = jnp.logical_or(k > v, jnp.logical_and(k == v, spos <= pidx))
        sel = jnp.logical_and(sel, spos <= tpos)
        negm_scr[pl.ds(ks, kc), :] = jnp.where(sel, 0.0, NEG_BIG)
        return carry

    lax.fori_loop(0, nch, mask_chunk, 0)

    for n in range(N_KV_HEADS):
        qn = qT_ref[0, n * N_REP * HEAD_DIM:(n + 1) * N_REP * HEAD_DIM, :]
        q_all = jnp.concatenate([qn[g * HEAD_DIM:(g + 1) * HEAD_DIM, :] for g in range(N_REP)], axis=1)

        def att_chunk(c, carry, n=n, q_all=q_all):
            m, l, acc = carry
            ks = pl.multiple_of(c * kc, kc)
            kcn = k_ref[0, n, pl.ds(ks, kc), :]
            s = jnp.dot(kcn, q_all, preferred_element_type=F32)
            w0 = pl.multiple_of(jnp.maximum(BIAS_C - (q0 - ks), 0), LANES)
            negm = negm_scr[pl.ds(ks, kc), :]
            lg = jnp.concatenate(
                [s[:, g * tq:(g + 1) * tq] + pt_ref[n * N_REP + g, pl.ds(w0, kc), :] + negm
                 for g in range(N_REP)], axis=1)
            m_new = jnp.maximum(m, jnp.max(lg, axis=0, keepdims=True))
            alpha = jnp.exp(m - m_new)
            p = jnp.exp(lg - m_new)
            l_new = alpha * l + jnp.sum(p, axis=0, keepdims=True)
            vtc = vT_ref[0, c, n * HEAD_DIM:(n + 1) * HEAD_DIM, :]
            acc_new = alpha * acc + jnp.dot(vtc, p.astype(BF16), preferred_element_type=F32)
            return m_new, l_new, acc_new

        m0 = jnp.full((1, N_REP * tq), NEG_BIG, F32)
        l0 = jnp.zeros((1, N_REP * tq), F32)
        a0 = jnp.zeros((HEAD_DIM, N_REP * tq), F32)
        _, l_f, acc_f = lax.fori_loop(0, nch, att_chunk, (m0, l0, a0))
        o_t = acc_f / l_f
        for g in range(N_REP):
            r0 = (n * N_REP + g) * HEAD_DIM
            oT_scr[r0:r0 + HEAD_DIM, :] = o_t[:, g * tq:(g + 1) * tq]

    o_ref[0] = oT_scr[...].T.astype(BF16)


def _attention(qT, qiT, wiT, k4, vT, ki, ptab, top_k):
    bsz, dq, seq = qT.shape
    tq = ATT_TQ
    kern = functools.partial(_attn_kernel, top_k=top_k)
    return pl.pallas_call(
        kern,
        out_shape=jax.ShapeDtypeStruct((bsz, seq, dq), BF16),
        grid=(bsz, seq // tq),
        in_specs=[
            pl.BlockSpec((1, dq, tq), lambda b, i: (b, 0, i)),
            pl.BlockSpec((1, qiT.shape[1], tq), lambda b, i: (b, 0, i)),
            pl.BlockSpec((1, wiT.shape[1], tq), lambda b, i: (b, 0, i)),
            pl.BlockSpec((1,) + k4.shape[1:], lambda b, i: (b, 0, 0, 0)),
            pl.BlockSpec((1,) + vT.shape[1:], lambda b, i: (b, 0, 0, 0)),
            pl.BlockSpec((1,) + ki.shape[1:], lambda b, i: (b, 0, 0)),
            pl.BlockSpec(ptab.shape, lambda b, i: (0, 0, 0)),
        ],
        out_specs=pl.BlockSpec((1, tq, dq), lambda b, i: (b, i, 0)),
        scratch_shapes=[
            pltpu.VMEM((seq, tq), I32),
            pltpu.VMEM((seq, tq), F32),
            pltpu.VMEM((1, tq), I32),
            pltpu.VMEM((dq, tq), F32),
        ],
        compiler_params=_cparams("parallel", "arbitrary"),
        name="attn_core",
    )(qT, qiT, wiT, k4, vT, ki, ptab)


def _proj_res_kernel(a_ref, x_ref, gate_ref, w_ref, o_ref):
    y = jnp.dot(a_ref[0], w_ref[...], preferred_element_type=F32)
    o_ref[0] = x_ref[0] + gate_ref[0] * y


def _proj_residual(a, x, gate, w, tm=512):
    bsz, seq, d = x.shape
    tm = min(tm, seq)
    return pl.pallas_call(
        _proj_res_kernel,
        out_shape=jax.ShapeDtypeStruct((bsz, seq, d), F32),
        grid=(bsz, seq // tm),
        in_specs=[
            pl.BlockSpec((1, tm, a.shape[2]), lambda b, i: (b, i, 0)),
            pl.BlockSpec((1, tm, d), lambda b, i: (b, i, 0)),
            pl.BlockSpec((1, 1, d), lambda b, i: (b, 0, 0)),
            pl.BlockSpec(w.shape, lambda b, i: (0, 0)),
        ],
        out_specs=pl.BlockSpec((1, tm, d), lambda b, i: (b, i, 0)),
        compiler_params=_cparams("parallel", "parallel"),
        name="proj_residual",
    )(a, x, gate, w)


def _attn_layer(x, g, sh, sc, gate, w_in, q_gain, k_gain, w_out, rel_bias):
    bsz, seq, d = x.shape
    top_k = min(TOPK_MAX, seq // 4)
    dq = N_HEADS * HEAD_DIM
    dk = N_KV_HEADS * HEAD_DIM
    dqi = IDX_HEADS * IDX_DIM
    nmain = dq + 2 * dk + dqi
    w_main = w_in[:, :nmain].astype(BF16)
    w_small = jnp.pad(w_in[:, nmain:], ((0, 0), (0, LANES - (IDX_DIM + IDX_HEADS)))).astype(BF16)
    q, k, v, qi, kiwi = _qkv_proj(x, g, sh, sc, w_main, w_small, q_gain, k_gain)
    ki = kiwi[:, :, :IDX_DIM].astype(BF16)
    wi = kiwi[:, :, IDX_DIM:IDX_DIM + IDX_HEADS] * (IDX_HEADS ** -0.5 * IDX_DIM ** -0.5)
    qT = jnp.swapaxes(q, 1, 2)
    qiT = jnp.swapaxes(qi, 1, 2)
    wiT = jnp.swapaxes(wi, 1, 2)
    vT = jnp.swapaxes(v.reshape(bsz, seq // ATT_KC, ATT_KC, dk), 2, 3)
    k4 = jnp.swapaxes(k.reshape(bsz, seq, N_KV_HEADS, HEAD_DIM), 1, 2)
    ptab = _bias_table(rel_bias)
    attn = _attention(qT, qiT, wiT, k4, vT, ki, ptab, top_k)
    return _proj_residual(attn, x, gate, w_out.astype(BF16))


def _s5_prep_kernel(lre_ref, lim_ref, ls_ref, bre_ref, bim_ref, cre_ref, cim_ref,
                    kmat_ref, bcre_ref, bcim_ref, ccre_ref, ccim_ref, are_ref, aim_ref):
    t_len = SSM_CHUNK
    lre = jnp.minimum(lre_ref[0], -1e-4)
    lim = lim_ref[0]
    step = jnp.exp(ls_ref[0])
    ar = lre * step
    ai = lim * step

    def powers(jv):
        mag = jnp.exp(jv * ar)
        return mag * jnp.cos(jv * ai), mag * jnp.sin(jv * ai)

    lb_re, lb_im = powers(1.0)
    nr = lb_re - 1.0
    ni = lb_im
    den = lre * lre + lim * lim
    cf_re = (nr * lre + ni * lim) / den
    cf_im = (ni * lre - nr * lim) / den
    bre = bre_ref[0]
    bim = bim_ref[0]
    bb_re = cf_re * bre - cf_im * bim
    bb_im = cf_re * bim + cf_im * bre
    cre = cre_ref[0]
    cim = cim_ref[0]
    nst = lre.shape[-1]
    jv = lax.broadcasted_iota(I32, (t_len, 1, nst), 0).astype(F32)
    pj_re, pj_im = powers(jv)
    a_re = (cre[None] * pj_re - cim[None] * pj_im).reshape(t_len * SSM_GROUP, nst)
    a_im = (cre[None] * pj_im + cim[None] * pj_re).reshape(t_len * SSM_GROUP, nst)
    dn = (((1,), (1,)), ((), ()))
    kmat_ref[0] = (lax.dot_general(a_re, bb_re, dn, preferred_element_type=F32, precision=HIGHEST)
                   - lax.dot_general(a_im, bb_im, dn, preferred_element_type=F32, precision=HIGHEST))
    pr_re, pr_im = powers((t_len - 1.0) - jv)
    bcre_ref[0] = (bb_re[None] * pr_re - bb_im[None] * pr_im).reshape(t_len * SSM_GROUP, nst)
    bcim_ref[0] = (bb_re[None] * pr_im + bb_im[None] * pr_re).reshape(t_len * SSM_GROUP, nst)
    pn_re, pn_im = powers(jv + 1.0)
    ccre_ref[0] = (cre[None] * pn_re - cim[None] * pn_im).reshape(t_len * SSM_GROUP, nst)
    ccim_ref[0] = (cre[None] * pn_im + cim[None] * pn_re).reshape(t_len * SSM_GROUP, nst)
    at_re, at_im = powers(float(t_len))
    are_ref[0] = at_re
    aim_ref[0] = at_im


def _s5_prep(lam_re, lam_im, log_step, b_re, b_im, c_re, c_im):
    ng, nst = lam_re.shape
    tc = SSM_CHUNK * SSM_GROUP
    vec = pl.BlockSpec((1, 1, nst), lambda gi: (gi, 0, 0))
    mat = pl.BlockSpec((1, SSM_GROUP, nst), lambda gi: (gi, 0, 0))
    big = pl.BlockSpec((1, tc, nst), lambda gi: (gi, 0, 0))
    return pl.pallas_call(
        _s5_prep_kernel,
        out_shape=[
            jax.ShapeDtypeStruct((ng, tc, SSM_GROUP), F32),
            jax.ShapeDtypeStruct((ng, tc, nst), F32),
            jax.ShapeDtypeStruct((ng, tc, nst), F32),
            jax.ShapeDtypeStruct((ng, tc, nst), F32),
            jax.ShapeDtypeStruct((ng, tc, nst), F32),
            jax.ShapeDtypeStruct((ng, 1, nst), F32),
            jax.ShapeDtypeStruct((ng, 1, nst), F32),
        ],
        grid=(ng,),
        in_specs=[vec, vec, pl.BlockSpec((1, 1, 1), lambda gi: (gi, 0, 0)), mat, mat, mat, mat],
        out_specs=[pl.BlockSpec((1, tc, SSM_GROUP), lambda gi: (gi, 0, 0)), big, big, big, big, vec, vec],
        compiler_params=_cparams("parallel"),
        name="s5_prep",
    )(lam_re.reshape(ng, 1, nst), lam_im.reshape(ng, 1, nst), log_step.reshape(ng, 1, 1),
      jnp.swapaxes(b_re, 1, 2), jnp.swapaxes(b_im, 1, 2), c_re, c_im)


def _s5_pre_kernel(x_ref, g_ref, sh_ref, sc_ref, o_ref):
    h = _norm_mod(x_ref[0], g_ref[...], sh_ref[0], sc_ref[0]).astype(BF16)
    for q in range(o_ref.shape[0]):
        o_ref[q] = h[:, q * LANES:(q + 1) * LANES]


def _s5_pre(x, g, sh, sc, tm=512):
    bsz, seq, d = x.shape
    tm = min(tm, seq)
    nt = seq // tm
    nq = d // LANES
    mod_spec = pl.BlockSpec((1, 1, d), lambda b, i: (b, 0, 0))
    return pl.pallas_call(
        _s5_pre_kernel,
        out_shape=jax.ShapeDtypeStruct((nq, bsz * seq, LANES), BF16),
        grid=(bsz, nt),
        in_specs=[pl.BlockSpec((1, tm, d), lambda b, i: (b, i, 0)),
                  pl.BlockSpec((1, d), lambda b, i: (0, 0)), mod_spec, mod_spec],
        out_specs=pl.BlockSpec((nq, tm, LANES), lambda b, i: (0, b * nt + i, 0)),
        compiler_params=_cparams("parallel", "parallel"),
        name="s5_pre",
    )(x, g, sh, sc)


def _s5_scan_kernel(x_ref, bc_ref, mt_ref, cc_ref, are_ref, aim_ref, y_ref, re_scr, im_scr):
    x = x_ref[0]
    nrow = x.shape[0]
    pad = re_scr.shape[0] - nrow
    v = jnp.dot(x, bc_ref[0], preferred_element_type=F32)
    nh = v.shape[1] // 2
    s_re = v[:, :nh]
    s_im = v[:, nh:]
    a_re = are_ref[0]
    a_im = aim_ref[0]
    re_scr[0:pad, :] = jnp.zeros((pad, nh), F32)
    im_scr[0:pad, :] = jnp.zeros((pad, nh), F32)

    def shifted(scr, val, dist):
        scr[pad:pad + nrow, :] = val
        return scr[pad - dist:pad - dist + nrow, :]

    dist = 1
    while dist < nrow:
        sh_re = shifted(re_scr, s_re, dist)
        sh_im = shifted(im_scr, s_im, dist)
        s_re, s_im = (s_re + a_re * sh_re - a_im * sh_im, s_im + a_re * sh_im + a_im * sh_re)
        a_re, a_im = (a_re * a_re - a_im * a_im, 2.0 * a_re * a_im)
        dist *= 2
    sp = jnp.concatenate([shifted(re_scr, s_re, 1), shifted(im_scr, s_im, 1)], axis=1).astype(BF16)
    y_ref[0] = (jnp.dot(x, mt_ref[0], preferred_element_type=F32)
                + jnp.dot(sp, cc_ref[0], preferred_element_type=F32))


def _s5_scan(hq, bcq, mtq, ccq, a_re, a_im, bsz):
    nq, nrows, width = hq.shape
    nj = nrows // bsz
    nh = a_re.shape[-1]
    pad = max(nj // 2, SUBLANES)
    tile = pl.BlockSpec((1, nj, width), lambda q, b: (q, b, 0))
    wspec = lambda a: pl.BlockSpec((1,) + a.shape[1:], lambda q, b: (q, 0, 0))
    return pl.pallas_call(
        _s5_scan_kernel,
        out_shape=jax.ShapeDtypeStruct((nq, nrows, width), F32),
        grid=(nq, bsz),
        in_specs=[tile, wspec(bcq), wspec(mtq), wspec(ccq), wspec(a_re), wspec(a_im)],
        out_specs=tile,
        scratch_shapes=[pltpu.VMEM((pad + nj, nh), F32), pltpu.VMEM((pad + nj, nh), F32)],
        compiler_params=_cparams("parallel", "parallel"),
        name="s5_scan",
    )(hq, bcq, mtq, ccq, a_re, a_im)


def _s5_post_kernel(x_ref, y_ref, g_ref, sh_ref, sc_ref, gate_ref, dsk_ref, w_ref, o_ref, *, d):
    x = x_ref[0]
    h = _norm_mod(x, g_ref[...], sh_ref[0], sc_ref[0])
    y = jnp.concatenate([y_ref[q] for q in range(y_ref.shape[0])], axis=1)
    yy = y + dsk_ref[...] * h
    gl = jax.nn.gelu(yy).astype(BF16)
    z = jnp.dot(gl, w_ref[...], preferred_element_type=F32)
    o_ref[0] = x + gate_ref[0] * (z[:, :d] * jax.nn.sigmoid(z[:, d:]))


def _s5_post(x, yq, g, sh, sc, gate, d_skip, w_glu, tm=512):
    bsz, seq, d = x.shape
    tm = min(tm, seq)
    nt = seq // tm
    nq = d // LANES
    kern = functools.partial(_s5_post_kernel, d=d)
    mod_spec = pl.BlockSpec((1, 1, d), lambda b, i: (b, 0, 0))
    tile = pl.BlockSpec((1, tm, d), lambda b, i: (b, i, 0))
    return pl.pallas_call(
        kern,
        out_shape=jax.ShapeDtypeStruct((bsz, seq, d), F32),
        grid=(bsz, nt),
        in_specs=[tile, pl.BlockSpec((nq, tm, LANES), lambda b, i: (0, b * nt + i, 0)),
                  pl.BlockSpec((1, d), lambda b, i: (0, 0)), mod_spec, mod_spec, mod_spec,
                  pl.BlockSpec((1, d), lambda b, i: (0, 0)),
                  pl.BlockSpec(w_glu.shape, lambda b, i: (0, 0))],
        out_specs=tile,
        compiler_params=_cparams("parallel", "parallel"),
        name="s5_post",
    )(x, yq, g, sh, sc, gate, d_skip, w_glu)


def _s5_layer(x, g, sh, sc, gate, lam_re, lam_im, log_step, b_re, b_im, c_re, c_im, d_skip, w_glu):
    bsz, seq, d = x.shape
    ng, nst = lam_re.shape
    t_len, cg = SSM_CHUNK, SSM_GROUP
    nq = d // LANES
    gq = LANES // cg
    nj = seq // t_len
    kmat, bc_re, bc_im, cc_re, cc_im, a_re, a_im = _s5_prep(lam_re, lam_im, log_step, b_re, b_im, c_re, c_im)
    eye = jnp.eye(gq, dtype=F32)

    def state_in(bc):
        return jnp.einsum('qgtcp,gh->qtgchp', bc.reshape(nq, gq, t_len, cg, nst), eye)

    bcq = jnp.stack([state_in(bc_re), state_in(bc_im)], axis=4)
    bcq = bcq.reshape(nq, t_len * LANES, 2 * gq * nst).astype(BF16)
    tt = np.arange(t_len)
    lag = tt[:, None] - tt[None, :]
    k4 = kmat.reshape(ng, t_len, cg, cg)[:, np.maximum(lag, 0)]
    k4 = jnp.where(jnp.asarray(lag >= 0)[None, :, :, None, None], k4, 0.0)
    mtq = jnp.einsum('qgtscd,gh->qshdtgc', k4.reshape(nq, gq, t_len, t_len, cg, cg), eye)
    mtq = mtq.reshape(nq, t_len * LANES, t_len * LANES).astype(BF16)

    def state_out(cc):
        return jnp.einsum('qgtcp,gh->qhptgc', cc.reshape(nq, gq, t_len, cg, nst), eye)

    ccq = jnp.stack([state_out(cc_re), -state_out(cc_im)], axis=1)
    ccq = ccq.reshape(nq, 2 * gq * nst, t_len * LANES).astype(BF16)
    aq_re = a_re.reshape(nq, 1, gq * nst)
    aq_im = a_im.reshape(nq, 1, gq * nst)

    hq = _s5_pre(x, g, sh, sc).reshape(nq, bsz * nj, t_len * LANES)
    yq = _s5_scan(hq, bcq, mtq, ccq, aq_re, aq_im, bsz).reshape(nq, bsz * seq, LANES)
    return _s5_post(x, yq, g, sh, sc, gate, d_skip.reshape(1, d).astype(F32), w_glu.astype(BF16))


def kernel(x, c, ada_w, ada_b, norm_g, conv_w_in, conv_w, conv_w_out, attn_w_in, attn_q_gain, attn_k_gain, attn_w_out, rel_bias, ssm_lambda_re, ssm_lambda_im, ssm_log_step, ssm_b_re, ssm_b_im, ssm_c_re, ssm_c_im, ssm_d, ssm_w_glu, ffn_w_gu, ffn_w_down, moe_router_w, moe_router_b, moe_w_gu, moe_w_down):
    bsz, seq, d = x.shape
    depth = ada_w.shape[0]
    mod = _ada_mod(c, ada_w, ada_b).reshape(depth, bsz, 6, 1, d)
    for i in range(depth):
        sh1, sc1, g1, sh2, sc2, g2 = (mod[i, :, r] for r in range(6))
        gn1 = norm_g[i, 0].reshape(1, d)
        gn2 = norm_g[i, 1].reshape(1, d)
        j = i // N_MIXERS
        if i % N_MIXERS == 0:
            x = _conv_layer(x, gn1, sh1, sc1, g1, conv_w_in[j].astype(BF16), conv_w[j],
                            conv_w_out[j].astype(BF16))
        elif i % N_MIXERS == 1:
            x = _attn_layer(x, gn1, sh1, sc1, g1, attn_w_in[j], attn_q_gain[j], attn_k_gain[j],
                            attn_w_out[j], rel_bias)
        else:
            x = _s5_layer(x, gn1, sh1, sc1, g1, ssm_lambda_re[j], ssm_lambda_im[j], ssm_log_step[j],
                          ssm_b_re[j], ssm_b_im[j], ssm_c_re[j], ssm_c_im[j], ssm_d[j], ssm_w_glu[j])
        if i % 2 == 0:
            x = _ffn_layer(x, gn2, sh2, sc2, g2, ffn_w_gu[i // 2].astype(BF16), ffn_w_down[i // 2].astype(BF16))
        else:
            x = _moe_layer(x, gn2, sh2, sc2, g2, moe_router_w[i // 2], moe_router_b[i // 2],
                           moe_w_gu[i // 2].astype(BF16), moe_w_down[i // 2].astype(BF16))
    return x
```

```python
import functools
import math

import numpy as np
import jax
import jax.numpy as jnp
from jax import lax
from jax.experimental import pallas as pl
from jax.experimental.pallas import tpu as pltpu

F32 = jnp.float32
BF16 = jnp.bfloat16
I32 = jnp.int32
HIGHEST = lax.Precision.HIGHEST

DEPTH = 4
N_MIXERS = 3
EPS = 1e-6
CONV_WIDTH = 3
N_HEADS = 16
N_KV_HEADS = 4
N_REP = N_HEADS // N_KV_HEADS
HEAD_DIM = 64
IDX_HEADS = 8
IDX_DIM = 64
TOPK_MAX = 256
REL_BUCKETS = 32
REL_MAX_DIST = 128
SSM_GROUP = 16
SSM_STATE = 64
N_EXPERTS = 8
TOP_K_EXPERTS = 2

VMEM_LIMIT_BYTES = 56 * 1024 * 1024
LANES = 128
SUBLANES = 8

INT_MIN = -(2 ** 31)
INT_MAX = 2 ** 31 - 1
NEG_BIG = -1e30

ATT_TQ = 128
ATT_KC = 256
BIAS_C = 384
BIAS_W = BIAS_C + ATT_KC

SSM_CHUNK = 8


def _cparams(*sem):
    return pltpu.CompilerParams(dimension_semantics=sem, vmem_limit_bytes=VMEM_LIMIT_BYTES)


def _norm_mod(x, g, shift, scale):
    ms = jnp.mean(x * x, axis=-1, keepdims=True)
    y = x * lax.rsqrt(ms + EPS)
    return (y * g) * (1.0 + scale) + shift


def _silu(x):
    return x * jax.nn.sigmoid(x)


def _ada_kernel(c_ref, w_ref, b_ref, o_ref):
    c = c_ref[...]
    cond = _silu(c)
    o_ref[0] = jnp.dot(cond, w_ref[0], preferred_element_type=F32, precision=HIGHEST) + b_ref[0]


def _ada_mod(c, ada_w, ada_b):
    depth, d, d6 = ada_w.shape
    bsz = c.shape[0]
    tn = d
    return pl.pallas_call(
        _ada_kernel,
        out_shape=jax.ShapeDtypeStruct((depth, bsz, d6), F32),
        grid=(depth, d6 // tn),
        in_specs=[
            pl.BlockSpec((bsz, d), lambda i, j: (0, 0)),
            pl.BlockSpec((1, d, tn), lambda i, j: (i, 0, j)),
            pl.BlockSpec((1, 1, tn), lambda i, j: (i, 0, j)),
        ],
        out_specs=pl.BlockSpec((1, bsz, tn), lambda i, j: (i, 0, j)),
        compiler_params=_cparams("parallel", "parallel"),
        name="ada_mod",
    )(c, ada_w, ada_b.reshape(depth, 1, d6))


def _conv_kernel(x_ref, xh_ref, g_ref, sh_ref, sc_ref, gate_ref, win_ref, wc_ref, wout_ref,
                 o_ref, u_scr, *, tm, d):
    i = pl.program_id(1)
    g = g_ref[...]
    sh = sh_ref[0]
    sc = sc_ref[0]
    x = x_ref[0]
    h = _norm_mod(x, g, sh, sc).astype(BF16)
    z = jnp.dot(h, win_ref[...], preferred_element_type=F32)
    b_gate = z[:, :d]
    u = z[:, d:2 * d] * z[:, 2 * d:]
    hh = _norm_mod(xh_ref[0], g, sh, sc).astype(BF16)
    zh = jnp.dot(hh, win_ref[:, d:], preferred_element_type=F32)
    uh = zh[:, :d] * zh[:, d:]
    uh = jnp.where(i > 0, uh, 0.0)
    u_scr[0:SUBLANES, :] = uh
    u_scr[SUBLANES:SUBLANES + tm, :] = u
    wc = wc_ref[...]
    conv = (wc[0:1, :] * u_scr[SUBLANES - 2:SUBLANES - 2 + tm, :]
            + wc[1:2, :] * u_scr[SUBLANES - 1:SUBLANES - 1 + tm, :]
            + wc[2:3, :] * u)
    y = jnp.dot((b_gate * conv).astype(BF16), wout_ref[...], preferred_element_type=F32)
    o_ref[0] = x + gate_ref[0] * y


def _conv_layer(x, g, sh, sc, gate, w_in, w_conv, w_out, tm=512):
    bsz, seq, d = x.shape
    tm = min(tm, seq)
    nt = seq // tm
    hb = tm // SUBLANES
    kern = functools.partial(_conv_kernel, tm=tm, d=d)
    mod_spec = pl.BlockSpec((1, 1, d), lambda b, i: (b, 0, 0))
    return pl.pallas_call(
        kern,
        out_shape=jax.ShapeDtypeStruct((bsz, seq, d), F32),
        grid=(bsz, nt),
        in_specs=[
            pl.BlockSpec((1, tm, d), lambda b, i: (b, i, 0)),
            pl.BlockSpec((1, SUBLANES, d), lambda b, i: (b, jnp.maximum(i * hb - 1, 0), 0)),
            pl.BlockSpec((1, d), lambda b, i: (0, 0)),
            mod_spec, mod_spec, mod_spec,
            pl.BlockSpec((d, 3 * d), lambda b, i: (0, 0)),
            pl.BlockSpec((CONV_WIDTH, d), lambda b, i: (0, 0)),
            pl.BlockSpec((d, d), lambda b, i: (0, 0)),
        ],
        out_specs=pl.BlockSpec((1, tm, d), lambda b, i: (b, i, 0)),
        scratch_shapes=[pltpu.VMEM((tm + SUBLANES, d), F32)],
        compiler_params=_cparams("parallel", "parallel"),
        name="conv_mixer",
    )(x, x, g, sh, sc, gate, w_in, w_conv, w_out)


def _ffn_kernel(x_ref, g_ref, sh_ref, sc_ref, gate_ref, wgu_ref, wd_ref, o_ref, *, dff, nchunk):
    x = x_ref[0]
    h = _norm_mod(x, g_ref[...], sh_ref[0], sc_ref[0]).astype(BF16)
    cols = dff // nchunk
    acc = jnp.zeros(x.shape, F32)
    for c in range(nchunk):
        gg = jnp.dot(h, wgu_ref[:, c * cols:(c + 1) * cols], preferred_element_type=F32)
        uu = jnp.dot(h, wgu_ref[:, dff + c * cols:dff + (c + 1) * cols], preferred_element_type=F32)
        a = (_silu(gg) * uu).astype(BF16)
        acc = acc + jnp.dot(a, wd_ref[c * cols:(c + 1) * cols, :], preferred_element_type=F32)
    o_ref[0] = x + gate_ref[0] * acc


def _ffn_layer(x, g, sh, sc, gate, w_gu, w_down, tm=512):
    bsz, seq, d = x.shape
    dff = w_down.shape[0]
    tm = min(tm, seq)
    kern = functools.partial(_ffn_kernel, dff=dff, nchunk=2)
    mod_spec = pl.BlockSpec((1, 1, d), lambda b, i: (b, 0, 0))
    return pl.pallas_call(
        kern,
        out_shape=jax.ShapeDtypeStruct((bsz, seq, d), F32),
        grid=(bsz, seq // tm),
        in_specs=[
            pl.BlockSpec((1, tm, d), lambda b, i: (b, i, 0)),
            pl.BlockSpec((1, d), lambda b, i: (0, 0)),
            mod_spec, mod_spec, mod_spec,
            pl.BlockSpec((d, 2 * dff), lambda b, i: (0, 0), pipeline_mode=pl.Buffered(1)),
            pl.BlockSpec((dff, d), lambda b, i: (0, 0), pipeline_mode=pl.Buffered(1)),
        ],
        out_specs=pl.BlockSpec((1, tm, d), lambda b, i: (b, i, 0)),
        compiler_params=_cparams("parallel", "parallel"),
        name="ffn_dense",
    )(x, g, sh, sc, gate, w_gu, w_down)


MOE_SB = 512
MOE_RT = 512
MOE_ALIGN = 16
MOE_WIN = 256
MOE_CAP = MOE_WIN - MOE_ALIGN


def _moe_router_kernel(x_ref, g_ref, sh_ref, sc_ref, rw_ref, rb_ref, h_ref, meta_ref):
    hf = _norm_mod(x_ref[...], g_ref[...], sh_ref[0], sc_ref[0])
    h_ref[...] = hf.astype(BF16)
    logits = jnp.dot(hf, rw_ref[...], preferred_element_type=F32, precision=HIGHEST) + rb_ref[...]
    mx = jnp.max(logits, axis=-1, keepdims=True)
    ex = jnp.exp(logits - mx)
    probs = ex / jnp.sum(ex, axis=-1, keepdims=True)
    lane = lax.broadcasted_iota(I32, probs.shape, 1)
    m1 = jnp.max(probs, axis=-1, keepdims=True)
    i1 = jnp.min(jnp.where(probs == m1, lane, LANES), axis=-1, keepdims=True)
    rest = jnp.where(lane == i1, -1.0, probs)
    m2 = jnp.max(rest, axis=-1, keepdims=True)
    i2 = jnp.min(jnp.where(rest == m2, lane, LANES), axis=-1, keepdims=True)
    den = m1 + m2
    gates = jnp.where(lane == i1, m1 / den, 0.0) + jnp.where(lane == i2, m2 / den, 0.0)
    chosen = jnp.where(jnp.logical_or(lane == i1, lane == i2), 1.0, 0.0)
    meta_ref[...] = gates + pltpu.roll(chosen, N_EXPERTS, 1)


def _moe_router(xt, g, sh, sc, rw_pad, rb_pad, tpb, tm):
    n, d = xt.shape
    mod_spec = pl.BlockSpec((1, 1, d), lambda i: (i // tpb, 0, 0))
    return pl.pallas_call(
        _moe_router_kernel,
        out_shape=[jax.ShapeDtypeStruct((n, d), BF16), jax.ShapeDtypeStruct((n, LANES), F32)],
        grid=(n // tm,),
        in_specs=[pl.BlockSpec((tm, d), lambda i: (i, 0)),
                  pl.BlockSpec((1, d), lambda i: (0, 0)), mod_spec, mod_spec,
                  pl.BlockSpec((d, LANES), lambda i: (0, 0)),
                  pl.BlockSpec((1, LANES), lambda i: (0, 0))],
        out_specs=[pl.BlockSpec((tm, d), lambda i: (i, 0)), pl.BlockSpec((tm, LANES), lambda i: (i, 0))],
        compiler_params=_cparams("parallel"),
        name="moe_router",
    )(xt, g, sh, sc, rw_pad, rb_pad)


def _window(start, count, w):
    s = start + w * MOE_CAP
    n = jnp.minimum(count - w * MOE_CAP, MOE_CAP)
    a = pl.multiple_of((s // MOE_ALIGN) * MOE_ALIGN, MOE_ALIGN)
    return s, n, a


def _moe_dispatch_kernel(start_ref, cnt_ref, h_ref, pos_ref, xs_init_ref, xs_ref,
                         buf, carry, sems, pending):
    del xs_init_ref
    b = pl.program_id(0)
    nb = pl.num_programs(0)
    ne = buf.shape[0]

    @pl.when(b == 0)
    def _():
        carry[...] = jnp.zeros(carry.shape, BF16)
        for e in range(ne):
            pending[e] = 0

    h = h_ref[...]
    riota = lax.broadcasted_iota(I32, (MOE_WIN, h.shape[0]), 0)

    def out_copy(e, a):
        return pltpu.make_async_copy(buf.at[e], xs_ref.at[pl.ds(a, MOE_WIN)], sems.at[e])

    for e in range(ne):
        posrow = pos_ref[0, e:e + 1, :]
        nwin = (cnt_ref[b, e] + MOE_CAP - 1) // MOE_CAP

        def wbody(w, carry_unused, e=e, posrow=posrow):
            s, n, a = _window(start_ref[b, e], cnt_ref[b, e], w)

            @pl.when(pending[e] == 1)
            def _():
                out_copy(e, 0).wait()

            hit = jnp.logical_and(posrow - a == riota,
                                  jnp.logical_and(posrow >= s, posrow < s + n))
            onehot = jnp.where(hit, 1.0, 0.0).astype(BF16)
            buf[e] = jnp.dot(onehot, h, preferred_element_type=F32).astype(BF16)
            buf[e, 0:MOE_ALIGN, :] = buf[e, 0:MOE_ALIGN, :] + carry[e]
            c0 = pl.multiple_of(((s + n) // MOE_ALIGN) * MOE_ALIGN - a, MOE_ALIGN)
            carry[e] = buf[e, pl.ds(c0, MOE_ALIGN), :]
            out_copy(e, a).start()
            pending[e] = 1
            return carry_unused

        lax.fori_loop(0, nwin, wbody, 0)

    @pl.when(b == nb - 1)
    def _():
        for e in range(ne):
            @pl.when(pending[e] == 1)
            def _():
                out_copy(e, 0).wait()


def _moe_dispatch(start, cnt, h, pos_t, ncap):
    n, d = h.shape
    nb, ne, sb = pos_t.shape
    xs_init = jnp.zeros((ncap, d), BF16)
    return pl.pallas_call(
        _moe_dispatch_kernel,
        out_shape=jax.ShapeDtypeStruct((ncap, d), BF16),
        grid_spec=pltpu.PrefetchScalarGridSpec(
            num_scalar_prefetch=2,
            grid=(nb,),
            in_specs=[pl.BlockSpec((sb, d), lambda b, *_: (b, 0)),
                      pl.BlockSpec((1, ne, sb), lambda b, *_: (b, 0, 0)),
                      pl.BlockSpec(memory_space=pl.ANY)],
            out_specs=pl.BlockSpec(memory_space=pl.ANY),
            scratch_shapes=[pltpu.VMEM((ne, MOE_WIN, d), BF16),
                            pltpu.VMEM((ne, MOE_ALIGN, d), BF16),
                            pltpu.SemaphoreType.DMA((ne,)),
                            pltpu.SMEM((ne,), I32)],
        ),
        input_output_aliases={4: 0},
        compiler_params=_cparams("arbitrary"),
        name="moe_dispatch",
    )(start, cnt, h, pos_t, xs_init)


def _moe_ffn_kernel(blk_ref, exp_ref, nt_ref, xs_ref, wgu_ref, wd_ref, ys_init_ref, ys_ref, *, dff, nchunk):
    del blk_ref, exp_ref, ys_init_ref
    k = pl.program_id(0)

    @pl.when(k < nt_ref[0])
    def _():
        x = xs_ref[...]
        cols = dff // nchunk
        acc = jnp.zeros(x.shape, F32)
        for c in range(nchunk):
            gg = jnp.dot(x, wgu_ref[0, :, c * cols:(c + 1) * cols], preferred_element_type=F32)
            uu = jnp.dot(x, wgu_ref[0, :, dff + c * cols:dff + (c + 1) * cols], preferred_element_type=F32)
            a = (_silu(gg) * uu).astype(BF16)
            acc = acc + jnp.dot(a, wd_ref[0, c * cols:(c + 1) * cols, :], preferred_element_type=F32)
        ys_ref[...] = acc.astype(BF16)


def _moe_ffn(tile_blk, tile_exp, ntiles, xs, w_gu, w_down):
    ncap, d = xs.shape
    ne, dff, _ = w_down.shape
    kern = functools.partial(_moe_ffn_kernel, dff=dff, nchunk=2)
    ys_init = jnp.zeros((ncap, d), BF16)
    return pl.pallas_call(
        kern,
        out_shape=jax.ShapeDtypeStruct((ncap, d), BF16),
        grid_spec=pltpu.PrefetchScalarGridSpec(
            num_scalar_prefetch=3,
            grid=(ncap // MOE_RT,),
            in_specs=[pl.BlockSpec((MOE_RT, d), lambda k, blk, exp, nt: (blk[k], 0)),
                      pl.BlockSpec((1, d, 2 * dff), lambda k, blk, exp, nt: (exp[k], 0, 0)),
                      pl.BlockSpec((1, dff, d), lambda k, blk, exp, nt: (exp[k], 0, 0)),
                      pl.BlockSpec(memory_space=pl.ANY)],
            out_specs=pl.BlockSpec((MOE_RT, d), lambda k, blk, exp, nt: (blk[k], 0)),
        ),
        input_output_aliases={6: 0},
        compiler_params=_cparams("arbitrary"),
        name="moe_ffn",
    )(tile_blk, tile_exp, ntiles, xs, w_gu, w_down, ys_init)


def _moe_combine_kernel(start_ref, cnt_ref, x_ref, gate_ref, pos_ref, gts_ref, ys_ref, o_ref,
                        buf, sems, acc_scr):
    b = pl.program_id(0)
    ne = buf.shape[0]
    sb = x_ref.shape[0]

    def in_copy(e, a):
        return pltpu.make_async_copy(ys_ref.at[pl.ds(a, MOE_WIN)], buf.at[e], sems.at[e])

    for e in range(ne):
        @pl.when(cnt_ref[b, e] > 0)
        def _():
            _, _, a = _window(start_ref[b, e], cnt_ref[b, e], 0)
            in_copy(e, a).start()

    acc_scr[...] = jnp.zeros(acc_scr.shape, F32)
    liota = lax.broadcasted_iota(I32, (sb, MOE_WIN), 1)
    for e in range(ne):
        poscol = pos_ref[:, e:e + 1]
        gcol = gts_ref[:, e:e + 1]
        nwin = (cnt_ref[b, e] + MOE_CAP - 1) // MOE_CAP

        def wbody(w, carry_unused, e=e, poscol=poscol, gcol=gcol):
            s, n, a = _window(start_ref[b, e], cnt_ref[b, e], w)

            @pl.when(w > 0)
            def _():
                in_copy(e, a).start()

            in_copy(e, a).wait()
            hit = jnp.logical_and(poscol - a == liota,
                                  jnp.logical_and(poscol >= s, poscol < s + n))
            onehot = jnp.where(hit, 1.0, 0.0).astype(BF16)
            acc_scr[...] += gcol * jnp.dot(onehot, buf[e], preferred_element_type=F32)
            return carry_unused

        lax.fori_loop(0, nwin, wbody, 0)

    o_ref[...] = x_ref[...] + gate_ref[0] * acc_scr[...]


def _moe_combine(start, cnt, xt, gate, pos_n, gates_n, ys, tpb):
    n, d = xt.shape
    nb, ne = cnt.shape
    sb = n // nb
    return pl.pallas_call(
        _moe_combine_kernel,
        out_shape=jax.ShapeDtypeStruct((n, d), F32),
        grid_spec=pltpu.PrefetchScalarGridSpec(
            num_scalar_prefetch=2,
            grid=(nb,),
            in_specs=[pl.BlockSpec((sb, d), lambda b, *_: (b, 0)),
                      pl.BlockSpec((1, 1, d), lambda b, *_: (b // tpb, 0, 0)),
                      pl.BlockSpec((sb, ne), lambda b, *_: (b, 0)),
                      pl.BlockSpec((sb, ne), lambda b, *_: (b, 0)),
                      pl.BlockSpec(memory_space=pl.ANY)],
            out_specs=pl.BlockSpec((sb, d), lambda b, *_: (b, 0)),
            scratch_shapes=[pltpu.VMEM((ne, MOE_WIN, d), BF16),
                            pltpu.SemaphoreType.DMA((ne,)),
                            pltpu.VMEM((sb, d), F32)],
        ),
        compiler_params=_cparams("arbitrary"),
        name="moe_combine",
    )(start, cnt, xt, gate, pos_n, gates_n, ys)


def _moe_layer(x, g, sh, sc, gate, router_w, router_b, w_gu, w_down):
    bsz, seq, d = x.shape
    ne, dff, _ = w_down.shape
    n = bsz * seq
    sb = min(MOE_SB, seq)
    tpb = seq // sb
    nb = n // sb
    rt = MOE_RT
    xt = x.reshape(n, d)
    rw_pad = jnp.pad(router_w.astype(F32), ((0, 0), (0, LANES - ne)))
    rb_pad = jnp.pad(router_b.astype(F32).reshape(1, ne), ((0, 0), (0, LANES - ne)), constant_values=NEG_BIG)
    h, meta = _moe_router(xt, g, sh, sc, rw_pad, rb_pad, tpb, sb)
    gates = meta[:, :ne]
    sel = (meta[:, ne:2 * ne] > 0.5).astype(I32)

    selb = sel.reshape(nb, sb, ne)
    cnt = jnp.sum(selb, axis=1)
    rank = jnp.cumsum(selb, axis=1) - selb
    total = jnp.sum(cnt, axis=0)
    region = ((total + MOE_WIN + rt - 1) // rt) * rt
    off = jnp.cumsum(region) - region
    start = (off[None, :] + jnp.cumsum(cnt, axis=0) - cnt).astype(I32)
    pos = jnp.where(selb > 0, start[:, None, :] + rank, -1).astype(I32)
    ncap = 2 * n + ne * (MOE_WIN + rt)
    tiles_e = (total + rt - 1) // rt
    tcum = jnp.cumsum(tiles_e)
    ntiles = tcum[-1]
    kk = jnp.minimum(jnp.arange(ncap // rt), ntiles - 1)
    tile_exp = jnp.searchsorted(tcum, kk, side='right').astype(I32)
    tile_blk = (off[tile_exp] // rt + kk - (tcum - tiles_e)[tile_exp]).astype(I32)

    xs = _moe_dispatch(start, cnt.astype(I32), h, jnp.swapaxes(pos, 1, 2), ncap)
    ys = _moe_ffn(tile_blk, tile_exp, ntiles.reshape(1).astype(I32), xs, w_gu, w_down)
    out = _moe_combine(start, cnt.astype(I32), xt, gate, pos.reshape(n, ne), gates, ys, tpb)
    return out.reshape(bsz, seq, d)


def _head_norm(q, hsum_ref, hexp_ref, gain, scale):
    ms = jnp.dot((q * q).astype(BF16), hsum_ref[...], preferred_element_type=F32)
    r = lax.rsqrt(ms + EPS)
    r_hi = r.astype(BF16)
    r_lo = (r - r_hi.astype(F32)).astype(BF16)
    rexp = (jnp.dot(r_hi, hexp_ref[...], preferred_element_type=F32)
            + jnp.dot(r_lo, hexp_ref[...], preferred_element_type=F32))
    return q * rexp * (gain * scale)


def _qkv_kernel(x_ref, g_ref, sh_ref, sc_ref, wm_ref, ws_ref, qg_ref, kg_ref,
                hsq_ref, heq_ref, hsk_ref, hek_ref,
                q_ref, k_ref, v_ref, qi_ref, kiwi_ref, *, dq, dk, dqi):
    x = x_ref[0]
    h = _norm_mod(x, g_ref[...], sh_ref[0], sc_ref[0]).astype(BF16)
    z = jnp.dot(h, wm_ref[...], preferred_element_type=F32)
    q = z[:, :dq]
    k = z[:, dq:dq + dk]
    v = z[:, dq + dk:dq + 2 * dk]
    qi = z[:, dq + 2 * dk:dq + 2 * dk + dqi]
    q_ref[0] = _head_norm(q, hsq_ref, heq_ref, qg_ref[...], HEAD_DIM ** -0.5).astype(BF16)
    k_ref[0] = _head_norm(k, hsk_ref, hek_ref, kg_ref[...], 1.0).astype(BF16)
    v_ref[0] = v.astype(BF16)
    qi_ref[0] = qi.astype(BF16)
    kiwi_ref[0] = jnp.dot(h, ws_ref[...], preferred_element_type=F32)


def _head_indicators(nheads):
    hs = np.zeros((nheads * HEAD_DIM, LANES), np.float32)
    he = np.zeros((LANES, nheads * HEAD_DIM), np.float32)
    for hd in range(nheads):
        hs[hd * HEAD_DIM:(hd + 1) * HEAD_DIM, hd] = 1.0 / HEAD_DIM
        he[hd, hd * HEAD_DIM:(hd + 1) * HEAD_DIM] = 1.0
    return jnp.asarray(hs, BF16), jnp.asarray(he, BF16)


def _qkv_proj(x, g, sh, sc, w_main, w_small, q_gain, k_gain, tm=512):
    bsz, seq, d = x.shape
    tm = min(tm, seq)
    dq = N_HEADS * HEAD_DIM
    dk = N_KV_HEADS * HEAD_DIM
    dqi = IDX_HEADS * IDX_DIM
    hsq, heq = _head_indicators(N_HEADS)
    hsk, hek = _head_indicators(N_KV_HEADS)
    qg = jnp.tile(q_gain.reshape(1, HEAD_DIM), (1, N_HEADS)).astype(F32)
    kg = jnp.tile(k_gain.reshape(1, HEAD_DIM), (1, N_KV_HEADS)).astype(F32)
    kern = functools.partial(_qkv_kernel, dq=dq, dk=dk, dqi=dqi)
    mod_spec = pl.BlockSpec((1, 1, d), lambda b, i: (b, 0, 0))

    def full(a):
        return pl.BlockSpec(a.shape, lambda b, i: (0,) * a.ndim)

    def out(n):
        return pl.BlockSpec((1, tm, n), lambda b, i: (b, i, 0))

    return pl.pallas_call(
        kern,
        out_shape=[
            jax.ShapeDtypeStruct((bsz, seq, dq), BF16),
            jax.ShapeDtypeStruct((bsz, seq, dk), BF16),
            jax.ShapeDtypeStruct((bsz, seq, dk), BF16),
            jax.ShapeDtypeStruct((bsz, seq, dqi), BF16),
            jax.ShapeDtypeStruct((bsz, seq, LANES), F32),
        ],
        grid=(bsz, seq // tm),
        in_specs=[
            pl.BlockSpec((1, tm, d), lambda b, i: (b, i, 0)),
            pl.BlockSpec((1, d), lambda b, i: (0, 0)),
            mod_spec, mod_spec,
            full(w_main), full(w_small), full(qg), full(kg),
            full(hsq), full(heq), full(hsk), full(hek),
        ],
        out_specs=[out(dq), out(dk), out(dk), out(dqi), out(LANES)],
        compiler_params=_cparams("parallel", "parallel"),
        name="attn_qkv",
    )(x, g, sh, sc, w_main, w_small, qg, kg, hsq, heq, hsk, hek)


def _rel_bucket_np(dist):
    max_exact = REL_BUCKETS // 2
    d = np.maximum(dist, 1).astype(np.float64)
    large = max_exact + (np.log(d / max_exact) / math.log(REL_MAX_DIST / max_exact)
                         * (REL_BUCKETS - max_exact)).astype(np.int32)
    large = np.minimum(large, REL_BUCKETS - 1)
    return np.where(dist < max_exact, dist, large).astype(np.int32)


def _bias_table_kernel(bucket_ref, rb_ref, o_ref):
    hd = pl.program_id(0)
    bucket = bucket_ref[...]
    acc = jnp.zeros(bucket.shape, F32)
    for b in range(REL_BUCKETS):
        acc = jnp.where(bucket == b, rb_ref[b, hd], acc)
    o_ref[0] = acc


def _bias_table(rel_bias):
    w = np.arange(BIAS_W)[:, None]
    i = np.arange(ATT_TQ)[None, :]
    bucket = jnp.asarray(_rel_bucket_np(np.maximum(i - w + BIAS_C, 0)))
    return pl.pallas_call(
        _bias_table_kernel,
        out_shape=jax.ShapeDtypeStruct((N_HEADS, BIAS_W, ATT_TQ), F32),
        grid=(N_HEADS,),
        in_specs=[
            pl.BlockSpec((BIAS_W, ATT_TQ), lambda hd: (0, 0)),
            pl.BlockSpec(memory_space=pltpu.SMEM),
        ],
        out_specs=pl.BlockSpec((1, BIAS_W, ATT_TQ), lambda hd: (hd, 0, 0)),
        compiler_params=_cparams("arbitrary"),
        name="attn_bias_table",
    )(bucket, rel_bias.astype(F32))


def _attn_kernel(qT_ref, qiT_ref, wiT_ref, k_ref, vT_ref, ki_ref, pt_ref, o_ref,
                 keys_scr, negm_scr, pidx_scr, oT_scr, *, top_k):
    tq, kc = ATT_TQ, ATT_KC
    qt = pl.program_id(1)
    q0 = qt * tq
    nch = (q0 + tq + kc - 1) // kc
    tpos = q0 + lax.broadcasted_iota(I32, (kc, tq), 1)
    srow = lax.broadcasted_iota(I32, (kc, tq), 0)

    qiT = qiT_ref[0]
    qi_all = jnp.concatenate([qiT[hd * IDX_DIM:(hd + 1) * IDX_DIM, :] for hd in range(IDX_HEADS)], axis=1)
    wiT = wiT_ref[0]

    def score_chunk(c, carry):
        ks = pl.multiple_of(c * kc, kc)
        kic = ki_ref[0, pl.ds(ks, kc), :]
        dots = jnp.dot(kic, qi_all, preferred_element_type=F32)
        acc = jnp.zeros((kc, tq), F32)
        for hd in range(IDX_HEADS):
            acc = acc + jnp.maximum(dots[:, hd * tq:(hd + 1) * tq], 0.0) * wiT[hd:hd + 1, :]
        acc = jnp.where(acc == 0.0, 0.0, acc)
        bits = pltpu.bitcast(acc, I32)
        key = jnp.where(bits < 0, bits ^ INT_MAX, bits)
        key = jnp.where(ks + srow <= tpos, key, INT_MIN)
        keys_scr[pl.ds(ks, kc), :] = key
        return carry

    lax.fori_loop(0, nch, score_chunk, 0)

    def count(pred):
        def body(c, acc):
            ks = pl.multiple_of(c * kc, kc)
            m = pred(keys_scr[pl.ds(ks, kc), :], ks + srow).astype(I32)
            return acc + jnp.sum(m.reshape(kc // SUBLANES, SUBLANES, tq), axis=0)
        acc = lax.fori_loop(0, nch, body, jnp.zeros((SUBLANES, tq), I32))
        return jnp.sum(acc, axis=0, keepdims=True)

    def bit_body(it, p):
        cand_p = p | lax.shift_left(jnp.int32(1), 31 - it)
        cand = cand_p ^ INT_MIN
        cnt = count(lambda k, s: k >= cand)
        return jnp.where(cnt >= top_k, cand_p, p)

    p_fin = lax.fori_loop(0, 32, bit_body, jnp.zeros((1, tq), I32))
    v = p_fin ^ INT_MIN

    cnt_gt = count(lambda k, s: k > v)
    cnt_eq = count(lambda k, s: k == v)
    need = top_k - cnt_gt
    pidx_scr[...] = jnp.full((1, tq), INT_MAX, I32)

    @pl.when(jnp.max(cnt_eq - need) > 0)
    def _():
        def ibit(it, p):
            cand = p | lax.shift_left(jnp.int32(1), 11 - it)
            cnt = count(lambda k, s: jnp.logical_and(k == v, s < cand))
            return jnp.where(cnt < need, cand, p)
        pidx_scr[...] = lax.fori_loop(0, 12, ibit, jnp.zeros((1, tq), I32))

    pidx = pidx_scr[...]

    def mask_chunk(c, carry):
        ks = pl.multiple_of(c * kc, kc)
        k = keys_scr[pl.ds(ks, kc), :]
        spos = ks + srow
        sel = jnp.logical_or(k > v, jnp.logical_and(k == v, spos <= pidx))
        sel = jnp.logical_and(sel, spos <= tpos)
        negm_scr[pl.ds(ks, kc), :] = jnp.where(sel, 0.0, NEG_BIG)
        return carry

    lax.fori_loop(0, nch, mask_chunk, 0)

    for n in range(N_KV_HEADS):
        qn = qT_ref[0, n * N_REP * HEAD_DIM:(n + 1) * N_REP * HEAD_DIM, :]
        q_all = jnp.concatenate([qn[g * HEAD_DIM:(g + 1) * HEAD_DIM, :] for g in range(N_REP)], axis=1)

        def att_chunk(c, carry, n=n, q_all=q_all):
            m, l, acc = carry
            ks = pl.multiple_of(c * kc, kc)
            kcn = k_ref[0, n, pl.ds(ks, kc), :]
            s = jnp.dot(kcn, q_all, preferred_element_type=F32)
            w0 = pl.multiple_of(jnp.maximum(BIAS_C - (q0 - ks), 0), LANES)
            negm = negm_scr[pl.ds(ks, kc), :]
            lg = jnp.concatenate(
                [s[:, g * tq:(g + 1) * tq] + pt_ref[n * N_REP + g, pl.ds(w0, kc), :] + negm
                 for g in range(N_REP)], axis=1)
            m_new = jnp.maximum(m, jnp.max(lg, axis=0, keepdims=True))
            alpha = jnp.exp(m - m_new)
            p = jnp.exp(lg - m_new)
            l_new = alpha * l + jnp.sum(p, axis=0, keepdims=True)
            vtc = vT_ref[0, c, n * HEAD_DIM:(n + 1) * HEAD_DIM, :]
            acc_new = alpha * acc + jnp.dot(vtc, p.astype(BF16), preferred_element_type=F32)
            return m_new, l_new, acc_new

        m0 = jnp.full((1, N_REP * tq), NEG_BIG, F32)
        l0 = jnp.zeros((1, N_REP * tq), F32)
        a0 = jnp.zeros((HEAD_DIM, N_REP * tq), F32)
        _, l_f, acc_f = lax.fori_loop(0, nch, att_chunk, (m0, l0, a0))
        o_t = acc_f / l_f
        for g in range(N_REP):
            r0 = (n * N_REP + g) * HEAD_DIM
            oT_scr[r0:r0 + HEAD_DIM, :] = o_t[:, g * tq:(g + 1) * tq]

    o_ref[0] = oT_scr[...].T.astype(BF16)


def _attention(qT, qiT, wiT, k4, vT, ki, ptab, top_k):
    bsz, dq, seq = qT.shape
    tq = ATT_TQ
    kern = functools.partial(_attn_kernel, top_k=top_k)
    return pl.pallas_call(
        kern,
        out_shape=jax.ShapeDtypeStruct((bsz, seq, dq), BF16),
        grid=(bsz, seq // tq),
        in_specs=[
            pl.BlockSpec((1, dq, tq), lambda b, i: (b, 0, i)),
            pl.BlockSpec((1, qiT.shape[1], tq), lambda b, i: (b, 0, i)),
            pl.BlockSpec((1, wiT.shape[1], tq), lambda b, i: (b, 0, i)),
            pl.BlockSpec((1,) + k4.shape[1:], lambda b, i: (b, 0, 0, 0)),
            pl.BlockSpec((1,) + vT.shape[1:], lambda b, i: (b, 0, 0, 0)),
            pl.BlockSpec((1,) + ki.shape[1:], lambda b, i: (b, 0, 0)),
            pl.BlockSpec(ptab.shape, lambda b, i: (0, 0, 0)),
        ],
        out_specs=pl.BlockSpec((1, tq, dq), lambda b, i: (b, i, 0)),
        scratch_shapes=[
            pltpu.VMEM((seq, tq), I32),
            pltpu.VMEM((seq, tq), F32),
            pltpu.VMEM((1, tq), I32),
            pltpu.VMEM((dq, tq), F32),
        ],
        compiler_params=_cparams("parallel", "arbitrary"),
        name="attn_core",
    )(qT, qiT, wiT, k4, vT, ki, ptab)


def _proj_res_kernel(a_ref, x_ref, gate_ref, w_ref, o_ref):
    y = jnp.dot(a_ref[0], w_ref[...], preferred_element_type=F32)
    o_ref[0] = x_ref[0] + gate_ref[0] * y


def _proj_residual(a, x, gate, w, tm=512):
    bsz, seq, d = x.shape
    tm = min(tm, seq)
    return pl.pallas_call(
        _proj_res_kernel,
        out_shape=jax.ShapeDtypeStruct((bsz, seq, d), F32),
        grid=(bsz, seq // tm),
        in_specs=[
            pl.BlockSpec((1, tm, a.shape[2]), lambda b, i: (b, i, 0)),
            pl.BlockSpec((1, tm, d), lambda b, i: (b, i, 0)),
            pl.BlockSpec((1, 1, d), lambda b, i: (b, 0, 0)),
            pl.BlockSpec(w.shape, lambda b, i: (0, 0)),
        ],
        out_specs=pl.BlockSpec((1, tm, d), lambda b, i: (b, i, 0)),
        compiler_params=_cparams("parallel", "parallel"),
        name="proj_residual",
    )(a, x, gate, w)


def _attn_layer(x, g, sh, sc, gate, w_in, q_gain, k_gain, w_out, rel_bias):
    bsz, seq, d = x.shape
    top_k = min(TOPK_MAX, seq // 4)
    dq = N_HEADS * HEAD_DIM
    dk = N_KV_HEADS * HEAD_DIM
    dqi = IDX_HEADS * IDX_DIM
    nmain = dq + 2 * dk + dqi
    w_main = w_in[:, :nmain].astype(BF16)
    w_small = jnp.pad(w_in[:, nmain:], ((0, 0), (0, LANES - (IDX_DIM + IDX_HEADS)))).astype(BF16)
    q, k, v, qi, kiwi = _qkv_proj(x, g, sh, sc, w_main, w_small, q_gain, k_gain)
    ki = kiwi[:, :, :IDX_DIM].astype(BF16)
    wi = kiwi[:, :, IDX_DIM:IDX_DIM + IDX_HEADS] * (IDX_HEADS ** -0.5 * IDX_DIM ** -0.5)
    qT = jnp.swapaxes(q, 1, 2)
    qiT = jnp.swapaxes(qi, 1, 2)
    wiT = jnp.swapaxes(wi, 1, 2)
    vT = jnp.swapaxes(v.reshape(bsz, seq // ATT_KC, ATT_KC, dk), 2, 3)
    k4 = jnp.swapaxes(k.reshape(bsz, seq, N_KV_HEADS, HEAD_DIM), 1, 2)
    ptab = _bias_table(rel_bias)
    attn = _attention(qT, qiT, wiT, k4, vT, ki, ptab, top_k)
    return _proj_residual(attn, x, gate, w_out.astype(BF16))


def _s5_prep_kernel(lre_ref, lim_ref, ls_ref, bre_ref, bim_ref, cre_ref, cim_ref,
                    kmat_ref, bcre_ref, bcim_ref, ccre_ref, ccim_ref, are_ref, aim_ref):
    t_len = SSM_CHUNK
    lre = jnp.minimum(lre_ref[0], -1e-4)
    lim = lim_ref[0]
    step = jnp.exp(ls_ref[0])
    ar = lre * step
    ai = lim * step

    def powers(jv):
        mag = jnp.exp(jv * ar)
        return mag * jnp.cos(jv * ai), mag * jnp.sin(jv * ai)

    lb_re, lb_im = powers(1.0)
    nr = lb_re - 1.0
    ni = lb_im
    den = lre * lre + lim * lim
    cf_re = (nr * lre + ni * lim) / den
    cf_im = (ni * lre - nr * lim) / den
    bre = bre_ref[0]
    bim = bim_ref[0]
    bb_re = cf_re * bre - cf_im * bim
    bb_im = cf_re * bim + cf_im * bre
    cre = cre_ref[0]
    cim = cim_ref[0]
    nst = lre.shape[-1]
    jv = lax.broadcasted_iota(I32, (t_len, 1, nst), 0).astype(F32)
    pj_re, pj_im = powers(jv)
    a_re = (cre[None] * pj_re - cim[None] * pj_im).reshape(t_len * SSM_GROUP, nst)
    a_im = (cre[None] * pj_im + cim[None] * pj_re).reshape(t_len * SSM_GROUP, nst)
    dn = (((1,), (1,)), ((), ()))
    kmat_ref[0] = (lax.dot_general(a_re, bb_re, dn, preferred_element_type=F32, precision=HIGHEST)
                   - lax.dot_general(a_im, bb_im, dn, preferred_element_type=F32, precision=HIGHEST))
    pr_re, pr_im = powers((t_len - 1.0) - jv)
    bcre_ref[0] = (bb_re[None] * pr_re - bb_im[None] * pr_im).reshape(t_len * SSM_GROUP, nst)
    bcim_ref[0] = (bb_re[None] * pr_im + bb_im[None] * pr_re).reshape(t_len * SSM_GROUP, nst)
    pn_re, pn_im = powers(jv + 1.0)
    ccre_ref[0] = (cre[None] * pn_re - cim[None] * pn_im).reshape(t_len * SSM_GROUP, nst)
    ccim_ref[0] = (cre[None] * pn_im + cim[None] * pn_re).reshape(t_len * SSM_GROUP, nst)
    at_re, at_im = powers(float(t_len))
    are_ref[0] = at_re
    aim_ref[0] = at_im


def _s5_prep(lam_re, lam_im, log_step, b_re, b_im, c_re, c_im):
    ng, nst = lam_re.shape
    tc = SSM_CHUNK * SSM_GROUP
    vec = pl.BlockSpec((1, 1, nst), lambda gi: (gi, 0, 0))
    mat = pl.BlockSpec((1, SSM_GROUP, nst), lambda gi: (gi, 0, 0))
    big = pl.BlockSpec((1, tc, nst), lambda gi: (gi, 0, 0))
    return pl.pallas_call(
        _s5_prep_kernel,
        out_shape=[
            jax.ShapeDtypeStruct((ng, tc, SSM_GROUP), F32),
            jax.ShapeDtypeStruct((ng, tc, nst), F32),
            jax.ShapeDtypeStruct((ng, tc, nst), F32),
            jax.ShapeDtypeStruct((ng, tc, nst), F32),
            jax.ShapeDtypeStruct((ng, tc, nst), F32),
            jax.ShapeDtypeStruct((ng, 1, nst), F32),
            jax.ShapeDtypeStruct((ng, 1, nst), F32),
        ],
        grid=(ng,),
        in_specs=[vec, vec, pl.BlockSpec((1, 1, 1), lambda gi: (gi, 0, 0)), mat, mat, mat, mat],
        out_specs=[pl.BlockSpec((1, tc, SSM_GROUP), lambda gi: (gi, 0, 0)), big, big, big, big, vec, vec],
        compiler_params=_cparams("parallel"),
        name="s5_prep",
    )(lam_re.reshape(ng, 1, nst), lam_im.reshape(ng, 1, nst), log_step.reshape(ng, 1, 1),
      jnp.swapaxes(b_re, 1, 2), jnp.swapaxes(b_im, 1, 2), c_re, c_im)


def _s5_pre_kernel(x_ref, g_ref, sh_ref, sc_ref, o_ref):
    h = _norm_mod(x_ref[0], g_ref[...], sh_ref[0], sc_ref[0]).astype(BF16)
    for q in range(o_ref.shape[0]):
        o_ref[q] = h[:, q * LANES:(q + 1) * LANES]


def _s5_pre(x, g, sh, sc, tm=512):
    bsz, seq, d = x.shape
    tm = min(tm, seq)
    nt = seq // tm
    nq = d // LANES
    mod_spec = pl.BlockSpec((1, 1, d), lambda b, i: (b, 0, 0))
    return pl.pallas_call(
        _s5_pre_kernel,
        out_shape=jax.ShapeDtypeStruct((nq, bsz * seq, LANES), BF16),
        grid=(bsz, nt),
        in_specs=[pl.BlockSpec((1, tm, d), lambda b, i: (b, i, 0)),
                  pl.BlockSpec((1, d), lambda b, i: (0, 0)), mod_spec, mod_spec],
        out_specs=pl.BlockSpec((nq, tm, LANES), lambda b, i: (0, b * nt + i, 0)),
        compiler_params=_cparams("parallel", "parallel"),
        name="s5_pre",
    )(x, g, sh, sc)


def _s5_scan_kernel(x_ref, bc_ref, mt_ref, cc_ref, are_ref, aim_ref, y_ref, re_scr, im_scr):
    x = x_ref[0]
    nrow = x.shape[0]
    pad = re_scr.shape[0] - nrow
    v = jnp.dot(x, bc_ref[0], preferred_element_type=F32)
    nh = v.shape[1] // 2
    s_re = v[:, :nh]
    s_im = v[:, nh:]
    a_re = are_ref[0]
    a_im = aim_ref[0]
    re_scr[0:pad, :] = jnp.zeros((pad, nh), F32)
    im_scr[0:pad, :] = jnp.zeros((pad, nh), F32)

    def shifted(scr, val, dist):
        scr[pad:pad + nrow, :] = val
        return scr[pad - dist:pad - dist + nrow, :]

    dist = 1
    while dist < nrow:
        sh_re = shifted(re_scr, s_re, dist)
        sh_im = shifted(im_scr, s_im, dist)
        s_re, s_im = (s_re + a_re * sh_re - a_im * sh_im, s_im + a_re * sh_im + a_im * sh_re)
        a_re, a_im = (a_re * a_re - a_im * a_im, 2.0 * a_re * a_im)
        dist *= 2
    sp = jnp.concatenate([shifted(re_scr, s_re, 1), shifted(im_scr, s_im, 1)], axis=1).astype(BF16)
    y_ref[0] = (jnp.dot(x, mt_ref[0], preferred_element_type=F32)
                + jnp.dot(sp, cc_ref[0], preferred_element_type=F32))


def _s5_scan(hq, bcq, mtq, ccq, a_re, a_im, bsz):
    nq, nrows, width = hq.shape
    nj = nrows // bsz
    nh = a_re.shape[-1]
    pad = max(nj // 2, SUBLANES)
    tile = pl.BlockSpec((1, nj, width), lambda q, b: (q, b, 0))
    wspec = lambda a: pl.BlockSpec((1,) + a.shape[1:], lambda q, b: (q, 0, 0))
    return pl.pallas_call(
        _s5_scan_kernel,
        out_shape=jax.ShapeDtypeStruct((nq, nrows, width), F32),
        grid=(nq, bsz),
        in_specs=[tile, wspec(bcq), wspec(mtq), wspec(ccq), wspec(a_re), wspec(a_im)],
        out_specs=tile,
        scratch_shapes=[pltpu.VMEM((pad + nj, nh), F32), pltpu.VMEM((pad + nj, nh), F32)],
        compiler_params=_cparams("parallel", "parallel"),
        name="s5_scan",
    )(hq, bcq, mtq, ccq, a_re, a_im)


def _s5_post_kernel(x_ref, y_ref, g_ref, sh_ref, sc_ref, gate_ref, dsk_ref, w_ref, o_ref, *, d):
    x = x_ref[0]
    h = _norm_mod(x, g_ref[...], sh_ref[0], sc_ref[0])
    y = jnp.concatenate([y_ref[q] for q in range(y_ref.shape[0])], axis=1)
    yy = y + dsk_ref[...] * h
    gl = jax.nn.gelu(yy).astype(BF16)
    z = jnp.dot(gl, w_ref[...], preferred_element_type=F32)
    o_ref[0] = x + gate_ref[0] * (z[:, :d] * jax.nn.sigmoid(z[:, d:]))


def _s5_post(x, yq, g, sh, sc, gate, d_skip, w_glu, tm=512):
    bsz, seq, d = x.shape
    tm = min(tm, seq)
    nt = seq // tm
    nq = d // LANES
    kern = functools.partial(_s5_post_kernel, d=d)
    mod_spec = pl.BlockSpec((1, 1, d), lambda b, i: (b, 0, 0))
    tile = pl.BlockSpec((1, tm, d), lambda b, i: (b, i, 0))
    return pl.pallas_call(
        kern,
        out_shape=jax.ShapeDtypeStruct((bsz, seq, d), F32),
        grid=(bsz, nt),
        in_specs=[tile, pl.BlockSpec((nq, tm, LANES), lambda b, i: (0, b * nt + i, 0)),
                  pl.BlockSpec((1, d), lambda b, i: (0, 0)), mod_spec, mod_spec, mod_spec,
                  pl.BlockSpec((1, d), lambda b, i: (0, 0)),
                  pl.BlockSpec(w_glu.shape, lambda b, i: (0, 0))],
        out_specs=tile,
        compiler_params=_cparams("parallel", "parallel"),
        name="s5_post",
    )(x, yq, g, sh, sc, gate, d_skip, w_glu)


def _s5_layer(x, g, sh, sc, gate, lam_re, lam_im, log_step, b_re, b_im, c_re, c_im, d_skip, w_glu):
    bsz, seq, d = x.shape
    ng, nst = lam_re.shape
    t_len, cg = SSM_CHUNK, SSM_GROUP
    nq = d // LANES
    gq = LANES // cg
    nj = seq // t_len
    kmat, bc_re, bc_im, cc_re, cc_im, a_re, a_im = _s5_prep(lam_re, lam_im, log_step, b_re, b_im, c_re, c_im)
    eye = jnp.eye(gq, dtype=F32)

    def state_in(bc):
        return jnp.einsum('qgtcp,gh->qtgchp', bc.reshape(nq, gq, t_len, cg, nst), eye)

    bcq = jnp.stack([state_in(bc_re), state_in(bc_im)], axis=4)
    bcq = bcq.reshape(nq, t_len * LANES, 2 * gq * nst).astype(BF16)
    tt = np.arange(t_len)
    lag = tt[:, None] - tt[None, :]
    k4 = kmat.reshape(ng, t_len, cg, cg)[:, np.maximum(lag, 0)]
    k4 = jnp.where(jnp.asarray(lag >= 0)[None, :, :, None, None], k4, 0.0)
    mtq = jnp.einsum('qgtscd,gh->qshdtgc', k4.reshape(nq, gq, t_len, t_len, cg, cg), eye)
    mtq = mtq.reshape(nq, t_len * LANES, t_len * LANES).astype(BF16)

    def state_out(cc):
        return jnp.einsum('qgtcp,gh->qhptgc', cc.reshape(nq, gq, t_len, cg, nst), eye)

    ccq = jnp.stack([state_out(cc_re), -state_out(cc_im)], axis=1)
    ccq = ccq.reshape(nq, 2 * gq * nst, t_len * LANES).astype(BF16)
    aq_re = a_re.reshape(nq, 1, gq * nst)
    aq_im = a_im.reshape(nq, 1, gq * nst)

    hq = _s5_pre(x, g, sh, sc).reshape(nq, bsz * nj, t_len * LANES)
    yq = _s5_scan(hq, bcq, mtq, ccq, aq_re, aq_im, bsz).reshape(nq, bsz * seq, LANES)
    return _s5_post(x, yq, g, sh, sc, gate, d_skip.reshape(1, d).astype(F32), w_glu.astype(BF16))


def kernel(x, c, ada_w, ada_b, norm_g, conv_w_in, conv_w, conv_w_out, attn_w_in, attn_q_gain, attn_k_gain, attn_w_out, rel_bias, ssm_lambda_re, ssm_lambda_im, ssm_log_step, ssm_b_re, ssm_b_im, ssm_c_re, ssm_c_im, ssm_d, ssm_w_glu, ffn_w_gu, ffn_w_down, moe_router_w, moe_router_b, moe_w_gu, moe_w_down):
    bsz, seq, d = x.shape
    depth = ada_w.shape[0]
    mod = _ada_mod(c, ada_w, ada_b).reshape(depth, bsz, 6, 1, d)
    for i in range(depth):
        sh1, sc1, g1, sh2, sc2, g2 = (mod[i, :, r] for r in range(6))
        gn1 = norm_g[i, 0].reshape(1, d)
        gn2 = norm_g[i, 1].reshape(1, d)
        j = i // N_MIXERS
        if i % N_MIXERS == 0:
            x = _conv_layer(x, gn1, sh1, sc1, g1, conv_w_in[j].astype(BF16), conv_w[j],
                            conv_w_out[j].astype(BF16))
        elif i % N_MIXERS == 1:
            x = _attn_layer(x, gn1, sh1, sc1, g1, attn_w_in[j], attn_q_gain[j], attn_k_gain[j],
                            attn_w_out[j], rel_bias)
        else:
            x = _s5_layer(x, gn1, sh1, sc1, g1, ssm_lambda_re[j], ssm_lambda_im[j], ssm_log_step[j],
                          ssm_b_re[j], ssm_b_im[j], ssm_c_re[j], ssm_c_im[j], ssm_d[j], ssm_w_glu[j])
        if i % 2 == 0:
            x = _ffn_layer(x, gn2, sh2, sc2, g2, ffn_w_gu[i // 2].astype(BF16), ffn_w_down[i // 2].astype(BF16))
        else:
            x = _moe_layer(x, gn2, sh2, sc2, g2, moe_router_w[i // 2], moe_router_b[i // 2],
                           moe_w_gu[i // 2].astype(BF16), moe_w_down[i // 2].astype(BF16))
    return x
```

```python
import functools
import math

import numpy as np
import jax
import jax.numpy as jnp
from jax import lax
from jax.experimental import pallas as pl
from jax.experimental.pallas import tpu as pltpu

F32 = jnp.float32
BF16 = jnp.bfloat16
I32 = jnp.int32
HIGHEST = lax.Precision.HIGHEST

DEPTH = 4
N_MIXERS = 3
EPS = 1e-6
CONV_WIDTH = 3
N_HEADS = 16
N_KV_HEADS = 4
N_REP = N_HEADS // N_KV_HEADS
HEAD_DIM = 64
IDX_HEADS = 8
IDX_DIM = 64
TOPK_MAX = 256
REL_BUCKETS = 32
REL_MAX_DIST = 128
SSM_GROUP = 16
SSM_STATE = 64
N_EXPERTS = 8
TOP_K_EXPERTS = 2

VMEM_LIMIT_BYTES = 56 * 1024 * 1024
LANES = 128
SUBLANES = 8

INT_MIN = -(2 ** 31)
INT_MAX = 2 ** 31 - 1
NEG_BIG = -1e30

ATT_TQ = 128
ATT_KC = 256
BIAS_C = 384
BIAS_W = BIAS_C + ATT_KC
ATT_VROWS = 80

SSM_CHUNK = 8


def _cparams(*sem):
    return pltpu.CompilerParams(dimension_semantics=sem, vmem_limit_bytes=VMEM_LIMIT_BYTES)


def _norm_mod(x, g, shift, scale):
    ms = jnp.mean(x * x, axis=-1, keepdims=True)
    y = x * lax.rsqrt(ms + EPS)
    return (y * g) * (1.0 + scale) + shift


def _silu(x):
    return x * jax.nn.sigmoid(x)


def _ada_kernel(c_ref, w_ref, b_ref, o_ref):
    c = c_ref[...]
    cond = _silu(c)
    o_ref[0] = jnp.dot(cond, w_ref[0], preferred_element_type=F32, precision=HIGHEST) + b_ref[0]


def _ada_mod(c, ada_w, ada_b):
    depth, d, d6 = ada_w.shape
    bsz = c.shape[0]
    tn = d
    return pl.pallas_call(
        _ada_kernel,
        out_shape=jax.ShapeDtypeStruct((depth, bsz, d6), F32),
        grid=(depth, d6 // tn),
        in_specs=[
            pl.BlockSpec((bsz, d), lambda i, j: (0, 0)),
            pl.BlockSpec((1, d, tn), lambda i, j: (i, 0, j)),
            pl.BlockSpec((1, 1, tn), lambda i, j: (i, 0, j)),
        ],
        out_specs=pl.BlockSpec((1, bsz, tn), lambda i, j: (i, 0, j)),
        compiler_params=_cparams("parallel", "parallel"),
        name="ada_mod",
    )(c, ada_w, ada_b.reshape(depth, 1, d6))


def _conv_kernel(x_ref, xh_ref, g_ref, sh_ref, sc_ref, gate_ref, win_ref, wc_ref, wout_ref,
                 o_ref, u_scr, *, tm, d):
    i = pl.program_id(1)
    g = g_ref[...]
    sh = sh_ref[0]
    sc = sc_ref[0]
    x = x_ref[0]
    h = _norm_mod(x, g, sh, sc).astype(BF16)
    z = jnp.dot(h, win_ref[...], preferred_element_type=F32)
    b_gate = z[:, :d]
    u = z[:, d:2 * d] * z[:, 2 * d:]
    hh = _norm_mod(xh_ref[0], g, sh, sc).astype(BF16)
    zh = jnp.dot(hh, win_ref[:, d:], preferred_element_type=F32)
    uh = zh[:, :d] * zh[:, d:]
    uh = jnp.where(i > 0, uh, 0.0)
    u_scr[0:SUBLANES, :] = uh
    u_scr[SUBLANES:SUBLANES + tm, :] = u
    wc = wc_ref[...]
    conv = (wc[0:1, :] * u_scr[SUBLANES - 2:SUBLANES - 2 + tm, :]
            + wc[1:2, :] * u_scr[SUBLANES - 1:SUBLANES - 1 + tm, :]
            + wc[2:3, :] * u)
    y = jnp.dot((b_gate * conv).astype(BF16), wout_ref[...], preferred_element_type=F32)
    o_ref[0] = x + gate_ref[0] * y


def _conv_layer(x, g, sh, sc, gate, w_in, w_conv, w_out, tm=512):
    bsz, seq, d = x.shape
    tm = min(tm, seq)
    nt = seq // tm
    hb = tm // SUBLANES
    kern = functools.partial(_conv_kernel, tm=tm, d=d)
    mod_spec = pl.BlockSpec((1, 1, d), lambda b, i: (b, 0, 0))
    return pl.pallas_call(
        kern,
        out_shape=jax.ShapeDtypeStruct((bsz, seq, d), F32),
        grid=(bsz, nt),
        in_specs=[
            pl.BlockSpec((1, tm, d), lambda b, i: (b, i, 0)),
            pl.BlockSpec((1, SUBLANES, d), lambda b, i: (b, jnp.maximum(i * hb - 1, 0), 0)),
            pl.BlockSpec((1, d), lambda b, i: (0, 0)),
            mod_spec, mod_spec, mod_spec,
            pl.BlockSpec((d, 3 * d), lambda b, i: (0, 0)),
            pl.BlockSpec((CONV_WIDTH, d), lambda b, i: (0, 0)),
            pl.BlockSpec((d, d), lambda b, i: (0, 0)),
        ],
        out_specs=pl.BlockSpec((1, tm, d), lambda b, i: (b, i, 0)),
        scratch_shapes=[pltpu.VMEM((tm + SUBLANES, d), F32)],
        compiler_params=_cparams("parallel", "parallel"),
        name="conv_mixer",
    )(x, x, g, sh, sc, gate, w_in, w_conv, w_out)


def _ffn_kernel(x_ref, g_ref, sh_ref, sc_ref, gate_ref, wgu_ref, wd_ref, o_ref, *, dff, nchunk):
    x = x_ref[0]
    h = _norm_mod(x, g_ref[...], sh_ref[0], sc_ref[0]).astype(BF16)
    cols = dff // nchunk
    acc = jnp.zeros(x.shape, F32)
    for c in range(nchunk):
        gg = jnp.dot(h, wgu_ref[:, c * cols:(c + 1) * cols], preferred_element_type=F32)
        uu = jnp.dot(h, wgu_ref[:, dff + c * cols:dff + (c + 1) * cols], preferred_element_type=F32)
        a = (_silu(gg) * uu).astype(BF16)
        acc = acc + jnp.dot(a, wd_ref[c * cols:(c + 1) * cols, :], preferred_element_type=F32)
    o_ref[0] = x + gate_ref[0] * acc


def _ffn_layer(x, g, sh, sc, gate, w_gu, w_down, tm=512):
    bsz, seq, d = x.shape
    dff = w_down.shape[0]
    tm = min(tm, seq)
    kern = functools.partial(_ffn_kernel, dff=dff, nchunk=2)
    mod_spec = pl.BlockSpec((1, 1, d), lambda b, i: (b, 0, 0))
    return pl.pallas_call(
        kern,
        out_shape=jax.ShapeDtypeStruct((bsz, seq, d), F32),
        grid=(bsz, seq // tm),
        in_specs=[
            pl.BlockSpec((1, tm, d), lambda b, i: (b, i, 0)),
            pl.BlockSpec((1, d), lambda b, i: (0, 0)),
            mod_spec, mod_spec, mod_spec,
            pl.BlockSpec((d, 2 * dff), lambda b, i: (0, 0), pipeline_mode=pl.Buffered(1)),
            pl.BlockSpec((dff, d), lambda b, i: (0, 0), pipeline_mode=pl.Buffered(1)),
        ],
        out_specs=pl.BlockSpec((1, tm, d), lambda b, i: (b, i, 0)),
        compiler_params=_cparams("parallel", "parallel"),
        name="ffn_dense",
    )(x, g, sh, sc, gate, w_gu, w_down)


MOE_SB = 512
MOE_RT = 512
MOE_ALIGN = 16
MOE_WIN = 256
MOE_CAP = MOE_WIN - MOE_ALIGN


def _moe_router_kernel(x_ref, g_ref, sh_ref, sc_ref, rw_ref, rb_ref, h_ref, meta_ref):
    hf = _norm_mod(x_ref[...], g_ref[...], sh_ref[0], sc_ref[0])
    h_ref[...] = hf.astype(BF16)
    logits = jnp.dot(hf, rw_ref[...], preferred_element_type=F32, precision=HIGHEST) + rb_ref[...]
    mx = jnp.max(logits, axis=-1, keepdims=True)
    ex = jnp.exp(logits - mx)
    probs = ex / jnp.sum(ex, axis=-1, keepdims=True)
    lane = lax.broadcasted_iota(I32, probs.shape, 1)
    m1 = jnp.max(probs, axis=-1, keepdims=True)
    i1 = jnp.min(jnp.where(probs == m1, lane, LANES), axis=-1, keepdims=True)
    rest = jnp.where(lane == i1, -1.0, probs)
    m2 = jnp.max(rest, axis=-1, keepdims=True)
    i2 = jnp.min(jnp.where(rest == m2, lane, LANES), axis=-1, keepdims=True)
    den = m1 + m2
    gates = jnp.where(lane == i1, m1 / den, 0.0) + jnp.where(lane == i2, m2 / den, 0.0)
    chosen = jnp.where(jnp.logical_or(lane == i1, lane == i2), 1.0, 0.0)
    meta_ref[...] = gates + pltpu.roll(chosen, N_EXPERTS, 1)


def _moe_router(xt, g, sh, sc, rw_pad, rb_pad, tpb, tm):
    n, d = xt.shape
    mod_spec = pl.BlockSpec((1, 1, d), lambda i: (i // tpb, 0, 0))
    return pl.pallas_call(
        _moe_router_kernel,
        out_shape=[jax.ShapeDtypeStruct((n, d), BF16), jax.ShapeDtypeStruct((n, LANES), F32)],
        grid=(n // tm,),
        in_specs=[pl.BlockSpec((tm, d), lambda i: (i, 0)),
                  pl.BlockSpec((1, d), lambda i: (0, 0)), mod_spec, mod_spec,
                  pl.BlockSpec((d, LANES), lambda i: (0, 0)),
                  pl.BlockSpec((1, LANES), lambda i: (0, 0))],
        out_specs=[pl.BlockSpec((tm, d), lambda i: (i, 0)), pl.BlockSpec((tm, LANES), lambda i: (i, 0))],
        compiler_params=_cparams("parallel"),
        name="moe_router",
    )(xt, g, sh, sc, rw_pad, rb_pad)


def _window(start, count, w):
    s = start + w * MOE_CAP
    n = jnp.minimum(count - w * MOE_CAP, MOE_CAP)
    a = pl.multiple_of((s // MOE_ALIGN) * MOE_ALIGN, MOE_ALIGN)
    return s, n, a


def _moe_dispatch_kernel(start_ref, cnt_ref, h_ref, pos_ref, xs_init_ref, xs_ref,
                         buf, carry, sems, pending):
    del xs_init_ref
    b = pl.program_id(0)
    nb = pl.num_programs(0)
    ne = buf.shape[0]

    @pl.when(b == 0)
    def _():
        carry[...] = jnp.zeros(carry.shape, BF16)
        for e in range(ne):
            pending[e] = 0

    h = h_ref[...]
    riota = lax.broadcasted_iota(I32, (MOE_WIN, h.shape[0]), 0)

    def out_copy(e, a):
        return pltpu.make_async_copy(buf.at[e], xs_ref.at[pl.ds(a, MOE_WIN)], sems.at[e])

    for e in range(ne):
        posrow = pos_ref[0, e:e + 1, :]
        nwin = (cnt_ref[b, e] + MOE_CAP - 1) // MOE_CAP

        def wbody(w, carry_unused, e=e, posrow=posrow):
            s, n, a = _window(start_ref[b, e], cnt_ref[b, e], w)

            @pl.when(pending[e] == 1)
            def _():
                out_copy(e, 0).wait()

            hit = jnp.logical_and(posrow - a == riota,
                                  jnp.logical_and(posrow >= s, posrow < s + n))
            onehot = jnp.where(hit, 1.0, 0.0).astype(BF16)
            buf[e] = jnp.dot(onehot, h, preferred_element_type=F32).astype(BF16)
            buf[e, 0:MOE_ALIGN, :] = buf[e, 0:MOE_ALIGN, :] + carry[e]
            c0 = pl.multiple_of(((s + n) // MOE_ALIGN) * MOE_ALIGN - a, MOE_ALIGN)
            carry[e] = buf[e, pl.ds(c0, MOE_ALIGN), :]
            out_copy(e, a).start()
            pending[e] = 1
            return carry_unused

        lax.fori_loop(0, nwin, wbody, 0)

    @pl.when(b == nb - 1)
    def _():
        for e in range(ne):
            @pl.when(pending[e] == 1)
            def _():
                out_copy(e, 0).wait()


def _moe_dispatch(start, cnt, h, pos_t, ncap):
    n, d = h.shape
    nb, ne, sb = pos_t.shape
    xs_init = jnp.zeros((ncap, d), BF16)
    return pl.pallas_call(
        _moe_dispatch_kernel,
        out_shape=jax.ShapeDtypeStruct((ncap, d), BF16),
        grid_spec=pltpu.PrefetchScalarGridSpec(
            num_scalar_prefetch=2,
            grid=(nb,),
            in_specs=[pl.BlockSpec((sb, d), lambda b, *_: (b, 0)),
                      pl.BlockSpec((1, ne, sb), lambda b, *_: (b, 0, 0)),
                      pl.BlockSpec(memory_space=pl.ANY)],
            out_specs=pl.BlockSpec(memory_space=pl.ANY),
            scratch_shapes=[pltpu.VMEM((ne, MOE_WIN, d), BF16),
                            pltpu.VMEM((ne, MOE_ALIGN, d), BF16),
                            pltpu.SemaphoreType.DMA((ne,)),
                            pltpu.SMEM((ne,), I32)],
        ),
        input_output_aliases={4: 0},
        compiler_params=_cparams("arbitrary"),
        name="moe_dispatch",
    )(start, cnt, h, pos_t, xs_init)


def _moe_ffn_kernel(blk_ref, exp_ref, nt_ref, xs_ref, wgu_ref, wd_ref, ys_init_ref, ys_ref, *, dff, nchunk):
    del blk_ref, exp_ref, ys_init_ref
    k = pl.program_id(0)

    @pl.when(k < nt_ref[0])
    def _():
        x = xs_ref[...]
        cols = dff // nchunk
        acc = jnp.zeros(x.shape, F32)
        for c in range(nchunk):
            gg = jnp.dot(x, wgu_ref[0, :, c * cols:(c + 1) * cols], preferred_element_type=F32)
            uu = jnp.dot(x, wgu_ref[0, :, dff + c * cols:dff + (c + 1) * cols], preferred_element_type=F32)
            a = (_silu(gg) * uu).astype(BF16)
            acc = acc + jnp.dot(a, wd_ref[0, c * cols:(c + 1) * cols, :], preferred_element_type=F32)
        ys_ref[...] = acc.astype(BF16)


def _moe_ffn(tile_blk, tile_exp, ntiles, xs, w_gu, w_down):
    ncap, d = xs.shape
    ne, dff, _ = w_down.shape
    kern = functools.partial(_moe_ffn_kernel, dff=dff, nchunk=2)
    ys_init = jnp.zeros((ncap, d), BF16)
    return pl.pallas_call(
        kern,
        out_shape=jax.ShapeDtypeStruct((ncap, d), BF16),
        grid_spec=pltpu.PrefetchScalarGridSpec(
            num_scalar_prefetch=3,
            grid=(ncap // MOE_RT,),
            in_specs=[pl.BlockSpec((MOE_RT, d), lambda k, blk, exp, nt: (blk[k], 0)),
                      pl.BlockSpec((1, d, 2 * dff), lambda k, blk, exp, nt: (exp[k], 0, 0)),
                      pl.BlockSpec((1, dff, d), lambda k, blk, exp, nt: (exp[k], 0, 0)),
                      pl.BlockSpec(memory_space=pl.ANY)],
            out_specs=pl.BlockSpec((MOE_RT, d), lambda k, blk, exp, nt: (blk[k], 0)),
        ),
        input_output_aliases={6: 0},
        compiler_params=_cparams("arbitrary"),
        name="moe_ffn",
    )(tile_blk, tile_exp, ntiles, xs, w_gu, w_down, ys_init)


def _moe_combine_kernel(start_ref, cnt_ref, x_ref, gate_ref, pos_ref, gts_ref, ys_ref, o_ref,
                        buf, sems, acc_scr):
    b = pl.program_id(0)
    ne = buf.shape[0]
    sb = x_ref.shape[0]

    def in_copy(e, a):
        return pltpu.make_async_copy(ys_ref.at[pl.ds(a, MOE_WIN)], buf.at[e], sems.at[e])

    for e in range(ne):
        @pl.when(cnt_ref[b, e] > 0)
        def _():
            _, _, a = _window(start_ref[b, e], cnt_ref[b, e], 0)
            in_copy(e, a).start()

    acc_scr[...] = jnp.zeros(acc_scr.shape, F32)
    liota = lax.broadcasted_iota(I32, (sb, MOE_WIN), 1)
    for e in range(ne):
        poscol = pos_ref[:, e:e + 1]
        gcol = gts_ref[:, e:e + 1]
        nwin = (cnt_ref[b, e] + MOE_CAP - 1) // MOE_CAP

        def wbody(w, carry_unused, e=e, poscol=poscol, gcol=gcol):
            s, n, a = _window(start_ref[b, e], cnt_ref[b, e], w)

            @pl.when(w > 0)
            def _():
                in_copy(e, a).start()

            in_copy(e, a).wait()
            hit = jnp.logical_and(poscol - a == liota,
                                  jnp.logical_and(poscol >= s, poscol < s + n))
            onehot = jnp.where(hit, 1.0, 0.0).astype(BF16)
            acc_scr[...] += gcol * jnp.dot(onehot, buf[e], preferred_element_type=F32)
            return carry_unused

        lax.fori_loop(0, nwin, wbody, 0)

    o_ref[...] = x_ref[...] + gate_ref[0] * acc_scr[...]


def _moe_combine(start, cnt, xt, gate, pos_n, gates_n, ys, tpb):
    n, d = xt.shape
    nb, ne = cnt.shape
    sb = n // nb
    return pl.pallas_call(
        _moe_combine_kernel,
        out_shape=jax.ShapeDtypeStruct((n, d), F32),
        grid_spec=pltpu.PrefetchScalarGridSpec(
            num_scalar_prefetch=2,
            grid=(nb,),
            in_specs=[pl.BlockSpec((sb, d), lambda b, *_: (b, 0)),
                      pl.BlockSpec((1, 1, d), lambda b, *_: (b // tpb, 0, 0)),
                      pl.BlockSpec((sb, ne), lambda b, *_: (b, 0)),
                      pl.BlockSpec((sb, ne), lambda b, *_: (b, 0)),
                      pl.BlockSpec(memory_space=pl.ANY)],
            out_specs=pl.BlockSpec((sb, d), lambda b, *_: (b, 0)),
            scratch_shapes=[pltpu.VMEM((ne, MOE_WIN, d), BF16),
                            pltpu.SemaphoreType.DMA((ne,)),
                            pltpu.VMEM((sb, d), F32)],
        ),
        compiler_params=_cparams("arbitrary"),
        name="moe_combine",
    )(start, cnt, xt, gate, pos_n, gates_n, ys)


def _moe_layer(x, g, sh, sc, gate, router_w, router_b, w_gu, w_down):
    bsz, seq, d = x.shape
    ne, dff, _ = w_down.shape
    n = bsz * seq
    sb = min(MOE_SB, seq)
    tpb = seq // sb
    nb = n // sb
    rt = MOE_RT
    xt = x.reshape(n, d)
    rw_pad = jnp.pad(router_w.astype(F32), ((0, 0), (0, LANES - ne)))
    rb_pad = jnp.pad(router_b.astype(F32).reshape(1, ne), ((0, 0), (0, LANES - ne)), constant_values=NEG_BIG)
    h, meta = _moe_router(xt, g, sh, sc, rw_pad, rb_pad, tpb, sb)
    gates = meta[:, :ne]
    sel = (meta[:, ne:2 * ne] > 0.5).astype(I32)

    selb = sel.reshape(nb, sb, ne)
    cnt = jnp.sum(selb, axis=1)
    rank = jnp.cumsum(selb, axis=1) - selb
    total = jnp.sum(cnt, axis=0)
    region = ((total + MOE_WIN + rt - 1) // rt) * rt
    off = jnp.cumsum(region) - region
    start = (off[None, :] + jnp.cumsum(cnt, axis=0) - cnt).astype(I32)
    pos = jnp.where(selb > 0, start[:, None, :] + rank, -1).astype(I32)
    ncap = 2 * n + ne * (MOE_WIN + rt)
    tiles_e = (total + rt - 1) // rt
    tcum = jnp.cumsum(tiles_e)
    ntiles = tcum[-1]
    kk = jnp.minimum(jnp.arange(ncap // rt), ntiles - 1)
    tile_exp = jnp.searchsorted(tcum, kk, side='right').astype(I32)
    tile_blk = (off[tile_exp] // rt + kk - (tcum - tiles_e)[tile_exp]).astype(I32)

    xs = _moe_dispatch(start, cnt.astype(I32), h, jnp.swapaxes(pos, 1, 2), ncap)
    ys = _moe_ffn(tile_blk, tile_exp, ntiles.reshape(1).astype(I32), xs, w_gu, w_down)
    out = _moe_combine(start, cnt.astype(I32), xt, gate, pos.reshape(n, ne), gates, ys, tpb)
    return out.reshape(bsz, seq, d)


def _head_norm(q, hsum_ref, hexp_ref, gain, scale):
    ms = jnp.dot((q * q).astype(BF16), hsum_ref[...], preferred_element_type=F32)
    r = lax.rsqrt(ms + EPS)
    r_hi = r.astype(BF16)
    r_lo = (r - r_hi.astype(F32)).astype(BF16)
    rexp = (jnp.dot(r_hi, hexp_ref[...], preferred_element_type=F32)
            + jnp.dot(r_lo, hexp_ref[...], preferred_element_type=F32))
    return q * rexp * (gain * scale)


def _qkv_kernel(x_ref, g_ref, sh_ref, sc_ref, wm_ref, ws_ref, qg_ref, kg_ref,
                hsq_ref, heq_ref, hsk_ref, hek_ref,
                q_ref, k_ref, v_ref, qi_ref, kiwi_ref, *, dq, dk, dqi):
    x = x_ref[0]
    h = _norm_mod(x, g_ref[...], sh_ref[0], sc_ref[0]).astype(BF16)
    z = jnp.dot(h, wm_ref[...], preferred_element_type=F32)
    q = z[:, :dq]
    k = z[:, dq:dq + dk]
    v = z[:, dq + dk:dq + 2 * dk]
    qi = z[:, dq + 2 * dk:dq + 2 * dk + dqi]
    q_ref[0] = _head_norm(q, hsq_ref, heq_ref, qg_ref[...], HEAD_DIM ** -0.5).astype(BF16)
    k_ref[0] = _head_norm(k, hsk_ref, hek_ref, kg_ref[...], 1.0).astype(BF16)
    v_ref[0] = v.astype(BF16)
    qi_ref[0] = qi.astype(BF16)
    kiwi_ref[0] = jnp.dot(h, ws_ref[...], preferred_element_type=F32)


def _head_indicators(nheads):
    hs = np.zeros((nheads * HEAD_DIM, LANES), np.float32)
    he = np.zeros((LANES, nheads * HEAD_DIM), np.float32)
    for hd in range(nheads):
        hs[hd * HEAD_DIM:(hd + 1) * HEAD_DIM, hd] = 1.0 / HEAD_DIM
        he[hd, hd * HEAD_DIM:(hd + 1) * HEAD_DIM] = 1.0
    return jnp.asarray(hs, BF16), jnp.asarray(he, BF16)


def _qkv_proj(x, g, sh, sc, w_main, w_small, q_gain, k_gain, tm=512):
    bsz, seq, d = x.shape
    tm = min(tm, seq)
    dq = N_HEADS * HEAD_DIM
    dk = N_KV_HEADS * HEAD_DIM
    dqi = IDX_HEADS * IDX_DIM
    hsq, heq = _head_indicators(N_HEADS)
    hsk, hek = _head_indicators(N_KV_HEADS)
    qg = jnp.tile(q_gain.reshape(1, HEAD_DIM), (1, N_HEADS)).astype(F32)
    kg = jnp.tile(k_gain.reshape(1, HEAD_DIM), (1, N_KV_HEADS)).astype(F32)
    kern = functools.partial(_qkv_kernel, dq=dq, dk=dk, dqi=dqi)
    mod_spec = pl.BlockSpec((1, 1, d), lambda b, i: (b, 0, 0))

    def full(a):
        return pl.BlockSpec(a.shape, lambda b, i: (0,) * a.ndim)

    def out(n):
        return pl.BlockSpec((1, tm, n), lambda b, i: (b, i, 0))

    return pl.pallas_call(
        kern,
        out_shape=[
            jax.ShapeDtypeStruct((bsz, seq, dq), BF16),
            jax.ShapeDtypeStruct((bsz, seq, dk), BF16),
            jax.ShapeDtypeStruct((bsz, seq, dk), BF16),
            jax.ShapeDtypeStruct((bsz, seq, dqi), BF16),
            jax.ShapeDtypeStruct((bsz, seq, LANES), F32),
        ],
        grid=(bsz, seq // tm),
        in_specs=[
            pl.BlockSpec((1, tm, d), lambda b, i: (b, i, 0)),
            pl.BlockSpec((1, d), lambda b, i: (0, 0)),
            mod_spec, mod_spec,
            full(w_main), full(w_small), full(qg), full(kg),
            full(hsq), full(heq), full(hsk), full(hek),
        ],
        out_specs=[out(dq), out(dk), out(dk), out(dqi), out(LANES)],
        compiler_params=_cparams("parallel", "parallel"),
        name="attn_qkv",
    )(x, g, sh, sc, w_main, w_small, qg, kg, hsq, heq, hsk, hek)


def _rel_bucket_np(dist):
    max_exact = REL_BUCKETS // 2
    d = np.maximum(dist, 1).astype(np.float64)
    large = max_exact + (np.log(d / max_exact) / math.log(REL_MAX_DIST / max_exact)
                         * (REL_BUCKETS - max_exact)).astype(np.int32)
    large = np.minimum(large, REL_BUCKETS - 1)
    return np.where(dist < max_exact, dist, large).astype(np.int32)


def _bias_table_kernel(bucket_ref, rb_ref, o_ref):
    hd = pl.program_id(0)
    bucket = bucket_ref[...]
    acc = jnp.zeros(bucket.shape, F32)
    for b in range(REL_BUCKETS):
        acc = jnp.where(bucket == b, rb_ref[b, hd], acc)
    o_ref[0] = acc


def _bias_table(rel_bias):
    w = np.arange(BIAS_W)[:, None]
    i = np.arange(ATT_TQ)[None, :]
    bucket = jnp.asarray(_rel_bucket_np(np.maximum(i - w + BIAS_C, 0)))
    return pl.pallas_call(
        _bias_table_kernel,
        out_shape=jax.ShapeDtypeStruct((N_HEADS, BIAS_W, ATT_TQ), F32),
        grid=(N_HEADS,),
        in_specs=[
            pl.BlockSpec((BIAS_W, ATT_TQ), lambda hd: (0, 0)),
            pl.BlockSpec(memory_space=pltpu.SMEM),
        ],
        out_specs=pl.BlockSpec((1, BIAS_W, ATT_TQ), lambda hd: (hd, 0, 0)),
        compiler_params=_cparams("arbitrary"),
        name="attn_bias_table",
    )(bucket, rel_bias.astype(F32))


def _attn_kernel(qT_ref, qiT_ref, wiT_ref, k_ref, vT_ref, ki_ref, pt_ref, o_ref,
                 keys_scr, negm_scr, pidx_scr, oT_scr, acc_scr, qall_scr, sa_scr, sb_scr, *, top_k):
    tq, kc = ATT_TQ, ATT_KC
    qt = pl.program_id(1)
    q0 = qt * tq
    nch = (q0 + tq + kc - 1) // kc
    tpos = q0 + lax.broadcasted_iota(I32, (kc, tq), 1)
    srow = lax.broadcasted_iota(I32, (kc, tq), 0)

    qiT = qiT_ref[0]
    qi_all = jnp.concatenate([qiT[hd * IDX_DIM:(hd + 1) * IDX_DIM, :] for hd in range(IDX_HEADS)], axis=1)
    wiT = wiT_ref[0]

    def score_chunk(c):
        ks = pl.multiple_of(c * kc, kc)
        kic = ki_ref[0, pl.ds(ks, kc), :]
        dots = jnp.dot(kic, qi_all, preferred_element_type=F32)
        acc = jnp.zeros((kc, tq), F32)
        for hd in range(IDX_HEADS):
            acc = acc + jnp.maximum(dots[:, hd * tq:(hd + 1) * tq], 0.0) * wiT[hd:hd + 1, :]
        acc = jnp.where(acc == 0.0, 0.0, acc)
        bits = pltpu.bitcast(acc, I32)
        key = jnp.where(bits < 0, bits ^ INT_MAX, bits)
        key = jnp.where(ks + srow <= tpos, key, INT_MIN)
        keys_scr[pl.ds(ks, kc), :] = key

    npair = (nch + 1) // 2

    def score_pair(i, carry):
        score_chunk(2 * i)
        score_chunk(2 * i + 1)
        return carry

    lax.fori_loop(0, npair, score_pair, 0)

    @pl.when(nch % 2 == 1)
    def _():
        negm_scr[pl.ds(pl.multiple_of(nch * kc, kc), kc), :] = jnp.full((kc, tq), NEG_BIG, F32)

    srow2 = lax.broadcasted_iota(I32, (2 * kc, tq), 0)

    def count(pred):
        def body(c, acc):
            ks = pl.multiple_of(c * 2 * kc, 2 * kc)
            m = pred(keys_scr[pl.ds(ks, 2 * kc), :], ks + srow2).astype(I32)
            return acc + jnp.sum(m.reshape(2 * kc // SUBLANES, SUBLANES, tq), axis=0)
        acc = lax.fori_loop(0, npair, body, jnp.zeros((SUBLANES, tq), I32))
        return jnp.sum(acc, axis=0, keepdims=True)

    def bit_body(it, p):
        cand_p = p | lax.shift_left(jnp.int32(1), 31 - it)
        cand = cand_p ^ INT_MIN
        cnt = count(lambda k, s: k >= cand)
        return jnp.where(cnt >= top_k, cand_p, p)

    p_fin = lax.fori_loop(0, 32, bit_body, jnp.zeros((1, tq), I32))
    v = p_fin ^ INT_MIN

    cnt_gt = count(lambda k, s: k > v)
    cnt_eq = count(lambda k, s: k == v)
    need = top_k - cnt_gt
    pidx_scr[...] = jnp.full((1, tq), INT_MAX, I32)

    @pl.when(jnp.max(cnt_eq - need) > 0)
    def _():
        def ibit(it, p):
            cand = p | lax.shift_left(jnp.int32(1), 11 - it)
            cnt = count(lambda k, s: jnp.logical_and(k == v, s < cand))
            return jnp.where(cnt < need, cand, p)
        pidx_scr[...] = lax.fori_loop(0, 12, ibit, jnp.zeros((1, tq), I32))

    pidx = pidx_scr[...]

    def mask_chunk(c, carry):
        ks = pl.multiple_of(c * kc, kc)
        k = keys_scr[pl.ds(ks, kc), :]
        spos = ks + srow
        sel = jnp.logical_or(k > v, jnp.logical_and(k == v, spos <= pidx))
        sel = jnp.logical_and(sel, spos <= tpos)
        negm_scr[pl.ds(ks, kc), :] = jnp.where(sel, 0.0, NEG_BIG)
        return carry

    lax.fori_loop(0, nch, mask_chunk, 0)

    acc_scr[...] = jnp.zeros(acc_scr.shape, F32)
    for n in range(N_KV_HEADS):
        r0 = n * N_REP * HEAD_DIM
        qall_scr[n] = jnp.concatenate(
            [qT_ref[0, r0 + g * HEAD_DIM:r0 + (g + 1) * HEAD_DIM, :] for g in range(N_REP)], axis=1)

    def qk_chunk(c, s_ref):
        ks = pl.multiple_of(c * kc, kc)
        for n in range(N_KV_HEADS):
            s_ref[n] = jnp.dot(k_ref[0, n, pl.ds(ks, kc), :], qall_scr[n], preferred_element_type=F32)

    def softmax_pv(c, s_ref, ms, far):
        ks = pl.multiple_of(c * kc, kc)
        negm = negm_scr[pl.ds(ks, kc), :]
        w0 = pl.multiple_of(jnp.clip(BIAS_C - (q0 - ks), 0, BIAS_C), LANES)
        new_ms = []
        for n in range(N_KV_HEADS):
            s = s_ref[n]
            if far:
                cvec = jnp.concatenate([pt_ref[n * N_REP + g, 0:1, :] for g in range(N_REP)], axis=1)
                lg = jnp.concatenate([s[:, g * tq:(g + 1) * tq] + negm for g in range(N_REP)], axis=1)
                m_new = jnp.maximum(ms[n], jnp.max(lg, axis=0, keepdims=True) + cvec)
                p = jnp.exp(lg - (m_new - cvec))
            else:
                lg = jnp.concatenate(
                    [s[:, g * tq:(g + 1) * tq] + pt_ref[n * N_REP + g, pl.ds(w0, kc), :] + negm
                     for g in range(N_REP)], axis=1)
                m_new = jnp.maximum(ms[n], jnp.max(lg, axis=0, keepdims=True))
                p = jnp.exp(lg - m_new)
            alpha = jnp.exp(ms[n] - m_new)
            acc_scr[n] = alpha * acc_scr[n] + jnp.dot(vT_ref[0, c, n], p.astype(BF16),
                                                      preferred_element_type=F32)
            new_ms.append(m_new)
        return tuple(new_ms)

    last_chunk = k_ref.shape[2] // kc - 1

    def pair_step(i, ms, far):
        c0 = 2 * i
        qk_chunk(c0 + 1, sb_scr)
        ms = softmax_pv(c0, sa_scr, ms, far)
        qk_chunk(jnp.minimum(c0 + 2, last_chunk), sa_scr)
        return softmax_pv(c0 + 1, sb_scr, ms, far)

    n_far = jnp.clip((q0 - BIAS_C + kc) // kc, 0, nch)
    ms = tuple(jnp.full((1, N_REP * tq), NEG_BIG, F32) for _ in range(N_KV_HEADS))
    qk_chunk(0, sa_scr)
    ms = lax.fori_loop(0, n_far // 2, functools.partial(pair_step, far=True), ms)
    lax.fori_loop(n_far // 2, npair, functools.partial(pair_step, far=False), ms)
    for n in range(N_KV_HEADS):
        o_t = acc_scr[n, 0:HEAD_DIM, :] / acc_scr[n, HEAD_DIM:HEAD_DIM + 1, :]
        for g in range(N_REP):
            r0 = (n * N_REP + g) * HEAD_DIM
            oT_scr[r0:r0 + HEAD_DIM, :] = o_t[:, g * tq:(g + 1) * tq]

    o_ref[0] = oT_scr[...].T.astype(BF16)


def _attention(qT, qiT, wiT, k4, vT, ki, ptab, top_k):
    bsz, dq, seq = qT.shape
    tq = ATT_TQ
    kern = functools.partial(_attn_kernel, top_k=top_k)
    return pl.pallas_call(
        kern,
        out_shape=jax.ShapeDtypeStruct((bsz, seq, dq), BF16),
        grid=(bsz, seq // tq),
        in_specs=[
            pl.BlockSpec((1, dq, tq), lambda b, i: (b, 0, i)),
            pl.BlockSpec((1, qiT.shape[1], tq), lambda b, i: (b, 0, i)),
            pl.BlockSpec((1, wiT.shape[1], tq), lambda b, i: (b, 0, i)),
            pl.BlockSpec((1,) + k4.shape[1:], lambda b, i: (b, 0, 0, 0)),
            pl.BlockSpec((1,) + vT.shape[1:], lambda b, i: (b, 0, 0, 0, 0)),
            pl.BlockSpec((1,) + ki.shape[1:], lambda b, i: (b, 0, 0)),
            pl.BlockSpec(ptab.shape, lambda b, i: (0, 0, 0)),
        ],
        out_specs=pl.BlockSpec((1, tq, dq), lambda b, i: (b, i, 0)),
        scratch_shapes=[
            pltpu.VMEM((seq, tq), I32),
            pltpu.VMEM((seq, tq), F32),
            pltpu.VMEM((1, tq), I32),
            pltpu.VMEM((dq, tq), F32),
            pltpu.VMEM((N_KV_HEADS, ATT_VROWS, N_REP * tq), F32),
            pltpu.VMEM((N_KV_HEADS, HEAD_DIM, N_REP * tq), BF16),
            pltpu.VMEM((N_KV_HEADS, ATT_KC, N_REP * tq), F32),
            pltpu.VMEM((N_KV_HEADS, ATT_KC, N_REP * tq), F32),
        ],
        compiler_params=_cparams("parallel", "arbitrary"),
        name="attn_core",
    )(qT, qiT, wiT, k4, vT, ki, ptab)


def _proj_res_kernel(a_ref, x_ref, gate_ref, w_ref, o_ref):
    y = jnp.dot(a_ref[0], w_ref[...], preferred_element_type=F32)
    o_ref[0] = x_ref[0] + gate_ref[0] * y


def _proj_residual(a, x, gate, w, tm=512):
    bsz, seq, d = x.shape
    tm = min(tm, seq)
    return pl.pallas_call(
        _proj_res_kernel,
        out_shape=jax.ShapeDtypeStruct((bsz, seq, d), F32),
        grid=(bsz, seq // tm),
        in_specs=[
            pl.BlockSpec((1, tm, a.shape[2]), lambda b, i: (b, i, 0)),
            pl.BlockSpec((1, tm, d), lambda b, i: (b, i, 0)),
            pl.BlockSpec((1, 1, d), lambda b, i: (b, 0, 0)),
            pl.BlockSpec(w.shape, lambda b, i: (0, 0)),
        ],
        out_specs=pl.BlockSpec((1, tm, d), lambda b, i: (b, i, 0)),
        compiler_params=_cparams("parallel", "parallel"),
        name="proj_residual",
    )(a, x, gate, w)


def _attn_layer(x, g, sh, sc, gate, w_in, q_gain, k_gain, w_out, rel_bias):
    bsz, seq, d = x.shape
    top_k = min(TOPK_MAX, seq // 4)
    dq = N_HEADS * HEAD_DIM
    dk = N_KV_HEADS * HEAD_DIM
    dqi = IDX_HEADS * IDX_DIM
    nmain = dq + 2 * dk + dqi
    w_main = w_in[:, :nmain].astype(BF16)
    w_small = jnp.pad(w_in[:, nmain:], ((0, 0), (0, LANES - (IDX_DIM + IDX_HEADS)))).astype(BF16)
    q, k, v, qi, kiwi = _qkv_proj(x, g, sh, sc, w_main, w_small, q_gain, k_gain)
    ki = kiwi[:, :, :IDX_DIM].astype(BF16)
    wi = kiwi[:, :, IDX_DIM:IDX_DIM + IDX_HEADS] * (IDX_HEADS ** -0.5 * IDX_DIM ** -0.5)
    qT = jnp.swapaxes(q, 1, 2)
    qiT = jnp.swapaxes(qi, 1, 2)
    wiT = jnp.swapaxes(wi, 1, 2)
    nck = seq // ATT_KC
    vT = jnp.swapaxes(v.reshape(bsz, nck, ATT_KC, dk), 2, 3).reshape(bsz, nck, N_KV_HEADS, HEAD_DIM, ATT_KC)
    vT = jnp.concatenate([
        vT, jnp.ones((bsz, nck, N_KV_HEADS, 1, ATT_KC), BF16),
        jnp.zeros((bsz, nck, N_KV_HEADS, ATT_VROWS - HEAD_DIM - 1, ATT_KC), BF16)], axis=3)
    k4 = jnp.swapaxes(k.reshape(bsz, seq, N_KV_HEADS, HEAD_DIM), 1, 2)
    ptab = _bias_table(rel_bias)
    attn = _attention(qT, qiT, wiT, k4, vT, ki, ptab, top_k)
    return _proj_residual(attn, x, gate, w_out.astype(BF16))


def _s5_prep_kernel(lre_ref, lim_ref, ls_ref, bre_ref, bim_ref, cre_ref, cim_ref,
                    kmat_ref, bcre_ref, bcim_ref, ccre_ref, ccim_ref, are_ref, aim_ref):
    t_len = SSM_CHUNK
    lre = jnp.minimum(lre_ref[0], -1e-4)
    lim = lim_ref[0]
    step = jnp.exp(ls_ref[0])
    ar = lre * step
    ai = lim * step

    def powers(jv):
        mag = jnp.exp(jv * ar)
        return mag * jnp.cos(jv * ai), mag * jnp.sin(jv * ai)

    lb_re, lb_im = powers(1.0)
    nr = lb_re - 1.0
    ni = lb_im
    den = lre * lre + lim * lim
    cf_re = (nr * lre + ni * lim) / den
    cf_im = (ni * lre - nr * lim) / den
    bre = bre_ref[0]
    bim = bim_ref[0]
    bb_re = cf_re * bre - cf_im * bim
    bb_im = cf_re * bim + cf_im * bre
    cre = cre_ref[0]
    cim = cim_ref[0]
    nst = lre.shape[-1]
    jv = lax.broadcasted_iota(I32, (t_len, 1, nst), 0).astype(F32)
    pj_re, pj_im = powers(jv)
    a_re = (cre[None] * pj_re - cim[None] * pj_im).reshape(t_len * SSM_GROUP, nst)
    a_im = (cre[None] * pj_im + cim[None] * pj_re).reshape(t_len * SSM_GROUP, nst)
    dn = (((1,), (1,)), ((), ()))
    kmat_ref[0] = (lax.dot_general(a_re, bb_re, dn, preferred_element_type=F32, precision=HIGHEST)
                   - lax.dot_general(a_im, bb_im, dn, preferred_element_type=F32, precision=HIGHEST))
    pr_re, pr_im = powers((t_len - 1.0) - jv)
    bcre_ref[0] = (bb_re[None] * pr_re - bb_im[None] * pr_im).reshape(t_len * SSM_GROUP, nst)
    bcim_ref[0] = (bb_re[None] * pr_im + bb_im[None] * pr_re).reshape(t_len * SSM_GROUP, nst)
    pn_re, pn_im = powers(jv + 1.0)
    ccre_ref[0] = (cre[None] * pn_re - cim[None] * pn_im).reshape(t_len * SSM_GROUP, nst)
    ccim_ref[0] = (cre[None] * pn_im + cim[None] * pn_re).reshape(t_len * SSM_GROUP, nst)
    at_re, at_im = powers(float(t_len))
    are_ref[0] = at_re
    aim_ref[0] = at_im


def _s5_prep(lam_re, lam_im, log_step, b_re, b_im, c_re, c_im):
    ng, nst = lam_re.shape
    tc = SSM_CHUNK * SSM_GROUP
    vec = pl.BlockSpec((1, 1, nst), lambda gi: (gi, 0, 0))
    mat = pl.BlockSpec((1, SSM_GROUP, nst), lambda gi: (gi, 0, 0))
    big = pl.BlockSpec((1, tc, nst), lambda gi: (gi, 0, 0))
    return pl.pallas_call(
        _s5_prep_kernel,
        out_shape=[
            jax.ShapeDtypeStruct((ng, tc, SSM_GROUP), F32),
            jax.ShapeDtypeStruct((ng, tc, nst), F32),
            jax.ShapeDtypeStruct((ng, tc, nst), F32),
            jax.ShapeDtypeStruct((ng, tc, nst), F32),
            jax.ShapeDtypeStruct((ng, tc, nst), F32),
            jax.ShapeDtypeStruct((ng, 1, nst), F32),
            jax.ShapeDtypeStruct((ng, 1, nst), F32),
        ],
        grid=(ng,),
        in_specs=[vec, vec, pl.BlockSpec((1, 1, 1), lambda gi: (gi, 0, 0)), mat, mat, mat, mat],
        out_specs=[pl.BlockSpec((1, tc, SSM_GROUP), lambda gi: (gi, 0, 0)), big, big, big, big, vec, vec],
        compiler_params=_cparams("parallel"),
        name="s5_prep",
    )(lam_re.reshape(ng, 1, nst), lam_im.reshape(ng, 1, nst), log_step.reshape(ng, 1, 1),
      jnp.swapaxes(b_re, 1, 2), jnp.swapaxes(b_im, 1, 2), c_re, c_im)


def _s5_pre_kernel(x_ref, g_ref, sh_ref, sc_ref, o_ref):
    h = _norm_mod(x_ref[0], g_ref[...], sh_ref[0], sc_ref[0]).astype(BF16)
    for q in range(o_ref.shape[0]):
        o_ref[q] = h[:, q * LANES:(q + 1) * LANES]


def _s5_pre(x, g, sh, sc, tm=512):
    bsz, seq, d = x.shape
    tm = min(tm, seq)
    nt = seq // tm
    nq = d // LANES
    mod_spec = pl.BlockSpec((1, 1, d), lambda b, i: (b, 0, 0))
    return pl.pallas_call(
        _s5_pre_kernel,
        out_shape=jax.ShapeDtypeStruct((nq, bsz * seq, LANES), BF16),
        grid=(bsz, nt),
        in_specs=[pl.BlockSpec((1, tm, d), lambda b, i: (b, i, 0)),
                  pl.BlockSpec((1, d), lambda b, i: (0, 0)), mod_spec, mod_spec],
        out_specs=pl.BlockSpec((nq, tm, LANES), lambda b, i: (0, b * nt + i, 0)),
        compiler_params=_cparams("parallel", "parallel"),
        name="s5_pre",
    )(x, g, sh, sc)


def _s5_scan_kernel(x_ref, bc_ref, mt_ref, cc_ref, are_ref, aim_ref, y_ref, re_scr, im_scr):
    x = x_ref[0]
    nrow = x.shape[0]
    pad = re_scr.shape[0] - nrow
    v = jnp.dot(x, bc_ref[0], preferred_element_type=F32)
    nh = v.shape[1] // 2
    s_re = v[:, :nh]
    s_im = v[:, nh:]
    a_re = are_ref[0]
    a_im = aim_ref[0]
    re_scr[0:pad, :] = jnp.zeros((pad, nh), F32)
    im_scr[0:pad, :] = jnp.zeros((pad, nh), F32)

    def shifted(scr, val, dist):
        scr[pad:pad + nrow, :] = val
        return scr[pad - dist:pad - dist + nrow, :]

    dist = 1
    while dist < nrow:
        sh_re = shifted(re_scr, s_re, dist)
        sh_im = shifted(im_scr, s_im, dist)
        s_re, s_im = (s_re + a_re * sh_re - a_im * sh_im, s_im + a_re * sh_im + a_im * sh_re)
        a_re, a_im = (a_re * a_re - a_im * a_im, 2.0 * a_re * a_im)
        dist *= 2
    sp = jnp.concatenate([shifted(re_scr, s_re, 1), shifted(im_scr, s_im, 1)], axis=1).astype(BF16)
    y_ref[0] = (jnp.dot(x, mt_ref[0], preferred_element_type=F32)
                + jnp.dot(sp, cc_ref[0], preferred_element_type=F32))


def _s5_scan(hq, bcq, mtq, ccq, a_re, a_im, bsz):
    nq, nrows, width = hq.shape
    nj = nrows // bsz
    nh = a_re.shape[-1]
    pad = max(nj // 2, SUBLANES)
    tile = pl.BlockSpec((1, nj, width), lambda q, b: (q, b, 0))
    wspec = lambda a: pl.BlockSpec((1,) + a.shape[1:], lambda q, b: (q, 0, 0))
    return pl.pallas_call(
        _s5_scan_kernel,
        out_shape=jax.ShapeDtypeStruct((nq, nrows, width), F32),
        grid=(nq, bsz),
        in_specs=[tile, wspec(bcq), wspec(mtq), wspec(ccq), wspec(a_re), wspec(a_im)],
        out_specs=tile,
        scratch_shapes=[pltpu.VMEM((pad + nj, nh), F32), pltpu.VMEM((pad + nj, nh), F32)],
        compiler_params=_cparams("parallel", "parallel"),
        name="s5_scan",
    )(hq, bcq, mtq, ccq, a_re, a_im)


def _s5_post_kernel(x_ref, y_ref, g_ref, sh_ref, sc_ref, gate_ref, dsk_ref, w_ref, o_ref, *, d):
    x = x_ref[0]
    h = _norm_mod(x, g_ref[...], sh_ref[0], sc_ref[0])
    y = jnp.concatenate([y_ref[q] for q in range(y_ref.shape[0])], axis=1)
    yy = y + dsk_ref[...] * h
    gl = jax.nn.gelu(yy).astype(BF16)
    z = jnp.dot(gl, w_ref[...], preferred_element_type=F32)
    o_ref[0] = x + gate_ref[0] * (z[:, :d] * jax.nn.sigmoid(z[:, d:]))


def _s5_post(x, yq, g, sh, sc, gate, d_skip, w_glu, tm=512):
    bsz, seq, d = x.shape
    tm = min(tm, seq)
    nt = seq // tm
    nq = d // LANES
    kern = functools.partial(_s5_post_kernel, d=d)
    mod_spec = pl.BlockSpec((1, 1, d), lambda b, i: (b, 0, 0))
    tile = pl.BlockSpec((1, tm, d), lambda b, i: (b, i, 0))
    return pl.pallas_call(
        kern,
        out_shape=jax.ShapeDtypeStruct((bsz, seq, d), F32),
        grid=(bsz, nt),
        in_specs=[tile, pl.BlockSpec((nq, tm, LANES), lambda b, i: (0, b * nt + i, 0)),
                  pl.BlockSpec((1, d), lambda b, i: (0, 0)), mod_spec, mod_spec, mod_spec,
                  pl.BlockSpec((1, d), lambda b, i: (0, 0)),
                  pl.BlockSpec(w_glu.shape, lambda b, i: (0, 0))],
        out_specs=tile,
        compiler_params=_cparams("parallel", "parallel"),
        name="s5_post",
    )(x, yq, g, sh, sc, gate, d_skip, w_glu)


def _s5_layer(x, g, sh, sc, gate, lam_re, lam_im, log_step, b_re, b_im, c_re, c_im, d_skip, w_glu):
    bsz, seq, d = x.shape
    ng, nst = lam_re.shape
    t_len, cg = SSM_CHUNK, SSM_GROUP
    nq = d // LANES
    gq = LANES // cg
    nj = seq // t_len
    kmat, bc_re, bc_im, cc_re, cc_im, a_re, a_im = _s5_prep(lam_re, lam_im, log_step, b_re, b_im, c_re, c_im)
    eye = jnp.eye(gq, dtype=F32)

    def state_in(bc):
        return jnp.einsum('qgtcp,gh->qtgchp', bc.reshape(nq, gq, t_len, cg, nst), eye)

    bcq = jnp.stack([state_in(bc_re), state_in(bc_im)], axis=4)
    bcq = bcq.reshape(nq, t_len * LANES, 2 * gq * nst).astype(BF16)
    tt = np.arange(t_len)
    lag = tt[:, None] - tt[None, :]
    k4 = kmat.reshape(ng, t_len, cg, cg)[:, np.maximum(lag, 0)]
    k4 = jnp.where(jnp.asarray(lag >= 0)[None, :, :, None, None], k4, 0.0)
    mtq = jnp.einsum('qgtscd,gh->qshdtgc', k4.reshape(nq, gq, t_len, t_len, cg, cg), eye)
    mtq = mtq.reshape(nq, t_len * LANES, t_len * LANES).astype(BF16)

    def state_out(cc):
        return jnp.einsum('qgtcp,gh->qhptgc', cc.reshape(nq, gq, t_len, cg, nst), eye)

    ccq = jnp.stack([state_out(cc_re), -state_out(cc_im)], axis=1)
    ccq = ccq.reshape(nq, 2 * gq * nst, t_len * LANES).astype(BF16)
    aq_re = a_re.reshape(nq, 1, gq * nst)
    aq_im = a_im.reshape(nq, 1, gq * nst)

    hq = _s5_pre(x, g, sh, sc).reshape(nq, bsz * nj, t_len * LANES)
    yq = _s5_scan(hq, bcq, mtq, ccq, aq_re, aq_im, bsz).reshape(nq, bsz * seq, LANES)
    return _s5_post(x, yq, g, sh, sc, gate, d_skip.reshape(1, d).astype(F32), w_glu.astype(BF16))


def kernel(x, c, ada_w, ada_b, norm_g, conv_w_in, conv_w, conv_w_out, attn_w_in, attn_q_gain, attn_k_gain, attn_w_out, rel_bias, ssm_lambda_re, ssm_lambda_im, ssm_log_step, ssm_b_re, ssm_b_im, ssm_c_re, ssm_c_im, ssm_d, ssm_w_glu, ffn_w_gu, ffn_w_down, moe_router_w, moe_router_b, moe_w_gu, moe_w_down):
    bsz, seq, d = x.shape
    depth = ada_w.shape[0]
    mod = _ada_mod(c, ada_w, ada_b).reshape(depth, bsz, 6, 1, d)
    for i in range(depth):
        sh1, sc1, g1, sh2, sc2, g2 = (mod[i, :, r] for r in range(6))
        gn1 = norm_g[i, 0].reshape(1, d)
        gn2 = norm_g[i, 1].reshape(1, d)
        j = i // N_MIXERS
        if i % N_MIXERS == 0:
            x = _conv_layer(x, gn1, sh1, sc1, g1, conv_w_in[j].astype(BF16), conv_w[j],
                            conv_w_out[j].astype(BF16))
        elif i % N_MIXERS == 1:
            x = _attn_layer(x, gn1, sh1, sc1, g1, attn_w_in[j], attn_q_gain[j], attn_k_gain[j],
                            attn_w_out[j], rel_bias)
        else:
            x = _s5_layer(x, gn1, sh1, sc1, g1, ssm_lambda_re[j], ssm_lambda_im[j], ssm_log_step[j],
                          ssm_b_re[j], ssm_b_im[j], ssm_c_re[j], ssm_c_im[j], ssm_d[j], ssm_w_glu[j])
        if i % 2 == 0:
            x = _ffn_layer(x, gn2, sh2, sc2, g2, ffn_w_gu[i // 2].astype(BF16), ffn_w_down[i // 2].astype(BF16))
        else:
            x = _moe_layer(x, gn2, sh2, sc2, g2, moe_router_w[i // 2], moe_router_b[i // 2],
                           moe_w_gu[i // 2].astype(BF16), moe_w_down[i // 2].astype(BF16))
    return x
```

```python
import functools
import math

import numpy as np
import jax
import jax.numpy as jnp
from jax import lax
from jax.experimental import pallas as pl
from jax.experimental.pallas import tpu as pltpu

F32 = jnp.float32
BF16 = jnp.bfloat16
I32 = jnp.int32
HIGHEST = lax.Precision.HIGHEST

DEPTH = 4
N_MIXERS = 3
EPS = 1e-6
CONV_WIDTH = 3
N_HEADS = 16
N_KV_HEADS = 4
N_REP = N_HEADS // N_KV_HEADS
HEAD_DIM = 64
IDX_HEADS = 8
IDX_DIM = 64
TOPK_MAX = 256
REL_BUCKETS = 32
REL_MAX_DIST = 128
SSM_GROUP = 16
SSM_STATE = 64
N_EXPERTS = 8
TOP_K_EXPERTS = 2

VMEM_LIMIT_BYTES = 56 * 1024 * 1024
LANES = 128
SUBLANES = 8

INT_MIN = -(2 ** 31)
INT_MAX = 2 ** 31 - 1
NEG_BIG = -1e30
LOG2E = 1.4426950408889634

ATT_TQ = 128
ATT_KC = 256
BIAS_C = 384
BIAS_W = BIAS_C + ATT_KC
ATT_VROWS = 80

SSM_CHUNK = 8


def _cparams(*sem):
    return pltpu.CompilerParams(dimension_semantics=sem, vmem_limit_bytes=VMEM_LIMIT_BYTES)


def _norm_mod(x, g, shift, scale):
    ms = jnp.mean(x * x, axis=-1, keepdims=True)
    y = x * lax.rsqrt(ms + EPS)
    return (y * g) * (1.0 + scale) + shift


def _silu(x):
    return x * jax.nn.sigmoid(x)


def _ada_kernel(c_ref, w_ref, b_ref, o_ref):
    c = c_ref[...]
    cond = _silu(c)
    o_ref[0] = jnp.dot(cond, w_ref[0], preferred_element_type=F32, precision=HIGHEST) + b_ref[0]


def _ada_mod(c, ada_w, ada_b):
    depth, d, d6 = ada_w.shape
    bsz = c.shape[0]
    tn = d
    return pl.pallas_call(
        _ada_kernel,
        out_shape=jax.ShapeDtypeStruct((depth, bsz, d6), F32),
        grid=(depth, d6 // tn),
        in_specs=[
            pl.BlockSpec((bsz, d), lambda i, j: (0, 0)),
            pl.BlockSpec((1, d, tn), lambda i, j: (i, 0, j)),
            pl.BlockSpec((1, 1, tn), lambda i, j: (i, 0, j)),
        ],
        out_specs=pl.BlockSpec((1, bsz, tn), lambda i, j: (i, 0, j)),
        compiler_params=_cparams("parallel", "parallel"),
        name="ada_mod",
    )(c, ada_w, ada_b.reshape(depth, 1, d6))


def _conv_kernel(x_ref, xh_ref, g_ref, sh_ref, sc_ref, gate_ref, win_ref, wc_ref, wout_ref,
                 o_ref, u_scr, *, tm, d):
    i = pl.program_id(1)
    g = g_ref[...]
    sh = sh_ref[0]
    sc = sc_ref[0]
    x = x_ref[0]
    h = _norm_mod(x, g, sh, sc).astype(BF16)
    z = jnp.dot(h, win_ref[...], preferred_element_type=F32)
    b_gate = z[:, :d]
    u = z[:, d:2 * d] * z[:, 2 * d:]
    hh = _norm_mod(xh_ref[0], g, sh, sc).astype(BF16)
    zh = jnp.dot(hh, win_ref[:, d:], preferred_element_type=F32)
    uh = zh[:, :d] * zh[:, d:]
    uh = jnp.where(i > 0, uh, 0.0)
    u_scr[0:SUBLANES, :] = uh
    u_scr[SUBLANES:SUBLANES + tm, :] = u
    wc = wc_ref[...]
    conv = (wc[0:1, :] * u_scr[SUBLANES - 2:SUBLANES - 2 + tm, :]
            + wc[1:2, :] * u_scr[SUBLANES - 1:SUBLANES - 1 + tm, :]
            + wc[2:3, :] * u)
    y = jnp.dot((b_gate * conv).astype(BF16), wout_ref[...], preferred_element_type=F32)
    o_ref[0] = x + gate_ref[0] * y


def _conv_layer(x, g, sh, sc, gate, w_in, w_conv, w_out, tm=512):
    bsz, seq, d = x.shape
    tm = min(tm, seq)
    nt = seq // tm
    hb = tm // SUBLANES
    kern = functools.partial(_conv_kernel, tm=tm, d=d)
    mod_spec = pl.BlockSpec((1, 1, d), lambda b, i: (b, 0, 0))
    return pl.pallas_call(
        kern,
        out_shape=jax.ShapeDtypeStruct((bsz, seq, d), F32),
        grid=(bsz, nt),
        in_specs=[
            pl.BlockSpec((1, tm, d), lambda b, i: (b, i, 0)),
            pl.BlockSpec((1, SUBLANES, d), lambda b, i: (b, jnp.maximum(i * hb - 1, 0), 0)),
            pl.BlockSpec((1, d), lambda b, i: (0, 0)),
            mod_spec, mod_spec, mod_spec,
            pl.BlockSpec((d, 3 * d), lambda b, i: (0, 0)),
            pl.BlockSpec((CONV_WIDTH, d), lambda b, i: (0, 0)),
            pl.BlockSpec((d, d), lambda b, i: (0, 0)),
        ],
        out_specs=pl.BlockSpec((1, tm, d), lambda b, i: (b, i, 0)),
        scratch_shapes=[pltpu.VMEM((tm + SUBLANES, d), F32)],
        compiler_params=_cparams("parallel", "parallel"),
        name="conv_mixer",
    )(x, x, g, sh, sc, gate, w_in, w_conv, w_out)


def _ffn_kernel(x_ref, g_ref, sh_ref, sc_ref, gate_ref, wgu_ref, wd_ref, o_ref, *, dff, nchunk):
    x = x_ref[0]
    h = _norm_mod(x, g_ref[...], sh_ref[0], sc_ref[0]).astype(BF16)
    cols = dff // nchunk
    acc = jnp.zeros(x.shape, F32)
    for c in range(nchunk):
        gg = jnp.dot(h, wgu_ref[:, c * cols:(c + 1) * cols], preferred_element_type=F32)
        uu = jnp.dot(h, wgu_ref[:, dff + c * cols:dff + (c + 1) * cols], preferred_element_type=F32)
        a = (_silu(gg) * uu).astype(BF16)
        acc = acc + jnp.dot(a, wd_ref[c * cols:(c + 1) * cols, :], preferred_element_type=F32)
    o_ref[0] = x + gate_ref[0] * acc


def _ffn_layer(x, g, sh, sc, gate, w_gu, w_down, tm=512):
    bsz, seq, d = x.shape
    dff = w_down.shape[0]
    tm = min(tm, seq)
    kern = functools.partial(_ffn_kernel, dff=dff, nchunk=2)
    mod_spec = pl.BlockSpec((1, 1, d), lambda b, i: (b, 0, 0))
    return pl.pallas_call(
        kern,
        out_shape=jax.ShapeDtypeStruct((bsz, seq, d), F32),
        grid=(bsz, seq // tm),
        in_specs=[
            pl.BlockSpec((1, tm, d), lambda b, i: (b, i, 0)),
            pl.BlockSpec((1, d), lambda b, i: (0, 0)),
            mod_spec, mod_spec, mod_spec,
            pl.BlockSpec((d, 2 * dff), lambda b, i: (0, 0), pipeline_mode=pl.Buffered(1)),
            pl.BlockSpec((dff, d), lambda b, i: (0, 0), pipeline_mode=pl.Buffered(1)),
        ],
        out_specs=pl.BlockSpec((1, tm, d), lambda b, i: (b, i, 0)),
        compiler_params=_cparams("parallel", "parallel"),
        name="ffn_dense",
    )(x, g, sh, sc, gate, w_gu, w_down)


MOE_SB = 512
MOE_RT = 512
MOE_ALIGN = 16
MOE_WIN = 256
MOE_CAP = MOE_WIN - MOE_ALIGN


def _moe_router_kernel(x_ref, g_ref, sh_ref, sc_ref, rw_ref, rb_ref, h_ref, meta_ref):
    hf = _norm_mod(x_ref[...], g_ref[...], sh_ref[0], sc_ref[0])
    h_ref[...] = hf.astype(BF16)
    logits = jnp.dot(hf, rw_ref[...], preferred_element_type=F32, precision=HIGHEST) + rb_ref[...]
    mx = jnp.max(logits, axis=-1, keepdims=True)
    ex = jnp.exp(logits - mx)
    probs = ex / jnp.sum(ex, axis=-1, keepdims=True)
    lane = lax.broadcasted_iota(I32, probs.shape, 1)
    m1 = jnp.max(probs, axis=-1, keepdims=True)
    i1 = jnp.min(jnp.where(probs == m1, lane, LANES), axis=-1, keepdims=True)
    rest = jnp.where(lane == i1, -1.0, probs)
    m2 = jnp.max(rest, axis=-1, keepdims=True)
    i2 = jnp.min(jnp.where(rest == m2, lane, LANES), axis=-1, keepdims=True)
    den = m1 + m2
    gates = jnp.where(lane == i1, m1 / den, 0.0) + jnp.where(lane == i2, m2 / den, 0.0)
    chosen = jnp.where(jnp.logical_or(lane == i1, lane == i2), 1.0, 0.0)
    meta_ref[...] = gates + pltpu.roll(chosen, N_EXPERTS, 1)


def _moe_router(xt, g, sh, sc, rw_pad, rb_pad, tpb, tm):
    n, d = xt.shape
    mod_spec = pl.BlockSpec((1, 1, d), lambda i: (i // tpb, 0, 0))
    return pl.pallas_call(
        _moe_router_kernel,
        out_shape=[jax.ShapeDtypeStruct((n, d), BF16), jax.ShapeDtypeStruct((n, LANES), F32)],
        grid=(n // tm,),
        in_specs=[pl.BlockSpec((tm, d), lambda i: (i, 0)),
                  pl.BlockSpec((1, d), lambda i: (0, 0)), mod_spec, mod_spec,
                  pl.BlockSpec((d, LANES), lambda i: (0, 0)),
                  pl.BlockSpec((1, LANES), lambda i: (0, 0))],
        out_specs=[pl.BlockSpec((tm, d), lambda i: (i, 0)), pl.BlockSpec((tm, LANES), lambda i: (i, 0))],
        compiler_params=_cparams("parallel"),
        name="moe_router",
    )(xt, g, sh, sc, rw_pad, rb_pad)


def _window(start, count, w):
    s = start + w * MOE_CAP
    n = jnp.minimum(count - w * MOE_CAP, MOE_CAP)
    a = pl.multiple_of((s // MOE_ALIGN) * MOE_ALIGN, MOE_ALIGN)
    return s, n, a


def _moe_dispatch_kernel(start_ref, cnt_ref, h_ref, pos_ref, xs_init_ref, xs_ref,
                         buf, carry, sems, pending):
    del xs_init_ref
    b = pl.program_id(0)
    nb = pl.num_programs(0)
    ne = buf.shape[0]

    @pl.when(b == 0)
    def _():
        carry[...] = jnp.zeros(carry.shape, BF16)
        for e in range(ne):
            pending[e] = 0

    h = h_ref[...]
    riota = lax.broadcasted_iota(I32, (MOE_WIN, h.shape[0]), 0)

    def out_copy(e, a):
        return pltpu.make_async_copy(buf.at[e], xs_ref.at[pl.ds(a, MOE_WIN)], sems.at[e])

    for e in range(ne):
        posrow = pos_ref[0, e:e + 1, :]
        nwin = (cnt_ref[b, e] + MOE_CAP - 1) // MOE_CAP

        def wbody(w, carry_unused, e=e, posrow=posrow):
            s, n, a = _window(start_ref[b, e], cnt_ref[b, e], w)

            @pl.when(pending[e] == 1)
            def _():
                out_copy(e, 0).wait()

            hit = jnp.logical_and(posrow - a == riota,
                                  jnp.logical_and(posrow >= s, posrow < s + n))
            onehot = jnp.where(hit, 1.0, 0.0).astype(BF16)
            buf[e] = jnp.dot(onehot, h, preferred_element_type=F32).astype(BF16)
            buf[e, 0:MOE_ALIGN, :] = buf[e, 0:MOE_ALIGN, :] + carry[e]
            c0 = pl.multiple_of(((s + n) // MOE_ALIGN) * MOE_ALIGN - a, MOE_ALIGN)
            carry[e] = buf[e, pl.ds(c0, MOE_ALIGN), :]
            out_copy(e, a).start()
            pending[e] = 1
            return carry_unused

        lax.fori_loop(0, nwin, wbody, 0)

    @pl.when(b == nb - 1)
    def _():
        for e in range(ne):
            @pl.when(pending[e] == 1)
            def _():
                out_copy(e, 0).wait()


def _moe_dispatch(start, cnt, h, pos_t, ncap):
    n, d = h.shape
    nb, ne, sb = pos_t.shape
    xs_init = jnp.zeros((ncap, d), BF16)
    return pl.pallas_call(
        _moe_dispatch_kernel,
        out_shape=jax.ShapeDtypeStruct((ncap, d), BF16),
        grid_spec=pltpu.PrefetchScalarGridSpec(
            num_scalar_prefetch=2,
            grid=(nb,),
            in_specs=[pl.BlockSpec((sb, d), lambda b, *_: (b, 0)),
                      pl.BlockSpec((1, ne, sb), lambda b, *_: (b, 0, 0)),
                      pl.BlockSpec(memory_space=pl.ANY)],
            out_specs=pl.BlockSpec(memory_space=pl.ANY),
            scratch_shapes=[pltpu.VMEM((ne, MOE_WIN, d), BF16),
                            pltpu.VMEM((ne, MOE_ALIGN, d), BF16),
                            pltpu.SemaphoreType.DMA((ne,)),
                            pltpu.SMEM((ne,), I32)],
        ),
        input_output_aliases={4: 0},
        compiler_params=_cparams("arbitrary"),
        name="moe_dispatch",
    )(start, cnt, h, pos_t, xs_init)


def _moe_ffn_kernel(blk_ref, exp_ref, nt_ref, xs_ref, wgu_ref, wd_ref, ys_init_ref, ys_ref, *, dff, nchunk):
    del blk_ref, exp_ref, ys_init_ref
    k = pl.program_id(0)

    @pl.when(k < nt_ref[0])
    def _():
        x = xs_ref[...]
        cols = dff // nchunk
        acc = jnp.zeros(x.shape, F32)
        for c in range(nchunk):
            gg = jnp.dot(x, wgu_ref[0, :, c * cols:(c + 1) * cols], preferred_element_type=F32)
            uu = jnp.dot(x, wgu_ref[0, :, dff + c * cols:dff + (c + 1) * cols], preferred_element_type=F32)
            a = (_silu(gg) * uu).astype(BF16)
            acc = acc + jnp.dot(a, wd_ref[0, c * cols:(c + 1) * cols, :], preferred_element_type=F32)
        ys_ref[...] = acc.astype(BF16)


def _moe_ffn(tile_blk, tile_exp, ntiles, xs, w_gu, w_down, ebase):
    ncap, d = xs.shape
    dff = w_down.shape[1]
    kern = functools.partial(_moe_ffn_kernel, dff=dff, nchunk=2)
    ys_init = jnp.zeros((ncap, d), BF16)
    return pl.pallas_call(
        kern,
        out_shape=jax.ShapeDtypeStruct((ncap, d), BF16),
        grid_spec=pltpu.PrefetchScalarGridSpec(
            num_scalar_prefetch=3,
            grid=(ncap // MOE_RT,),
            in_specs=[pl.BlockSpec((MOE_RT, d), lambda k, blk, exp, nt: (blk[k], 0)),
                      pl.BlockSpec((1, d, 2 * dff), lambda k, blk, exp, nt: (ebase + exp[k], 0, 0)),
                      pl.BlockSpec((1, dff, d), lambda k, blk, exp, nt: (ebase + exp[k], 0, 0)),
                      pl.BlockSpec(memory_space=pl.ANY)],
            out_specs=pl.BlockSpec((MOE_RT, d), lambda k, blk, exp, nt: (blk[k], 0)),
        ),
        input_output_aliases={6: 0},
        compiler_params=_cparams("arbitrary"),
        name="moe_ffn",
    )(tile_blk, tile_exp, ntiles, xs, w_gu, w_down, ys_init)


def _moe_combine_kernel(start_ref, cnt_ref, x_ref, gate_ref, pos_ref, gts_ref, ys_ref, o_ref,
                        buf, sems, acc_scr):
    b = pl.program_id(0)
    ne = buf.shape[0]
    sb = x_ref.shape[0]

    def in_copy(e, a):
        return pltpu.make_async_copy(ys_ref.at[pl.ds(a, MOE_WIN)], buf.at[e], sems.at[e])

    for e in range(ne):
        @pl.when(cnt_ref[b, e] > 0)
        def _():
            _, _, a = _window(start_ref[b, e], cnt_ref[b, e], 0)
            in_copy(e, a).start()

    acc_scr[...] = jnp.zeros(acc_scr.shape, F32)
    liota = lax.broadcasted_iota(I32, (sb, MOE_WIN), 1)
    for e in range(ne):
        poscol = pos_ref[:, e:e + 1]
        gcol = gts_ref[:, e:e + 1]
        nwin = (cnt_ref[b, e] + MOE_CAP - 1) // MOE_CAP

        def wbody(w, carry_unused, e=e, poscol=poscol, gcol=gcol):
            s, n, a = _window(start_ref[b, e], cnt_ref[b, e], w)

            @pl.when(w > 0)
            def _():
                in_copy(e, a).start()

            in_copy(e, a).wait()
            hit = jnp.logical_and(poscol - a == liota,
                                  jnp.logical_and(poscol >= s, poscol < s + n))
            onehot = jnp.where(hit, 1.0, 0.0).astype(BF16)
            acc_scr[...] += gcol * jnp.dot(onehot, buf[e], preferred_element_type=F32)
            return carry_unused

        lax.fori_loop(0, nwin, wbody, 0)

    o_ref[...] = x_ref[...] + gate_ref[0] * acc_scr[...]


def _moe_combine(start, cnt, xt, gate, pos_n, gates_n, ys, tpb):
    n, d = xt.shape
    nb, ne = cnt.shape
    sb = n // nb
    return pl.pallas_call(
        _moe_combine_kernel,
        out_shape=jax.ShapeDtypeStruct((n, d), F32),
        grid_spec=pltpu.PrefetchScalarGridSpec(
            num_scalar_prefetch=2,
            grid=(nb,),
            in_specs=[pl.BlockSpec((sb, d), lambda b, *_: (b, 0)),
                      pl.BlockSpec((1, 1, d), lambda b, *_: (b // tpb, 0, 0)),
                      pl.BlockSpec((sb, ne), lambda b, *_: (b, 0)),
                      pl.BlockSpec((sb, ne), lambda b, *_: (b, 0)),
                      pl.BlockSpec(memory_space=pl.ANY)],
            out_specs=pl.BlockSpec((sb, d), lambda b, *_: (b, 0)),
            scratch_shapes=[pltpu.VMEM((ne, MOE_WIN, d), BF16),
                            pltpu.SemaphoreType.DMA((ne,)),
                            pltpu.VMEM((sb, d), F32)],
        ),
        compiler_params=_cparams("arbitrary"),
        name="moe_combine",
    )(start, cnt, xt, gate, pos_n, gates_n, ys)


def _moe_layer(x, g, sh, sc, gate, router_w, router_b, w_gu, w_down, ebase=0):
    bsz, seq, d = x.shape
    ne = router_w.shape[1]
    n = bsz * seq
    sb = min(MOE_SB, seq)
    tpb = seq // sb
    nb = n // sb
    rt = MOE_RT
    xt = x.reshape(n, d)
    rw_pad = jnp.pad(router_w.astype(F32), ((0, 0), (0, LANES - ne)))
    rb_pad = jnp.pad(router_b.astype(F32).reshape(1, ne), ((0, 0), (0, LANES - ne)), constant_values=NEG_BIG)
    h, meta = _moe_router(xt, g, sh, sc, rw_pad, rb_pad, tpb, sb)
    gates = meta[:, :ne]
    sel = (meta[:, ne:2 * ne] > 0.5).astype(I32)

    selb = sel.reshape(nb, sb, ne)
    cnt = jnp.sum(selb, axis=1)
    rank = jnp.cumsum(selb, axis=1) - selb
    total = jnp.sum(cnt, axis=0)
    region = ((total + MOE_WIN + rt - 1) // rt) * rt
    off = jnp.cumsum(region) - region
    start = (off[None, :] + jnp.cumsum(cnt, axis=0) - cnt).astype(I32)
    pos = jnp.where(selb > 0, start[:, None, :] + rank, -1).astype(I32)
    ncap = 2 * n + ne * (MOE_WIN + rt)
    tiles_e = (total + rt - 1) // rt
    tcum = jnp.cumsum(tiles_e)
    ntiles = tcum[-1]
    kk = jnp.minimum(jnp.arange(ncap // rt), ntiles - 1)
    tile_exp = jnp.searchsorted(tcum, kk, side='right').astype(I32)
    tile_blk = (off[tile_exp] // rt + kk - (tcum - tiles_e)[tile_exp]).astype(I32)

    xs = _moe_dispatch(start, cnt.astype(I32), h, jnp.swapaxes(pos, 1, 2), ncap)
    ys = _moe_ffn(tile_blk, tile_exp, ntiles.reshape(1).astype(I32), xs, w_gu, w_down, ebase)
    out = _moe_combine(start, cnt.astype(I32), xt, gate, pos.reshape(n, ne), gates, ys, tpb)
    return out.reshape(bsz, seq, d)


def _head_norm(q, hsum_ref, hexp_ref, gain, scale):
    ms = jnp.dot((q * q).astype(BF16), hsum_ref[...], preferred_element_type=F32)
    r = lax.rsqrt(ms + EPS)
    r_hi = r.astype(BF16)
    r_lo = (r - r_hi.astype(F32)).astype(BF16)
    rexp = (jnp.dot(r_hi, hexp_ref[...], preferred_element_type=F32)
            + jnp.dot(r_lo, hexp_ref[...], preferred_element_type=F32))
    return q * rexp * (gain * scale)


def _qkv_kernel(x_ref, g_ref, sh_ref, sc_ref, wm_ref, ws_ref, qg_ref, kg_ref,
                hsq_ref, heq_ref, hsk_ref, hek_ref,
                q_ref, k_ref, v_ref, qi_ref, kiwi_ref, *, dq, dk, dqi):
    x = x_ref[0]
    h = _norm_mod(x, g_ref[...], sh_ref[0], sc_ref[0]).astype(BF16)
    z = jnp.dot(h, wm_ref[...], preferred_element_type=F32)
    q = z[:, :dq]
    k = z[:, dq:dq + dk]
    v = z[:, dq + dk:dq + 2 * dk]
    qi = z[:, dq + 2 * dk:dq + 2 * dk + dqi]
    q_ref[0] = _head_norm(q, hsq_ref, heq_ref, qg_ref[...], HEAD_DIM ** -0.5 * LOG2E).astype(BF16)
    k_ref[0] = _head_norm(k, hsk_ref, hek_ref, kg_ref[...], 1.0).astype(BF16)
    v_ref[0] = v.astype(BF16)
    qi_ref[0] = qi.astype(BF16)
    kiwi_ref[0] = jnp.dot(h, ws_ref[...], preferred_element_type=F32)


def _head_indicators(nheads):
    hs = np.zeros((nheads * HEAD_DIM, LANES), np.float32)
    he = np.zeros((LANES, nheads * HEAD_DIM), np.float32)
    for hd in range(nheads):
        hs[hd * HEAD_DIM:(hd + 1) * HEAD_DIM, hd] = 1.0 / HEAD_DIM
        he[hd, hd * HEAD_DIM:(hd + 1) * HEAD_DIM] = 1.0
    return jnp.asarray(hs, BF16), jnp.asarray(he, BF16)


def _qkv_proj(x, g, sh, sc, w_main, w_small, q_gain, k_gain, tm=512):
    bsz, seq, d = x.shape
    tm = min(tm, seq)
    dq = N_HEADS * HEAD_DIM
    dk = N_KV_HEADS * HEAD_DIM
    dqi = IDX_HEADS * IDX_DIM
    hsq, heq = _head_indicators(N_HEADS)
    hsk, hek = _head_indicators(N_KV_HEADS)
    qg = jnp.tile(q_gain.reshape(1, HEAD_DIM), (1, N_HEADS)).astype(F32)
    kg = jnp.tile(k_gain.reshape(1, HEAD_DIM), (1, N_KV_HEADS)).astype(F32)
    kern = functools.partial(_qkv_kernel, dq=dq, dk=dk, dqi=dqi)
    mod_spec = pl.BlockSpec((1, 1, d), lambda b, i: (b, 0, 0))

    def full(a):
        return pl.BlockSpec(a.shape, lambda b, i: (0,) * a.ndim)

    def out(n):
        return pl.BlockSpec((1, tm, n), lambda b, i: (b, i, 0))

    return pl.pallas_call(
        kern,
        out_shape=[
            jax.ShapeDtypeStruct((bsz, seq, dq), BF16),
            jax.ShapeDtypeStruct((bsz, seq, dk), BF16),
            jax.ShapeDtypeStruct((bsz, seq, dk), BF16),
            jax.ShapeDtypeStruct((bsz, seq, dqi), BF16),
            jax.ShapeDtypeStruct((bsz, seq, LANES), F32),
        ],
        grid=(bsz, seq // tm),
        in_specs=[
            pl.BlockSpec((1, tm, d), lambda b, i: (b, i, 0)),
            pl.BlockSpec((1, d), lambda b, i: (0, 0)),
            mod_spec, mod_spec,
            full(w_main), full(w_small), full(qg), full(kg),
            full(hsq), full(heq), full(hsk), full(hek),
        ],
        out_specs=[out(dq), out(dk), out(dk), out(dqi), out(LANES)],
        compiler_params=_cparams("parallel", "parallel"),
        name="attn_qkv",
    )(x, g, sh, sc, w_main, w_small, qg, kg, hsq, heq, hsk, hek)


def _rel_bucket_np(dist):
    max_exact = REL_BUCKETS // 2
    d = np.maximum(dist, 1).astype(np.float64)
    large = max_exact + (np.log(d / max_exact) / math.log(REL_MAX_DIST / max_exact)
                         * (REL_BUCKETS - max_exact)).astype(np.int32)
    large = np.minimum(large, REL_BUCKETS - 1)
    return np.where(dist < max_exact, dist, large).astype(np.int32)


def _bias_table_kernel(bucket_ref, rb_ref, o_ref):
    hd = pl.program_id(0)
    bucket = bucket_ref[...]
    acc = jnp.zeros(bucket.shape, F32)
    for b in range(REL_BUCKETS):
        acc = jnp.where(bucket == b, rb_ref[b, hd] * LOG2E, acc)
    o_ref[0] = acc


def _bias_table(rel_bias):
    w = np.arange(BIAS_W)[:, None]
    i = np.arange(ATT_TQ)[None, :]
    bucket = jnp.asarray(_rel_bucket_np(np.maximum(i - w + BIAS_C, 0)))
    return pl.pallas_call(
        _bias_table_kernel,
        out_shape=jax.ShapeDtypeStruct((N_HEADS, BIAS_W, ATT_TQ), F32),
        grid=(N_HEADS,),
        in_specs=[
            pl.BlockSpec((BIAS_W, ATT_TQ), lambda hd: (0, 0)),
            pl.BlockSpec(memory_space=pltpu.SMEM),
        ],
        out_specs=pl.BlockSpec((1, BIAS_W, ATT_TQ), lambda hd: (hd, 0, 0)),
        compiler_params=_cparams("arbitrary"),
        name="attn_bias_table",
    )(bucket, rel_bias.astype(F32))


def _attn_kernel(qT_ref, qiT_ref, wiT_ref, k_ref, vT_ref, ki_ref, pt_ref, o_ref,
                 keys_scr, negm_scr, pidx_scr, oT_scr, acc_scr, qall_scr, sa_scr, sb_scr, *, top_k):
    tq, kc = ATT_TQ, ATT_KC
    qt = pl.program_id(1)
    q0 = qt * tq
    nch = (q0 + tq + kc - 1) // kc
    tpos = q0 + lax.broadcasted_iota(I32, (kc, tq), 1)
    srow = lax.broadcasted_iota(I32, (kc, tq), 0)

    qiT = qiT_ref[0]
    qi_all = jnp.concatenate([qiT[hd * IDX_DIM:(hd + 1) * IDX_DIM, :] for hd in range(IDX_HEADS)], axis=1)
    wiT = wiT_ref[0]

    def score_chunk(c):
        ks = pl.multiple_of(c * kc, kc)
        kic = ki_ref[0, pl.ds(ks, kc), :]
        dots = jnp.dot(kic, qi_all, preferred_element_type=F32)
        acc = jnp.zeros((kc, tq), F32)
        for hd in range(IDX_HEADS):
            acc = acc + jnp.maximum(dots[:, hd * tq:(hd + 1) * tq], 0.0) * wiT[hd:hd + 1, :]
        acc = jnp.where(acc == 0.0, 0.0, acc)
        bits = pltpu.bitcast(acc, I32)
        key = jnp.where(bits < 0, bits ^ INT_MAX, bits)
        key = jnp.where(ks + srow <= tpos, key, INT_MIN)
        keys_scr[pl.ds(ks, kc), :] = key

    npair = (nch + 1) // 2

    def score_pair(i, carry):
        score_chunk(2 * i)
        score_chunk(2 * i + 1)
        return carry

    lax.fori_loop(0, npair, score_pair, 0)

    @pl.when(nch % 2 == 1)
    def _():
        negm_scr[pl.ds(pl.multiple_of(nch * kc, kc), kc), :] = jnp.full((kc, tq), NEG_BIG, F32)

    srow2 = lax.broadcasted_iota(I32, (2 * kc, tq), 0)

    def count(pred):
        def body(c, acc):
            ks = pl.multiple_of(c * 2 * kc, 2 * kc)
            m = pred(keys_scr[pl.ds(ks, 2 * kc), :], ks + srow2).astype(I32)
            return acc + jnp.sum(m.reshape(2 * kc // SUBLANES, SUBLANES, tq), axis=0)
        acc = lax.fori_loop(0, npair, body, jnp.zeros((SUBLANES, tq), I32))
        return jnp.sum(acc, axis=0, keepdims=True)

    def bit_body(it, p):
        cand_p = p | lax.shift_left(jnp.int32(1), 31 - it)
        cand = cand_p ^ INT_MIN
        cnt = count(lambda k, s: k >= cand)
        return jnp.where(cnt >= top_k, cand_p, p)

    p_fin = lax.fori_loop(0, 32, bit_body, jnp.zeros((1, tq), I32))
    v = p_fin ^ INT_MIN

    cnt_gt = count(lambda k, s: k > v)
    cnt_eq = count(lambda k, s: k == v)
    need = top_k - cnt_gt
    pidx_scr[...] = jnp.full((1, tq), INT_MAX, I32)

    @pl.when(jnp.max(cnt_eq - need) > 0)
    def _():
        def ibit(it, p):
            cand = p | lax.shift_left(jnp.int32(1), 11 - it)
            cnt = count(lambda k, s: jnp.logical_and(k == v, s < cand))
            return jnp.where(cnt < need, cand, p)
        pidx_scr[...] = lax.fori_loop(0, 12, ibit, jnp.zeros((1, tq), I32))

    pidx = pidx_scr[...]

    def mask_chunk(c, carry):
        ks = pl.multiple_of(c * kc, kc)
        k = keys_scr[pl.ds(ks, kc), :]
        spos = ks + srow
        sel = jnp.logical_or(k > v, jnp.logical_and(k == v, spos <= pidx))
        sel = jnp.logical_and(sel, spos <= tpos)
        negm_scr[pl.ds(ks, kc), :] = jnp.where(sel, 0.0, NEG_BIG)
        return carry

    lax.fori_loop(0, nch, mask_chunk, 0)

    acc_scr[...] = jnp.zeros(acc_scr.shape, F32)
    for n in range(N_KV_HEADS):
        r0 = n * N_REP * HEAD_DIM
        qall_scr[n] = jnp.concatenate(
            [qT_ref[0, r0 + g * HEAD_DIM:r0 + (g + 1) * HEAD_DIM, :] for g in range(N_REP)], axis=1)

    def qk_chunk(c, s_ref):
        ks = pl.multiple_of(c * kc, kc)
        for n in range(N_KV_HEADS):
            s_ref[n] = jnp.dot(k_ref[0, n, pl.ds(ks, kc), :], qall_scr[n], preferred_element_type=F32)

    def softmax_pv(c, s_ref, ms, far):
        ks = pl.multiple_of(c * kc, kc)
        negm = negm_scr[pl.ds(ks, kc), :]
        w0 = pl.multiple_of(jnp.clip(BIAS_C - (q0 - ks), 0, BIAS_C), LANES)
        new_ms = []
        for n in range(N_KV_HEADS):
            s = s_ref[n]
            if far:
                cvec = jnp.concatenate([pt_ref[n * N_REP + g, 0:1, :] for g in range(N_REP)], axis=1)
                lg = jnp.concatenate([s[:, g * tq:(g + 1) * tq] + negm for g in range(N_REP)], axis=1)
                m_new = jnp.maximum(ms[n], jnp.max(lg, axis=0, keepdims=True) + cvec)
                p = jnp.exp2(lg - (m_new - cvec))
            else:
                lg = jnp.concatenate(
                    [s[:, g * tq:(g + 1) * tq] + pt_ref[n * N_REP + g, pl.ds(w0, kc), :] + negm
                     for g in range(N_REP)], axis=1)
                m_new = jnp.maximum(ms[n], jnp.max(lg, axis=0, keepdims=True))
                p = jnp.exp2(lg - m_new)
            alpha = jnp.exp2(ms[n] - m_new)
            acc_scr[n] = alpha * acc_scr[n] + jnp.dot(vT_ref[0, c, n], p.astype(BF16),
                                                      preferred_element_type=F32)
            new_ms.append(m_new)
        return tuple(new_ms)

    last_chunk = k_ref.shape[2] // kc - 1

    def pair_step(i, ms, far):
        c0 = 2 * i
        qk_chunk(c0 + 1, sb_scr)
        ms = softmax_pv(c0, sa_scr, ms, far)
        qk_chunk(jnp.minimum(c0 + 2, last_chunk), sa_scr)
        return softmax_pv(c0 + 1, sb_scr, ms, far)

    n_far = jnp.clip((q0 - BIAS_C + kc) // kc, 0, nch)
    ms = tuple(jnp.full((1, N_REP * tq), NEG_BIG, F32) for _ in range(N_KV_HEADS))
    qk_chunk(0, sa_scr)
    ms = lax.fori_loop(0, n_far // 2, functools.partial(pair_step, far=True), ms)
    lax.fori_loop(n_far // 2, npair, functools.partial(pair_step, far=False), ms)
    for n in range(N_KV_HEADS):
        o_t = acc_scr[n, 0:HEAD_DIM, :] / acc_scr[n, HEAD_DIM:HEAD_DIM + 1, :]
        for g in range(N_REP):
            r0 = (n * N_REP + g) * HEAD_DIM
            oT_scr[r0:r0 + HEAD_DIM, :] = o_t[:, g * tq:(g + 1) * tq]

    o_ref[0] = oT_scr[...].T.astype(BF16)


def _attention(qT, qiT, wiT, k4, vT, ki, ptab, top_k):
    bsz, dq, seq = qT.shape
    tq = ATT_TQ
    kern = functools.partial(_attn_kernel, top_k=top_k)
    return pl.pallas_call(
        kern,
        out_shape=jax.ShapeDtypeStruct((bsz, seq, dq), BF16),
        grid=(bsz, seq // tq),
        in_specs=[
            pl.BlockSpec((1, dq, tq), lambda b, i: (b, 0, i)),
            pl.BlockSpec((1, qiT.shape[1], tq), lambda b, i: (b, 0, i)),
            pl.BlockSpec((1, wiT.shape[1], tq), lambda b, i: (b, 0, i)),
            pl.BlockSpec((1,) + k4.shape[1:], lambda b, i: (b, 0, 0, 0)),
            pl.BlockSpec((1,) + vT.shape[1:], lambda b, i: (b, 0, 0, 0, 0)),
            pl.BlockSpec((1,) + ki.shape[1:], lambda b, i: (b, 0, 0)),
            pl.BlockSpec(ptab.shape, lambda b, i: (0, 0, 0)),
        ],
        out_specs=pl.BlockSpec((1, tq, dq), lambda b, i: (b, i, 0)),
        scratch_shapes=[
            pltpu.VMEM((seq, tq), I32),
            pltpu.VMEM((seq, tq), F32),
            pltpu.VMEM((1, tq), I32),
            pltpu.VMEM((dq, tq), F32),
            pltpu.VMEM((N_KV_HEADS, ATT_VROWS, N_REP * tq), F32),
            pltpu.VMEM((N_KV_HEADS, HEAD_DIM, N_REP * tq), BF16),
            pltpu.VMEM((N_KV_HEADS, ATT_KC, N_REP * tq), F32),
            pltpu.VMEM((N_KV_HEADS, ATT_KC, N_REP * tq), F32),
        ],
        compiler_params=_cparams("parallel", "arbitrary"),
        name="attn_core",
    )(qT, qiT, wiT, k4, vT, ki, ptab)


def _proj_res_kernel(a_ref, x_ref, gate_ref, w_ref, o_ref):
    y = jnp.dot(a_ref[0], w_ref[...], preferred_element_type=F32)
    o_ref[0] = x_ref[0] + gate_ref[0] * y


def _proj_residual(a, x, gate, w, tm=512):
    bsz, seq, d = x.shape
    tm = min(tm, seq)
    return pl.pallas_call(
        _proj_res_kernel,
        out_shape=jax.ShapeDtypeStruct((bsz, seq, d), F32),
        grid=(bsz, seq // tm),
        in_specs=[
            pl.BlockSpec((1, tm, a.shape[2]), lambda b, i: (b, i, 0)),
            pl.BlockSpec((1, tm, d), lambda b, i: (b, i, 0)),
            pl.BlockSpec((1, 1, d), lambda b, i: (b, 0, 0)),
            pl.BlockSpec(w.shape, lambda b, i: (0, 0)),
        ],
        out_specs=pl.BlockSpec((1, tm, d), lambda b, i: (b, i, 0)),
        compiler_params=_cparams("parallel", "parallel"),
        name="proj_residual",
    )(a, x, gate, w)


def _attn_layer(x, g, sh, sc, gate, w_in, q_gain, k_gain, w_out, rel_bias):
    bsz, seq, d = x.shape
    top_k = min(TOPK_MAX, seq // 4)
    dq = N_HEADS * HEAD_DIM
    dk = N_KV_HEADS * HEAD_DIM
    dqi = IDX_HEADS * IDX_DIM
    nmain = dq + 2 * dk + dqi
    w_main = w_in[:, :nmain].astype(BF16)
    w_small = jnp.pad(w_in[:, nmain:], ((0, 0), (0, LANES - (IDX_DIM + IDX_HEADS)))).astype(BF16)
    q, k, v, qi, kiwi = _qkv_proj(x, g, sh, sc, w_main, w_small, q_gain, k_gain)
    ki = kiwi[:, :, :IDX_DIM].astype(BF16)
    wi = kiwi[:, :, IDX_DIM:IDX_DIM + IDX_HEADS] * (IDX_HEADS ** -0.5 * IDX_DIM ** -0.5)
    qT = jnp.swapaxes(q, 1, 2)
    qiT = jnp.swapaxes(qi, 1, 2)
    wiT = jnp.swapaxes(wi, 1, 2)
    nck = seq // ATT_KC
    vT = jnp.swapaxes(v.reshape(bsz, nck, ATT_KC, dk), 2, 3).reshape(bsz, nck, N_KV_HEADS, HEAD_DIM, ATT_KC)
    vT = jnp.concatenate([
        vT, jnp.ones((bsz, nck, N_KV_HEADS, 1, ATT_KC), BF16),
        jnp.zeros((bsz, nck, N_KV_HEADS, ATT_VROWS - HEAD_DIM - 1, ATT_KC), BF16)], axis=3)
    k4 = jnp.swapaxes(k.reshape(bsz, seq, N_KV_HEADS, HEAD_DIM), 1, 2)
    ptab = _bias_table(rel_bias)
    attn = _attention(qT, qiT, wiT, k4, vT, ki, ptab, top_k)
    return _proj_residual(attn, x, gate, w_out.astype(BF16))


def _s5_prep_kernel(lre_ref, lim_ref, ls_ref, bre_ref, bim_ref, cre_ref, cim_ref,
                    bcw_ref, mtw_ref, ccw_ref, are_ref, aim_ref):
    t_len = SSM_CHUNK
    lre = jnp.minimum(lre_ref[0], -1e-4)
    lim = lim_ref[0]
    step = jnp.exp(ls_ref[0])
    ar = lre * step
    ai = lim * step

    def powers(jv):
        mag = jnp.exp(jv * ar)
        return mag * jnp.cos(jv * ai), mag * jnp.sin(jv * ai)

    lb_re, lb_im = powers(1.0)
    nr = lb_re - 1.0
    ni = lb_im
    den = lre * lre + lim * lim
    cf_re = (nr * lre + ni * lim) / den
    cf_im = (ni * lre - nr * lim) / den
    bre = bre_ref[0]
    bim = bim_ref[0]
    bb_re = cf_re * bre - cf_im * bim
    bb_im = cf_re * bim + cf_im * bre
    cre = cre_ref[0]
    cim = cim_ref[0]
    nst = lre.shape[-1]
    jv = lax.broadcasted_iota(I32, (t_len, 1, nst), 0).astype(F32)
    pj_re, pj_im = powers(jv)
    a_re = (cre[None] * pj_re - cim[None] * pj_im).reshape(t_len * SSM_GROUP, nst)
    a_im = (cre[None] * pj_im + cim[None] * pj_re).reshape(t_len * SSM_GROUP, nst)
    dn = (((1,), (1,)), ((), ()))
    cg = SSM_GROUP
    tc = t_len * cg
    width = t_len * LANES
    gq = pl.program_id(0) % (LANES // cg)
    nh = (LANES // cg) * nst
    cg_shift = cg.bit_length() - 1

    def place(nrows, target):
        r = lax.broadcasted_iota(I32, (nrows, width), 0)
        col = lax.broadcasted_iota(I32, (nrows, width), 1)
        return jnp.where(col == target(r), 1.0, 0.0).astype(BF16)

    pm = place(tc, lambda r: lax.shift_right_logical(r, cg_shift) * LANES + gq * cg + (r & (cg - 1)))
    k_t = (lax.dot_general(bb_re, a_re, dn, preferred_element_type=F32, precision=HIGHEST)
           - lax.dot_general(bb_im, a_im, dn, preferred_element_type=F32, precision=HIGHEST))
    lane = lax.broadcasted_iota(I32, (cg, tc), 1)
    mt_rows = [k_t] + [jnp.where(lane >= s * cg, pltpu.roll(k_t, s * cg, 1), 0.0) for s in range(1, t_len)]
    mt_t = jnp.concatenate(mt_rows, axis=0)
    mtw_ref[0] = jnp.dot(mt_t.astype(BF16), pm, preferred_element_type=F32).astype(BF16)
    pr_re, pr_im = powers((t_len - 1.0) - jv)
    bc_re = (bb_re[None] * pr_re - bb_im[None] * pr_im).reshape(tc, nst)
    bc_im = (bb_re[None] * pr_im + bb_im[None] * pr_re).reshape(tc, nst)
    pb_re = place(nst, lambda r: gq * nst + r)
    pb_im = place(nst, lambda r: nh + gq * nst + r)
    bcw_ref[0] = (jnp.dot(bc_re.astype(BF16), pb_re, preferred_element_type=F32)
                  + jnp.dot(bc_im.astype(BF16), pb_im, preferred_element_type=F32)).astype(BF16)
    pn_re, pn_im = powers(jv + 1.0)
    cc_re = (cre[None] * pn_re - cim[None] * pn_im).reshape(tc, nst)
    cc_im = (cre[None] * pn_im + cim[None] * pn_re).reshape(tc, nst)
    eye = jnp.where(lax.broadcasted_iota(I32, (nst, nst), 0) == lax.broadcasted_iota(I32, (nst, nst), 1),
                    1.0, 0.0)
    cct_re = lax.dot_general(eye, cc_re, dn, preferred_element_type=F32, precision=HIGHEST)
    cct_im = lax.dot_general(eye, cc_im, dn, preferred_element_type=F32, precision=HIGHEST)
    ccw_ref[0, 0:nst, :] = jnp.dot(cct_re.astype(BF16), pm, preferred_element_type=F32).astype(BF16)
    ccw_ref[0, nst:2 * nst, :] = (-jnp.dot(cct_im.astype(BF16), pm, preferred_element_type=F32)).astype(BF16)
    at_re, at_im = powers(float(t_len))
    are_ref[0] = at_re
    aim_ref[0] = at_im


def _s5_prep(lam_re, lam_im, log_step, b_re, b_im, c_re, c_im):
    ng, nst = lam_re.shape
    tc = SSM_CHUNK * SSM_GROUP
    vec = pl.BlockSpec((1, 1, nst), lambda gi: (gi, 0, 0))
    mat = pl.BlockSpec((1, SSM_GROUP, nst), lambda gi: (gi, 0, 0))
    width = SSM_CHUNK * LANES
    assert tc == 2 * nst and 2 * (LANES // SSM_GROUP) * nst == width
    big = pl.BlockSpec((1, tc, width), lambda gi: (gi, 0, 0))
    wide = jax.ShapeDtypeStruct((ng, tc, width), BF16)
    return pl.pallas_call(
        _s5_prep_kernel,
        out_shape=[wide, wide, wide,
                   jax.ShapeDtypeStruct((ng, 1, nst), F32), jax.ShapeDtypeStruct((ng, 1, nst), F32)],
        grid=(ng,),
        in_specs=[vec, vec, pl.BlockSpec((1, 1, 1), lambda gi: (gi, 0, 0)), mat, mat, mat, mat],
        out_specs=[big, big, big, vec, vec],
        compiler_params=_cparams("parallel"),
        name="s5_prep",
    )(lam_re.reshape(ng, 1, nst), lam_im.reshape(ng, 1, nst), log_step.reshape(ng, 1, 1),
      jnp.swapaxes(b_re, 1, 2), jnp.swapaxes(b_im, 1, 2), c_re, c_im)


def _s5_pre_kernel(x_ref, g_ref, sh_ref, sc_ref, o_ref):
    h = _norm_mod(x_ref[0], g_ref[...], sh_ref[0], sc_ref[0]).astype(BF16)
    for q in range(o_ref.shape[0]):
        o_ref[q] = h[:, q * LANES:(q + 1) * LANES]


def _s5_pre(x, g, sh, sc, tm=512):
    bsz, seq, d = x.shape
    tm = min(tm, seq)
    nt = seq // tm
    nq = d // LANES
    mod_spec = pl.BlockSpec((1, 1, d), lambda b, i: (b, 0, 0))
    return pl.pallas_call(
        _s5_pre_kernel,
        out_shape=jax.ShapeDtypeStruct((nq, bsz * seq, LANES), BF16),
        grid=(bsz, nt),
        in_specs=[pl.BlockSpec((1, tm, d), lambda b, i: (b, i, 0)),
                  pl.BlockSpec((1, d), lambda b, i: (0, 0)), mod_spec, mod_spec],
        out_specs=pl.BlockSpec((nq, tm, LANES), lambda b, i: (0, b * nt + i, 0)),
        compiler_params=_cparams("parallel", "parallel"),
        name="s5_pre",
    )(x, g, sh, sc)


def _s5_scan_kernel(x_ref, bc_ref, mt_ref, cc_ref, are_ref, aim_ref, y_ref, re_scr, im_scr):
    x = x_ref[0]
    nrow = x.shape[0]
    pad = re_scr.shape[0] - nrow
    v = jnp.dot(x, bc_ref[0], preferred_element_type=F32)
    nh = v.shape[1] // 2
    s_re = v[:, :nh]
    s_im = v[:, nh:]
    a_re = are_ref[0]
    a_im = aim_ref[0]
    re_scr[0:pad, :] = jnp.zeros((pad, nh), F32)
    im_scr[0:pad, :] = jnp.zeros((pad, nh), F32)

    def shifted(scr, val, dist):
        scr[pad:pad + nrow, :] = val
        return scr[pad - dist:pad - dist + nrow, :]

    dist = 1
    while dist < nrow:
        sh_re = shifted(re_scr, s_re, dist)
        sh_im = shifted(im_scr, s_im, dist)
        s_re, s_im = (s_re + a_re * sh_re - a_im * sh_im, s_im + a_re * sh_im + a_im * sh_re)
        a_re, a_im = (a_re * a_re - a_im * a_im, 2.0 * a_re * a_im)
        dist *= 2
    sp = jnp.concatenate([shifted(re_scr, s_re, 1), shifted(im_scr, s_im, 1)], axis=1).astype(BF16)
    y_ref[0] = (jnp.dot(x, mt_ref[0], preferred_element_type=F32)
                + jnp.dot(sp, cc_ref[0], preferred_element_type=F32))


def _s5_scan(hq, bcq, mtq, ccq, a_re, a_im, bsz):
    nq, nrows, width = hq.shape
    nj = nrows // bsz
    nh = a_re.shape[-1]
    pad = max(nj // 2, SUBLANES)
    tile = pl.BlockSpec((1, nj, width), lambda q, b: (q, b, 0))
    wspec = lambda a: pl.BlockSpec((1,) + a.shape[1:], lambda q, b: (q, 0, 0))
    return pl.pallas_call(
        _s5_scan_kernel,
        out_shape=jax.ShapeDtypeStruct((nq, nrows, width), F32),
        grid=(nq, bsz),
        in_specs=[tile, wspec(bcq), wspec(mtq), wspec(ccq), wspec(a_re), wspec(a_im)],
        out_specs=tile,
        scratch_shapes=[pltpu.VMEM((pad + nj, nh), F32), pltpu.VMEM((pad + nj, nh), F32)],
        compiler_params=_cparams("parallel", "parallel"),
        name="s5_scan",
    )(hq, bcq, mtq, ccq, a_re, a_im)


def _s5_post_kernel(x_ref, y_ref, g_ref, sh_ref, sc_ref, gate_ref, dsk_ref, w_ref, o_ref, *, d):
    x = x_ref[0]
    h = _norm_mod(x, g_ref[...], sh_ref[0], sc_ref[0])
    y = jnp.concatenate([y_ref[q] for q in range(y_ref.shape[0])], axis=1)
    yy = y + dsk_ref[...] * h
    gl = jax.nn.gelu(yy).astype(BF16)
    z = jnp.dot(gl, w_ref[...], preferred_element_type=F32)
    o_ref[0] = x + gate_ref[0] * (z[:, :d] * jax.nn.sigmoid(z[:, d:]))


def _s5_post(x, yq, g, sh, sc, gate, d_skip, w_glu, tm=512):
    bsz, seq, d = x.shape
    tm = min(tm, seq)
    nt = seq // tm
    nq = d // LANES
    kern = functools.partial(_s5_post_kernel, d=d)
    mod_spec = pl.BlockSpec((1, 1, d), lambda b, i: (b, 0, 0))
    tile = pl.BlockSpec((1, tm, d), lambda b, i: (b, i, 0))
    return pl.pallas_call(
        kern,
        out_shape=jax.ShapeDtypeStruct((bsz, seq, d), F32),
        grid=(bsz, nt),
        in_specs=[tile, pl.BlockSpec((nq, tm, LANES), lambda b, i: (0, b * nt + i, 0)),
                  pl.BlockSpec((1, d), lambda b, i: (0, 0)), mod_spec, mod_spec, mod_spec,
                  pl.BlockSpec((1, d), lambda b, i: (0, 0)),
                  pl.BlockSpec(w_glu.shape, lambda b, i: (0, 0))],
        out_specs=tile,
        compiler_params=_cparams("parallel", "parallel"),
        name="s5_post",
    )(x, yq, g, sh, sc, gate, d_skip, w_glu)


def _s5_layer(x, g, sh, sc, gate, lam_re, lam_im, log_step, b_re, b_im, c_re, c_im, d_skip, w_glu):
    bsz, seq, d = x.shape
    ng, nst = lam_re.shape
    t_len, cg = SSM_CHUNK, SSM_GROUP
    nq = d // LANES
    gq = LANES // cg
    nj = seq // t_len
    bcw, mtw, ccw, a_re, a_im = _s5_prep(lam_re, lam_im, log_step, b_re, b_im, c_re, c_im)
    width = t_len * LANES
    bcq = jnp.swapaxes(bcw.reshape(nq, gq, t_len, cg, width), 1, 2).reshape(nq, width, width)
    mtq = jnp.swapaxes(mtw.reshape(nq, gq, t_len, cg, width), 1, 2).reshape(nq, width, width)
    ccq = jnp.swapaxes(ccw.reshape(nq, gq, 2, nst, width), 1, 2).reshape(nq, 2 * gq * nst, width)
    aq_re = a_re.reshape(nq, 1, gq * nst)
    aq_im = a_im.reshape(nq, 1, gq * nst)

    hq = _s5_pre(x, g, sh, sc).reshape(nq, bsz * nj, t_len * LANES)
    yq = _s5_scan(hq, bcq, mtq, ccq, aq_re, aq_im, bsz).reshape(nq, bsz * seq, LANES)
    return _s5_post(x, yq, g, sh, sc, gate, d_skip.reshape(1, d).astype(F32), w_glu.astype(BF16))


def kernel(x, c, ada_w, ada_b, norm_g, conv_w_in, conv_w, conv_w_out, attn_w_in, attn_q_gain, attn_k_gain, attn_w_out, rel_bias, ssm_lambda_re, ssm_lambda_im, ssm_log_step, ssm_b_re, ssm_b_im, ssm_c_re, ssm_c_im, ssm_d, ssm_w_glu, ffn_w_gu, ffn_w_down, moe_router_w, moe_router_b, moe_w_gu, moe_w_down):
    bsz, seq, d = x.shape
    depth = ada_w.shape[0]
    mod = _ada_mod(c, ada_w, ada_b).reshape(depth, bsz, 6, 1, d)
    moe_gu = moe_w_gu.astype(BF16).reshape((-1,) + moe_w_gu.shape[2:])
    moe_down = moe_w_down.astype(BF16).reshape((-1,) + moe_w_down.shape[2:])
    for i in range(depth):
        sh1, sc1, g1, sh2, sc2, g2 = (mod[i, :, r] for r in range(6))
        gn1 = norm_g[i, 0].reshape(1, d)
        gn2 = norm_g[i, 1].reshape(1, d)
        j = i // N_MIXERS
        if i % N_MIXERS == 0:
            x = _conv_layer(x, gn1, sh1, sc1, g1, conv_w_in[j].astype(BF16), conv_w[j],
                            conv_w_out[j].astype(BF16))
        elif i % N_MIXERS == 1:
            x = _attn_layer(x, gn1, sh1, sc1, g1, attn_w_in[j], attn_q_gain[j], attn_k_gain[j],
                            attn_w_out[j], rel_bias)
        else:
            x = _s5_layer(x, gn1, sh1, sc1, g1, ssm_lambda_re[j], ssm_lambda_im[j], ssm_log_step[j],
                          ssm_b_re[j], ssm_b_im[j], ssm_c_re[j], ssm_c_im[j], ssm_d[j], ssm_w_glu[j])
        if i % 2 == 0:
            x = _ffn_layer(x, gn2, sh2, sc2, g2, ffn_w_gu[i // 2].astype(BF16), ffn_w_down[i // 2].astype(BF16))
        else:
            x = _moe_layer(x, gn2, sh2, sc2, g2, moe_router_w[i // 2], moe_router_b[i // 2],
                           moe_gu, moe_down, ebase=(i // 2) * moe_w_gu.shape[1])
    return x
```

```python
import functools
import math

import numpy as np
import jax
import jax.numpy as jnp
from jax import lax
from jax.experimental import pallas as pl
from jax.experimental.pallas import tpu as pltpu

F32 = jnp.float32
BF16 = jnp.bfloat16
I32 = jnp.int32
HIGHEST = lax.Precision.HIGHEST

DEPTH = 4
N_MIXERS = 3
EPS = 1e-6
CONV_WIDTH = 3
N_HEADS = 16
N_KV_HEADS = 4
N_REP = N_HEADS // N_KV_HEADS
HEAD_DIM = 64
IDX_HEADS = 8
IDX_DIM = 64
TOPK_MAX = 256
REL_BUCKETS = 32
REL_MAX_DIST = 128
SSM_GROUP = 16
SSM_STATE = 64
N_EXPERTS = 8
TOP_K_EXPERTS = 2

VMEM_LIMIT_BYTES = 56 * 1024 * 1024
LANES = 128
SUBLANES = 8

INT_MIN = -(2 ** 31)
INT_MAX = 2 ** 31 - 1
NEG_BIG = -1e30
LOG2E = 1.4426950408889634

ATT_TQ = 128
ATT_KC = 256
BIAS_C = 384
BIAS_W = BIAS_C + ATT_KC
ATT_VROWS = 80

SSM_CHUNK = 8


def _cparams(*sem):
    return pltpu.CompilerParams(dimension_semantics=sem, vmem_limit_bytes=VMEM_LIMIT_BYTES)


def _norm_mod(x, g, shift, scale):
    ms = jnp.mean(x * x, axis=-1, keepdims=True)
    y = x * lax.rsqrt(ms + EPS)
    return (y * g) * (1.0 + scale) + shift


def _silu(x):
    return x * jax.nn.sigmoid(x)


def _ada_kernel(c_ref, w_ref, b_ref, o_ref):
    c = c_ref[...]
    cond = _silu(c)
    o_ref[0] = jnp.dot(cond, w_ref[0], preferred_element_type=F32, precision=HIGHEST) + b_ref[0]


def _ada_mod(c, ada_w, ada_b):
    depth, d, d6 = ada_w.shape
    bsz = c.shape[0]
    tn = d
    return pl.pallas_call(
        _ada_kernel,
        out_shape=jax.ShapeDtypeStruct((depth, bsz, d6), F32),
        grid=(depth, d6 // tn),
        in_specs=[
            pl.BlockSpec((bsz, d), lambda i, j: (0, 0)),
            pl.BlockSpec((1, d, tn), lambda i, j: (i, 0, j)),
            pl.BlockSpec((1, 1, tn), lambda i, j: (i, 0, j)),
        ],
        out_specs=pl.BlockSpec((1, bsz, tn), lambda i, j: (i, 0, j)),
        compiler_params=_cparams("parallel", "parallel"),
        name="ada_mod",
    )(c, ada_w, ada_b.reshape(depth, 1, d6))


def _conv_kernel(x_ref, xh_ref, g_ref, sh_ref, sc_ref, gate_ref, win_ref, wc_ref, wout_ref,
                 o_ref, u_scr, *, tm, d):
    i = pl.program_id(1)
    g = g_ref[...]
    sh = sh_ref[0]
    sc = sc_ref[0]
    x = x_ref[0]
    h = _norm_mod(x, g, sh, sc).astype(BF16)
    z = jnp.dot(h, win_ref[...], preferred_element_type=F32)
    b_gate = z[:, :d]
    u = z[:, d:2 * d] * z[:, 2 * d:]
    hh = _norm_mod(xh_ref[0], g, sh, sc).astype(BF16)
    zh = jnp.dot(hh, win_ref[:, d:], preferred_element_type=F32)
    uh = zh[:, :d] * zh[:, d:]
    uh = jnp.where(i > 0, uh, 0.0)
    u_scr[0:SUBLANES, :] = uh
    u_scr[SUBLANES:SUBLANES + tm, :] = u
    wc = wc_ref[...]
    conv = (wc[0:1, :] * u_scr[SUBLANES - 2:SUBLANES - 2 + tm, :]
            + wc[1:2, :] * u_scr[SUBLANES - 1:SUBLANES - 1 + tm, :]
            + wc[2:3, :] * u)
    y = jnp.dot((b_gate * conv).astype(BF16), wout_ref[...], preferred_element_type=F32)
    o_ref[0] = x + gate_ref[0] * y


def _conv_layer(x, g, sh, sc, gate, w_in, w_conv, w_out, tm=512):
    bsz, seq, d = x.shape
    tm = min(tm, seq)
    nt = seq // tm
    hb = tm // SUBLANES
    kern = functools.partial(_conv_kernel, tm=tm, d=d)
    mod_spec = pl.BlockSpec((1, 1, d), lambda b, i: (b, 0, 0))
    return pl.pallas_call(
        kern,
        out_shape=jax.ShapeDtypeStruct((bsz, seq, d), F32),
        grid=(bsz, nt),
        in_specs=[
            pl.BlockSpec((1, tm, d), lambda b, i: (b, i, 0)),
            pl.BlockSpec((1, SUBLANES, d), lambda b, i: (b, jnp.maximum(i * hb - 1, 0), 0)),
            pl.BlockSpec((1, d), lambda b, i: (0, 0)),
            mod_spec, mod_spec, mod_spec,
            pl.BlockSpec((d, 3 * d), lambda b, i: (0, 0)),
            pl.BlockSpec((CONV_WIDTH, d), lambda b, i: (0, 0)),
            pl.BlockSpec((d, d), lambda b, i: (0, 0)),
        ],
        out_specs=pl.BlockSpec((1, tm, d), lambda b, i: (b, i, 0)),
        scratch_shapes=[pltpu.VMEM((tm + SUBLANES, d), F32)],
        compiler_params=_cparams("parallel", "parallel"),
        name="conv_mixer",
    )(x, x, g, sh, sc, gate, w_in, w_conv, w_out)


def _ffn_kernel(x_ref, g_ref, sh_ref, sc_ref, gate_ref, wgu_ref, wd_ref, o_ref, *, dff, nchunk):
    x = x_ref[0]
    h = _norm_mod(x, g_ref[...], sh_ref[0], sc_ref[0]).astype(BF16)
    cols = dff // nchunk
    acc = jnp.zeros(x.shape, F32)
    for c in range(nchunk):
        gg = jnp.dot(h, wgu_ref[:, c * cols:(c + 1) * cols], preferred_element_type=F32)
        uu = jnp.dot(h, wgu_ref[:, dff + c * cols:dff + (c + 1) * cols], preferred_element_type=F32)
        a = (_silu(gg) * uu).astype(BF16)
        acc = acc + jnp.dot(a, wd_ref[c * cols:(c + 1) * cols, :], preferred_element_type=F32)
    o_ref[0] = x + gate_ref[0] * acc


def _ffn_layer(x, g, sh, sc, gate, w_gu, w_down, tm=512):
    bsz, seq, d = x.shape
    dff = w_down.shape[0]
    tm = min(tm, seq)
    kern = functools.partial(_ffn_kernel, dff=dff, nchunk=2)
    mod_spec = pl.BlockSpec((1, 1, d), lambda b, i: (b, 0, 0))
    return pl.pallas_call(
        kern,
        out_shape=jax.ShapeDtypeStruct((bsz, seq, d), F32),
        grid=(bsz, seq // tm),
        in_specs=[
            pl.BlockSpec((1, tm, d), lambda b, i: (b, i, 0)),
            pl.BlockSpec((1, d), lambda b, i: (0, 0)),
            mod_spec, mod_spec, mod_spec,
            pl.BlockSpec((d, 2 * dff), lambda b, i: (0, 0), pipeline_mode=pl.Buffered(1)),
            pl.BlockSpec((dff, d), lambda b, i: (0, 0), pipeline_mode=pl.Buffered(1)),
        ],
        out_specs=pl.BlockSpec((1, tm, d), lambda b, i: (b, i, 0)),
        compiler_params=_cparams("parallel", "parallel"),
        name="ffn_dense",
    )(x, g, sh, sc, gate, w_gu, w_down)


MOE_SB = 512
MOE_RT = 512
MOE_ALIGN = 16
MOE_WIN = 256
MOE_CAP = MOE_WIN - MOE_ALIGN


def _moe_router_kernel(x_ref, g_ref, sh_ref, sc_ref, rw_ref, rb_ref, h_ref, meta_ref):
    hf = _norm_mod(x_ref[...], g_ref[...], sh_ref[0], sc_ref[0])
    h_ref[...] = hf.astype(BF16)
    logits = jnp.dot(hf, rw_ref[...], preferred_element_type=F32, precision=HIGHEST) + rb_ref[...]
    mx = jnp.max(logits, axis=-1, keepdims=True)
    ex = jnp.exp(logits - mx)
    probs = ex / jnp.sum(ex, axis=-1, keepdims=True)
    lane = lax.broadcasted_iota(I32, probs.shape, 1)
    m1 = jnp.max(probs, axis=-1, keepdims=True)
    i1 = jnp.min(jnp.where(probs == m1, lane, LANES), axis=-1, keepdims=True)
    rest = jnp.where(lane == i1, -1.0, probs)
    m2 = jnp.max(rest, axis=-1, keepdims=True)
    i2 = jnp.min(jnp.where(rest == m2, lane, LANES), axis=-1, keepdims=True)
    den = m1 + m2
    gates = jnp.where(lane == i1, m1 / den, 0.0) + jnp.where(lane == i2, m2 / den, 0.0)
    chosen = jnp.where(jnp.logical_or(lane == i1, lane == i2), 1.0, 0.0)
    nrow = chosen.shape[0]
    below = jnp.where(lax.broadcasted_iota(I32, (nrow, nrow), 1) < lax.broadcasted_iota(I32, (nrow, nrow), 0),
                      1.0, 0.0).astype(BF16)
    rank = jnp.dot(below, chosen.astype(BF16), preferred_element_type=F32)
    meta_ref[...] = gates + pltpu.roll(chosen, N_EXPERTS, 1) + pltpu.roll(rank, 2 * N_EXPERTS, 1)


def _moe_router(xt, g, sh, sc, rw_pad, rb_pad, tpb, tm):
    n, d = xt.shape
    mod_spec = pl.BlockSpec((1, 1, d), lambda i: (i // tpb, 0, 0))
    return pl.pallas_call(
        _moe_router_kernel,
        out_shape=[jax.ShapeDtypeStruct((n, d), BF16), jax.ShapeDtypeStruct((n, LANES), F32)],
        grid=(n // tm,),
        in_specs=[pl.BlockSpec((tm, d), lambda i: (i, 0)),
                  pl.BlockSpec((1, d), lambda i: (0, 0)), mod_spec, mod_spec,
                  pl.BlockSpec((d, LANES), lambda i: (0, 0)),
                  pl.BlockSpec((1, LANES), lambda i: (0, 0))],
        out_specs=[pl.BlockSpec((tm, d), lambda i: (i, 0)), pl.BlockSpec((tm, LANES), lambda i: (i, 0))],
        compiler_params=_cparams("parallel"),
        name="moe_router",
    )(xt, g, sh, sc, rw_pad, rb_pad)


def _window(start, count, w):
    s = start + w * MOE_CAP
    n = jnp.minimum(count - w * MOE_CAP, MOE_CAP)
    a = pl.multiple_of((s // MOE_ALIGN) * MOE_ALIGN, MOE_ALIGN)
    return s, n, a


def _moe_dispatch_kernel(start_ref, cnt_ref, h_ref, pos_ref, xs_init_ref, xs_ref,
                         buf, carry, sems, pending):
    del xs_init_ref
    b = pl.program_id(0)
    nb = pl.num_programs(0)
    ne = buf.shape[0]

    @pl.when(b == 0)
    def _():
        carry[...] = jnp.zeros(carry.shape, BF16)
        for e in range(ne):
            pending[e] = 0

    h = h_ref[...]
    riota = lax.broadcasted_iota(I32, (MOE_WIN, h.shape[0]), 0)

    def out_copy(e, a):
        return pltpu.make_async_copy(buf.at[e], xs_ref.at[pl.ds(a, MOE_WIN)], sems.at[e])

    for e in range(ne):
        posrow = pos_ref[0, e:e + 1, :]
        nwin = (cnt_ref[b, e] + MOE_CAP - 1) // MOE_CAP

        def wbody(w, carry_unused, e=e, posrow=posrow):
            s, n, a = _window(start_ref[b, e], cnt_ref[b, e], w)

            @pl.when(pending[e] == 1)
            def _():
                out_copy(e, 0).wait()

            hit = jnp.logical_and(posrow - a == riota,
                                  jnp.logical_and(posrow >= s, posrow < s + n))
            onehot = jnp.where(hit, 1.0, 0.0).astype(BF16)
            buf[e] = jnp.dot(onehot, h, preferred_element_type=F32).astype(BF16)
            buf[e, 0:MOE_ALIGN, :] = buf[e, 0:MOE_ALIGN, :] + carry[e]
            c0 = pl.multiple_of(((s + n) // MOE_ALIGN) * MOE_ALIGN - a, MOE_ALIGN)
            carry[e] = buf[e, pl.ds(c0, MOE_ALIGN), :]
            out_copy(e, a).start()
            pending[e] = 1
            return carry_unused

        lax.fori_loop(0, nwin, wbody, 0)

    @pl.when(b == nb - 1)
    def _():
        for e in range(ne):
            @pl.when(pending[e] == 1)
            def _():
                out_copy(e, 0).wait()


def _moe_dispatch(start, cnt, h, pos_t, ncap):
    n, d = h.shape
    nb, ne, sb = pos_t.shape
    xs_init = jnp.zeros((ncap, d), BF16)
    return pl.pallas_call(
        _moe_dispatch_kernel,
        out_shape=jax.ShapeDtypeStruct((ncap, d), BF16),
        grid_spec=pltpu.PrefetchScalarGridSpec(
            num_scalar_prefetch=2,
            grid=(nb,),
            in_specs=[pl.BlockSpec((sb, d), lambda b, *_: (b, 0)),
                      pl.BlockSpec((1, ne, sb), lambda b, *_: (b, 0, 0)),
                      pl.BlockSpec(memory_space=pl.ANY)],
            out_specs=pl.BlockSpec(memory_space=pl.ANY),
            scratch_shapes=[pltpu.VMEM((ne, MOE_WIN, d), BF16),
                            pltpu.VMEM((ne, MOE_ALIGN, d), BF16),
                            pltpu.SemaphoreType.DMA((ne,)),
                            pltpu.SMEM((ne,), I32)],
        ),
        input_output_aliases={4: 0},
        compiler_params=_cparams("arbitrary"),
        name="moe_dispatch",
    )(start, cnt, h, pos_t, xs_init)


def _moe_ffn_kernel(blk_ref, exp_ref, nt_ref, xs_ref, wgu_ref, wd_ref, ys_init_ref, ys_ref, *, dff, nchunk):
    del blk_ref, exp_ref, ys_init_ref
    k = pl.program_id(0)

    @pl.when(k < nt_ref[0])
    def _():
        x = xs_ref[...]
        cols = dff // nchunk
        acc = jnp.zeros(x.shape, F32)
        for c in range(nchunk):
            gg = jnp.dot(x, wgu_ref[0, :, c * cols:(c + 1) * cols], preferred_element_type=F32)
            uu = jnp.dot(x, wgu_ref[0, :, dff + c * cols:dff + (c + 1) * cols], preferred_element_type=F32)
            a = (_silu(gg) * uu).astype(BF16)
            acc = acc + jnp.dot(a, wd_ref[0, c * cols:(c + 1) * cols, :], preferred_element_type=F32)
        ys_ref[...] = acc.astype(BF16)


def _moe_ffn(tile_blk, tile_exp, ntiles, xs, w_gu, w_down, ebase):
    ncap, d = xs.shape
    dff = w_down.shape[1]
    kern = functools.partial(_moe_ffn_kernel, dff=dff, nchunk=2)
    ys_init = jnp.zeros((ncap, d), BF16)
    return pl.pallas_call(
        kern,
        out_shape=jax.ShapeDtypeStruct((ncap, d), BF16),
        grid_spec=pltpu.PrefetchScalarGridSpec(
            num_scalar_prefetch=3,
            grid=(ncap // MOE_RT,),
            in_specs=[pl.BlockSpec((MOE_RT, d), lambda k, blk, exp, nt: (blk[k], 0)),
                      pl.BlockSpec((1, d, 2 * dff), lambda k, blk, exp, nt: (ebase + exp[k], 0, 0)),
                      pl.BlockSpec((1, dff, d), lambda k, blk, exp, nt: (ebase + exp[k], 0, 0)),
                      pl.BlockSpec(memory_space=pl.ANY)],
            out_specs=pl.BlockSpec((MOE_RT, d), lambda k, blk, exp, nt: (blk[k], 0)),
        ),
        input_output_aliases={6: 0},
        compiler_params=_cparams("arbitrary"),
        name="moe_ffn",
    )(tile_blk, tile_exp, ntiles, xs, w_gu, w_down, ys_init)


def _moe_combine_kernel(start_ref, cnt_ref, x_ref, gate_ref, pos_ref, gts_ref, ys_ref, o_ref,
                        buf, sems, acc_scr):
    b = pl.program_id(0)
    ne = buf.shape[0]
    sb = x_ref.shape[0]

    def in_copy(e, a):
        return pltpu.make_async_copy(ys_ref.at[pl.ds(a, MOE_WIN)], buf.at[e], sems.at[e])

    for e in range(ne):
        @pl.when(cnt_ref[b, e] > 0)
        def _():
            _, _, a = _window(start_ref[b, e], cnt_ref[b, e], 0)
            in_copy(e, a).start()

    acc_scr[...] = jnp.zeros(acc_scr.shape, F32)
    liota = lax.broadcasted_iota(I32, (sb, MOE_WIN), 1)
    for e in range(ne):
        poscol = pos_ref[:, e:e + 1]
        gcol = gts_ref[:, e:e + 1]
        nwin = (cnt_ref[b, e] + MOE_CAP - 1) // MOE_CAP

        def wbody(w, carry_unused, e=e, poscol=poscol, gcol=gcol):
            s, n, a = _window(start_ref[b, e], cnt_ref[b, e], w)

            @pl.when(w > 0)
            def _():
                in_copy(e, a).start()

            in_copy(e, a).wait()
            hit = jnp.logical_and(poscol - a == liota,
                                  jnp.logical_and(poscol >= s, poscol < s + n))
            onehot = jnp.where(hit, 1.0, 0.0).astype(BF16)
            acc_scr[...] += gcol * jnp.dot(onehot, buf[e], preferred_element_type=F32)
            return carry_unused

        lax.fori_loop(0, nwin, wbody, 0)

    o_ref[...] = x_ref[...] + gate_ref[0] * acc_scr[...]


def _moe_combine(start, cnt, xt, gate, pos_n, gates_n, ys, tpb):
    n, d = xt.shape
    nb, ne = cnt.shape
    sb = n // nb
    return pl.pallas_call(
        _moe_combine_kernel,
        out_shape=jax.ShapeDtypeStruct((n, d), F32),
        grid_spec=pltpu.PrefetchScalarGridSpec(
            num_scalar_prefetch=2,
            grid=(nb,),
            in_specs=[pl.BlockSpec((sb, d), lambda b, *_: (b, 0)),
                      pl.BlockSpec((1, 1, d), lambda b, *_: (b // tpb, 0, 0)),
                      pl.BlockSpec((sb, ne), lambda b, *_: (b, 0)),
                      pl.BlockSpec((sb, ne), lambda b, *_: (b, 0)),
                      pl.BlockSpec(memory_space=pl.ANY)],
            out_specs=pl.BlockSpec((sb, d), lambda b, *_: (b, 0)),
            scratch_shapes=[pltpu.VMEM((ne, MOE_WIN, d), BF16),
                            pltpu.SemaphoreType.DMA((ne,)),
                            pltpu.VMEM((sb, d), F32)],
        ),
        compiler_params=_cparams("arbitrary"),
        name="moe_combine",
    )(start, cnt, xt, gate, pos_n, gates_n, ys)


def _moe_layer(x, g, sh, sc, gate, router_w, router_b, w_gu, w_down, ebase=0):
    bsz, seq, d = x.shape
    ne = router_w.shape[1]
    n = bsz * seq
    sb = min(MOE_SB, seq)
    tpb = seq // sb
    nb = n // sb
    rt = MOE_RT
    xt = x.reshape(n, d)
    rw_pad = jnp.pad(router_w.astype(F32), ((0, 0), (0, LANES - ne)))
    rb_pad = jnp.pad(router_b.astype(F32).reshape(1, ne), ((0, 0), (0, LANES - ne)), constant_values=NEG_BIG)
    h, meta = _moe_router(xt, g, sh, sc, rw_pad, rb_pad, tpb, sb)
    gates = meta[:, :ne]
    sel = (meta[:, ne:2 * ne] > 0.5).astype(I32)

    selb = sel.reshape(nb, sb, ne)
    cnt = jnp.sum(selb, axis=1)
    rank = meta[:, 2 * ne:3 * ne].astype(I32).reshape(nb, sb, ne)
    total = jnp.sum(cnt, axis=0)
    region = ((total + MOE_WIN + rt - 1) // rt) * rt
    off = jnp.cumsum(region) - region
    start = (off[None, :] + jnp.cumsum(cnt, axis=0) - cnt).astype(I32)
    pos = jnp.where(selb > 0, start[:, None, :] + rank, -1).astype(I32)
    ncap = 2 * n + ne * (MOE_WIN + rt)
    tiles_e = (total + rt - 1) // rt
    tcum = jnp.cumsum(tiles_e)
    ntiles = tcum[-1]
    kk = jnp.minimum(jnp.arange(ncap // rt), ntiles - 1)
    tile_exp = jnp.searchsorted(tcum, kk, side='right').astype(I32)
    tile_blk = (off[tile_exp] // rt + kk - (tcum - tiles_e)[tile_exp]).astype(I32)

    xs = _moe_dispatch(start, cnt.astype(I32), h, jnp.swapaxes(pos, 1, 2), ncap)
    ys = _moe_ffn(tile_blk, tile_exp, ntiles.reshape(1).astype(I32), xs, w_gu, w_down, ebase)
    out = _moe_combine(start, cnt.astype(I32), xt, gate, pos.reshape(n, ne), gates, ys, tpb)
    return out.reshape(bsz, seq, d)


def _head_norm(q, hsum_ref, hexp_ref, gain, scale):
    ms = jnp.dot((q * q).astype(BF16), hsum_ref[...], preferred_element_type=F32)
    r = lax.rsqrt(ms + EPS)
    r_hi = r.astype(BF16)
    r_lo = (r - r_hi.astype(F32)).astype(BF16)
    rexp = (jnp.dot(r_hi, hexp_ref[...], preferred_element_type=F32)
            + jnp.dot(r_lo, hexp_ref[...], preferred_element_type=F32))
    return q * rexp * (gain * scale)


def _qkv_kernel(x_ref, g_ref, sh_ref, sc_ref, wm_ref, ws_ref, qg_ref, kg_ref,
                hsq_ref, heq_ref, hsk_ref, hek_ref,
                q_ref, k_ref, v_ref, qi_ref, kiwi_ref, *, dq, dk, dqi):
    x = x_ref[0]
    h = _norm_mod(x, g_ref[...], sh_ref[0], sc_ref[0]).astype(BF16)
    z = jnp.dot(h, wm_ref[...], preferred_element_type=F32)
    q = z[:, :dq]
    k = z[:, dq:dq + dk]
    v = z[:, dq + dk:dq + 2 * dk]
    qi = z[:, dq + 2 * dk:dq + 2 * dk + dqi]
    q_ref[0] = _head_norm(q, hsq_ref, heq_ref, qg_ref[...], HEAD_DIM ** -0.5 * LOG2E).astype(BF16)
    k_ref[0] = _head_norm(k, hsk_ref, hek_ref, kg_ref[...], 1.0).astype(BF16)
    v_ref[0] = v.astype(BF16)
    qi_ref[0] = qi.astype(BF16)
    kiwi_ref[0] = jnp.dot(h, ws_ref[...], preferred_element_type=F32)


def _head_indicators(nheads):
    hs = np.zeros((nheads * HEAD_DIM, LANES), np.float32)
    he = np.zeros((LANES, nheads * HEAD_DIM), np.float32)
    for hd in range(nheads):
        hs[hd * HEAD_DIM:(hd + 1) * HEAD_DIM, hd] = 1.0 / HEAD_DIM
        he[hd, hd * HEAD_DIM:(hd + 1) * HEAD_DIM] = 1.0
    return jnp.asarray(hs, BF16), jnp.asarray(he, BF16)


def _qkv_proj(x, g, sh, sc, w_main, w_small, q_gain, k_gain, tm=512):
    bsz, seq, d = x.shape
    tm = min(tm, seq)
    dq = N_HEADS * HEAD_DIM
    dk = N_KV_HEADS * HEAD_DIM
    dqi = IDX_HEADS * IDX_DIM
    hsq, heq = _head_indicators(N_HEADS)
    hsk, hek = _head_indicators(N_KV_HEADS)
    qg = jnp.tile(q_gain.reshape(1, HEAD_DIM), (1, N_HEADS)).astype(F32)
    kg = jnp.tile(k_gain.reshape(1, HEAD_DIM), (1, N_KV_HEADS)).astype(F32)
    kern = functools.partial(_qkv_kernel, dq=dq, dk=dk, dqi=dqi)
    mod_spec = pl.BlockSpec((1, 1, d), lambda b, i: (b, 0, 0))

    def full(a):
        return pl.BlockSpec(a.shape, lambda b, i: (0,) * a.ndim)

    def out(n):
        return pl.BlockSpec((1, tm, n), lambda b, i: (b, i, 0))

    return pl.pallas_call(
        kern,
        out_shape=[
            jax.ShapeDtypeStruct((bsz, seq, dq), BF16),
            jax.ShapeDtypeStruct((bsz, seq, dk), BF16),
            jax.ShapeDtypeStruct((bsz, seq, dk), BF16),
            jax.ShapeDtypeStruct((bsz, seq, dqi), BF16),
            jax.ShapeDtypeStruct((bsz, seq, LANES), F32),
        ],
        grid=(bsz, seq // tm),
        in_specs=[
            pl.BlockSpec((1, tm, d), lambda b, i: (b, i, 0)),
            pl.BlockSpec((1, d), lambda b, i: (0, 0)),
            mod_spec, mod_spec,
            full(w_main), full(w_small), full(qg), full(kg),
            full(hsq), full(heq), full(hsk), full(hek),
        ],
        out_specs=[out(dq), out(dk), out(dk), out(dqi), out(LANES)],
        compiler_params=_cparams("parallel", "parallel"),
        name="attn_qkv",
    )(x, g, sh, sc, w_main, w_small, qg, kg, hsq, heq, hsk, hek)


def _rel_bucket_np(dist):
    max_exact = REL_BUCKETS // 2
    d = np.maximum(dist, 1).astype(np.float64)
    large = max_exact + (np.log(d / max_exact) / math.log(REL_MAX_DIST / max_exact)
                         * (REL_BUCKETS - max_exact)).astype(np.int32)
    large = np.minimum(large, REL_BUCKETS - 1)
    return np.where(dist < max_exact, dist, large).astype(np.int32)


def _bias_table_kernel(bucket_ref, rb_ref, o_ref):
    hd = pl.program_id(0)
    bucket = bucket_ref[...]
    acc = jnp.zeros(bucket.shape, F32)
    for b in range(REL_BUCKETS):
        acc = jnp.where(bucket == b, rb_ref[b, hd] * LOG2E, acc)
    o_ref[0] = acc.astype(BF16)


def _bias_table(rel_bias):
    w = np.arange(BIAS_W)[:, None]
    i = np.arange(ATT_TQ)[None, :]
    bucket = jnp.asarray(_rel_bucket_np(np.maximum(i - w + BIAS_C, 0)))
    return pl.pallas_call(
        _bias_table_kernel,
        out_shape=jax.ShapeDtypeStruct((N_HEADS, BIAS_W, ATT_TQ), BF16),
        grid=(N_HEADS,),
        in_specs=[
            pl.BlockSpec((BIAS_W, ATT_TQ), lambda hd: (0, 0)),
            pl.BlockSpec(memory_space=pltpu.SMEM),
        ],
        out_specs=pl.BlockSpec((1, BIAS_W, ATT_TQ), lambda hd: (hd, 0, 0)),
        compiler_params=_cparams("arbitrary"),
        name="attn_bias_table",
    )(bucket, rel_bias.astype(F32))


def _attn_kernel(qT_ref, qiT_ref, wiT_ref, k_ref, vT_ref, ki_ref, pt_ref, o_ref,
                 keys_scr, negm_scr, pidx_scr, oT_scr, acc_scr, qall_scr, sa_scr, sb_scr, *, top_k):
    tq, kc = ATT_TQ, ATT_KC
    qt = pl.program_id(1)
    q0 = qt * tq
    nch = (q0 + tq + kc - 1) // kc
    tpos = q0 + lax.broadcasted_iota(I32, (kc, tq), 1)
    srow = lax.broadcasted_iota(I32, (kc, tq), 0)

    qiT = qiT_ref[0]
    qi_all = jnp.concatenate([qiT[hd * IDX_DIM:(hd + 1) * IDX_DIM, :] for hd in range(IDX_HEADS)], axis=1)
    wiT = wiT_ref[0]

    def score_chunk(c):
        ks = pl.multiple_of(c * kc, kc)
        kic = ki_ref[0, pl.ds(ks, kc), :]
        dots = jnp.dot(kic, qi_all, preferred_element_type=F32)
        acc = jnp.zeros((kc, tq), F32)
        for hd in range(IDX_HEADS):
            acc = acc + jnp.maximum(dots[:, hd * tq:(hd + 1) * tq], 0.0) * wiT[hd:hd + 1, :]
        acc = jnp.where(acc == 0.0, 0.0, acc)
        bits = pltpu.bitcast(acc, I32)
        key = jnp.where(bits < 0, bits ^ INT_MAX, bits)
        key = jnp.where(ks + srow <= tpos, key, INT_MIN)
        keys_scr[pl.ds(ks, kc), :] = key

    npair = (nch + 1) // 2

    def score_pair(i, carry):
        score_chunk(2 * i)
        score_chunk(2 * i + 1)
        return carry

    lax.fori_loop(0, npair, score_pair, 0)

    @pl.when(nch % 2 == 1)
    def _():
        negm_scr[pl.ds(pl.multiple_of(nch * kc, kc), kc), :] = jnp.full((kc, tq), NEG_BIG, BF16)

    srow2 = lax.broadcasted_iota(I32, (2 * kc, tq), 0)

    def count(pred):
        def body(c, acc):
            ks = pl.multiple_of(c * 2 * kc, 2 * kc)
            m = pred(keys_scr[pl.ds(ks, 2 * kc), :], ks + srow2).astype(I32)
            return acc + jnp.sum(m.reshape(2 * kc // SUBLANES, SUBLANES, tq), axis=0)
        acc = lax.fori_loop(0, npair, body, jnp.zeros((SUBLANES, tq), I32))
        return jnp.sum(acc, axis=0, keepdims=True)

    def bit_body(it, p):
        cand_p = p | lax.shift_left(jnp.int32(1), 31 - it)
        cand = cand_p ^ INT_MIN
        cnt = count(lambda k, s: k >= cand)
        return jnp.where(cnt >= top_k, cand_p, p)

    p_fin = lax.fori_loop(0, 32, bit_body, jnp.zeros((1, tq), I32))
    v = p_fin ^ INT_MIN

    cnt_gt = count(lambda k, s: k > v)
    cnt_eq = count(lambda k, s: k == v)
    need = top_k - cnt_gt
    pidx_scr[...] = jnp.full((1, tq), INT_MAX, I32)

    @pl.when(jnp.max(cnt_eq - need) > 0)
    def _():
        def ibit(it, p):
            cand = p | lax.shift_left(jnp.int32(1), 11 - it)
            cnt = count(lambda k, s: jnp.logical_and(k == v, s < cand))
            return jnp.where(cnt < need, cand, p)
        pidx_scr[...] = lax.fori_loop(0, 12, ibit, jnp.zeros((1, tq), I32))

    pidx = pidx_scr[...]

    def mask_chunk(c, carry):
        ks = pl.multiple_of(c * kc, kc)
        k = keys_scr[pl.ds(ks, kc), :]
        spos = ks + srow
        sel = jnp.logical_or(k > v, jnp.logical_and(k == v, spos <= pidx))
        sel = jnp.logical_and(sel, spos <= tpos)
        negm_scr[pl.ds(ks, kc), :] = jnp.where(sel, 0.0, NEG_BIG).astype(BF16)
        return carry

    lax.fori_loop(0, nch, mask_chunk, 0)

    acc_scr[...] = jnp.zeros(acc_scr.shape, F32)
    for n in range(N_KV_HEADS):
        r0 = n * N_REP * HEAD_DIM
        qall_scr[n] = jnp.concatenate(
            [qT_ref[0, r0 + g * HEAD_DIM:r0 + (g + 1) * HEAD_DIM, :] for g in range(N_REP)], axis=1)

    def qk_chunk(c, s_ref):
        ks = pl.multiple_of(c * kc, kc)
        for n in range(N_KV_HEADS):
            s_ref[n] = jnp.dot(k_ref[0, n, pl.ds(ks, kc), :], qall_scr[n],
                               preferred_element_type=F32).astype(BF16)

    def softmax_pv(c, s_ref, ms):
        ks = pl.multiple_of(c * kc, kc)
        negm = negm_scr[pl.ds(ks, kc), :]
        w0 = pl.multiple_of(jnp.clip(BIAS_C - (q0 - ks), 0, BIAS_C), LANES)
        new_ms = []
        for n in range(N_KV_HEADS):
            s = s_ref[n]
            lg = jnp.concatenate(
                [s[:, g * tq:(g + 1) * tq] + pt_ref[n * N_REP + g, pl.ds(w0, kc), :] + negm
                 for g in range(N_REP)], axis=1)
            m_b = jnp.maximum(ms[n], jnp.max(lg, axis=0, keepdims=True).astype(F32)).astype(BF16)
            m_new = m_b.astype(F32)
            p = jnp.exp2(lg - m_b)
            alpha = jnp.exp2(ms[n] - m_new)
            acc_scr[n] = alpha * acc_scr[n] + jnp.dot(vT_ref[0, c, n], p, preferred_element_type=F32)
            new_ms.append(m_new)
        return tuple(new_ms)

    last_chunk = k_ref.shape[2] // kc - 1

    def pair_step(i, ms):
        c0 = 2 * i
        qk_chunk(c0 + 1, sb_scr)
        ms = softmax_pv(c0, sa_scr, ms)
        qk_chunk(jnp.minimum(c0 + 2, last_chunk), sa_scr)
        return softmax_pv(c0 + 1, sb_scr, ms)

    ms = tuple(jnp.full((1, N_REP * tq), NEG_BIG, F32) for _ in range(N_KV_HEADS))
    qk_chunk(0, sa_scr)
    lax.fori_loop(0, npair, pair_step, ms)
    for n in range(N_KV_HEADS):
        o_t = acc_scr[n, 0:HEAD_DIM, :] / acc_scr[n, HEAD_DIM:HEAD_DIM + 1, :]
        for g in range(N_REP):
            r0 = (n * N_REP + g) * HEAD_DIM
            oT_scr[r0:r0 + HEAD_DIM, :] = o_t[:, g * tq:(g + 1) * tq]

    o_ref[0] = oT_scr[...].T.astype(BF16)


def _attention(qT, qiT, wiT, k4, vT, ki, ptab, top_k):
    bsz, dq, seq = qT.shape
    tq = ATT_TQ
    assert seq % (2 * ATT_KC) == 0
    kern = functools.partial(_attn_kernel, top_k=top_k)
    return pl.pallas_call(
        kern,
        out_shape=jax.ShapeDtypeStruct((bsz, seq, dq), BF16),
        grid=(bsz, seq // tq),
        in_specs=[
            pl.BlockSpec((1, dq, tq), lambda b, i: (b, 0, i)),
            pl.BlockSpec((1, qiT.shape[1], tq), lambda b, i: (b, 0, i)),
            pl.BlockSpec((1, wiT.shape[1], tq), lambda b, i: (b, 0, i)),
            pl.BlockSpec((1,) + k4.shape[1:], lambda b, i: (b, 0, 0, 0)),
            pl.BlockSpec((1,) + vT.shape[1:], lambda b, i: (b, 0, 0, 0, 0)),
            pl.BlockSpec((1,) + ki.shape[1:], lambda b, i: (b, 0, 0)),
            pl.BlockSpec(ptab.shape, lambda b, i: (0, 0, 0)),
        ],
        out_specs=pl.BlockSpec((1, tq, dq), lambda b, i: (b, i, 0)),
        scratch_shapes=[
            pltpu.VMEM((seq, tq), I32),
            pltpu.VMEM((seq, tq), BF16),
            pltpu.VMEM((1, tq), I32),
            pltpu.VMEM((dq, tq), F32),
            pltpu.VMEM((N_KV_HEADS, ATT_VROWS, N_REP * tq), F32),
            pltpu.VMEM((N_KV_HEADS, HEAD_DIM, N_REP * tq), BF16),
            pltpu.VMEM((N_KV_HEADS, ATT_KC, N_REP * tq), BF16),
            pltpu.VMEM((N_KV_HEADS, ATT_KC, N_REP * tq), BF16),
        ],
        compiler_params=_cparams("parallel", "arbitrary"),
        name="attn_core",
    )(qT, qiT, wiT, k4, vT, ki, ptab)


def _proj_res_kernel(a_ref, x_ref, gate_ref, w_ref, o_ref):
    y = jnp.dot(a_ref[0], w_ref[...], preferred_element_type=F32)
    o_ref[0] = x_ref[0] + gate_ref[0] * y


def _proj_residual(a, x, gate, w, tm=512):
    bsz, seq, d = x.shape
    tm = min(tm, seq)
    return pl.pallas_call(
        _proj_res_kernel,
        out_shape=jax.ShapeDtypeStruct((bsz, seq, d), F32),
        grid=(bsz, seq // tm),
        in_specs=[
            pl.BlockSpec((1, tm, a.shape[2]), lambda b, i: (b, i, 0)),
            pl.BlockSpec((1, tm, d), lambda b, i: (b, i, 0)),
            pl.BlockSpec((1, 1, d), lambda b, i: (b, 0, 0)),
            pl.BlockSpec(w.shape, lambda b, i: (0, 0)),
        ],
        out_specs=pl.BlockSpec((1, tm, d), lambda b, i: (b, i, 0)),
        compiler_params=_cparams("parallel", "parallel"),
        name="proj_residual",
    )(a, x, gate, w)


def _attn_layer(x, g, sh, sc, gate, w_in, q_gain, k_gain, w_out, rel_bias):
    bsz, seq, d = x.shape
    top_k = min(TOPK_MAX, seq // 4)
    dq = N_HEADS * HEAD_DIM
    dk = N_KV_HEADS * HEAD_DIM
    dqi = IDX_HEADS * IDX_DIM
    nmain = dq + 2 * dk + dqi
    w_main = w_in[:, :nmain].astype(BF16)
    w_small = jnp.pad(w_in[:, nmain:], ((0, 0), (0, LANES - (IDX_DIM + IDX_HEADS)))).astype(BF16)
    q, k, v, qi, kiwi = _qkv_proj(x, g, sh, sc, w_main, w_small, q_gain, k_gain)
    ki = kiwi[:, :, :IDX_DIM].astype(BF16)
    wi = kiwi[:, :, IDX_DIM:IDX_DIM + IDX_HEADS] * (IDX_HEADS ** -0.5 * IDX_DIM ** -0.5)
    qT = jnp.swapaxes(q, 1, 2)
    qiT = jnp.swapaxes(qi, 1, 2)
    wiT = jnp.swapaxes(wi, 1, 2)
    nck = seq // ATT_KC
    vT = jnp.swapaxes(v.reshape(bsz, nck, ATT_KC, dk), 2, 3).reshape(bsz, nck, N_KV_HEADS, HEAD_DIM, ATT_KC)
    vT = jnp.concatenate([
        vT, jnp.ones((bsz, nck, N_KV_HEADS, 1, ATT_KC), BF16),
        jnp.zeros((bsz, nck, N_KV_HEADS, ATT_VROWS - HEAD_DIM - 1, ATT_KC), BF16)], axis=3)
    k4 = jnp.swapaxes(k.reshape(bsz, seq, N_KV_HEADS, HEAD_DIM), 1, 2)
    ptab = _bias_table(rel_bias)
    attn = _attention(qT, qiT, wiT, k4, vT, ki, ptab, top_k)
    return _proj_residual(attn, x, gate, w_out.astype(BF16))


def _s5_prep_kernel(lre_ref, lim_ref, ls_ref, bre_ref, bim_ref, cre_ref, cim_ref,
                    bcw_ref, mtw_ref, ccw_ref, are_ref, aim_ref):
    t_len = SSM_CHUNK
    lre = jnp.minimum(lre_ref[0], -1e-4)
    lim = lim_ref[0]
    step = jnp.exp(ls_ref[0])
    ar = lre * step
    ai = lim * step

    def powers(jv):
        mag = jnp.exp(jv * ar)
        return mag * jnp.cos(jv * ai), mag * jnp.sin(jv * ai)

    lb_re, lb_im = powers(1.0)
    nr = lb_re - 1.0
    ni = lb_im
    den = lre * lre + lim * lim
    cf_re = (nr * lre + ni * lim) / den
    cf_im = (ni * lre - nr * lim) / den
    bre = bre_ref[0]
    bim = bim_ref[0]
    bb_re = cf_re * bre - cf_im * bim
    bb_im = cf_re * bim + cf_im * bre
    cre = cre_ref[0]
    cim = cim_ref[0]
    nst = lre.shape[-1]
    jv = lax.broadcasted_iota(I32, (t_len, 1, nst), 0).astype(F32)
    pj_re, pj_im = powers(jv)
    a_re = (cre[None] * pj_re - cim[None] * pj_im).reshape(t_len * SSM_GROUP, nst)
    a_im = (cre[None] * pj_im + cim[None] * pj_re).reshape(t_len * SSM_GROUP, nst)
    dn = (((1,), (1,)), ((), ()))
    cg = SSM_GROUP
    tc = t_len * cg
    width = t_len * LANES
    gq = pl.program_id(0) % (LANES // cg)
    nh = (LANES // cg) * nst
    cg_shift = cg.bit_length() - 1

    def place(nrows, target):
        r = lax.broadcasted_iota(I32, (nrows, width), 0)
        col = lax.broadcasted_iota(I32, (nrows, width), 1)
        return jnp.where(col == target(r), 1.0, 0.0).astype(BF16)

    pm = place(tc, lambda r: lax.shift_right_logical(r, cg_shift) * LANES + gq * cg + (r & (cg - 1)))
    k_t = (lax.dot_general(bb_re, a_re, dn, preferred_element_type=F32, precision=HIGHEST)
           - lax.dot_general(bb_im, a_im, dn, preferred_element_type=F32, precision=HIGHEST))
    lane = lax.broadcasted_iota(I32, (cg, tc), 1)
    mt_rows = [k_t] + [jnp.where(lane >= s * cg, pltpu.roll(k_t, s * cg, 1), 0.0) for s in range(1, t_len)]
    mt_t = jnp.concatenate(mt_rows, axis=0)
    mtw_ref[0] = jnp.dot(mt_t.astype(BF16), pm, preferred_element_type=F32).astype(BF16)
    pr_re, pr_im = powers((t_len - 1.0) - jv)
    bc_re = (bb_re[None] * pr_re - bb_im[None] * pr_im).reshape(tc, nst)
    bc_im = (bb_re[None] * pr_im + bb_im[None] * pr_re).reshape(tc, nst)
    pb_re = place(nst, lambda r: gq * nst + r)
    pb_im = place(nst, lambda r: nh + gq * nst + r)
    bcw_ref[0] = (jnp.dot(bc_re.astype(BF16), pb_re, preferred_element_type=F32)
                  + jnp.dot(bc_im.astype(BF16), pb_im, preferred_element_type=F32)).astype(BF16)
    pn_re, pn_im = powers(jv + 1.0)
    cc_re = (cre[None] * pn_re - cim[None] * pn_im).reshape(tc, nst)
    cc_im = (cre[None] * pn_im + cim[None] * pn_re).reshape(tc, nst)
    eye = jnp.where(lax.broadcasted_iota(I32, (nst, nst), 0) == lax.broadcasted_iota(I32, (nst, nst), 1),
                    1.0, 0.0)
    cct_re = lax.dot_general(eye, cc_re, dn, preferred_element_type=F32, precision=HIGHEST)
    cct_im = lax.dot_general(eye, cc_im, dn, preferred_element_type=F32, precision=HIGHEST)
    ccw_ref[0, 0:nst, :] = jnp.dot(cct_re.astype(BF16), pm, preferred_element_type=F32).astype(BF16)
    ccw_ref[0, nst:2 * nst, :] = (-jnp.dot(cct_im.astype(BF16), pm, preferred_element_type=F32)).astype(BF16)
    at_re, at_im = powers(float(t_len))
    rs = lax.broadcasted_iota(I32, (nst, nh), 0)
    cs = lax.broadcasted_iota(I32, (nst, nh), 1)
    pa = jnp.where(cs == gq * nst + rs, 1.0, 0.0)
    are_ref[0] = jnp.dot(at_re, pa, preferred_element_type=F32, precision=HIGHEST)
    aim_ref[0] = jnp.dot(at_im, pa, preferred_element_type=F32, precision=HIGHEST)


def _s5_prep(lam_re, lam_im, log_step, b_re, b_im, c_re, c_im):
    ng, nst = lam_re.shape
    tc = SSM_CHUNK * SSM_GROUP
    vec = pl.BlockSpec((1, 1, nst), lambda gi: (gi, 0, 0))
    mat = pl.BlockSpec((1, SSM_GROUP, nst), lambda gi: (gi, 0, 0))
    width = SSM_CHUNK * LANES
    assert tc == 2 * nst and 2 * (LANES // SSM_GROUP) * nst == width
    big = pl.BlockSpec((1, tc, width), lambda gi: (gi, 0, 0))
    wide = jax.ShapeDtypeStruct((ng, tc, width), BF16)
    return pl.pallas_call(
        _s5_prep_kernel,
        out_shape=[wide, wide, wide,
                   jax.ShapeDtypeStruct((ng, 1, width // 2), F32),
                   jax.ShapeDtypeStruct((ng, 1, width // 2), F32)],
        grid=(ng,),
        in_specs=[vec, vec, pl.BlockSpec((1, 1, 1), lambda gi: (gi, 0, 0)), mat, mat, mat, mat],
        out_specs=[big, big, big, pl.BlockSpec((1, 1, width // 2), lambda gi: (gi, 0, 0)),
                   pl.BlockSpec((1, 1, width // 2), lambda gi: (gi, 0, 0))],
        compiler_params=_cparams("parallel"),
        name="s5_prep",
    )(lam_re.reshape(ng, 1, nst), lam_im.reshape(ng, 1, nst), log_step.reshape(ng, 1, 1),
      jnp.swapaxes(b_re, 1, 2), jnp.swapaxes(b_im, 1, 2), c_re, c_im)


def _s5_pre_kernel(x_ref, g_ref, sh_ref, sc_ref, o_ref):
    h = _norm_mod(x_ref[0], g_ref[...], sh_ref[0], sc_ref[0]).astype(BF16)
    for q in range(o_ref.shape[0]):
        o_ref[q] = h[:, q * LANES:(q + 1) * LANES]


def _s5_pre(x, g, sh, sc, tm=512):
    bsz, seq, d = x.shape
    tm = min(tm, seq)
    nt = seq // tm
    nq = d // LANES
    mod_spec = pl.BlockSpec((1, 1, d), lambda b, i: (b, 0, 0))
    return pl.pallas_call(
        _s5_pre_kernel,
        out_shape=jax.ShapeDtypeStruct((nq, bsz * seq, LANES), BF16),
        grid=(bsz, nt),
        in_specs=[pl.BlockSpec((1, tm, d), lambda b, i: (b, i, 0)),
                  pl.BlockSpec((1, d), lambda b, i: (0, 0)), mod_spec, mod_spec],
        out_specs=pl.BlockSpec((nq, tm, LANES), lambda b, i: (0, b * nt + i, 0)),
        compiler_params=_cparams("parallel", "parallel"),
        name="s5_pre",
    )(x, g, sh, sc)


def _s5_scan_kernel(x_ref, bc_ref, mt_ref, cc_ref, are_ref, aim_ref, y_ref, re_scr, im_scr):
    x = x_ref[0]
    nrow = x.shape[0]
    pad = re_scr.shape[0] - nrow
    v = jnp.dot(x, bc_ref[0], preferred_element_type=F32)
    nh = v.shape[1] // 2
    s_re = v[:, :nh]
    s_im = v[:, nh:]
    a_re = are_ref[0]
    a_im = aim_ref[0]
    re_scr[0:pad, :] = jnp.zeros((pad, nh), F32)
    im_scr[0:pad, :] = jnp.zeros((pad, nh), F32)

    def shifted(scr, val, dist):
        scr[pad:pad + nrow, :] = val
        return scr[pad - dist:pad - dist + nrow, :]

    dist = 1
    while dist < nrow:
        sh_re = shifted(re_scr, s_re, dist)
        sh_im = shifted(im_scr, s_im, dist)
        s_re, s_im = (s_re + a_re * sh_re - a_im * sh_im, s_im + a_re * sh_im + a_im * sh_re)
        a_re, a_im = (a_re * a_re - a_im * a_im, 2.0 * a_re * a_im)
        dist *= 2
    sp = jnp.concatenate([shifted(re_scr, s_re, 1), shifted(im_scr, s_im, 1)], axis=1).astype(BF16)
    y_ref[0] = (jnp.dot(x, mt_ref[0], preferred_element_type=F32)
                + jnp.dot(sp, cc_ref[0], preferred_element_type=F32))


def _s5_scan(hq, bcq, mtq, ccq, a_re, a_im, bsz):
    nq, nrows, width = hq.shape
    nj = nrows // bsz
    nh = a_re.shape[-1]
    pad = max(nj // 2, SUBLANES)
    tile = pl.BlockSpec((1, nj, width), lambda q, b: (q, b, 0))
    wspec = lambda a: pl.BlockSpec((1,) + a.shape[1:], lambda q, b: (q, 0, 0))
    return pl.pallas_call(
        _s5_scan_kernel,
        out_shape=jax.ShapeDtypeStruct((nq, nrows, width), F32),
        grid=(nq, bsz),
        in_specs=[tile, wspec(bcq), wspec(mtq), wspec(ccq), wspec(a_re), wspec(a_im)],
        out_specs=tile,
        scratch_shapes=[pltpu.VMEM((pad + nj, nh), F32), pltpu.VMEM((pad + nj, nh), F32)],
        compiler_params=_cparams("parallel", "parallel"),
        name="s5_scan",
    )(hq, bcq, mtq, ccq, a_re, a_im)


def _s5_post_kernel(x_ref, y_ref, g_ref, sh_ref, sc_ref, gate_ref, dsk_ref, w_ref, o_ref, *, d):
    x = x_ref[0]
    h = _norm_mod(x, g_ref[...], sh_ref[0], sc_ref[0])
    y = jnp.concatenate([y_ref[q] for q in range(y_ref.shape[0])], axis=1)
    yy = y + dsk_ref[...] * h
    gl = jax.nn.gelu(yy).astype(BF16)
    z = jnp.dot(gl, w_ref[...], preferred_element_type=F32)
    o_ref[0] = x + gate_ref[0] * (z[:, :d] * jax.nn.sigmoid(z[:, d:]))


def _s5_post(x, yq, g, sh, sc, gate, d_skip, w_glu, tm=512):
    bsz, seq, d = x.shape
    tm = min(tm, seq)
    nt = seq // tm
    nq = d // LANES
    kern = functools.partial(_s5_post_kernel, d=d)
    mod_spec = pl.BlockSpec((1, 1, d), lambda b, i: (b, 0, 0))
    tile = pl.BlockSpec((1, tm, d), lambda b, i: (b, i, 0))
    return pl.pallas_call(
        kern,
        out_shape=jax.ShapeDtypeStruct((bsz, seq, d), F32),
        grid=(bsz, nt),
        in_specs=[tile, pl.BlockSpec((nq, tm, LANES), lambda b, i: (0, b * nt + i, 0)),
                  pl.BlockSpec((1, d), lambda b, i: (0, 0)), mod_spec, mod_spec, mod_spec,
                  pl.BlockSpec((1, d), lambda b, i: (0, 0)),
                  pl.BlockSpec(w_glu.shape, lambda b, i: (0, 0))],
        out_specs=tile,
        compiler_params=_cparams("parallel", "parallel"),
        name="s5_post",
    )(x, yq, g, sh, sc, gate, d_skip, w_glu)


def _s5_layer(x, g, sh, sc, gate, lam_re, lam_im, log_step, b_re, b_im, c_re, c_im, d_skip, w_glu):
    bsz, seq, d = x.shape
    ng, nst = lam_re.shape
    t_len, cg = SSM_CHUNK, SSM_GROUP
    nq = d // LANES
    gq = LANES // cg
    nj = seq // t_len
    bcw, mtw, ccw, a_re, a_im = _s5_prep(lam_re, lam_im, log_step, b_re, b_im, c_re, c_im)
    width = t_len * LANES
    bcq = jnp.swapaxes(bcw.reshape(nq, gq, t_len, cg, width), 1, 2).reshape(nq, width, width)
    mtq = jnp.swapaxes(mtw.reshape(nq, gq, t_len, cg, width), 1, 2).reshape(nq, width, width)
    ccq = jnp.swapaxes(ccw.reshape(nq, gq, 2, nst, width), 1, 2).reshape(nq, 2 * gq * nst, width)
    aq_re = jnp.sum(a_re.reshape(nq, gq, 1, gq * nst), axis=1)
    aq_im = jnp.sum(a_im.reshape(nq, gq, 1, gq * nst), axis=1)

    hq = _s5_pre(x, g, sh, sc).reshape(nq, bsz * nj, t_len * LANES)
    yq = _s5_scan(hq, bcq, mtq, ccq, aq_re, aq_im, bsz).reshape(nq, bsz * seq, LANES)
    return _s5_post(x, yq, g, sh, sc, gate, d_skip.reshape(1, d).astype(F32), w_glu.astype(BF16))


def kernel(x, c, ada_w, ada_b, norm_g, conv_w_in, conv_w, conv_w_out, attn_w_in, attn_q_gain, attn_k_gain, attn_w_out, rel_bias, ssm_lambda_re, ssm_lambda_im, ssm_log_step, ssm_b_re, ssm_b_im, ssm_c_re, ssm_c_im, ssm_d, ssm_w_glu, ffn_w_gu, ffn_w_down, moe_router_w, moe_router_b, moe_w_gu, moe_w_down):
    bsz, seq, d = x.shape
    depth = ada_w.shape[0]
    mod = _ada_mod(c, ada_w, ada_b).reshape(depth, bsz, 6, 1, d)
    moe_gu = moe_w_gu.astype(BF16).reshape((-1,) + moe_w_gu.shape[2:])
    moe_down = moe_w_down.astype(BF16).reshape((-1,) + moe_w_down.shape[2:])
    for i in range(depth):
        sh1, sc1, g1, sh2, sc2, g2 = (mod[i, :, r] for r in range(6))
        gn1 = norm_g[i, 0].reshape(1, d)
        gn2 = norm_g[i, 1].reshape(1, d)
        j = i // N_MIXERS
        if i % N_MIXERS == 0:
            x = _conv_layer(x, gn1, sh1, sc1, g1, conv_w_in[j].astype(BF16), conv_w[j],
                            conv_w_out[j].astype(BF16))
        elif i % N_MIXERS == 1:
            x = _attn_layer(x, gn1, sh1, sc1, g1, attn_w_in[j], attn_q_gain[j], attn_k_gain[j],
                            attn_w_out[j], rel_bias)
        else:
            x = _s5_layer(x, gn1, sh1, sc1, g1, ssm_lambda_re[j], ssm_lambda_im[j], ssm_log_step[j],
                          ssm_b_re[j], ssm_b_im[j], ssm_c_re[j], ssm_c_im[j], ssm_d[j], ssm_w_glu[j])
        if i % 2 == 0:
            x = _ffn_layer(x, gn2, sh2, sc2, g2, ffn_w_gu[i // 2].astype(BF16), ffn_w_down[i // 2].astype(BF16))
        else:
            x = _moe_layer(x, gn2, sh2, sc2, g2, moe_router_w[i // 2], moe_router_b[i // 2],
                           moe_gu, moe_down, ebase=(i // 2) * moe_w_gu.shape[1])
    return x
```

```python
import functools
import math

import numpy as np
import jax
import jax.numpy as jnp
from jax import lax
from jax.experimental import pallas as pl
from jax.experimental.pallas import tpu as pltpu

F32 = jnp.float32
BF16 = jnp.bfloat16
I32 = jnp.int32
HIGHEST = lax.Precision.HIGHEST

DEPTH = 4
N_MIXERS = 3
EPS = 1e-6
CONV_WIDTH = 3
N_HEADS = 16
N_KV_HEADS = 4
N_REP = N_HEADS // N_KV_HEADS
HEAD_DIM = 64
IDX_HEADS = 8
IDX_DIM = 64
TOPK_MAX = 256
REL_BUCKETS = 32
REL_MAX_DIST = 128
SSM_GROUP = 16
SSM_STATE = 64
N_EXPERTS = 8
TOP_K_EXPERTS = 2

VMEM_LIMIT_BYTES = 56 * 1024 * 1024
LANES = 128
SUBLANES = 8

INT_MIN = -(2 ** 31)
INT_MAX = 2 ** 31 - 1
NEG_BIG = -1e30
LOG2E = 1.4426950408889634

ATT_TQ = 128
ATT_KC = 256
BIAS_C = 384
BIAS_W = BIAS_C + ATT_KC
ATT_VROWS = 80

SSM_CHUNK = 8


def _cparams(*sem):
    return pltpu.CompilerParams(dimension_semantics=sem, vmem_limit_bytes=VMEM_LIMIT_BYTES)


def _norm_mod(x, g, shift, scale):
    ms = jnp.mean(x * x, axis=-1, keepdims=True)
    y = x * lax.rsqrt(ms + EPS)
    return (y * g) * (1.0 + scale) + shift


def _silu(x):
    return x * jax.nn.sigmoid(x)


def _ada_kernel(c_ref, w_ref, b_ref, o_ref):
    c = c_ref[...]
    cond = _silu(c)
    o_ref[0] = jnp.dot(cond, w_ref[0], preferred_element_type=F32, precision=HIGHEST) + b_ref[0]


def _ada_mod(c, ada_w, ada_b):
    depth, d, d6 = ada_w.shape
    bsz = c.shape[0]
    tn = d
    return pl.pallas_call(
        _ada_kernel,
        out_shape=jax.ShapeDtypeStruct((depth, bsz, d6), F32),
        grid=(depth, d6 // tn),
        in_specs=[
            pl.BlockSpec((bsz, d), lambda i, j: (0, 0)),
            pl.BlockSpec((1, d, tn), lambda i, j: (i, 0, j)),
            pl.BlockSpec((1, 1, tn), lambda i, j: (i, 0, j)),
        ],
        out_specs=pl.BlockSpec((1, bsz, tn), lambda i, j: (i, 0, j)),
        compiler_params=_cparams("parallel", "parallel"),
        name="ada_mod",
    )(c, ada_w, ada_b.reshape(depth, 1, d6))


def _conv_kernel(x_ref, xh_ref, g_ref, sh_ref, sc_ref, gate_ref, win_ref, wc_ref, wout_ref,
                 o_ref, u_scr, *, tm, d):
    i = pl.program_id(1)
    g = g_ref[...]
    sh = sh_ref[0]
    sc = sc_ref[0]
    x = x_ref[0]
    h = _norm_mod(x, g, sh, sc).astype(BF16)
    z = jnp.dot(h, win_ref[...], preferred_element_type=F32)
    b_gate = z[:, :d]
    u = z[:, d:2 * d] * z[:, 2 * d:]
    hh = _norm_mod(xh_ref[0], g, sh, sc).astype(BF16)
    zh = jnp.dot(hh, win_ref[:, d:], preferred_element_type=F32)
    uh = zh[:, :d] * zh[:, d:]
    uh = jnp.where(i > 0, uh, 0.0)
    u_scr[0:SUBLANES, :] = uh
    u_scr[SUBLANES:SUBLANES + tm, :] = u
    wc = wc_ref[...]
    conv = (wc[0:1, :] * u_scr[SUBLANES - 2:SUBLANES - 2 + tm, :]
            + wc[1:2, :] * u_scr[SUBLANES - 1:SUBLANES - 1 + tm, :]
            + wc[2:3, :] * u)
    y = jnp.dot((b_gate * conv).astype(BF16), wout_ref[...], preferred_element_type=F32)
    o_ref[0] = x + gate_ref[0] * y


def _conv_layer(x, g, sh, sc, gate, w_in, w_conv, w_out, tm=512):
    bsz, seq, d = x.shape
    tm = min(tm, seq)
    nt = seq // tm
    hb = tm // SUBLANES
    kern = functools.partial(_conv_kernel, tm=tm, d=d)
    mod_spec = pl.BlockSpec((1, 1, d), lambda b, i: (b, 0, 0))
    return pl.pallas_call(
        kern,
        out_shape=jax.ShapeDtypeStruct((bsz, seq, d), F32),
        grid=(bsz, nt),
        in_specs=[
            pl.BlockSpec((1, tm, d), lambda b, i: (b, i, 0)),
            pl.BlockSpec((1, SUBLANES, d), lambda b, i: (b, jnp.maximum(i * hb - 1, 0), 0)),
            pl.BlockSpec((1, d), lambda b, i: (0, 0)),
            mod_spec, mod_spec, mod_spec,
            pl.BlockSpec((d, 3 * d), lambda b, i: (0, 0)),
            pl.BlockSpec((CONV_WIDTH, d), lambda b, i: (0, 0)),
            pl.BlockSpec((d, d), lambda b, i: (0, 0)),
        ],
        out_specs=pl.BlockSpec((1, tm, d), lambda b, i: (b, i, 0)),
        scratch_shapes=[pltpu.VMEM((tm + SUBLANES, d), F32)],
        compiler_params=_cparams("parallel", "parallel"),
        name="conv_mixer",
    )(x, x, g, sh, sc, gate, w_in, w_conv, w_out)


def _ffn_kernel(x_ref, g_ref, sh_ref, sc_ref, gate_ref, wgu_ref, wd_ref, o_ref, *, dff, nchunk):
    x = x_ref[0]
    h = _norm_mod(x, g_ref[...], sh_ref[0], sc_ref[0]).astype(BF16)
    cols = dff // nchunk
    acc = jnp.zeros(x.shape, F32)
    for c in range(nchunk):
        gg = jnp.dot(h, wgu_ref[:, c * cols:(c + 1) * cols], preferred_element_type=F32)
        uu = jnp.dot(h, wgu_ref[:, dff + c * cols:dff + (c + 1) * cols], preferred_element_type=F32)
        a = (_silu(gg) * uu).astype(BF16)
        acc = acc + jnp.dot(a, wd_ref[c * cols:(c + 1) * cols, :], preferred_element_type=F32)
    o_ref[0] = x + gate_ref[0] * acc


def _ffn_layer(x, g, sh, sc, gate, w_gu, w_down, tm=512):
    bsz, seq, d = x.shape
    dff = w_down.shape[0]
    tm = min(tm, seq)
    kern = functools.partial(_ffn_kernel, dff=dff, nchunk=2)
    mod_spec = pl.BlockSpec((1, 1, d), lambda b, i: (b, 0, 0))
    return pl.pallas_call(
        kern,
        out_shape=jax.ShapeDtypeStruct((bsz, seq, d), F32),
        grid=(bsz, seq // tm),
        in_specs=[
            pl.BlockSpec((1, tm, d), lambda b, i: (b, i, 0)),
            pl.BlockSpec((1, d), lambda b, i: (0, 0)),
            mod_spec, mod_spec, mod_spec,
            pl.BlockSpec((d, 2 * dff), lambda b, i: (0, 0), pipeline_mode=pl.Buffered(1)),
            pl.BlockSpec((dff, d), lambda b, i: (0, 0), pipeline_mode=pl.Buffered(1)),
        ],
        out_specs=pl.BlockSpec((1, tm, d), lambda b, i: (b, i, 0)),
        compiler_params=_cparams("parallel", "parallel"),
        name="ffn_dense",
    )(x, g, sh, sc, gate, w_gu, w_down)


MOE_SB = 512
MOE_RT = 512
MOE_ALIGN = 16
MOE_WIN = 256
MOE_CAP = MOE_WIN - MOE_ALIGN


def _moe_router_kernel(x_ref, g_ref, sh_ref, sc_ref, rw_ref, rb_ref, h_ref, meta_ref):
    hf = _norm_mod(x_ref[...], g_ref[...], sh_ref[0], sc_ref[0])
    h_ref[...] = hf.astype(BF16)
    logits = jnp.dot(hf, rw_ref[...], preferred_element_type=F32, precision=HIGHEST) + rb_ref[...]
    mx = jnp.max(logits, axis=-1, keepdims=True)
    ex = jnp.exp(logits - mx)
    probs = ex / jnp.sum(ex, axis=-1, keepdims=True)
    lane = lax.broadcasted_iota(I32, probs.shape, 1)
    m1 = jnp.max(probs, axis=-1, keepdims=True)
    i1 = jnp.min(jnp.where(probs == m1, lane, LANES), axis=-1, keepdims=True)
    rest = jnp.where(lane == i1, -1.0, probs)
    m2 = jnp.max(rest, axis=-1, keepdims=True)
    i2 = jnp.min(jnp.where(rest == m2, lane, LANES), axis=-1, keepdims=True)
    den = m1 + m2
    gates = jnp.where(lane == i1, m1 / den, 0.0) + jnp.where(lane == i2, m2 / den, 0.0)
    chosen = jnp.where(jnp.logical_or(lane == i1, lane == i2), 1.0, 0.0)
    nrow = chosen.shape[0]
    below = jnp.where(lax.broadcasted_iota(I32, (nrow, nrow), 1) < lax.broadcasted_iota(I32, (nrow, nrow), 0),
                      1.0, 0.0).astype(BF16)
    rank = jnp.dot(below, chosen.astype(BF16), preferred_element_type=F32)
    meta_ref[...] = gates + pltpu.roll(chosen, N_EXPERTS, 1) + pltpu.roll(rank, 2 * N_EXPERTS, 1)


def _moe_router(xt, g, sh, sc, rw_pad, rb_pad, tpb, tm):
    n, d = xt.shape
    mod_spec = pl.BlockSpec((1, 1, d), lambda i: (i // tpb, 0, 0))
    return pl.pallas_call(
        _moe_router_kernel,
        out_shape=[jax.ShapeDtypeStruct((n, d), BF16), jax.ShapeDtypeStruct((n, LANES), F32)],
        grid=(n // tm,),
        in_specs=[pl.BlockSpec((tm, d), lambda i: (i, 0)),
                  pl.BlockSpec((1, d), lambda i: (0, 0)), mod_spec, mod_spec,
                  pl.BlockSpec((d, LANES), lambda i: (0, 0)),
                  pl.BlockSpec((1, LANES), lambda i: (0, 0))],
        out_specs=[pl.BlockSpec((tm, d), lambda i: (i, 0)), pl.BlockSpec((tm, LANES), lambda i: (i, 0))],
        compiler_params=_cparams("parallel"),
        name="moe_router",
    )(xt, g, sh, sc, rw_pad, rb_pad)


def _window(start, count, w):
    s = start + w * MOE_CAP
    n = jnp.minimum(count - w * MOE_CAP, MOE_CAP)
    a = pl.multiple_of((s // MOE_ALIGN) * MOE_ALIGN, MOE_ALIGN)
    return s, n, a


def _moe_dispatch_kernel(start_ref, cnt_ref, h_ref, pos_ref, xs_init_ref, xs_ref,
                         buf, carry, sems, pending):
    del xs_init_ref
    b = pl.program_id(0)
    nb = pl.num_programs(0)
    ne = buf.shape[0]

    @pl.when(b == 0)
    def _():
        carry[...] = jnp.zeros(carry.shape, BF16)
        for e in range(ne):
            pending[e] = 0

    h = h_ref[...]
    riota = lax.broadcasted_iota(I32, (MOE_WIN, h.shape[0]), 0)

    def out_copy(e, a):
        return pltpu.make_async_copy(buf.at[e], xs_ref.at[pl.ds(a, MOE_WIN)], sems.at[e])

    for e in range(ne):
        posrow = pos_ref[0, e:e + 1, :]
        nwin = (cnt_ref[b, e] + MOE_CAP - 1) // MOE_CAP

        def wbody(w, carry_unused, e=e, posrow=posrow):
            s, n, a = _window(start_ref[b, e], cnt_ref[b, e], w)

            @pl.when(pending[e] == 1)
            def _():
                out_copy(e, 0).wait()

            hit = jnp.logical_and(posrow - a == riota,
                                  jnp.logical_and(posrow >= s, posrow < s + n))
            onehot = jnp.where(hit, 1.0, 0.0).astype(BF16)
            buf[e] = jnp.dot(onehot, h, preferred_element_type=F32).astype(BF16)
            buf[e, 0:MOE_ALIGN, :] = buf[e, 0:MOE_ALIGN, :] + carry[e]
            c0 = pl.multiple_of(((s + n) // MOE_ALIGN) * MOE_ALIGN - a, MOE_ALIGN)
            carry[e] = buf[e, pl.ds(c0, MOE_ALIGN), :]
            out_copy(e, a).start()
            pending[e] = 1
            return carry_unused

        lax.fori_loop(0, nwin, wbody, 0)

    @pl.when(b == nb - 1)
    def _():
        for e in range(ne):
            @pl.when(pending[e] == 1)
            def _():
                out_copy(e, 0).wait()


def _moe_dispatch(start, cnt, h, pos_t, ncap):
    n, d = h.shape
    nb, ne, sb = pos_t.shape
    xs_init = jnp.zeros((ncap, d), BF16)
    return pl.pallas_call(
        _moe_dispatch_kernel,
        out_shape=jax.ShapeDtypeStruct((ncap, d), BF16),
        grid_spec=pltpu.PrefetchScalarGridSpec(
            num_scalar_prefetch=2,
            grid=(nb,),
            in_specs=[pl.BlockSpec((sb, d), lambda b, *_: (b, 0)),
                      pl.BlockSpec((1, ne, sb), lambda b, *_: (b, 0, 0)),
                      pl.BlockSpec(memory_space=pl.ANY)],
            out_specs=pl.BlockSpec(memory_space=pl.ANY),
            scratch_shapes=[pltpu.VMEM((ne, MOE_WIN, d), BF16),
                            pltpu.VMEM((ne, MOE_ALIGN, d), BF16),
                            pltpu.SemaphoreType.DMA((ne,)),
                            pltpu.SMEM((ne,), I32)],
        ),
        input_output_aliases={4: 0},
        compiler_params=_cparams("arbitrary"),
        name="moe_dispatch",
    )(start, cnt, h, pos_t, xs_init)


def _moe_ffn_kernel(blk_ref, exp_ref, nt_ref, xs_ref, wgu_ref, wd_ref, ys_init_ref, ys_ref, *, dff, nchunk):
    del blk_ref, exp_ref, ys_init_ref
    k = pl.program_id(0)

    @pl.when(k < nt_ref[0])
    def _():
        x = xs_ref[...]
        cols = dff // nchunk
        acc = jnp.zeros(x.shape, F32)
        for c in range(nchunk):
            gg = jnp.dot(x, wgu_ref[0, :, c * cols:(c + 1) * cols], preferred_element_type=F32)
            uu = jnp.dot(x, wgu_ref[0, :, dff + c * cols:dff + (c + 1) * cols], preferred_element_type=F32)
            a = (_silu(gg) * uu).astype(BF16)
            acc = acc + jnp.dot(a, wd_ref[0, c * cols:(c + 1) * cols, :], preferred_element_type=F32)
        ys_ref[...] = acc.astype(BF16)


def _moe_ffn(tile_blk, tile_exp, ntiles, xs, w_gu, w_down, ebase):
    ncap, d = xs.shape
    dff = w_down.shape[1]
    kern = functools.partial(_moe_ffn_kernel, dff=dff, nchunk=2)
    ys_init = jnp.zeros((ncap, d), BF16)
    return pl.pallas_call(
        kern,
        out_shape=jax.ShapeDtypeStruct((ncap, d), BF16),
        grid_spec=pltpu.PrefetchScalarGridSpec(
            num_scalar_prefetch=3,
            grid=(ncap // MOE_RT,),
            in_specs=[pl.BlockSpec((MOE_RT, d), lambda k, blk, exp, nt: (blk[k], 0)),
                      pl.BlockSpec((1, d, 2 * dff), lambda k, blk, exp, nt: (ebase + exp[k], 0, 0)),
                      pl.BlockSpec((1, dff, d), lambda k, blk, exp, nt: (ebase + exp[k], 0, 0)),
                      pl.BlockSpec(memory_space=pl.ANY)],
            out_specs=pl.BlockSpec((MOE_RT, d), lambda k, blk, exp, nt: (blk[k], 0)),
        ),
        input_output_aliases={6: 0},
        compiler_params=_cparams("arbitrary"),
        name="moe_ffn",
    )(tile_blk, tile_exp, ntiles, xs, w_gu, w_down, ys_init)


def _moe_combine_kernel(start_ref, cnt_ref, x_ref, gate_ref, pos_ref, gts_ref, ys_ref, o_ref,
                        buf, sems, acc_scr):
    b = pl.program_id(0)
    ne = buf.shape[0]
    sb = x_ref.shape[0]

    def in_copy(e, a):
        return pltpu.make_async_copy(ys_ref.at[pl.ds(a, MOE_WIN)], buf.at[e], sems.at[e])

    for e in range(ne):
        @pl.when(cnt_ref[b, e] > 0)
        def _():
            _, _, a = _window(start_ref[b, e], cnt_ref[b, e], 0)
            in_copy(e, a).start()

    acc_scr[...] = jnp.zeros(acc_scr.shape, F32)
    liota = lax.broadcasted_iota(I32, (sb, MOE_WIN), 1)
    for e in range(ne):
        poscol = pos_ref[:, e:e + 1]
        gcol = gts_ref[:, e:e + 1]
        nwin = (cnt_ref[b, e] + MOE_CAP - 1) // MOE_CAP

        def wbody(w, carry_unused, e=e, poscol=poscol, gcol=gcol):
            s, n, a = _window(start_ref[b, e], cnt_ref[b, e], w)

            @pl.when(w > 0)
            def _():
                in_copy(e, a).start()

            in_copy(e, a).wait()
            hit = jnp.logical_and(poscol - a == liota,
                                  jnp.logical_and(poscol >= s, poscol < s + n))
            onehot = jnp.where(hit, 1.0, 0.0).astype(BF16)
            acc_scr[...] += gcol * jnp.dot(onehot, buf[e], preferred_element_type=F32)
            return carry_unused

        lax.fori_loop(0, nwin, wbody, 0)

    o_ref[...] = x_ref[...] + gate_ref[0] * acc_scr[...]


def _moe_combine(start, cnt, xt, gate, pos_n, gates_n, ys, tpb):
    n, d = xt.shape
    nb, ne = cnt.shape
    sb = n // nb
    return pl.pallas_call(
        _moe_combine_kernel,
        out_shape=jax.ShapeDtypeStruct((n, d), F32),
        grid_spec=pltpu.PrefetchScalarGridSpec(
            num_scalar_prefetch=2,
            grid=(nb,),
            in_specs=[pl.BlockSpec((sb, d), lambda b, *_: (b, 0)),
                      pl.BlockSpec((1, 1, d), lambda b, *_: (b // tpb, 0, 0)),
                      pl.BlockSpec((sb, ne), lambda b, *_: (b, 0)),
                      pl.BlockSpec((sb, ne), lambda b, *_: (b, 0)),
                      pl.BlockSpec(memory_space=pl.ANY)],
            out_specs=pl.BlockSpec((sb, d), lambda b, *_: (b, 0)),
            scratch_shapes=[pltpu.VMEM((ne, MOE_WIN, d), BF16),
                            pltpu.SemaphoreType.DMA((ne,)),
                            pltpu.VMEM((sb, d), F32)],
        ),
        compiler_params=_cparams("arbitrary"),
        name="moe_combine",
    )(start, cnt, xt, gate, pos_n, gates_n, ys)


def _moe_layer(x, g, sh, sc, gate, router_w, router_b, w_gu, w_down, ebase=0):
    bsz, seq, d = x.shape
    ne = router_w.shape[1]
    n = bsz * seq
    sb = min(MOE_SB, seq)
    tpb = seq // sb
    nb = n // sb
    rt = MOE_RT
    xt = x.reshape(n, d)
    rw_pad = jnp.pad(router_w.astype(F32), ((0, 0), (0, LANES - ne)))
    rb_pad = jnp.pad(router_b.astype(F32).reshape(1, ne), ((0, 0), (0, LANES - ne)), constant_values=NEG_BIG)
    h, meta = _moe_router(xt, g, sh, sc, rw_pad, rb_pad, tpb, sb)
    gates = meta[:, :ne]
    sel = (meta[:, ne:2 * ne] > 0.5).astype(I32)

    selb = sel.reshape(nb, sb, ne)
    cnt = jnp.sum(selb, axis=1)
    rank = meta[:, 2 * ne:3 * ne].astype(I32).reshape(nb, sb, ne)
    total = jnp.sum(cnt, axis=0)
    region = ((total + MOE_WIN + rt - 1) // rt) * rt
    off = jnp.cumsum(region) - region
    start = (off[None, :] + jnp.cumsum(cnt, axis=0) - cnt).astype(I32)
    pos = jnp.where(selb > 0, start[:, None, :] + rank, -1).astype(I32)
    ncap = 2 * n + ne * (MOE_WIN + rt)
    tiles_e = (total + rt - 1) // rt
    tcum = jnp.cumsum(tiles_e)
    ntiles = tcum[-1]
    kk = jnp.minimum(jnp.arange(ncap // rt), ntiles - 1)
    tile_exp = jnp.searchsorted(tcum, kk, side='right').astype(I32)
    tile_blk = (off[tile_exp] // rt + kk - (tcum - tiles_e)[tile_exp]).astype(I32)

    xs = _moe_dispatch(start, cnt.astype(I32), h, jnp.swapaxes(pos, 1, 2), ncap)
    ys = _moe_ffn(tile_blk, tile_exp, ntiles.reshape(1).astype(I32), xs, w_gu, w_down, ebase)
    out = _moe_combine(start, cnt.astype(I32), xt, gate, pos.reshape(n, ne), gates, ys, tpb)
    return out.reshape(bsz, seq, d)


def _head_norm(q, hsum_ref, hexp_ref, gain, scale):
    ms = jnp.dot((q * q).astype(BF16), hsum_ref[...], preferred_element_type=F32)
    r = lax.rsqrt(ms + EPS)
    r_hi = r.astype(BF16)
    r_lo = (r - r_hi.astype(F32)).astype(BF16)
    rexp = (jnp.dot(r_hi, hexp_ref[...], preferred_element_type=F32)
            + jnp.dot(r_lo, hexp_ref[...], preferred_element_type=F32))
    return q * rexp * (gain * scale)


def _qkv_kernel(x_ref, g_ref, sh_ref, sc_ref, wm_ref, ws_ref, qg_ref, kg_ref,
                hsq_ref, heq_ref, hsk_ref, hek_ref,
                q_ref, k_ref, v_ref, qi_ref, kiwi_ref, *, dq, dk, dqi):
    x = x_ref[0]
    h = _norm_mod(x, g_ref[...], sh_ref[0], sc_ref[0]).astype(BF16)
    z = jnp.dot(h, wm_ref[...], preferred_element_type=F32)
    q = z[:, :dq]
    k = z[:, dq:dq + dk]
    v = z[:, dq + dk:dq + 2 * dk]
    qi = z[:, dq + 2 * dk:dq + 2 * dk + dqi]
    q_ref[0] = _head_norm(q, hsq_ref, heq_ref, qg_ref[...], HEAD_DIM ** -0.5 * LOG2E).astype(BF16)
    k_ref[0] = _head_norm(k, hsk_ref, hek_ref, kg_ref[...], 1.0).astype(BF16)
    v_ref[0] = v.astype(BF16)
    qi_ref[0] = qi.astype(BF16)
    kiwi_ref[0] = jnp.dot(h, ws_ref[...], preferred_element_type=F32)


def _head_indicators(nheads):
    hs = np.zeros((nheads * HEAD_DIM, LANES), np.float32)
    he = np.zeros((LANES, nheads * HEAD_DIM), np.float32)
    for hd in range(nheads):
        hs[hd * HEAD_DIM:(hd + 1) * HEAD_DIM, hd] = 1.0 / HEAD_DIM
        he[hd, hd * HEAD_DIM:(hd + 1) * HEAD_DIM] = 1.0
    return jnp.asarray(hs, BF16), jnp.asarray(he, BF16)


def _qkv_proj(x, g, sh, sc, w_main, w_small, q_gain, k_gain, tm=512):
    bsz, seq, d = x.shape
    tm = min(tm, seq)
    dq = N_HEADS * HEAD_DIM
    dk = N_KV_HEADS * HEAD_DIM
    dqi = IDX_HEADS * IDX_DIM
    hsq, heq = _head_indicators(N_HEADS)
    hsk, hek = _head_indicators(N_KV_HEADS)
    qg = jnp.tile(q_gain.reshape(1, HEAD_DIM), (1, N_HEADS)).astype(F32)
    kg = jnp.tile(k_gain.reshape(1, HEAD_DIM), (1, N_KV_HEADS)).astype(F32)
    kern = functools.partial(_qkv_kernel, dq=dq, dk=dk, dqi=dqi)
    mod_spec = pl.BlockSpec((1, 1, d), lambda b, i: (b, 0, 0))

    def full(a):
        return pl.BlockSpec(a.shape, lambda b, i: (0,) * a.ndim)

    def out(n):
        return pl.BlockSpec((1, tm, n), lambda b, i: (b, i, 0))

    return pl.pallas_call(
        kern,
        out_shape=[
            jax.ShapeDtypeStruct((bsz, seq, dq), BF16),
            jax.ShapeDtypeStruct((bsz, seq, dk), BF16),
            jax.ShapeDtypeStruct((bsz, seq, dk), BF16),
            jax.ShapeDtypeStruct((bsz, seq, dqi), BF16),
            jax.ShapeDtypeStruct((bsz, seq, LANES), F32),
        ],
        grid=(bsz, seq // tm),
        in_specs=[
            pl.BlockSpec((1, tm, d), lambda b, i: (b, i, 0)),
            pl.BlockSpec((1, d), lambda b, i: (0, 0)),
            mod_spec, mod_spec,
            full(w_main), full(w_small), full(qg), full(kg),
            full(hsq), full(heq), full(hsk), full(hek),
        ],
        out_specs=[out(dq), out(dk), out(dk), out(dqi), out(LANES)],
        compiler_params=_cparams("parallel", "parallel"),
        name="attn_qkv",
    )(x, g, sh, sc, w_main, w_small, qg, kg, hsq, heq, hsk, hek)


def _rel_bucket_np(dist):
    max_exact = REL_BUCKETS // 2
    d = np.maximum(dist, 1).astype(np.float64)
    large = max_exact + (np.log(d / max_exact) / math.log(REL_MAX_DIST / max_exact)
                         * (REL_BUCKETS - max_exact)).astype(np.int32)
    large = np.minimum(large, REL_BUCKETS - 1)
    return np.where(dist < max_exact, dist, large).astype(np.int32)


def _bias_table_kernel(bucket_ref, rb_ref, o_ref):
    hd = pl.program_id(0)
    bucket = bucket_ref[...]
    acc = jnp.zeros(bucket.shape, F32)
    for b in range(REL_BUCKETS):
        acc = jnp.where(bucket == b, rb_ref[b, hd] * LOG2E, acc)
    o_ref[0] = acc


def _bias_table(rel_bias):
    w = np.arange(BIAS_W)[:, None]
    i = np.arange(ATT_TQ)[None, :]
    bucket = jnp.asarray(_rel_bucket_np(np.maximum(i - w + BIAS_C, 0)))
    return pl.pallas_call(
        _bias_table_kernel,
        out_shape=jax.ShapeDtypeStruct((N_HEADS, BIAS_W, ATT_TQ), F32),
        grid=(N_HEADS,),
        in_specs=[
            pl.BlockSpec((BIAS_W, ATT_TQ), lambda hd: (0, 0)),
            pl.BlockSpec(memory_space=pltpu.SMEM),
        ],
        out_specs=pl.BlockSpec((1, BIAS_W, ATT_TQ), lambda hd: (hd, 0, 0)),
        compiler_params=_cparams("arbitrary"),
        name="attn_bias_table",
    )(bucket, rel_bias.astype(F32))


def _attn_kernel(qT_ref, qiT_ref, wiT_ref, k_ref, vT_ref, ki_ref, pt_ref, o_ref,
                 keys_scr, negm_scr, pidx_scr, oT_scr, acc_scr, qall_scr, sa_scr, sb_scr, *, top_k):
    tq, kc = ATT_TQ, ATT_KC
    qt = pl.program_id(1)
    q0 = qt * tq
    nch = (q0 + tq + kc - 1) // kc
    tpos = q0 + lax.broadcasted_iota(I32, (kc, tq), 1)
    srow = lax.broadcasted_iota(I32, (kc, tq), 0)

    qiT = qiT_ref[0]
    qi_all = jnp.concatenate([qiT[hd * IDX_DIM:(hd + 1) * IDX_DIM, :] for hd in range(IDX_HEADS)], axis=1)
    wiT = wiT_ref[0]

    def score_chunk(c):
        ks = pl.multiple_of(c * kc, kc)
        kic = ki_ref[0, pl.ds(ks, kc), :]
        dots = jnp.dot(kic, qi_all, preferred_element_type=F32)
        acc = jnp.zeros((kc, tq), F32)
        for hd in range(IDX_HEADS):
            acc = acc + jnp.maximum(dots[:, hd * tq:(hd + 1) * tq], 0.0) * wiT[hd:hd + 1, :]
        acc = jnp.where(acc == 0.0, 0.0, acc)
        bits = pltpu.bitcast(acc, I32)
        key = jnp.where(bits < 0, bits ^ INT_MAX, bits)
        key = jnp.where(ks + srow <= tpos, key, INT_MIN)
        keys_scr[pl.ds(ks, kc), :] = key

    npair = (nch + 1) // 2

    def score_pair(i, carry):
        score_chunk(2 * i)
        score_chunk(2 * i + 1)
        return carry

    lax.fori_loop(0, npair, score_pair, 0)

    @pl.when(nch % 2 == 1)
    def _():
        negm_scr[pl.ds(pl.multiple_of(nch * kc, kc), kc), :] = jnp.full((kc, tq), NEG_BIG, F32)

    srow2 = lax.broadcasted_iota(I32, (2 * kc, tq), 0)

    def count(pred):
        def body(c, acc):
            ks = pl.multiple_of(c * 2 * kc, 2 * kc)
            m = pred(keys_scr[pl.ds(ks, 2 * kc), :], ks + srow2).astype(I32)
            return acc + jnp.sum(m.reshape(2 * kc // SUBLANES, SUBLANES, tq), axis=0)
        acc = lax.fori_loop(0, npair, body, jnp.zeros((SUBLANES, tq), I32))
        return jnp.sum(acc, axis=0, keepdims=True)

    def bit_body(it, p):
        cand_p = p | lax.shift_left(jnp.int32(1), 31 - it)
        cand = cand_p ^ INT_MIN
        cnt = count(lambda k, s: k >= cand)
        return jnp.where(cnt >= top_k, cand_p, p)

    p_fin = lax.fori_loop(0, 32, bit_body, jnp.zeros((1, tq), I32))
    v = p_fin ^ INT_MIN

    cnt_gt = count(lambda k, s: k > v)
    cnt_eq = count(lambda k, s: k == v)
    need = top_k - cnt_gt
    pidx_scr[...] = jnp.full((1, tq), INT_MAX, I32)

    @pl.when(jnp.max(cnt_eq - need) > 0)
    def _():
        def ibit(it, p):
            cand = p | lax.shift_left(jnp.int32(1), 11 - it)
            cnt = count(lambda k, s: jnp.logical_and(k == v, s < cand))
            return jnp.where(cnt < need, cand, p)
        pidx_scr[...] = lax.fori_loop(0, 12, ibit, jnp.zeros((1, tq), I32))

    pidx = pidx_scr[...]

    def mask_chunk(c, carry):
        ks = pl.multiple_of(c * kc, kc)
        k = keys_scr[pl.ds(ks, kc), :]
        spos = ks + srow
        sel = jnp.logical_or(k > v, jnp.logical_and(k == v, spos <= pidx))
        sel = jnp.logical_and(sel, spos <= tpos)
        negm_scr[pl.ds(ks, kc), :] = jnp.where(sel, 0.0, NEG_BIG)
        return carry

    lax.fori_loop(0, nch, mask_chunk, 0)

    acc_scr[...] = jnp.zeros(acc_scr.shape, F32)
    for n in range(N_KV_HEADS):
        r0 = n * N_REP * HEAD_DIM
        qall_scr[n] = jnp.concatenate(
            [qT_ref[0, r0 + g * HEAD_DIM:r0 + (g + 1) * HEAD_DIM, :] for g in range(N_REP)], axis=1)

    def qk_chunk(c, s_ref):
        ks = pl.multiple_of(c * kc, kc)
        for n in range(N_KV_HEADS):
            s_ref[n] = jnp.dot(k_ref[0, n, pl.ds(ks, kc), :], qall_scr[n], preferred_element_type=F32)

    def softmax_pv(c, s_ref, ms, far):
        ks = pl.multiple_of(c * kc, kc)
        negm = negm_scr[pl.ds(ks, kc), :]
        w0 = pl.multiple_of(jnp.clip(BIAS_C - (q0 - ks), 0, BIAS_C), LANES)
        new_ms = []
        for n in range(N_KV_HEADS):
            s = s_ref[n]
            if far:
                cvec = jnp.concatenate([pt_ref[n * N_REP + g, 0:1, :] for g in range(N_REP)], axis=1)
                lg = jnp.concatenate([s[:, g * tq:(g + 1) * tq] + negm for g in range(N_REP)], axis=1)
                m_new = jnp.maximum(ms[n], jnp.max(lg, axis=0, keepdims=True) + cvec)
                p = jnp.exp2(lg - (m_new - cvec))
            else:
                lg = jnp.concatenate(
                    [s[:, g * tq:(g + 1) * tq] + pt_ref[n * N_REP + g, pl.ds(w0, kc), :] + negm
                     for g in range(N_REP)], axis=1)
                m_new = jnp.maximum(ms[n], jnp.max(lg, axis=0, keepdims=True))
                p = jnp.exp2(lg - m_new)
            alpha = jnp.exp2(ms[n] - m_new)
            acc_scr[n] = alpha * acc_scr[n] + jnp.dot(vT_ref[0, c, n], p.astype(BF16),
                                                      preferred_element_type=F32)
            new_ms.append(m_new)
        return tuple(new_ms)

    last_chunk = k_ref.shape[2] // kc - 1

    def pair_step(i, ms, far):
        c0 = 2 * i
        qk_chunk(c0 + 1, sb_scr)
        ms = softmax_pv(c0, sa_scr, ms, far)
        qk_chunk(jnp.minimum(c0 + 2, last_chunk), sa_scr)
        return softmax_pv(c0 + 1, sb_scr, ms, far)

    n_far = jnp.clip((q0 - BIAS_C + kc) // kc, 0, nch)
    ms = tuple(jnp.full((1, N_REP * tq), NEG_BIG, F32) for _ in range(N_KV_HEADS))
    qk_chunk(0, sa_scr)
    ms = lax.fori_loop(0, n_far // 2, functools.partial(pair_step, far=True), ms)
    lax.fori_loop(n_far // 2, npair, functools.partial(pair_step, far=False), ms)
    for n in range(N_KV_HEADS):
        o_t = acc_scr[n, 0:HEAD_DIM, :] / acc_scr[n, HEAD_DIM:HEAD_DIM + 1, :]
        for g in range(N_REP):
            r0 = (n * N_REP + g) * HEAD_DIM
            oT_scr[r0:r0 + HEAD_DIM, :] = o_t[:, g * tq:(g + 1) * tq]

    o_ref[0] = oT_scr[...].T.astype(BF16)


def _attention(qT, qiT, wiT, k4, vT, ki, ptab, top_k):
    bsz, dq, seq = qT.shape
    tq = ATT_TQ
    assert seq % (2 * ATT_KC) == 0
    kern = functools.partial(_attn_kernel, top_k=top_k)
    return pl.pallas_call(
        kern,
        out_shape=jax.ShapeDtypeStruct((bsz, seq, dq), BF16),
        grid=(bsz, seq // tq),
        in_specs=[
            pl.BlockSpec((1, dq, tq), lambda b, i: (b, 0, i)),
            pl.BlockSpec((1, qiT.shape[1], tq), lambda b, i: (b, 0, i)),
            pl.BlockSpec((1, wiT.shape[1], tq), lambda b, i: (b, 0, i)),
            pl.BlockSpec((1,) + k4.shape[1:], lambda b, i: (b, 0, 0, 0)),
            pl.BlockSpec((1,) + vT.shape[1:], lambda b, i: (b, 0, 0, 0, 0)),
            pl.BlockSpec((1,) + ki.shape[1:], lambda b, i: (b, 0, 0)),
            pl.BlockSpec(ptab.shape, lambda b, i: (0, 0, 0)),
        ],
        out_specs=pl.BlockSpec((1, tq, dq), lambda b, i: (b, i, 0)),
        scratch_shapes=[
            pltpu.VMEM((seq, tq), I32),
            pltpu.VMEM((seq, tq), F32),
            pltpu.VMEM((1, tq), I32),
            pltpu.VMEM((dq, tq), F32),
            pltpu.VMEM((N_KV_HEADS, ATT_VROWS, N_REP * tq), F32),
            pltpu.VMEM((N_KV_HEADS, HEAD_DIM, N_REP * tq), BF16),
            pltpu.VMEM((N_KV_HEADS, ATT_KC, N_REP * tq), F32),
            pltpu.VMEM((N_KV_HEADS, ATT_KC, N_REP * tq), F32),
        ],
        compiler_params=_cparams("parallel", "arbitrary"),
        name="attn_core",
    )(qT, qiT, wiT, k4, vT, ki, ptab)


def _proj_res_kernel(a_ref, x_ref, gate_ref, w_ref, o_ref):
    y = jnp.dot(a_ref[0], w_ref[...], preferred_element_type=F32)
    o_ref[0] = x_ref[0] + gate_ref[0] * y


def _proj_residual(a, x, gate, w, tm=512):
    bsz, seq, d = x.shape
    tm = min(tm, seq)
    return pl.pallas_call(
        _proj_res_kernel,
        out_shape=jax.ShapeDtypeStruct((bsz, seq, d), F32),
        grid=(bsz, seq // tm),
        in_specs=[
            pl.BlockSpec((1, tm, a.shape[2]), lambda b, i: (b, i, 0)),
            pl.BlockSpec((1, tm, d), lambda b, i: (b, i, 0)),
            pl.BlockSpec((1, 1, d), lambda b, i: (b, 0, 0)),
            pl.BlockSpec(w.shape, lambda b, i: (0, 0)),
        ],
        out_specs=pl.BlockSpec((1, tm, d), lambda b, i: (b, i, 0)),
        compiler_params=_cparams("parallel", "parallel"),
        name="proj_residual",
    )(a, x, gate, w)


def _attn_layer(x, g, sh, sc, gate, w_in, q_gain, k_gain, w_out, rel_bias):
    bsz, seq, d = x.shape
    top_k = min(TOPK_MAX, seq // 4)
    dq = N_HEADS * HEAD_DIM
    dk = N_KV_HEADS * HEAD_DIM
    dqi = IDX_HEADS * IDX_DIM
    nmain = dq + 2 * dk + dqi
    w_main = w_in[:, :nmain].astype(BF16)
    w_small = jnp.pad(w_in[:, nmain:], ((0, 0), (0, LANES - (IDX_DIM + IDX_HEADS)))).astype(BF16)
    q, k, v, qi, kiwi = _qkv_proj(x, g, sh, sc, w_main, w_small, q_gain, k_gain)
    ki = kiwi[:, :, :IDX_DIM].astype(BF16)
    wi = kiwi[:, :, IDX_DIM:IDX_DIM + IDX_HEADS] * (IDX_HEADS ** -0.5 * IDX_DIM ** -0.5)
    qT = jnp.swapaxes(q, 1, 2)
    qiT = jnp.swapaxes(qi, 1, 2)
    wiT = jnp.swapaxes(wi, 1, 2)
    nck = seq // ATT_KC
    vT = jnp.swapaxes(v.reshape(bsz, nck, ATT_KC, dk), 2, 3).reshape(bsz, nck, N_KV_HEADS, HEAD_DIM, ATT_KC)
    vT = jnp.concatenate([
        vT, jnp.ones((bsz, nck, N_KV_HEADS, 1, ATT_KC), BF16),
        jnp.zeros((bsz, nck, N_KV_HEADS, ATT_VROWS - HEAD_DIM - 1, ATT_KC), BF16)], axis=3)
    k4 = jnp.swapaxes(k.reshape(bsz, seq, N_KV_HEADS, HEAD_DIM), 1, 2)
    ptab = _bias_table(rel_bias)
    attn = _attention(qT, qiT, wiT, k4, vT, ki, ptab, top_k)
    return _proj_residual(attn, x, gate, w_out.astype(BF16))


def _s5_prep_kernel(lre_ref, lim_ref, ls_ref, bre_ref, bim_ref, cre_ref, cim_ref,
                    bcw_ref, mtw_ref, ccw_ref, are_ref, aim_ref):
    t_len = SSM_CHUNK
    lre = jnp.minimum(lre_ref[0], -1e-4)
    lim = lim_ref[0]
    step = jnp.exp(ls_ref[0])
    ar = lre * step
    ai = lim * step

    def powers(jv):
        mag = jnp.exp(jv * ar)
        return mag * jnp.cos(jv * ai), mag * jnp.sin(jv * ai)

    lb_re, lb_im = powers(1.0)
    nr = lb_re - 1.0
    ni = lb_im
    den = lre * lre + lim * lim
    cf_re = (nr * lre + ni * lim) / den
    cf_im = (ni * lre - nr * lim) / den
    bre = bre_ref[0]
    bim = bim_ref[0]
    bb_re = cf_re * bre - cf_im * bim
    bb_im = cf_re * bim + cf_im * bre
    cre = cre_ref[0]
    cim = cim_ref[0]
    nst = lre.shape[-1]
    jv = lax.broadcasted_iota(I32, (t_len, 1, nst), 0).astype(F32)
    pj_re, pj_im = powers(jv)
    a_re = (cre[None] * pj_re - cim[None] * pj_im).reshape(t_len * SSM_GROUP, nst)
    a_im = (cre[None] * pj_im + cim[None] * pj_re).reshape(t_len * SSM_GROUP, nst)
    dn = (((1,), (1,)), ((), ()))
    cg = SSM_GROUP
    tc = t_len * cg
    width = t_len * LANES
    gq = pl.program_id(0) % (LANES // cg)
    nh = (LANES // cg) * nst
    cg_shift = cg.bit_length() - 1

    def place(nrows, target):
        r = lax.broadcasted_iota(I32, (nrows, width), 0)
        col = lax.broadcasted_iota(I32, (nrows, width), 1)
        return jnp.where(col == target(r), 1.0, 0.0).astype(BF16)

    pm = place(tc, lambda r: lax.shift_right_logical(r, cg_shift) * LANES + gq * cg + (r & (cg - 1)))
    k_t = (lax.dot_general(bb_re, a_re, dn, preferred_element_type=F32, precision=HIGHEST)
           - lax.dot_general(bb_im, a_im, dn, preferred_element_type=F32, precision=HIGHEST))
    lane = lax.broadcasted_iota(I32, (cg, tc), 1)
    mt_rows = [k_t] + [jnp.where(lane >= s * cg, pltpu.roll(k_t, s * cg, 1), 0.0) for s in range(1, t_len)]
    mt_t = jnp.concatenate(mt_rows, axis=0)
    mtw_ref[0] = jnp.dot(mt_t.astype(BF16), pm, preferred_element_type=F32).astype(BF16)
    pr_re, pr_im = powers((t_len - 1.0) - jv)
    bc_re = (bb_re[None] * pr_re - bb_im[None] * pr_im).reshape(tc, nst)
    bc_im = (bb_re[None] * pr_im + bb_im[None] * pr_re).reshape(tc, nst)
    pb_re = place(nst, lambda r: gq * nst + r)
    pb_im = place(nst, lambda r: nh + gq * nst + r)
    bcw_ref[0] = (jnp.dot(bc_re.astype(BF16), pb_re, preferred_element_type=F32)
                  + jnp.dot(bc_im.astype(BF16), pb_im, preferred_element_type=F32)).astype(BF16)
    pn_re, pn_im = powers(jv + 1.0)
    cc_re = (cre[None] * pn_re - cim[None] * pn_im).reshape(tc, nst)
    cc_im = (cre[None] * pn_im + cim[None] * pn_re).reshape(tc, nst)
    eye = jnp.where(lax.broadcasted_iota(I32, (nst, nst), 0) == lax.broadcasted_iota(I32, (nst, nst), 1),
                    1.0, 0.0)
    cct_re = lax.dot_general(eye, cc_re, dn, preferred_element_type=F32, precision=HIGHEST)
    cct_im = lax.dot_general(eye, cc_im, dn, preferred_element_type=F32, precision=HIGHEST)
    ccw_ref[0, 0:nst, :] = jnp.dot(cct_re.astype(BF16), pm, preferred_element_type=F32).astype(BF16)
    ccw_ref[0, nst:2 * nst, :] = (-jnp.dot(cct_im.astype(BF16), pm, preferred_element_type=F32)).astype(BF16)
    at_re, at_im = powers(float(t_len))
    rs = lax.broadcasted_iota(I32, (nst, nh), 0)
    cs = lax.broadcasted_iota(I32, (nst, nh), 1)
    pa = jnp.where(cs == gq * nst + rs, 1.0, 0.0)
    are_ref[0] = jnp.dot(at_re, pa, preferred_element_type=F32, precision=HIGHEST)
    aim_ref[0] = jnp.dot(at_im, pa, preferred_element_type=F32, precision=HIGHEST)


def _s5_prep(lam_re, lam_im, log_step, b_re, b_im, c_re, c_im):
    ng, nst = lam_re.shape
    tc = SSM_CHUNK * SSM_GROUP
    vec = pl.BlockSpec((1, 1, nst), lambda gi: (gi, 0, 0))
    mat = pl.BlockSpec((1, SSM_GROUP, nst), lambda gi: (gi, 0, 0))
    width = SSM_CHUNK * LANES
    assert tc == 2 * nst and 2 * (LANES // SSM_GROUP) * nst == width
    big = pl.BlockSpec((1, tc, width), lambda gi: (gi, 0, 0))
    wide = jax.ShapeDtypeStruct((ng, tc, width), BF16)
    return pl.pallas_call(
        _s5_prep_kernel,
        out_shape=[wide, wide, wide,
                   jax.ShapeDtypeStruct((ng, 1, width // 2), F32),
                   jax.ShapeDtypeStruct((ng, 1, width // 2), F32)],
        grid=(ng,),
        in_specs=[vec, vec, pl.BlockSpec((1, 1, 1), lambda gi: (gi, 0, 0)), mat, mat, mat, mat],
        out_specs=[big, big, big, pl.BlockSpec((1, 1, width // 2), lambda gi: (gi, 0, 0)),
                   pl.BlockSpec((1, 1, width // 2), lambda gi: (gi, 0, 0))],
        compiler_params=_cparams("parallel"),
        name="s5_prep",
    )(lam_re.reshape(ng, 1, nst), lam_im.reshape(ng, 1, nst), log_step.reshape(ng, 1, 1),
      jnp.swapaxes(b_re, 1, 2), jnp.swapaxes(b_im, 1, 2), c_re, c_im)


def _s5_pre_kernel(x_ref, g_ref, sh_ref, sc_ref, o_ref, h_scr):
    t_len = SSM_CHUNK
    h = _norm_mod(x_ref[0], g_ref[...], sh_ref[0], sc_ref[0])
    nj = o_ref.shape[1]
    for q in range(o_ref.shape[0]):
        h_scr[q] = h[:, q * LANES:(q + 1) * LANES]
        for t in range(t_len):
            o_ref[q, :, t * LANES:(t + 1) * LANES] = h_scr[q, pl.ds(t, nj, stride=t_len), :].astype(BF16)


def _s5_pre(x, g, sh, sc, tm=512):
    bsz, seq, d = x.shape
    tm = min(tm, seq)
    nt = seq // tm
    nq = d // LANES
    t_len = SSM_CHUNK
    mod_spec = pl.BlockSpec((1, 1, d), lambda b, i: (b, 0, 0))
    return pl.pallas_call(
        _s5_pre_kernel,
        out_shape=jax.ShapeDtypeStruct((nq, bsz * seq // t_len, t_len * LANES), BF16),
        grid=(bsz, nt),
        in_specs=[pl.BlockSpec((1, tm, d), lambda b, i: (b, i, 0)),
                  pl.BlockSpec((1, d), lambda b, i: (0, 0)), mod_spec, mod_spec],
        out_specs=pl.BlockSpec((nq, tm // t_len, t_len * LANES), lambda b, i: (0, b * nt + i, 0)),
        scratch_shapes=[pltpu.VMEM((nq, tm, LANES), F32)],
        compiler_params=_cparams("parallel", "parallel"),
        name="s5_pre",
    )(x, g, sh, sc)


def _s5_scan_kernel(x_ref, bc_ref, mt_ref, cc_ref, are_ref, aim_ref, y_ref, re_scr, im_scr):
    x = x_ref[0]
    nrow = x.shape[0]
    pad = re_scr.shape[0] - nrow
    v = jnp.dot(x, bc_ref[0], preferred_element_type=F32)
    nh = v.shape[1] // 2
    s_re = v[:, :nh]
    s_im = v[:, nh:]
    a_re = are_ref[0]
    a_im = aim_ref[0]
    re_scr[0:pad, :] = jnp.zeros((pad, nh), F32)
    im_scr[0:pad, :] = jnp.zeros((pad, nh), F32)

    def shifted(scr, val, dist):
        scr[pad:pad + nrow, :] = val
        return scr[pad - dist:pad - dist + nrow, :]

    dist = 1
    while dist < nrow:
        sh_re = shifted(re_scr, s_re, dist)
        sh_im = shifted(im_scr, s_im, dist)
        s_re, s_im = (s_re + a_re * sh_re - a_im * sh_im, s_im + a_re * sh_im + a_im * sh_re)
        a_re, a_im = (a_re * a_re - a_im * a_im, 2.0 * a_re * a_im)
        dist *= 2
    sp = jnp.concatenate([shifted(re_scr, s_re, 1), shifted(im_scr, s_im, 1)], axis=1).astype(BF16)
    y_ref[0] = (jnp.dot(x, mt_ref[0], preferred_element_type=F32)
                + jnp.dot(sp, cc_ref[0], preferred_element_type=F32))


def _s5_scan(hq, bcq, mtq, ccq, a_re, a_im, bsz):
    nq, nrows, width = hq.shape
    nj = nrows // bsz
    nh = a_re.shape[-1]
    pad = max(nj // 2, SUBLANES)
    tile = pl.BlockSpec((1, nj, width), lambda q, b: (q, b, 0))
    wspec = lambda a: pl.BlockSpec((1,) + a.shape[1:], lambda q, b: (q, 0, 0))
    return pl.pallas_call(
        _s5_scan_kernel,
        out_shape=jax.ShapeDtypeStruct((nq, nrows, width), F32),
        grid=(nq, bsz),
        in_specs=[tile, wspec(bcq), wspec(mtq), wspec(ccq), wspec(a_re), wspec(a_im)],
        out_specs=tile,
        scratch_shapes=[pltpu.VMEM((pad + nj, nh), F32), pltpu.VMEM((pad + nj, nh), F32)],
        compiler_params=_cparams("parallel", "parallel"),
        name="s5_scan",
    )(hq, bcq, mtq, ccq, a_re, a_im)


def _s5_post_kernel(x_ref, y_ref, g_ref, sh_ref, sc_ref, gate_ref, dsk_ref, w_ref, o_ref, y_scr, *, d):
    t_len = SSM_CHUNK
    x = x_ref[0]
    h = _norm_mod(x, g_ref[...], sh_ref[0], sc_ref[0])
    nj = y_ref.shape[1]
    for q in range(y_ref.shape[0]):
        for t in range(t_len):
            y_scr[q, pl.ds(t, nj, stride=t_len), :] = y_ref[q, :, t * LANES:(t + 1) * LANES]
    y = jnp.concatenate([y_scr[q] for q in range(y_ref.shape[0])], axis=1)
    yy = y + dsk_ref[...] * h
    gl = jax.nn.gelu(yy).astype(BF16)
    z = jnp.dot(gl, w_ref[...], preferred_element_type=F32)
    o_ref[0] = x + gate_ref[0] * (z[:, :d] * jax.nn.sigmoid(z[:, d:]))


def _s5_post(x, yq, g, sh, sc, gate, d_skip, w_glu, tm=512):
    bsz, seq, d = x.shape
    tm = min(tm, seq)
    nt = seq // tm
    nq = d // LANES
    kern = functools.partial(_s5_post_kernel, d=d)
    mod_spec = pl.BlockSpec((1, 1, d), lambda b, i: (b, 0, 0))
    tile = pl.BlockSpec((1, tm, d), lambda b, i: (b, i, 0))
    return pl.pallas_call(
        kern,
        out_shape=jax.ShapeDtypeStruct((bsz, seq, d), F32),
        grid=(bsz, nt),
        in_specs=[tile,
                  pl.BlockSpec((nq, tm // SSM_CHUNK, SSM_CHUNK * LANES), lambda b, i: (0, b * nt + i, 0)),
                  pl.BlockSpec((1, d), lambda b, i: (0, 0)), mod_spec, mod_spec, mod_spec,
                  pl.BlockSpec((1, d), lambda b, i: (0, 0)),
                  pl.BlockSpec(w_glu.shape, lambda b, i: (0, 0))],
        out_specs=tile,
        scratch_shapes=[pltpu.VMEM((nq, tm, LANES), F32)],
        compiler_params=_cparams("parallel", "parallel"),
        name="s5_post",
    )(x, yq, g, sh, sc, gate, d_skip, w_glu)


def _s5_layer(x, g, sh, sc, gate, lam_re, lam_im, log_step, b_re, b_im, c_re, c_im, d_skip, w_glu):
    bsz, seq, d = x.shape
    ng, nst = lam_re.shape
    t_len, cg = SSM_CHUNK, SSM_GROUP
    nq = d // LANES
    gq = LANES // cg
    nj = seq // t_len
    bcw, mtw, ccw, a_re, a_im = _s5_prep(lam_re, lam_im, log_step, b_re, b_im, c_re, c_im)
    width = t_len * LANES
    bcq = jnp.swapaxes(bcw.reshape(nq, gq, t_len, cg, width), 1, 2).reshape(nq, width, width)
    mtq = jnp.swapaxes(mtw.reshape(nq, gq, t_len, cg, width), 1, 2).reshape(nq, width, width)
    ccq = jnp.swapaxes(ccw.reshape(nq, gq, 2, nst, width), 1, 2).reshape(nq, 2 * gq * nst, width)
    aq_re = jnp.sum(a_re.reshape(nq, gq, 1, gq * nst), axis=1)
    aq_im = jnp.sum(a_im.reshape(nq, gq, 1, gq * nst), axis=1)

    hq = _s5_pre(x, g, sh, sc)
    yq = _s5_scan(hq, bcq, mtq, ccq, aq_re, aq_im, bsz)
    return _s5_post(x, yq, g, sh, sc, gate, d_skip.reshape(1, d).astype(F32), w_glu.astype(BF16))


def kernel(x, c, ada_w, ada_b, norm_g, conv_w_in, conv_w, conv_w_out, attn_w_in, attn_q_gain, attn_k_gain, attn_w_out, rel_bias, ssm_lambda_re, ssm_lambda_im, ssm_log_step, ssm_b_re, ssm_b_im, ssm_c_re, ssm_c_im, ssm_d, ssm_w_glu, ffn_w_gu, ffn_w_down, moe_router_w, moe_router_b, moe_w_gu, moe_w_down):
    bsz, seq, d = x.shape
    depth = ada_w.shape[0]
    mod = _ada_mod(c, ada_w, ada_b).reshape(depth, bsz, 6, 1, d)
    moe_gu = moe_w_gu.astype(BF16).reshape((-1,) + moe_w_gu.shape[2:])
    moe_down = moe_w_down.astype(BF16).reshape((-1,) + moe_w_down.shape[2:])
    for i in range(depth):
        sh1, sc1, g1, sh2, sc2, g2 = (mod[i, :, r] for r in range(6))
        gn1 = norm_g[i, 0].reshape(1, d)
        gn2 = norm_g[i, 1].reshape(1, d)
        j = i // N_MIXERS
        if i % N_MIXERS == 0:
            x = _conv_layer(x, gn1, sh1, sc1, g1, conv_w_in[j].astype(BF16), conv_w[j],
                            conv_w_out[j].astype(BF16))
        elif i % N_MIXERS == 1:
            x = _attn_layer(x, gn1, sh1, sc1, g1, attn_w_in[j], attn_q_gain[j], attn_k_gain[j],
                            attn_w_out[j], rel_bias)
        else:
            x = _s5_layer(x, gn1, sh1, sc1, g1, ssm_lambda_re[j], ssm_lambda_im[j], ssm_log_step[j],
                          ssm_b_re[j], ssm_b_im[j], ssm_c_re[j], ssm_c_im[j], ssm_d[j], ssm_w_glu[j])
        if i % 2 == 0:
            x = _ffn_layer(x, gn2, sh2, sc2, g2, ffn_w_gu[i // 2].astype(BF16), ffn_w_down[i // 2].astype(BF16))
        else:
            x = _moe_layer(x, gn2, sh2, sc2, g2, moe_router_w[i // 2], moe_router_b[i // 2],
                           moe_gu, moe_down, ebase=(i // 2) * moe_w_gu.shape[1])
    return x
```

```python
import functools
import math

import numpy as np
import jax
import jax.numpy as jnp
from jax import lax
from jax.experimental import pallas as pl
from jax.experimental.pallas import tpu as pltpu

F32 = jnp.float32
BF16 = jnp.bfloat16
I32 = jnp.int32
HIGHEST = lax.Precision.HIGHEST

DEPTH = 4
N_MIXERS = 3
EPS = 1e-6
CONV_WIDTH = 3
N_HEADS = 16
N_KV_HEADS = 4
N_REP = N_HEADS // N_KV_HEADS
HEAD_DIM = 64
IDX_HEADS = 8
IDX_DIM = 64
TOPK_MAX = 256
REL_BUCKETS = 32
REL_MAX_DIST = 128
SSM_GROUP = 16
SSM_STATE = 64
N_EXPERTS = 8
TOP_K_EXPERTS = 2

VMEM_LIMIT_BYTES = 56 * 1024 * 1024
LANES = 128
SUBLANES = 8

INT_MIN = -(2 ** 31)
INT_MAX = 2 ** 31 - 1
NEG_BIG = -1e30
LOG2E = 1.4426950408889634

ATT_TQ = 128
ATT_KC = 256
BIAS_C = 384
BIAS_W = BIAS_C + ATT_KC
ATT_VROWS = 80

SSM_CHUNK = 8


def _cparams(*sem):
    return pltpu.CompilerParams(dimension_semantics=sem, vmem_limit_bytes=VMEM_LIMIT_BYTES)


def _norm_mod(x, g, shift, scale):
    ms = jnp.mean(x * x, axis=-1, keepdims=True)
    y = x * lax.rsqrt(ms + EPS)
    return (y * g) * (1.0 + scale) + shift


def _silu(x):
    return x * jax.nn.sigmoid(x)


def _ada_kernel(c_ref, w_ref, b_ref, o_ref):
    c = c_ref[...]
    cond = _silu(c)
    o_ref[0] = jnp.dot(cond, w_ref[0], preferred_element_type=F32, precision=HIGHEST) + b_ref[0]


def _ada_mod(c, ada_w, ada_b):
    depth, d, d6 = ada_w.shape
    bsz = c.shape[0]
    tn = d
    return pl.pallas_call(
        _ada_kernel,
        out_shape=jax.ShapeDtypeStruct((depth, bsz, d6), F32),
        grid=(depth, d6 // tn),
        in_specs=[
            pl.BlockSpec((bsz, d), lambda i, j: (0, 0)),
            pl.BlockSpec((1, d, tn), lambda i, j: (i, 0, j)),
            pl.BlockSpec((1, 1, tn), lambda i, j: (i, 0, j)),
        ],
        out_specs=pl.BlockSpec((1, bsz, tn), lambda i, j: (i, 0, j)),
        compiler_params=_cparams("parallel", "parallel"),
        name="ada_mod",
    )(c, ada_w, ada_b.reshape(depth, 1, d6))


def _conv_kernel(x_ref, xh_ref, g_ref, sh_ref, sc_ref, gate_ref, win_ref, wc_ref, wout_ref,
                 o_ref, u_scr, *, tm, d):
    i = pl.program_id(1)
    g = g_ref[...]
    sh = sh_ref[0]
    sc = sc_ref[0]
    x = x_ref[0]
    h = _norm_mod(x, g, sh, sc).astype(BF16)
    z = jnp.dot(h, win_ref[...], preferred_element_type=F32)
    b_gate = z[:, :d]
    u = z[:, d:2 * d] * z[:, 2 * d:]
    hh = _norm_mod(xh_ref[0], g, sh, sc).astype(BF16)
    zh = jnp.dot(hh, win_ref[:, d:], preferred_element_type=F32)
    uh = zh[:, :d] * zh[:, d:]
    uh = jnp.where(i > 0, uh, 0.0)
    u_scr[0:SUBLANES, :] = uh
    u_scr[SUBLANES:SUBLANES + tm, :] = u
    wc = wc_ref[...]
    conv = (wc[0:1, :] * u_scr[SUBLANES - 2:SUBLANES - 2 + tm, :]
            + wc[1:2, :] * u_scr[SUBLANES - 1:SUBLANES - 1 + tm, :]
            + wc[2:3, :] * u)
    y = jnp.dot((b_gate * conv).astype(BF16), wout_ref[...], preferred_element_type=F32)
    o_ref[0] = x + gate_ref[0] * y


def _conv_layer(x, g, sh, sc, gate, w_in, w_conv, w_out, tm=512):
    bsz, seq, d = x.shape
    tm = min(tm, seq)
    nt = seq // tm
    hb = tm // SUBLANES
    kern = functools.partial(_conv_kernel, tm=tm, d=d)
    mod_spec = pl.BlockSpec((1, 1, d), lambda b, i: (b, 0, 0))
    return pl.pallas_call(
        kern,
        out_shape=jax.ShapeDtypeStruct((bsz, seq, d), F32),
        grid=(bsz, nt),
        in_specs=[
            pl.BlockSpec((1, tm, d), lambda b, i: (b, i, 0)),
            pl.BlockSpec((1, SUBLANES, d), lambda b, i: (b, jnp.maximum(i * hb - 1, 0), 0)),
            pl.BlockSpec((1, d), lambda b, i: (0, 0)),
            mod_spec, mod_spec, mod_spec,
            pl.BlockSpec((d, 3 * d), lambda b, i: (0, 0)),
            pl.BlockSpec((CONV_WIDTH, d), lambda b, i: (0, 0)),
            pl.BlockSpec((d, d), lambda b, i: (0, 0)),
        ],
        out_specs=pl.BlockSpec((1, tm, d), lambda b, i: (b, i, 0)),
        scratch_shapes=[pltpu.VMEM((tm + SUBLANES, d), F32)],
        compiler_params=_cparams("parallel", "parallel"),
        name="conv_mixer",
    )(x, x, g, sh, sc, gate, w_in, w_conv, w_out)


def _ffn_kernel(x_ref, g_ref, sh_ref, sc_ref, gate_ref, wgu_ref, wd_ref, o_ref, *, dff, nchunk):
    x = x_ref[0]
    h = _norm_mod(x, g_ref[...], sh_ref[0], sc_ref[0]).astype(BF16)
    cols = dff // nchunk
    acc = jnp.zeros(x.shape, F32)
    for c in range(nchunk):
        gg = jnp.dot(h, wgu_ref[:, c * cols:(c + 1) * cols], preferred_element_type=F32)
        uu = jnp.dot(h, wgu_ref[:, dff + c * cols:dff + (c + 1) * cols], preferred_element_type=F32)
        a = (_silu(gg) * uu).astype(BF16)
        acc = acc + jnp.dot(a, wd_ref[c * cols:(c + 1) * cols, :], preferred_element_type=F32)
    o_ref[0] = x + gate_ref[0] * acc


def _ffn_layer(x, g, sh, sc, gate, w_gu, w_down, tm=512):
    bsz, seq, d = x.shape
    dff = w_down.shape[0]
    tm = min(tm, seq)
    kern = functools.partial(_ffn_kernel, dff=dff, nchunk=2)
    mod_spec = pl.BlockSpec((1, 1, d), lambda b, i: (b, 0, 0))
    return pl.pallas_call(
        kern,
        out_shape=jax.ShapeDtypeStruct((bsz, seq, d), F32),
        grid=(bsz, seq // tm),
        in_specs=[
            pl.BlockSpec((1, tm, d), lambda b, i: (b, i, 0)),
            pl.BlockSpec((1, d), lambda b, i: (0, 0)),
            mod_spec, mod_spec, mod_spec,
            pl.BlockSpec((d, 2 * dff), lambda b, i: (0, 0), pipeline_mode=pl.Buffered(1)),
            pl.BlockSpec((dff, d), lambda b, i: (0, 0), pipeline_mode=pl.Buffered(1)),
        ],
        out_specs=pl.BlockSpec((1, tm, d), lambda b, i: (b, i, 0)),
        compiler_params=_cparams("parallel", "parallel"),
        name="ffn_dense",
    )(x, g, sh, sc, gate, w_gu, w_down)


MOE_SB = 512
MOE_RT = 512
MOE_ALIGN = 16
MOE_WIN = 256
MOE_CAP = MOE_WIN - MOE_ALIGN


def _moe_router_kernel(x_ref, g_ref, sh_ref, sc_ref, rw_ref, rb_ref, h_ref, meta_ref):
    hf = _norm_mod(x_ref[...], g_ref[...], sh_ref[0], sc_ref[0])
    h_ref[...] = hf.astype(BF16)
    logits = jnp.dot(hf, rw_ref[...], preferred_element_type=F32, precision=HIGHEST) + rb_ref[...]
    mx = jnp.max(logits, axis=-1, keepdims=True)
    ex = jnp.exp(logits - mx)
    probs = ex / jnp.sum(ex, axis=-1, keepdims=True)
    lane = lax.broadcasted_iota(I32, probs.shape, 1)
    m1 = jnp.max(probs, axis=-1, keepdims=True)
    i1 = jnp.min(jnp.where(probs == m1, lane, LANES), axis=-1, keepdims=True)
    rest = jnp.where(lane == i1, -1.0, probs)
    m2 = jnp.max(rest, axis=-1, keepdims=True)
    i2 = jnp.min(jnp.where(rest == m2, lane, LANES), axis=-1, keepdims=True)
    den = m1 + m2
    gates = jnp.where(lane == i1, m1 / den, 0.0) + jnp.where(lane == i2, m2 / den, 0.0)
    chosen = jnp.where(jnp.logical_or(lane == i1, lane == i2), 1.0, 0.0)
    nrow = chosen.shape[0]
    below = jnp.where(lax.broadcasted_iota(I32, (nrow, nrow), 1) < lax.broadcasted_iota(I32, (nrow, nrow), 0),
                      1.0, 0.0).astype(BF16)
    rank = jnp.dot(below, chosen.astype(BF16), preferred_element_type=F32)
    meta_ref[...] = gates + pltpu.roll(chosen, N_EXPERTS, 1) + pltpu.roll(rank, 2 * N_EXPERTS, 1)


def _moe_router(xt, g, sh, sc, rw_pad, rb_pad, tpb, tm):
    n, d = xt.shape
    mod_spec = pl.BlockSpec((1, 1, d), lambda i: (i // tpb, 0, 0))
    return pl.pallas_call(
        _moe_router_kernel,
        out_shape=[jax.ShapeDtypeStruct((n, d), BF16), jax.ShapeDtypeStruct((n, LANES), F32)],
        grid=(n // tm,),
        in_specs=[pl.BlockSpec((tm, d), lambda i: (i, 0)),
                  pl.BlockSpec((1, d), lambda i: (0, 0)), mod_spec, mod_spec,
                  pl.BlockSpec((d, LANES), lambda i: (0, 0)),
                  pl.BlockSpec((1, LANES), lambda i: (0, 0))],
        out_specs=[pl.BlockSpec((tm, d), lambda i: (i, 0)), pl.BlockSpec((tm, LANES), lambda i: (i, 0))],
        compiler_params=_cparams("parallel"),
        name="moe_router",
    )(xt, g, sh, sc, rw_pad, rb_pad)


def _window(start, count, w):
    s = start + w * MOE_CAP
    n = jnp.minimum(count - w * MOE_CAP, MOE_CAP)
    a = pl.multiple_of((s // MOE_ALIGN) * MOE_ALIGN, MOE_ALIGN)
    return s, n, a


def _moe_dispatch_kernel(start_ref, cnt_ref, h_ref, pos_ref, xs_init_ref, xs_ref,
                         buf, carry, sems, pending):
    del xs_init_ref
    b = pl.program_id(0)
    nb = pl.num_programs(0)
    ne = buf.shape[0]

    @pl.when(b == 0)
    def _():
        carry[...] = jnp.zeros(carry.shape, BF16)
        for e in range(ne):
            pending[e] = 0

    h = h_ref[...]
    riota = lax.broadcasted_iota(I32, (MOE_WIN, h.shape[0]), 0)

    def out_copy(e, a):
        return pltpu.make_async_copy(buf.at[e], xs_ref.at[pl.ds(a, MOE_WIN)], sems.at[e])

    for e in range(ne):
        posrow = pos_ref[0, e:e + 1, :]
        nwin = (cnt_ref[b, e] + MOE_CAP - 1) // MOE_CAP

        def wbody(w, carry_unused, e=e, posrow=posrow):
            s, n, a = _window(start_ref[b, e], cnt_ref[b, e], w)

            @pl.when(pending[e] == 1)
            def _():
                out_copy(e, 0).wait()

            hit = jnp.logical_and(posrow - a == riota,
                                  jnp.logical_and(posrow >= s, posrow < s + n))
            onehot = jnp.where(hit, 1.0, 0.0).astype(BF16)
            buf[e] = jnp.dot(onehot, h, preferred_element_type=F32).astype(BF16)
            buf[e, 0:MOE_ALIGN, :] = buf[e, 0:MOE_ALIGN, :] + carry[e]
            c0 = pl.multiple_of(((s + n) // MOE_ALIGN) * MOE_ALIGN - a, MOE_ALIGN)
            carry[e] = buf[e, pl.ds(c0, MOE_ALIGN), :]
            out_copy(e, a).start()
            pending[e] = 1
            return carry_unused

        lax.fori_loop(0, nwin, wbody, 0)

    @pl.when(b == nb - 1)
    def _():
        for e in range(ne):
            @pl.when(pending[e] == 1)
            def _():
                out_copy(e, 0).wait()


def _moe_dispatch(start, cnt, h, pos_t, ncap):
    n, d = h.shape
    nb, ne, sb = pos_t.shape
    xs_init = jnp.zeros((ncap, d), BF16)
    return pl.pallas_call(
        _moe_dispatch_kernel,
        out_shape=jax.ShapeDtypeStruct((ncap, d), BF16),
        grid_spec=pltpu.PrefetchScalarGridSpec(
            num_scalar_prefetch=2,
            grid=(nb,),
            in_specs=[pl.BlockSpec((sb, d), lambda b, *_: (b, 0)),
                      pl.BlockSpec((1, ne, sb), lambda b, *_: (b, 0, 0)),
                      pl.BlockSpec(memory_space=pl.ANY)],
            out_specs=pl.BlockSpec(memory_space=pl.ANY),
            scratch_shapes=[pltpu.VMEM((ne, MOE_WIN, d), BF16),
                            pltpu.VMEM((ne, MOE_ALIGN, d), BF16),
                            pltpu.SemaphoreType.DMA((ne,)),
                            pltpu.SMEM((ne,), I32)],
        ),
        input_output_aliases={4: 0},
        compiler_params=_cparams("arbitrary"),
        name="moe_dispatch",
    )(start, cnt, h, pos_t, xs_init)


def _moe_ffn_kernel(blk_ref, exp_ref, nt_ref, xs_ref, wgu_ref, wd_ref, ys_init_ref, ys_ref, *, dff, nchunk):
    del blk_ref, exp_ref, ys_init_ref
    k = pl.program_id(0)

    @pl.when(k < nt_ref[0])
    def _():
        x = xs_ref[...]
        cols = dff // nchunk
        acc = jnp.zeros(x.shape, F32)
        for c in range(nchunk):
            gg = jnp.dot(x, wgu_ref[0, :, c * cols:(c + 1) * cols], preferred_element_type=F32)
            uu = jnp.dot(x, wgu_ref[0, :, dff + c * cols:dff + (c + 1) * cols], preferred_element_type=F32)
            a = (_silu(gg) * uu).astype(BF16)
            acc = acc + jnp.dot(a, wd_ref[0, c * cols:(c + 1) * cols, :], preferred_element_type=F32)
        ys_ref[...] = acc.astype(BF16)


def _moe_ffn(tile_blk, tile_exp, ntiles, xs, w_gu, w_down, ebase):
    ncap, d = xs.shape
    dff = w_down.shape[1]
    kern = functools.partial(_moe_ffn_kernel, dff=dff, nchunk=2)
    ys_init = jnp.zeros((ncap, d), BF16)
    return pl.pallas_call(
        kern,
        out_shape=jax.ShapeDtypeStruct((ncap, d), BF16),
        grid_spec=pltpu.PrefetchScalarGridSpec(
            num_scalar_prefetch=3,
            grid=(ncap // MOE_RT,),
            in_specs=[pl.BlockSpec((MOE_RT, d), lambda k, blk, exp, nt: (blk[k], 0)),
                      pl.BlockSpec((1, d, 2 * dff), lambda k, blk, exp, nt: (ebase + exp[k], 0, 0)),
                      pl.BlockSpec((1, dff, d), lambda k, blk, exp, nt: (ebase + exp[k], 0, 0)),
                      pl.BlockSpec(memory_space=pl.ANY)],
            out_specs=pl.BlockSpec((MOE_RT, d), lambda k, blk, exp, nt: (blk[k], 0)),
        ),
        input_output_aliases={6: 0},
        compiler_params=_cparams("arbitrary"),
        name="moe_ffn",
    )(tile_blk, tile_exp, ntiles, xs, w_gu, w_down, ys_init)


def _moe_combine_kernel(start_ref, cnt_ref, x_ref, gate_ref, pos_ref, gts_ref, ys_ref, o_ref,
                        buf, sems, acc_scr):
    b = pl.program_id(0)
    nb = pl.num_programs(0)
    ne = buf.shape[1]
    sb = x_ref.shape[0]
    slot = b % 2

    def in_copy(sl, e, a):
        return pltpu.make_async_copy(ys_ref.at[pl.ds(a, MOE_WIN)], buf.at[sl, e], sems.at[sl, e])

    def start_first_windows(blk, sl):
        for e in range(ne):
            @pl.when(cnt_ref[blk, e] > 0)
            def _():
                _, _, a = _window(start_ref[blk, e], cnt_ref[blk, e], 0)
                in_copy(sl, e, a).start()

    @pl.when(b == 0)
    def _():
        start_first_windows(0, 0)

    @pl.when(b + 1 < nb)
    def _():
        start_first_windows(b + 1, 1 - slot)

    acc_scr[...] = jnp.zeros(acc_scr.shape, F32)
    liota = lax.broadcasted_iota(I32, (sb, MOE_WIN), 1)
    for e in range(ne):
        poscol = pos_ref[:, e:e + 1]
        gcol = gts_ref[:, e:e + 1]
        nwin = (cnt_ref[b, e] + MOE_CAP - 1) // MOE_CAP

        def wbody(w, carry_unused, e=e, poscol=poscol, gcol=gcol):
            s, n, a = _window(start_ref[b, e], cnt_ref[b, e], w)

            @pl.when(w > 0)
            def _():
                in_copy(slot, e, a).start()

            in_copy(slot, e, a).wait()
            hit = jnp.logical_and(poscol - a == liota,
                                  jnp.logical_and(poscol >= s, poscol < s + n))
            onehot = jnp.where(hit, 1.0, 0.0).astype(BF16)
            acc_scr[...] += gcol * jnp.dot(onehot, buf[slot, e], preferred_element_type=F32)
            return carry_unused

        lax.fori_loop(0, nwin, wbody, 0)

    o_ref[...] = x_ref[...] + gate_ref[0] * acc_scr[...]


def _moe_combine(start, cnt, xt, gate, pos_n, gates_n, ys, tpb):
    n, d = xt.shape
    nb, ne = cnt.shape
    sb = n // nb
    return pl.pallas_call(
        _moe_combine_kernel,
        out_shape=jax.ShapeDtypeStruct((n, d), F32),
        grid_spec=pltpu.PrefetchScalarGridSpec(
            num_scalar_prefetch=2,
            grid=(nb,),
            in_specs=[pl.BlockSpec((sb, d), lambda b, *_: (b, 0)),
                      pl.BlockSpec((1, 1, d), lambda b, *_: (b // tpb, 0, 0)),
                      pl.BlockSpec((sb, ne), lambda b, *_: (b, 0)),
                      pl.BlockSpec((sb, ne), lambda b, *_: (b, 0)),
                      pl.BlockSpec(memory_space=pl.ANY)],
            out_specs=pl.BlockSpec((sb, d), lambda b, *_: (b, 0)),
            scratch_shapes=[pltpu.VMEM((2, ne, MOE_WIN, d), BF16),
                            pltpu.SemaphoreType.DMA((2, ne)),
                            pltpu.VMEM((sb, d), F32)],
        ),
        compiler_params=_cparams("arbitrary"),
        name="moe_combine",
    )(start, cnt, xt, gate, pos_n, gates_n, ys)


def _moe_layer(x, g, sh, sc, gate, router_w, router_b, w_gu, w_down, ebase=0):
    bsz, seq, d = x.shape
    ne = router_w.shape[1]
    n = bsz * seq
    sb = min(MOE_SB, seq)
    tpb = seq // sb
    nb = n // sb
    rt = MOE_RT
    xt = x.reshape(n, d)
    rw_pad = jnp.pad(router_w.astype(F32), ((0, 0), (0, LANES - ne)))
    rb_pad = jnp.pad(router_b.astype(F32).reshape(1, ne), ((0, 0), (0, LANES - ne)), constant_values=NEG_BIG)
    h, meta = _moe_router(xt, g, sh, sc, rw_pad, rb_pad, tpb, sb)
    gates = meta[:, :ne]
    sel = (meta[:, ne:2 * ne] > 0.5).astype(I32)

    selb = sel.reshape(nb, sb, ne)
    cnt = jnp.sum(selb, axis=1)
    rank = meta[:, 2 * ne:3 * ne].astype(I32).reshape(nb, sb, ne)
    total = jnp.sum(cnt, axis=0)
    region = ((total + MOE_WIN + rt - 1) // rt) * rt
    off = jnp.cumsum(region) - region
    start = (off[None, :] + jnp.cumsum(cnt, axis=0) - cnt).astype(I32)
    pos = jnp.where(selb > 0, start[:, None, :] + rank, -1).astype(I32)
    ncap = 2 * n + ne * (MOE_WIN + rt)
    tiles_e = (total + rt - 1) // rt
    tcum = jnp.cumsum(tiles_e)
    ntiles = tcum[-1]
    kk = jnp.minimum(jnp.arange(ncap // rt), ntiles - 1)
    tile_exp = jnp.sum((kk[:, None] >= tcum[None, :]).astype(I32), axis=1)
    tile_blk = (off[tile_exp] // rt + kk - (tcum - tiles_e)[tile_exp]).astype(I32)

    xs = _moe_dispatch(start, cnt.astype(I32), h, jnp.swapaxes(pos, 1, 2), ncap)
    ys = _moe_ffn(tile_blk, tile_exp, ntiles.reshape(1).astype(I32), xs, w_gu, w_down, ebase)
    out = _moe_combine(start, cnt.astype(I32), xt, gate, pos.reshape(n, ne), gates, ys, tpb)
    return out.reshape(bsz, seq, d)


def _head_norm(q, hsum_ref, hexp_ref, gain, scale):
    ms = jnp.dot((q * q).astype(BF16), hsum_ref[...], preferred_element_type=F32)
    r = lax.rsqrt(ms + EPS)
    r_hi = r.astype(BF16)
    r_lo = (r - r_hi.astype(F32)).astype(BF16)
    rexp = (jnp.dot(r_hi, hexp_ref[...], preferred_element_type=F32)
            + jnp.dot(r_lo, hexp_ref[...], preferred_element_type=F32))
    return q * rexp * (gain * scale)


def _qkv_kernel(x_ref, g_ref, sh_ref, sc_ref, wm_ref, ws_ref, qg_ref, kg_ref,
                hsq_ref, heq_ref, hsk_ref, hek_ref,
                q_ref, k_ref, v_ref, qi_ref, kiwi_ref, *, dq, dk, dqi):
    x = x_ref[0]
    h = _norm_mod(x, g_ref[...], sh_ref[0], sc_ref[0]).astype(BF16)
    z = jnp.dot(h, wm_ref[...], preferred_element_type=F32)
    q = z[:, :dq]
    k = z[:, dq:dq + dk]
    v = z[:, dq + dk:dq + 2 * dk]
    qi = z[:, dq + 2 * dk:dq + 2 * dk + dqi]
    q_ref[0] = _head_norm(q, hsq_ref, heq_ref, qg_ref[...], HEAD_DIM ** -0.5 * LOG2E).astype(BF16)
    k_ref[0] = _head_norm(k, hsk_ref, hek_ref, kg_ref[...], 1.0).astype(BF16)
    v_ref[0] = v.astype(BF16)
    qi_ref[0] = qi.astype(BF16)
    kiwi_ref[0] = jnp.dot(h, ws_ref[...], preferred_element_type=F32)


def _head_indicators(nheads):
    hs = np.zeros((nheads * HEAD_DIM, LANES), np.float32)
    he = np.zeros((LANES, nheads * HEAD_DIM), np.float32)
    for hd in range(nheads):
        hs[hd * HEAD_DIM:(hd + 1) * HEAD_DIM, hd] = 1.0 / HEAD_DIM
        he[hd, hd * HEAD_DIM:(hd + 1) * HEAD_DIM] = 1.0
    return jnp.asarray(hs, BF16), jnp.asarray(he, BF16)


def _qkv_proj(x, g, sh, sc, w_main, w_small, q_gain, k_gain, tm=512):
    bsz, seq, d = x.shape
    tm = min(tm, seq)
    dq = N_HEADS * HEAD_DIM
    dk = N_KV_HEADS * HEAD_DIM
    dqi = IDX_HEADS * IDX_DIM
    hsq, heq = _head_indicators(N_HEADS)
    hsk, hek = _head_indicators(N_KV_HEADS)
    qg = jnp.tile(q_gain.reshape(1, HEAD_DIM), (1, N_HEADS)).astype(F32)
    kg = jnp.tile(k_gain.reshape(1, HEAD_DIM), (1, N_KV_HEADS)).astype(F32)
    kern = functools.partial(_qkv_kernel, dq=dq, dk=dk, dqi=dqi)
    mod_spec = pl.BlockSpec((1, 1, d), lambda b, i: (b, 0, 0))

    def full(a):
        return pl.BlockSpec(a.shape, lambda b, i: (0,) * a.ndim)

    def out(n):
        return pl.BlockSpec((1, tm, n), lambda b, i: (b, i, 0))

    return pl.pallas_call(
        kern,
        out_shape=[
            jax.ShapeDtypeStruct((bsz, seq, dq), BF16),
            jax.ShapeDtypeStruct((bsz, seq, dk), BF16),
            jax.ShapeDtypeStruct((bsz, seq, dk), BF16),
            jax.ShapeDtypeStruct((bsz, seq, dqi), BF16),
            jax.ShapeDtypeStruct((bsz, seq, LANES), F32),
        ],
        grid=(bsz, seq // tm),
        in_specs=[
            pl.BlockSpec((1, tm, d), lambda b, i: (b, i, 0)),
            pl.BlockSpec((1, d), lambda b, i: (0, 0)),
            mod_spec, mod_spec,
            full(w_main), full(w_small), full(qg), full(kg),
            full(hsq), full(heq), full(hsk), full(hek),
        ],
        out_specs=[out(dq), out(dk), out(dk), out(dqi), out(LANES)],
        compiler_params=_cparams("parallel", "parallel"),
        name="attn_qkv",
    )(x, g, sh, sc, w_main, w_small, qg, kg, hsq, heq, hsk, hek)


def _rel_bucket_np(dist):
    max_exact = REL_BUCKETS // 2
    d = np.maximum(dist, 1).astype(np.float64)
    large = max_exact + (np.log(d / max_exact) / math.log(REL_MAX_DIST / max_exact)
                         * (REL_BUCKETS - max_exact)).astype(np.int32)
    large = np.minimum(large, REL_BUCKETS - 1)
    return np.where(dist < max_exact, dist, large).astype(np.int32)


def _bias_table_kernel(bucket_ref, rb_ref, o_ref):
    hd = pl.program_id(0)
    bucket = bucket_ref[...]
    acc = jnp.zeros(bucket.shape, F32)
    for b in range(REL_BUCKETS):
        acc = jnp.where(bucket == b, rb_ref[b, hd] * LOG2E, acc)
    o_ref[0] = acc


def _bias_table(rel_bias):
    w = np.arange(BIAS_W)[:, None]
    i = np.arange(ATT_TQ)[None, :]
    bucket = jnp.asarray(_rel_bucket_np(np.maximum(i - w + BIAS_C, 0)))
    return pl.pallas_call(
        _bias_table_kernel,
        out_shape=jax.ShapeDtypeStruct((N_HEADS, BIAS_W, ATT_TQ), F32),
        grid=(N_HEADS,),
        in_specs=[
            pl.BlockSpec((BIAS_W, ATT_TQ), lambda hd: (0, 0)),
            pl.BlockSpec(memory_space=pltpu.SMEM),
        ],
        out_specs=pl.BlockSpec((1, BIAS_W, ATT_TQ), lambda hd: (hd, 0, 0)),
        compiler_params=_cparams("arbitrary"),
        name="attn_bias_table",
    )(bucket, rel_bias.astype(F32))


def _attn_kernel(qT_ref, qiT_ref, wiT_ref, k_ref, vT_ref, ki_ref, pt_ref, o_ref,
                 keys_scr, negm_scr, pidx_scr, oT_scr, acc_scr, qall_scr, sa_scr, sb_scr, *, top_k):
    tq, kc = ATT_TQ, ATT_KC
    qt = pl.program_id(1)
    q0 = qt * tq
    nch = (q0 + tq + kc - 1) // kc
    tpos = q0 + lax.broadcasted_iota(I32, (kc, tq), 1)
    srow = lax.broadcasted_iota(I32, (kc, tq), 0)

    qiT = qiT_ref[0]
    qi_all = jnp.concatenate([qiT[hd * IDX_DIM:(hd + 1) * IDX_DIM, :] for hd in range(IDX_HEADS)], axis=1)
    wiT = wiT_ref[0]

    def score_chunk(c):
        ks = pl.multiple_of(c * kc, kc)
        kic = ki_ref[0, pl.ds(ks, kc), :]
        dots = jnp.dot(kic, qi_all, preferred_element_type=F32)
        acc = jnp.zeros((kc, tq), F32)
        for hd in range(IDX_HEADS):
            acc = acc + jnp.maximum(dots[:, hd * tq:(hd + 1) * tq], 0.0) * wiT[hd:hd + 1, :]
        acc = jnp.where(acc == 0.0, 0.0, acc)
        bits = pltpu.bitcast(acc, I32)
        key = jnp.where(bits < 0, bits ^ INT_MAX, bits)
        key = jnp.where(ks + srow <= tpos, key, INT_MIN)
        keys_scr[pl.ds(ks, kc), :] = key

    npair = (nch + 1) // 2

    def score_pair(i, carry):
        score_chunk(2 * i)
        score_chunk(2 * i + 1)
        return carry

    lax.fori_loop(0, npair, score_pair, 0)

    @pl.when(nch % 2 == 1)
    def _():
        negm_scr[pl.ds(pl.multiple_of(nch * kc, kc), kc), :] = jnp.full((kc, tq), NEG_BIG, F32)

    srow2 = lax.broadcasted_iota(I32, (2 * kc, tq), 0)

    def count(pred):
        def body(c, acc):
            ks = pl.multiple_of(c * 2 * kc, 2 * kc)
            m = pred(keys_scr[pl.ds(ks, 2 * kc), :], ks + srow2).astype(I32)
            return acc + jnp.sum(m.reshape(2 * kc // SUBLANES, SUBLANES, tq), axis=0)
        acc = lax.fori_loop(0, npair, body, jnp.zeros((SUBLANES, tq), I32))
        return jnp.sum(acc, axis=0, keepdims=True)

    def bit_body(it, p):
        cand_p = p | lax.shift_left(jnp.int32(1), 31 - it)
        cand = cand_p ^ INT_MIN
        cnt = count(lambda k, s: k >= cand)
        return jnp.where(cnt >= top_k, cand_p, p)

    p_fin = lax.fori_loop(0, 32, bit_body, jnp.zeros((1, tq), I32))
    v = p_fin ^ INT_MIN

    cnt_gt = count(lambda k, s: k > v)
    cnt_eq = count(lambda k, s: k == v)
    need = top_k - cnt_gt
    pidx_scr[...] = jnp.full((1, tq), INT_MAX, I32)

    @pl.when(jnp.max(cnt_eq - need) > 0)
    def _():
        def ibit(it, p):
            cand = p | lax.shift_left(jnp.int32(1), 11 - it)
            cnt = count(lambda k, s: jnp.logical_and(k == v, s < cand))
            return jnp.where(cnt < need, cand, p)
        pidx_scr[...] = lax.fori_loop(0, 12, ibit, jnp.zeros((1, tq), I32))

    pidx = pidx_scr[...]

    def mask_chunk(c, carry):
        ks = pl.multiple_of(c * kc, kc)
        k = keys_scr[pl.ds(ks, kc), :]
        spos = ks + srow
        sel = jnp.logical_or(k > v, jnp.logical_and(k == v, spos <= pidx))
        sel = jnp.logical_and(sel, spos <= tpos)
        negm_scr[pl.ds(ks, kc), :] = jnp.where(sel, 0.0, NEG_BIG)
        return carry

    lax.fori_loop(0, nch, mask_chunk, 0)

    acc_scr[...] = jnp.zeros(acc_scr.shape, F32)
    for n in range(N_KV_HEADS):
        r0 = n * N_REP * HEAD_DIM
        qall_scr[n] = jnp.concatenate(
            [qT_ref[0, r0 + g * HEAD_DIM:r0 + (g + 1) * HEAD_DIM, :] for g in range(N_REP)], axis=1)

    def qk_chunk(c, s_ref):
        ks = pl.multiple_of(c * kc, kc)
        for n in range(N_KV_HEADS):
            s_ref[n] = jnp.dot(k_ref[0, n, pl.ds(ks, kc), :], qall_scr[n], preferred_element_type=F32)

    def softmax_pv(c, s_ref, ms, far):
        ks = pl.multiple_of(c * kc, kc)
        negm = negm_scr[pl.ds(ks, kc), :]
        w0 = pl.multiple_of(jnp.clip(BIAS_C - (q0 - ks), 0, BIAS_C), LANES)
        new_ms = []
        for n in range(N_KV_HEADS):
            s = s_ref[n]
            if far:
                cvec = jnp.concatenate([pt_ref[n * N_REP + g, 0:1, :] for g in range(N_REP)], axis=1)
                lg = jnp.concatenate([s[:, g * tq:(g + 1) * tq] + negm for g in range(N_REP)], axis=1)
                m_new = jnp.maximum(ms[n], jnp.max(lg, axis=0, keepdims=True) + cvec)
                p = jnp.exp2(lg - (m_new - cvec))
            else:
                lg = jnp.concatenate(
                    [s[:, g * tq:(g + 1) * tq] + pt_ref[n * N_REP + g, pl.ds(w0, kc), :] + negm
                     for g in range(N_REP)], axis=1)
                m_new = jnp.maximum(ms[n], jnp.max(lg, axis=0, keepdims=True))
                p = jnp.exp2(lg - m_new)
            alpha = jnp.exp2(ms[n] - m_new)
            acc_scr[n] = alpha * acc_scr[n] + jnp.dot(vT_ref[0, c, n], p.astype(BF16),
                                                      preferred_element_type=F32)
            new_ms.append(m_new)
        return tuple(new_ms)

    last_chunk = k_ref.shape[2] // kc - 1

    def pair_step(i, ms, far):
        c0 = 2 * i
        qk_chunk(c0 + 1, sb_scr)
        ms = softmax_pv(c0, sa_scr, ms, far)
        qk_chunk(jnp.minimum(c0 + 2, last_chunk), sa_scr)
        return softmax_pv(c0 + 1, sb_scr, ms, far)

    n_far = jnp.clip((q0 - BIAS_C + kc) // kc, 0, nch)
    ms = tuple(jnp.full((1, N_REP * tq), NEG_BIG, F32) for _ in range(N_KV_HEADS))
    qk_chunk(0, sa_scr)
    ms = lax.fori_loop(0, n_far // 2, functools.partial(pair_step, far=True), ms)
    lax.fori_loop(n_far // 2, npair, functools.partial(pair_step, far=False), ms)
    for n in range(N_KV_HEADS):
        o_t = acc_scr[n, 0:HEAD_DIM, :] / acc_scr[n, HEAD_DIM:HEAD_DIM + 1, :]
        for g in range(N_REP):
            r0 = (n * N_REP + g) * HEAD_DIM
            oT_scr[r0:r0 + HEAD_DIM, :] = o_t[:, g * tq:(g + 1) * tq]

    o_ref[0] = oT_scr[...].T.astype(BF16)


def _attention(qT, qiT, wiT, k4, vT, ki, ptab, top_k):
    bsz, dq, seq = qT.shape
    tq = ATT_TQ
    assert seq % (2 * ATT_KC) == 0
    kern = functools.partial(_attn_kernel, top_k=top_k)
    return pl.pallas_call(
        kern,
        out_shape=jax.ShapeDtypeStruct((bsz, seq, dq), BF16),
        grid=(bsz, seq // tq),
        in_specs=[
            pl.BlockSpec((1, dq, tq), lambda b, i: (b, 0, i)),
            pl.BlockSpec((1, qiT.shape[1], tq), lambda b, i: (b, 0, i)),
            pl.BlockSpec((1, wiT.shape[1], tq), lambda b, i: (b, 0, i)),
            pl.BlockSpec((1,) + k4.shape[1:], lambda b, i: (b, 0, 0, 0)),
            pl.BlockSpec((1,) + vT.shape[1:], lambda b, i: (b, 0, 0, 0, 0)),
            pl.BlockSpec((1,) + ki.shape[1:], lambda b, i: (b, 0, 0)),
            pl.BlockSpec(ptab.shape, lambda b, i: (0, 0, 0)),
        ],
        out_specs=pl.BlockSpec((1, tq, dq), lambda b, i: (b, i, 0)),
        scratch_shapes=[
            pltpu.VMEM((seq, tq), I32),
            pltpu.VMEM((seq, tq), F32),
            pltpu.VMEM((1, tq), I32),
            pltpu.VMEM((dq, tq), F32),
            pltpu.VMEM((N_KV_HEADS, ATT_VROWS, N_REP * tq), F32),
            pltpu.VMEM((N_KV_HEADS, HEAD_DIM, N_REP * tq), BF16),
            pltpu.VMEM((N_KV_HEADS, ATT_KC, N_REP * tq), F32),
            pltpu.VMEM((N_KV_HEADS, ATT_KC, N_REP * tq), F32),
        ],
        compiler_params=_cparams("parallel", "arbitrary"),
        name="attn_core",
    )(qT, qiT, wiT, k4, vT, ki, ptab)


def _proj_res_kernel(a_ref, x_ref, gate_ref, w_ref, o_ref):
    y = jnp.dot(a_ref[0], w_ref[...], preferred_element_type=F32)
    o_ref[0] = x_ref[0] + gate_ref[0] * y


def _proj_residual(a, x, gate, w, tm=512):
    bsz, seq, d = x.shape
    tm = min(tm, seq)
    return pl.pallas_call(
        _proj_res_kernel,
        out_shape=jax.ShapeDtypeStruct((bsz, seq, d), F32),
        grid=(bsz, seq // tm),
        in_specs=[
            pl.BlockSpec((1, tm, a.shape[2]), lambda b, i: (b, i, 0)),
            pl.BlockSpec((1, tm, d), lambda b, i: (b, i, 0)),
            pl.BlockSpec((1, 1, d), lambda b, i: (b, 0, 0)),
            pl.BlockSpec(w.shape, lambda b, i: (0, 0)),
        ],
        out_specs=pl.BlockSpec((1, tm, d), lambda b, i: (b, i, 0)),
        compiler_params=_cparams("parallel", "parallel"),
        name="proj_residual",
    )(a, x, gate, w)


def _attn_layer(x, g, sh, sc, gate, w_in, q_gain, k_gain, w_out, rel_bias):
    bsz, seq, d = x.shape
    top_k = min(TOPK_MAX, seq // 4)
    dq = N_HEADS * HEAD_DIM
    dk = N_KV_HEADS * HEAD_DIM
    dqi = IDX_HEADS * IDX_DIM
    nmain = dq + 2 * dk + dqi
    w_main = w_in[:, :nmain].astype(BF16)
    w_small = jnp.pad(w_in[:, nmain:], ((0, 0), (0, LANES - (IDX_DIM + IDX_HEADS)))).astype(BF16)
    q, k, v, qi, kiwi = _qkv_proj(x, g, sh, sc, w_main, w_small, q_gain, k_gain)
    ki = kiwi[:, :, :IDX_DIM].astype(BF16)
    wi = kiwi[:, :, IDX_DIM:IDX_DIM + IDX_HEADS] * (IDX_HEADS ** -0.5 * IDX_DIM ** -0.5)
    qT = jnp.swapaxes(q, 1, 2)
    qiT = jnp.swapaxes(qi, 1, 2)
    wiT = jnp.swapaxes(wi, 1, 2)
    nck = seq // ATT_KC
    vT = jnp.swapaxes(v.reshape(bsz, nck, ATT_KC, dk), 2, 3).reshape(bsz, nck, N_KV_HEADS, HEAD_DIM, ATT_KC)
    vT = jnp.concatenate([
        vT, jnp.ones((bsz, nck, N_KV_HEADS, 1, ATT_KC), BF16),
        jnp.zeros((bsz, nck, N_KV_HEADS, ATT_VROWS - HEAD_DIM - 1, ATT_KC), BF16)], axis=3)
    k4 = jnp.swapaxes(k.reshape(bsz, seq, N_KV_HEADS, HEAD_DIM), 1, 2)
    ptab = _bias_table(rel_bias)
    attn = _attention(qT, qiT, wiT, k4, vT, ki, ptab, top_k)
    return _proj_residual(attn, x, gate, w_out.astype(BF16))


def _s5_prep_kernel(lre_ref, lim_ref, ls_ref, bre_ref, bim_ref, cre_ref, cim_ref,
                    bcw_ref, mtw_ref, ccw_ref, are_ref, aim_ref):
    t_len = SSM_CHUNK
    lre = jnp.minimum(lre_ref[0], -1e-4)
    lim = lim_ref[0]
    step = jnp.exp(ls_ref[0])
    ar = lre * step
    ai = lim * step

    def powers(jv):
        mag = jnp.exp(jv * ar)
        return mag * jnp.cos(jv * ai), mag * jnp.sin(jv * ai)

    lb_re, lb_im = powers(1.0)
    nr = lb_re - 1.0
    ni = lb_im
    den = lre * lre + lim * lim
    cf_re = (nr * lre + ni * lim) / den
    cf_im = (ni * lre - nr * lim) / den
    bre = bre_ref[0]
    bim = bim_ref[0]
    bb_re = cf_re * bre - cf_im * bim
    bb_im = cf_re * bim + cf_im * bre
    cre = cre_ref[0]
    cim = cim_ref[0]
    nst = lre.shape[-1]
    jv = lax.broadcasted_iota(I32, (t_len, 1, nst), 0).astype(F32)
    pj_re, pj_im = powers(jv)
    a_re = (cre[None] * pj_re - cim[None] * pj_im).reshape(t_len * SSM_GROUP, nst)
    a_im = (cre[None] * pj_im + cim[None] * pj_re).reshape(t_len * SSM_GROUP, nst)
    dn = (((1,), (1,)), ((), ()))
    cg = SSM_GROUP
    tc = t_len * cg
    width = t_len * LANES
    gq = pl.program_id(0) % (LANES // cg)
    nh = (LANES // cg) * nst
    cg_shift = cg.bit_length() - 1

    def place(nrows, target):
        r = lax.broadcasted_iota(I32, (nrows, width), 0)
        col = lax.broadcasted_iota(I32, (nrows, width), 1)
        return jnp.where(col == target(r), 1.0, 0.0).astype(BF16)

    pm = place(tc, lambda r: lax.shift_right_logical(r, cg_shift) * LANES + gq * cg + (r & (cg - 1)))
    k_t = (lax.dot_general(bb_re, a_re, dn, preferred_element_type=F32, precision=HIGHEST)
           - lax.dot_general(bb_im, a_im, dn, preferred_element_type=F32, precision=HIGHEST))
    lane = lax.broadcasted_iota(I32, (cg, tc), 1)
    mt_rows = [k_t] + [jnp.where(lane >= s * cg, pltpu.roll(k_t, s * cg, 1), 0.0) for s in range(1, t_len)]
    mt_t = jnp.concatenate(mt_rows, axis=0)
    mtw_ref[0] = jnp.dot(mt_t.astype(BF16), pm, preferred_element_type=F32).astype(BF16)
    pr_re, pr_im = powers((t_len - 1.0) - jv)
    bc_re = (bb_re[None] * pr_re - bb_im[None] * pr_im).reshape(tc, nst)
    bc_im = (bb_re[None] * pr_im + bb_im[None] * pr_re).reshape(tc, nst)
    pb_re = place(nst, lambda r: gq * nst + r)
    pb_im = place(nst, lambda r: nh + gq * nst + r)
    bcw_ref[0] = (jnp.dot(bc_re.astype(BF16), pb_re, preferred_element_type=F32)
                  + jnp.dot(bc_im.astype(BF16), pb_im, preferred_element_type=F32)).astype(BF16)
    pn_re, pn_im = powers(jv + 1.0)
    cc_re = (cre[None] * pn_re - cim[None] * pn_im).reshape(tc, nst)
    cc_im = (cre[None] * pn_im + cim[None] * pn_re).reshape(tc, nst)
    eye = jnp.where(lax.broadcasted_iota(I32, (nst, nst), 0) == lax.broadcasted_iota(I32, (nst, nst), 1),
                    1.0, 0.0)
    cct_re = lax.dot_general(eye, cc_re, dn, preferred_element_type=F32, precision=HIGHEST)
    cct_im = lax.dot_general(eye, cc_im, dn, preferred_element_type=F32, precision=HIGHEST)
    ccw_ref[0, 0:nst, :] = jnp.dot(cct_re.astype(BF16), pm, preferred_element_type=F32).astype(BF16)
    ccw_ref[0, nst:2 * nst, :] = (-jnp.dot(cct_im.astype(BF16), pm, preferred_element_type=F32)).astype(BF16)
    at_re, at_im = powers(float(t_len))
    rs = lax.broadcasted_iota(I32, (nst, nh), 0)
    cs = lax.broadcasted_iota(I32, (nst, nh), 1)
    pa = jnp.where(cs == gq * nst + rs, 1.0, 0.0)
    are_ref[0] = jnp.dot(at_re, pa, preferred_element_type=F32, precision=HIGHEST)
    aim_ref[0] = jnp.dot(at_im, pa, preferred_element_type=F32, precision=HIGHEST)


def _s5_prep(lam_re, lam_im, log_step, b_re, b_im, c_re, c_im):
    ng, nst = lam_re.shape
    tc = SSM_CHUNK * SSM_GROUP
    vec = pl.BlockSpec((1, 1, nst), lambda gi: (gi, 0, 0))
    mat = pl.BlockSpec((1, SSM_GROUP, nst), lambda gi: (gi, 0, 0))
    width = SSM_CHUNK * LANES
    assert tc == 2 * nst and 2 * (LANES // SSM_GROUP) * nst == width
    big = pl.BlockSpec((1, tc, width), lambda gi: (gi, 0, 0))
    wide = jax.ShapeDtypeStruct((ng, tc, width), BF16)
    return pl.pallas_call(
        _s5_prep_kernel,
        out_shape=[wide, wide, wide,
                   jax.ShapeDtypeStruct((ng, 1, width // 2), F32),
                   jax.ShapeDtypeStruct((ng, 1, width // 2), F32)],
        grid=(ng,),
        in_specs=[vec, vec, pl.BlockSpec((1, 1, 1), lambda gi: (gi, 0, 0)), mat, mat, mat, mat],
        out_specs=[big, big, big, pl.BlockSpec((1, 1, width // 2), lambda gi: (gi, 0, 0)),
                   pl.BlockSpec((1, 1, width // 2), lambda gi: (gi, 0, 0))],
        compiler_params=_cparams("parallel"),
        name="s5_prep",
    )(lam_re.reshape(ng, 1, nst), lam_im.reshape(ng, 1, nst), log_step.reshape(ng, 1, 1),
      jnp.swapaxes(b_re, 1, 2), jnp.swapaxes(b_im, 1, 2), c_re, c_im)


def _s5_pre_kernel(x_ref, g_ref, sh_ref, sc_ref, o_ref, h_scr):
    t_len = SSM_CHUNK
    h = _norm_mod(x_ref[0], g_ref[...], sh_ref[0], sc_ref[0])
    nj = o_ref.shape[1]
    for q in range(o_ref.shape[0]):
        h_scr[q] = h[:, q * LANES:(q + 1) * LANES]
        for t in range(t_len):
            o_ref[q, :, t * LANES:(t + 1) * LANES] = h_scr[q, pl.ds(t, nj, stride=t_len), :].astype(BF16)


def _s5_pre(x, g, sh, sc, tm=512):
    bsz, seq, d = x.shape
    tm = min(tm, seq)
    nt = seq // tm
    nq = d // LANES
    t_len = SSM_CHUNK
    mod_spec = pl.BlockSpec((1, 1, d), lambda b, i: (b, 0, 0))
    return pl.pallas_call(
        _s5_pre_kernel,
        out_shape=jax.ShapeDtypeStruct((nq, bsz * seq // t_len, t_len * LANES), BF16),
        grid=(bsz, nt),
        in_specs=[pl.BlockSpec((1, tm, d), lambda b, i: (b, i, 0)),
                  pl.BlockSpec((1, d), lambda b, i: (0, 0)), mod_spec, mod_spec],
        out_specs=pl.BlockSpec((nq, tm // t_len, t_len * LANES), lambda b, i: (0, b * nt + i, 0)),
        scratch_shapes=[pltpu.VMEM((nq, tm, LANES), F32)],
        compiler_params=_cparams("parallel", "parallel"),
        name="s5_pre",
    )(x, g, sh, sc)


def _s5_scan_kernel(x_ref, bc_ref, mt_ref, cc_ref, are_ref, aim_ref, y_ref, re_scr, im_scr):
    x = x_ref[0]
    nrow = x.shape[0]
    pad = re_scr.shape[0] - nrow
    v = jnp.dot(x, bc_ref[0], preferred_element_type=F32)
    nh = v.shape[1] // 2
    s_re = v[:, :nh]
    s_im = v[:, nh:]
    a_re = are_ref[0]
    a_im = aim_ref[0]
    re_scr[0:pad, :] = jnp.zeros((pad, nh), F32)
    im_scr[0:pad, :] = jnp.zeros((pad, nh), F32)

    def shifted(scr, val, dist):
        scr[pad:pad + nrow, :] = val
        return scr[pad - dist:pad - dist + nrow, :]

    dist = 1
    while dist < nrow:
        sh_re = shifted(re_scr, s_re, dist)
        sh_im = shifted(im_scr, s_im, dist)
        s_re, s_im = (s_re + a_re * sh_re - a_im * sh_im, s_im + a_re * sh_im + a_im * sh_re)
        a_re, a_im = (a_re * a_re - a_im * a_im, 2.0 * a_re * a_im)
        dist *= 2
    sp = jnp.concatenate([shifted(re_scr, s_re, 1), shifted(im_scr, s_im, 1)], axis=1).astype(BF16)
    y_ref[0] = (jnp.dot(x, mt_ref[0], preferred_element_type=F32)
                + jnp.dot(sp, cc_ref[0], preferred_element_type=F32))


def _s5_scan(hq, bcq, mtq, ccq, a_re, a_im, bsz):
    nq, nrows, width = hq.shape
    nj = nrows // bsz
    nh = a_re.shape[-1]
    pad = max(nj // 2, SUBLANES)
    tile = pl.BlockSpec((1, nj, width), lambda q, b: (q, b, 0))
    wspec = lambda a: pl.BlockSpec((1,) + a.shape[1:], lambda q, b: (q, 0, 0))
    return pl.pallas_call(
        _s5_scan_kernel,
        out_shape=jax.ShapeDtypeStruct((nq, nrows, width), F32),
        grid=(nq, bsz),
        in_specs=[tile, wspec(bcq), wspec(mtq), wspec(ccq), wspec(a_re), wspec(a_im)],
        out_specs=tile,
        scratch_shapes=[pltpu.VMEM((pad + nj, nh), F32), pltpu.VMEM((pad + nj, nh), F32)],
        compiler_params=_cparams("parallel", "parallel"),
        name="s5_scan",
    )(hq, bcq, mtq, ccq, a_re, a_im)


def _s5_post_kernel(x_ref, y_ref, g_ref, sh_ref, sc_ref, gate_ref, dsk_ref, w_ref, o_ref, y_scr, *, d):
    t_len = SSM_CHUNK
    x = x_ref[0]
    h = _norm_mod(x, g_ref[...], sh_ref[0], sc_ref[0])
    nj = y_ref.shape[1]
    for q in range(y_ref.shape[0]):
        for t in range(t_len):
            y_scr[q, pl.ds(t, nj, stride=t_len), :] = y_ref[q, :, t * LANES:(t + 1) * LANES]
    y = jnp.concatenate([y_scr[q] for q in range(y_ref.shape[0])], axis=1)
    yy = y + dsk_ref[...] * h
    gl = jax.nn.gelu(yy).astype(BF16)
    z = jnp.dot(gl, w_ref[...], preferred_element_type=F32)
    o_ref[0] = x + gate_ref[0] * (z[:, :d] * jax.nn.sigmoid(z[:, d:]))


def _s5_post(x, yq, g, sh, sc, gate, d_skip, w_glu, tm=512):
    bsz, seq, d = x.shape
    tm = min(tm, seq)
    nt = seq // tm
    nq = d // LANES
    kern = functools.partial(_s5_post_kernel, d=d)
    mod_spec = pl.BlockSpec((1, 1, d), lambda b, i: (b, 0, 0))
    tile = pl.BlockSpec((1, tm, d), lambda b, i: (b, i, 0))
    return pl.pallas_call(
        kern,
        out_shape=jax.ShapeDtypeStruct((bsz, seq, d), F32),
        grid=(bsz, nt),
        in_specs=[tile,
                  pl.BlockSpec((nq, tm // SSM_CHUNK, SSM_CHUNK * LANES), lambda b, i: (0, b * nt + i, 0)),
                  pl.BlockSpec((1, d), lambda b, i: (0, 0)), mod_spec, mod_spec, mod_spec,
                  pl.BlockSpec((1, d), lambda b, i: (0, 0)),
                  pl.BlockSpec(w_glu.shape, lambda b, i: (0, 0))],
        out_specs=tile,
        scratch_shapes=[pltpu.VMEM((nq, tm, LANES), F32)],
        compiler_params=_cparams("parallel", "parallel"),
        name="s5_post",
    )(x, yq, g, sh, sc, gate, d_skip, w_glu)


def _s5_layer(x, g, sh, sc, gate, lam_re, lam_im, log_step, b_re, b_im, c_re, c_im, d_skip, w_glu):
    bsz, seq, d = x.shape
    ng, nst = lam_re.shape
    t_len, cg = SSM_CHUNK, SSM_GROUP
    nq = d // LANES
    gq = LANES // cg
    nj = seq // t_len
    bcw, mtw, ccw, a_re, a_im = _s5_prep(lam_re, lam_im, log_step, b_re, b_im, c_re, c_im)
    width = t_len * LANES
    bcq = jnp.swapaxes(bcw.reshape(nq, gq, t_len, cg, width), 1, 2).reshape(nq, width, width)
    mtq = jnp.swapaxes(mtw.reshape(nq, gq, t_len, cg, width), 1, 2).reshape(nq, width, width)
    ccq = jnp.swapaxes(ccw.reshape(nq, gq, 2, nst, width), 1, 2).reshape(nq, 2 * gq * nst, width)
    aq_re = jnp.sum(a_re.reshape(nq, gq, 1, gq * nst), axis=1)
    aq_im = jnp.sum(a_im.reshape(nq, gq, 1, gq * nst), axis=1)

    hq = _s5_pre(x, g, sh, sc)
    yq = _s5_scan(hq, bcq, mtq, ccq, aq_re, aq_im, bsz)
    return _s5_post(x, yq, g, sh, sc, gate, d_skip.reshape(1, d).astype(F32), w_glu.astype(BF16))


def kernel(x, c, ada_w, ada_b, norm_g, conv_w_in, conv_w, conv_w_out, attn_w_in, attn_q_gain, attn_k_gain, attn_w_out, rel_bias, ssm_lambda_re, ssm_lambda_im, ssm_log_step, ssm_b_re, ssm_b_im, ssm_c_re, ssm_c_im, ssm_d, ssm_w_glu, ffn_w_gu, ffn_w_down, moe_router_w, moe_router_b, moe_w_gu, moe_w_down):
    bsz, seq, d = x.shape
    depth = ada_w.shape[0]
    mod = _ada_mod(c, ada_w, ada_b).reshape(depth, bsz, 6, 1, d)
    moe_gu = moe_w_gu.astype(BF16).reshape((-1,) + moe_w_gu.shape[2:])
    moe_down = moe_w_down.astype(BF16).reshape((-1,) + moe_w_down.shape[2:])
    for i in range(depth):
        sh1, sc1, g1, sh2, sc2, g2 = (mod[i, :, r] for r in range(6))
        gn1 = norm_g[i, 0].reshape(1, d)
        gn2 = norm_g[i, 1].reshape(1, d)
        j = i // N_MIXERS
        if i % N_MIXERS == 0:
            x = _conv_layer(x, gn1, sh1, sc1, g1, conv_w_in[j].astype(BF16), conv_w[j],
                            conv_w_out[j].astype(BF16))
        elif i % N_MIXERS == 1:
            x = _attn_layer(x, gn1, sh1, sc1, g1, attn_w_in[j], attn_q_gain[j], attn_k_gain[j],
                            attn_w_out[j], rel_bias)
        else:
            x = _s5_layer(x, gn1, sh1, sc1, g1, ssm_lambda_re[j], ssm_lambda_im[j], ssm_log_step[j],
                          ssm_b_re[j], ssm_b_im[j], ssm_c_re[j], ssm_c_im[j], ssm_d[j], ssm_w_glu[j])
        if i % 2 == 0:
            x = _ffn_layer(x, gn2, sh2, sc2, g2, ffn_w_gu[i // 2].astype(BF16), ffn_w_down[i // 2].astype(BF16))
        else:
            x = _moe_layer(x, gn2, sh2, sc2, g2, moe_router_w[i // 2], moe_router_b[i // 2],
                           moe_gu, moe_down, ebase=(i // 2) * moe_w_gu.shape[1])
    return x
```

```python
import functools
import math

import numpy as np
import jax
import jax.numpy as jnp
from jax import lax
from jax.experimental import pallas as pl
from jax.experimental.pallas import tpu as pltpu

F32 = jnp.float32
BF16 = jnp.bfloat16
I32 = jnp.int32
HIGHEST = lax.Precision.HIGHEST

DEPTH = 4
N_MIXERS = 3
EPS = 1e-6
CONV_WIDTH = 3
N_HEADS = 16
N_KV_HEADS = 4
N_REP = N_HEADS // N_KV_HEADS
HEAD_DIM = 64
IDX_HEADS = 8
IDX_DIM = 64
TOPK_MAX = 256
REL_BUCKETS = 32
REL_MAX_DIST = 128
SSM_GROUP = 16
SSM_STATE = 64
N_EXPERTS = 8
TOP_K_EXPERTS = 2

VMEM_LIMIT_BYTES = 56 * 1024 * 1024
LANES = 128
SUBLANES = 8

INT_MIN = -(2 ** 31)
INT_MAX = 2 ** 31 - 1
NEG_BIG = -1e30
LOG2E = 1.4426950408889634

ATT_TQ = 128
ATT_KC = 256
BIAS_C = 384
BIAS_W = BIAS_C + ATT_KC
ATT_VROWS = 80

SSM_CHUNK = 8


def _cparams(*sem):
    return pltpu.CompilerParams(dimension_semantics=sem, vmem_limit_bytes=VMEM_LIMIT_BYTES)


def _norm_mod(x, g, shift, scale):
    ms = jnp.mean(x * x, axis=-1, keepdims=True)
    y = x * lax.rsqrt(ms + EPS)
    return (y * g) * (1.0 + scale) + shift


def _silu(x):
    return x * jax.nn.sigmoid(x)


def _ada_kernel(c_ref, w_ref, b_ref, o_ref):
    c = c_ref[...]
    cond = _silu(c)
    o_ref[0] = jnp.dot(cond, w_ref[0], preferred_element_type=F32, precision=HIGHEST) + b_ref[0]


def _ada_mod(c, ada_w, ada_b):
    depth, d, d6 = ada_w.shape
    bsz = c.shape[0]
    tn = d
    return pl.pallas_call(
        _ada_kernel,
        out_shape=jax.ShapeDtypeStruct((depth, bsz, d6), F32),
        grid=(depth, d6 // tn),
        in_specs=[
            pl.BlockSpec((bsz, d), lambda i, j: (0, 0)),
            pl.BlockSpec((1, d, tn), lambda i, j: (i, 0, j)),
            pl.BlockSpec((1, 1, tn), lambda i, j: (i, 0, j)),
        ],
        out_specs=pl.BlockSpec((1, bsz, tn), lambda i, j: (i, 0, j)),
        compiler_params=_cparams("parallel", "parallel"),
        name="ada_mod",
    )(c, ada_w, ada_b.reshape(depth, 1, d6))


def _conv_kernel(x_ref, xh_ref, g_ref, sh_ref, sc_ref, gate_ref, win_ref, wc_ref, wout_ref,
                 o_ref, u_scr, *, tm, d):
    i = pl.program_id(1)
    g = g_ref[...]
    sh = sh_ref[0]
    sc = sc_ref[0]
    x = x_ref[0]
    halo = xh_ref.shape[1]
    xe = jnp.concatenate([xh_ref[0], x], axis=0)
    he = _norm_mod(xe, g, sh, sc).astype(BF16)
    z = jnp.dot(he, win_ref[...], preferred_element_type=F32)
    u_all = z[:, d:2 * d] * z[:, 2 * d:]
    row = lax.broadcasted_iota(I32, (halo + tm, 1), 0)
    u_scr[...] = jnp.where(jnp.logical_or(i > 0, row >= halo), u_all, 0.0)
    b_gate = z[halo:, :d]
    wc = wc_ref[...]
    conv = (wc[0:1, :] * u_scr[halo - 2:halo - 2 + tm, :]
            + wc[1:2, :] * u_scr[halo - 1:halo - 1 + tm, :]
            + wc[2:3, :] * u_scr[halo:halo + tm, :])
    y = jnp.dot((b_gate * conv).astype(BF16), wout_ref[...], preferred_element_type=F32)
    o_ref[0] = x + gate_ref[0] * y


def _conv_layer(x, g, sh, sc, gate, w_in, w_conv, w_out, tm=512):
    bsz, seq, d = x.shape
    tm = min(tm, seq)
    nt = seq // tm
    halo = 2 * SUBLANES
    hb = tm // halo
    kern = functools.partial(_conv_kernel, tm=tm, d=d)
    mod_spec = pl.BlockSpec((1, 1, d), lambda b, i: (b, 0, 0))
    return pl.pallas_call(
        kern,
        out_shape=jax.ShapeDtypeStruct((bsz, seq, d), F32),
        grid=(bsz, nt),
        in_specs=[
            pl.BlockSpec((1, tm, d), lambda b, i: (b, i, 0)),
            pl.BlockSpec((1, halo, d), lambda b, i: (b, jnp.maximum(i * hb - 1, 0), 0)),
            pl.BlockSpec((1, d), lambda b, i: (0, 0)),
            mod_spec, mod_spec, mod_spec,
            pl.BlockSpec((d, 3 * d), lambda b, i: (0, 0)),
            pl.BlockSpec((CONV_WIDTH, d), lambda b, i: (0, 0)),
            pl.BlockSpec((d, d), lambda b, i: (0, 0)),
        ],
        out_specs=pl.BlockSpec((1, tm, d), lambda b, i: (b, i, 0)),
        scratch_shapes=[pltpu.VMEM((tm + halo, d), F32)],
        compiler_params=_cparams("parallel", "parallel"),
        name="conv_mixer",
    )(x, x, g, sh, sc, gate, w_in, w_conv, w_out)


def _ffn_kernel(x_ref, g_ref, sh_ref, sc_ref, gate_ref, wgu_ref, wd_ref, o_ref, *, dff, nchunk):
    x = x_ref[0]
    h = _norm_mod(x, g_ref[...], sh_ref[0], sc_ref[0]).astype(BF16)
    cols = dff // nchunk
    acc = jnp.zeros(x.shape, F32)
    for c in range(nchunk):
        gg = jnp.dot(h, wgu_ref[:, c * cols:(c + 1) * cols], preferred_element_type=F32)
        uu = jnp.dot(h, wgu_ref[:, dff + c * cols:dff + (c + 1) * cols], preferred_element_type=F32)
        a = (_silu(gg) * uu).astype(BF16)
        acc = acc + jnp.dot(a, wd_ref[c * cols:(c + 1) * cols, :], preferred_element_type=F32)
    o_ref[0] = x + gate_ref[0] * acc


def _ffn_layer(x, g, sh, sc, gate, w_gu, w_down, tm=512):
    bsz, seq, d = x.shape
    dff = w_down.shape[0]
    tm = min(tm, seq)
    kern = functools.partial(_ffn_kernel, dff=dff, nchunk=2)
    mod_spec = pl.BlockSpec((1, 1, d), lambda b, i: (b, 0, 0))
    return pl.pallas_call(
        kern,
        out_shape=jax.ShapeDtypeStruct((bsz, seq, d), F32),
        grid=(bsz, seq // tm),
        in_specs=[
            pl.BlockSpec((1, tm, d), lambda b, i: (b, i, 0)),
            pl.BlockSpec((1, d), lambda b, i: (0, 0)),
            mod_spec, mod_spec, mod_spec,
            pl.BlockSpec((d, 2 * dff), lambda b, i: (0, 0), pipeline_mode=pl.Buffered(1)),
            pl.BlockSpec((dff, d), lambda b, i: (0, 0), pipeline_mode=pl.Buffered(1)),
        ],
        out_specs=pl.BlockSpec((1, tm, d), lambda b, i: (b, i, 0)),
        compiler_params=_cparams("parallel", "parallel"),
        name="ffn_dense",
    )(x, g, sh, sc, gate, w_gu, w_down)


MOE_SB = 512
MOE_RT = 512
MOE_ALIGN = 16
MOE_WIN = 256
MOE_CAP = MOE_WIN - MOE_ALIGN


def _moe_router_kernel(x_ref, g_ref, sh_ref, sc_ref, rw_ref, rb_ref, h_ref, meta_ref):
    hf = _norm_mod(x_ref[...], g_ref[...], sh_ref[0], sc_ref[0])
    h_ref[...] = hf.astype(BF16)
    logits = jnp.dot(hf, rw_ref[...], preferred_element_type=F32, precision=HIGHEST) + rb_ref[...]
    mx = jnp.max(logits, axis=-1, keepdims=True)
    ex = jnp.exp(logits - mx)
    probs = ex / jnp.sum(ex, axis=-1, keepdims=True)
    lane = lax.broadcasted_iota(I32, probs.shape, 1)
    m1 = jnp.max(probs, axis=-1, keepdims=True)
    i1 = jnp.min(jnp.where(probs == m1, lane, LANES), axis=-1, keepdims=True)
    rest = jnp.where(lane == i1, -1.0, probs)
    m2 = jnp.max(rest, axis=-1, keepdims=True)
    i2 = jnp.min(jnp.where(rest == m2, lane, LANES), axis=-1, keepdims=True)
    den = m1 + m2
    gates = jnp.where(lane == i1, m1 / den, 0.0) + jnp.where(lane == i2, m2 / den, 0.0)
    chosen = jnp.where(jnp.logical_or(lane == i1, lane == i2), 1.0, 0.0)
    nrow = chosen.shape[0]
    below = jnp.where(lax.broadcasted_iota(I32, (nrow, nrow), 1) < lax.broadcasted_iota(I32, (nrow, nrow), 0),
                      1.0, 0.0).astype(BF16)
    rank = jnp.dot(below, chosen.astype(BF16), preferred_element_type=F32)
    meta_ref[...] = gates + pltpu.roll(chosen, N_EXPERTS, 1) + pltpu.roll(rank, 2 * N_EXPERTS, 1)


def _moe_router(xt, g, sh, sc, rw_pad, rb_pad, tpb, tm):
    n, d = xt.shape
    mod_spec = pl.BlockSpec((1, 1, d), lambda i: (i // tpb, 0, 0))
    return pl.pallas_call(
        _moe_router_kernel,
        out_shape=[jax.ShapeDtypeStruct((n, d), BF16), jax.ShapeDtypeStruct((n, LANES), F32)],
        grid=(n // tm,),
        in_specs=[pl.BlockSpec((tm, d), lambda i: (i, 0)),
                  pl.BlockSpec((1, d), lambda i: (0, 0)), mod_spec, mod_spec,
                  pl.BlockSpec((d, LANES), lambda i: (0, 0)),
                  pl.BlockSpec((1, LANES), lambda i: (0, 0))],
        out_specs=[pl.BlockSpec((tm, d), lambda i: (i, 0)), pl.BlockSpec((tm, LANES), lambda i: (i, 0))],
        compiler_params=_cparams("parallel"),
        name="moe_router",
    )(xt, g, sh, sc, rw_pad, rb_pad)


def _window(start, count, w):
    s = start + w * MOE_CAP
    n = jnp.minimum(count - w * MOE_CAP, MOE_CAP)
    a = pl.multiple_of((s // MOE_ALIGN) * MOE_ALIGN, MOE_ALIGN)
    return s, n, a


def _moe_dispatch_kernel(start_ref, cnt_ref, h_ref, pos_ref, xs_init_ref, xs_ref,
                         buf, carry, sems, pending):
    del xs_init_ref
    b = pl.program_id(0)
    nb = pl.num_programs(0)
    ne = buf.shape[0]

    @pl.when(b == 0)
    def _():
        carry[...] = jnp.zeros(carry.shape, BF16)
        for e in range(ne):
            pending[e] = 0

    h = h_ref[...]
    riota = lax.broadcasted_iota(I32, (MOE_WIN, h.shape[0]), 0)

    def out_copy(e, a):
        return pltpu.make_async_copy(buf.at[e], xs_ref.at[pl.ds(a, MOE_WIN)], sems.at[e])

    for e in range(ne):
        posrow = pos_ref[0, e:e + 1, :]
        nwin = (cnt_ref[b, e] + MOE_CAP - 1) // MOE_CAP

        def wbody(w, carry_unused, e=e, posrow=posrow):
            s, n, a = _window(start_ref[b, e], cnt_ref[b, e], w)

            @pl.when(pending[e] == 1)
            def _():
                out_copy(e, 0).wait()

            hit = jnp.logical_and(posrow - a == riota,
                                  jnp.logical_and(posrow >= s, posrow < s + n))
            onehot = jnp.where(hit, 1.0, 0.0).astype(BF16)
            buf[e] = jnp.dot(onehot, h, preferred_element_type=F32).astype(BF16)
            buf[e, 0:MOE_ALIGN, :] = buf[e, 0:MOE_ALIGN, :] + carry[e]
            c0 = pl.multiple_of(((s + n) // MOE_ALIGN) * MOE_ALIGN - a, MOE_ALIGN)
            carry[e] = buf[e, pl.ds(c0, MOE_ALIGN), :]
            out_copy(e, a).start()
            pending[e] = 1
            return carry_unused

        lax.fori_loop(0, nwin, wbody, 0)

    @pl.when(b == nb - 1)
    def _():
        for e in range(ne):
            @pl.when(pending[e] == 1)
            def _():
                out_copy(e, 0).wait()


def _moe_dispatch(start, cnt, h, pos_t, ncap):
    n, d = h.shape
    nb, ne, sb = pos_t.shape
    xs_init = jnp.zeros((ncap, d), BF16)
    return pl.pallas_call(
        _moe_dispatch_kernel,
        out_shape=jax.ShapeDtypeStruct((ncap, d), BF16),
        grid_spec=pltpu.PrefetchScalarGridSpec(
            num_scalar_prefetch=2,
            grid=(nb,),
            in_specs=[pl.BlockSpec((sb, d), lambda b, *_: (b, 0)),
                      pl.BlockSpec((1, ne, sb), lambda b, *_: (b, 0, 0)),
                      pl.BlockSpec(memory_space=pl.ANY)],
            out_specs=pl.BlockSpec(memory_space=pl.ANY),
            scratch_shapes=[pltpu.VMEM((ne, MOE_WIN, d), BF16),
                            pltpu.VMEM((ne, MOE_ALIGN, d), BF16),
                            pltpu.SemaphoreType.DMA((ne,)),
                            pltpu.SMEM((ne,), I32)],
        ),
        input_output_aliases={4: 0},
        compiler_params=_cparams("arbitrary"),
        name="moe_dispatch",
    )(start, cnt, h, pos_t, xs_init)


def _moe_ffn_kernel(blk_ref, exp_ref, nt_ref, xs_ref, wgu_ref, wd_ref, ys_init_ref, ys_ref, *, dff, nchunk):
    del blk_ref, exp_ref, ys_init_ref
    k = pl.program_id(0)

    @pl.when(k < nt_ref[0])
    def _():
        x = xs_ref[...]
        cols = dff // nchunk
        acc = jnp.zeros(x.shape, F32)
        for c in range(nchunk):
            gg = jnp.dot(x, wgu_ref[0, :, c * cols:(c + 1) * cols], preferred_element_type=F32)
            uu = jnp.dot(x, wgu_ref[0, :, dff + c * cols:dff + (c + 1) * cols], preferred_element_type=F32)
            a = (_silu(gg) * uu).astype(BF16)
            acc = acc + jnp.dot(a, wd_ref[0, c * cols:(c + 1) * cols, :], preferred_element_type=F32)
        ys_ref[...] = acc.astype(BF16)


def _moe_ffn(tile_blk, tile_exp, ntiles, xs, w_gu, w_down, ebase):
    ncap, d = xs.shape
    dff = w_down.shape[1]
    kern = functools.partial(_moe_ffn_kernel, dff=dff, nchunk=2)
    ys_init = jnp.zeros((ncap, d), BF16)
    return pl.pallas_call(
        kern,
        out_shape=jax.ShapeDtypeStruct((ncap, d), BF16),
        grid_spec=pltpu.PrefetchScalarGridSpec(
            num_scalar_prefetch=3,
            grid=(ncap // MOE_RT,),
            in_specs=[pl.BlockSpec((MOE_RT, d), lambda k, blk, exp, nt: (blk[k], 0)),
                      pl.BlockSpec((1, d, 2 * dff), lambda k, blk, exp, nt: (ebase + exp[k], 0, 0)),
                      pl.BlockSpec((1, dff, d), lambda k, blk, exp, nt: (ebase + exp[k], 0, 0)),
                      pl.BlockSpec(memory_space=pl.ANY)],
            out_specs=pl.BlockSpec((MOE_RT, d), lambda k, blk, exp, nt: (blk[k], 0)),
        ),
        input_output_aliases={6: 0},
        compiler_params=_cparams("arbitrary"),
        name="moe_ffn",
    )(tile_blk, tile_exp, ntiles, xs, w_gu, w_down, ys_init)


def _moe_combine_kernel(start_ref, cnt_ref, x_ref, gate_ref, pos_ref, gts_ref, ys_ref, o_ref,
                        buf, sems, acc_scr):
    b = pl.program_id(0)
    nb = pl.num_programs(0)
    ne = buf.shape[1]
    sb = x_ref.shape[0]
    slot = b % 2

    def in_copy(sl, e, a):
        return pltpu.make_async_copy(ys_ref.at[pl.ds(a, MOE_WIN)], buf.at[sl, e], sems.at[sl, e])

    def start_first_windows(blk, sl):
        for e in range(ne):
            @pl.when(cnt_ref[blk, e] > 0)
            def _():
                _, _, a = _window(start_ref[blk, e], cnt_ref[blk, e], 0)
                in_copy(sl, e, a).start()

    @pl.when(b == 0)
    def _():
        start_first_windows(0, 0)

    @pl.when(b + 1 < nb)
    def _():
        start_first_windows(b + 1, 1 - slot)

    acc_scr[...] = jnp.zeros(acc_scr.shape, F32)
    liota = lax.broadcasted_iota(I32, (sb, MOE_WIN), 1)
    for e in range(ne):
        poscol = pos_ref[:, e:e + 1]
        gcol = gts_ref[:, e:e + 1]
        nwin = (cnt_ref[b, e] + MOE_CAP - 1) // MOE_CAP

        def wbody(w, carry_unused, e=e, poscol=poscol, gcol=gcol):
            s, n, a = _window(start_ref[b, e], cnt_ref[b, e], w)

            @pl.when(w > 0)
            def _():
                in_copy(slot, e, a).start()

            in_copy(slot, e, a).wait()
            hit = jnp.logical_and(poscol - a == liota,
                                  jnp.logical_and(poscol >= s, poscol < s + n))
            onehot = jnp.where(hit, 1.0, 0.0).astype(BF16)
            acc_scr[...] += gcol * jnp.dot(onehot, buf[slot, e], preferred_element_type=F32)
            return carry_unused

        lax.fori_loop(0, nwin, wbody, 0)

    o_ref[...] = x_ref[...] + gate_ref[0] * acc_scr[...]


def _moe_combine(start, cnt, xt, gate, pos_n, gates_n, ys, tpb):
    n, d = xt.shape
    nb, ne = cnt.shape
    sb = n // nb
    return pl.pallas_call(
        _moe_combine_kernel,
        out_shape=jax.ShapeDtypeStruct((n, d), F32),
        grid_spec=pltpu.PrefetchScalarGridSpec(
            num_scalar_prefetch=2,
            grid=(nb,),
            in_specs=[pl.BlockSpec((sb, d), lambda b, *_: (b, 0)),
                      pl.BlockSpec((1, 1, d), lambda b, *_: (b // tpb, 0, 0)),
                      pl.BlockSpec((sb, ne), lambda b, *_: (b, 0)),
                      pl.BlockSpec((sb, ne), lambda b, *_: (b, 0)),
                      pl.BlockSpec(memory_space=pl.ANY)],
            out_specs=pl.BlockSpec((sb, d), lambda b, *_: (b, 0)),
            scratch_shapes=[pltpu.VMEM((2, ne, MOE_WIN, d), BF16),
                            pltpu.SemaphoreType.DMA((2, ne)),
                            pltpu.VMEM((sb, d), F32)],
        ),
        compiler_params=_cparams("arbitrary"),
        name="moe_combine",
    )(start, cnt, xt, gate, pos_n, gates_n, ys)


def _moe_layer(x, g, sh, sc, gate, router_w, router_b, w_gu, w_down, ebase=0):
    bsz, seq, d = x.shape
    ne = router_w.shape[1]
    n = bsz * seq
    sb = min(MOE_SB, seq)
    tpb = seq // sb
    nb = n // sb
    rt = MOE_RT
    xt = x.reshape(n, d)
    rw_pad = jnp.pad(router_w.astype(F32), ((0, 0), (0, LANES - ne)))
    rb_pad = jnp.pad(router_b.astype(F32).reshape(1, ne), ((0, 0), (0, LANES - ne)), constant_values=NEG_BIG)
    h, meta = _moe_router(xt, g, sh, sc, rw_pad, rb_pad, tpb, sb)
    gates = meta[:, :ne]
    sel = (meta[:, ne:2 * ne] > 0.5).astype(I32)

    selb = sel.reshape(nb, sb, ne)
    cnt = jnp.sum(selb, axis=1)
    rank = meta[:, 2 * ne:3 * ne].astype(I32).reshape(nb, sb, ne)
    total = jnp.sum(cnt, axis=0)
    region = ((total + MOE_WIN + rt - 1) // rt) * rt
    off = jnp.cumsum(region) - region
    start = (off[None, :] + jnp.cumsum(cnt, axis=0) - cnt).astype(I32)
    pos = jnp.where(selb > 0, start[:, None, :] + rank, -1).astype(I32)
    ncap = 2 * n + ne * (MOE_WIN + rt)
    tiles_e = (total + rt - 1) // rt
    tcum = jnp.cumsum(tiles_e)
    ntiles = tcum[-1]
    kk = jnp.minimum(jnp.arange(ncap // rt), ntiles - 1)
    tile_exp = jnp.sum((kk[:, None] >= tcum[None, :]).astype(I32), axis=1)
    tile_blk = (off[tile_exp] // rt + kk - (tcum - tiles_e)[tile_exp]).astype(I32)

    xs = _moe_dispatch(start, cnt.astype(I32), h, jnp.swapaxes(pos, 1, 2), ncap)
    ys = _moe_ffn(tile_blk, tile_exp, ntiles.reshape(1).astype(I32), xs, w_gu, w_down, ebase)
    out = _moe_combine(start, cnt.astype(I32), xt, gate, pos.reshape(n, ne), gates, ys, tpb)
    return out.reshape(bsz, seq, d)


def _head_norm(q, hsum_ref, hexp_ref, gain, scale):
    ms = jnp.dot((q * q).astype(BF16), hsum_ref[...], preferred_element_type=F32)
    r = lax.rsqrt(ms + EPS)
    r_hi = r.astype(BF16)
    r_lo = (r - r_hi.astype(F32)).astype(BF16)
    rexp = (jnp.dot(r_hi, hexp_ref[...], preferred_element_type=F32)
            + jnp.dot(r_lo, hexp_ref[...], preferred_element_type=F32))
    return q * rexp * (gain * scale)


def _qkv_kernel(x_ref, g_ref, sh_ref, sc_ref, wm_ref, ws_ref, qg_ref, kg_ref,
                hsq_ref, heq_ref, hsk_ref, hek_ref,
                q_ref, k_ref, v_ref, qi_ref, kiwi_ref, *, dq, dk, dqi):
    x = x_ref[0]
    h = _norm_mod(x, g_ref[...], sh_ref[0], sc_ref[0]).astype(BF16)
    z = jnp.dot(h, wm_ref[...], preferred_element_type=F32)
    q = z[:, :dq]
    k = z[:, dq:dq + dk]
    v = z[:, dq + dk:dq + 2 * dk]
    qi = z[:, dq + 2 * dk:dq + 2 * dk + dqi]
    q_ref[0] = _head_norm(q, hsq_ref, heq_ref, qg_ref[...], HEAD_DIM ** -0.5 * LOG2E).astype(BF16)
    k_ref[0] = _head_norm(k, hsk_ref, hek_ref, kg_ref[...], 1.0).astype(BF16)
    v_ref[0] = v.astype(BF16)
    qi_ref[0] = qi.astype(BF16)
    kiwi_ref[0] = jnp.dot(h, ws_ref[...], preferred_element_type=F32)


def _head_indicators(nheads):
    hs = np.zeros((nheads * HEAD_DIM, LANES), np.float32)
    he = np.zeros((LANES, nheads * HEAD_DIM), np.float32)
    for hd in range(nheads):
        hs[hd * HEAD_DIM:(hd + 1) * HEAD_DIM, hd] = 1.0 / HEAD_DIM
        he[hd, hd * HEAD_DIM:(hd + 1) * HEAD_DIM] = 1.0
    return jnp.asarray(hs, BF16), jnp.asarray(he, BF16)


def _qkv_proj(x, g, sh, sc, w_main, w_small, q_gain, k_gain, tm=512):
    bsz, seq, d = x.shape
    tm = min(tm, seq)
    dq = N_HEADS * HEAD_DIM
    dk = N_KV_HEADS * HEAD_DIM
    dqi = IDX_HEADS * IDX_DIM
    hsq, heq = _head_indicators(N_HEADS)
    hsk, hek = _head_indicators(N_KV_HEADS)
    qg = jnp.tile(q_gain.reshape(1, HEAD_DIM), (1, N_HEADS)).astype(F32)
    kg = jnp.tile(k_gain.reshape(1, HEAD_DIM), (1, N_KV_HEADS)).astype(F32)
    kern = functools.partial(_qkv_kernel, dq=dq, dk=dk, dqi=dqi)
    mod_spec = pl.BlockSpec((1, 1, d), lambda b, i: (b, 0, 0))

    def full(a):
        return pl.BlockSpec(a.shape, lambda b, i: (0,) * a.ndim)

    def out(n):
        return pl.BlockSpec((1, tm, n), lambda b, i: (b, i, 0))

    return pl.pallas_call(
        kern,
        out_shape=[
            jax.ShapeDtypeStruct((bsz, seq, dq), BF16),
            jax.ShapeDtypeStruct((bsz, seq, dk), BF16),
            jax.ShapeDtypeStruct((bsz, seq, dk), BF16),
            jax.ShapeDtypeStruct((bsz, seq, dqi), BF16),
            jax.ShapeDtypeStruct((bsz, seq, LANES), F32),
        ],
        grid=(bsz, seq // tm),
        in_specs=[
            pl.BlockSpec((1, tm, d), lambda b, i: (b, i, 0)),
            pl.BlockSpec((1, d), lambda b, i: (0, 0)),
            mod_spec, mod_spec,
            full(w_main), full(w_small), full(qg), full(kg),
            full(hsq), full(heq), full(hsk), full(hek),
        ],
        out_specs=[out(dq), out(dk), out(dk), out(dqi), out(LANES)],
        compiler_params=_cparams("parallel", "parallel"),
        name="attn_qkv",
    )(x, g, sh, sc, w_main, w_small, qg, kg, hsq, heq, hsk, hek)


def _rel_bucket_np(dist):
    max_exact = REL_BUCKETS // 2
    d = np.maximum(dist, 1).astype(np.float64)
    large = max_exact + (np.log(d / max_exact) / math.log(REL_MAX_DIST / max_exact)
                         * (REL_BUCKETS - max_exact)).astype(np.int32)
    large = np.minimum(large, REL_BUCKETS - 1)
    return np.where(dist < max_exact, dist, large).astype(np.int32)


def _bias_table_kernel(bucket_ref, rb_ref, o_ref):
    hd = pl.program_id(0)
    bucket = bucket_ref[...]
    acc = jnp.zeros(bucket.shape, F32)
    for b in range(REL_BUCKETS):
        acc = jnp.where(bucket == b, rb_ref[b, hd] * LOG2E, acc)
    o_ref[0] = acc


def _bias_table(rel_bias):
    w = np.arange(BIAS_W)[:, None]
    i = np.arange(ATT_TQ)[None, :]
    bucket = jnp.asarray(_rel_bucket_np(np.maximum(i - w + BIAS_C, 0)))
    return pl.pallas_call(
        _bias_table_kernel,
        out_shape=jax.ShapeDtypeStruct((N_HEADS, BIAS_W, ATT_TQ), F32),
        grid=(N_HEADS,),
        in_specs=[
            pl.BlockSpec((BIAS_W, ATT_TQ), lambda hd: (0, 0)),
            pl.BlockSpec(memory_space=pltpu.SMEM),
        ],
        out_specs=pl.BlockSpec((1, BIAS_W, ATT_TQ), lambda hd: (hd, 0, 0)),
        compiler_params=_cparams("arbitrary"),
        name="attn_bias_table",
    )(bucket, rel_bias.astype(F32))


def _attn_kernel(qT_ref, qiT_ref, wiT_ref, k_ref, vT_ref, ki_ref, pt_ref, o_ref,
                 keys_scr, negm_scr, pidx_scr, oT_scr, acc_scr, qall_scr, sa_scr, sb_scr, *, top_k):
    tq, kc = ATT_TQ, ATT_KC
    qt = pl.program_id(1)
    q0 = qt * tq
    nch = (q0 + tq + kc - 1) // kc
    tpos = q0 + lax.broadcasted_iota(I32, (kc, tq), 1)
    srow = lax.broadcasted_iota(I32, (kc, tq), 0)

    qiT = qiT_ref[0]
    qi_all = jnp.concatenate([qiT[hd * IDX_DIM:(hd + 1) * IDX_DIM, :] for hd in range(IDX_HEADS)], axis=1)
    wiT = wiT_ref[0]

    def score_chunk(c):
        ks = pl.multiple_of(c * kc, kc)
        kic = ki_ref[0, pl.ds(ks, kc), :]
        dots = jnp.dot(kic, qi_all, preferred_element_type=F32)
        acc = jnp.zeros((kc, tq), F32)
        for hd in range(IDX_HEADS):
            acc = acc + jnp.maximum(dots[:, hd * tq:(hd + 1) * tq], 0.0) * wiT[hd:hd + 1, :]
        acc = jnp.where(acc == 0.0, 0.0, acc)
        bits = pltpu.bitcast(acc, I32)
        key = jnp.where(bits < 0, bits ^ INT_MAX, bits)
        key = jnp.where(ks + srow <= tpos, key, INT_MIN)
        keys_scr[pl.ds(ks, kc), :] = key

    npair = (nch + 1) // 2

    def score_pair(i, carry):
        score_chunk(2 * i)
        score_chunk(2 * i + 1)
        return carry

    lax.fori_loop(0, npair, score_pair, 0)

    @pl.when(nch % 2 == 1)
    def _():
        negm_scr[pl.ds(pl.multiple_of(nch * kc, kc), kc), :] = jnp.full((kc, tq), NEG_BIG, F32)

    srow2 = lax.broadcasted_iota(I32, (2 * kc, tq), 0)

    def count(pred):
        def body(c, acc):
            ks = pl.multiple_of(c * 2 * kc, 2 * kc)
            m = pred(keys_scr[pl.ds(ks, 2 * kc), :], ks + srow2).astype(I32)
            return acc + jnp.sum(m.reshape(2 * kc // SUBLANES, SUBLANES, tq), axis=0)
        acc = lax.fori_loop(0, npair, body, jnp.zeros((SUBLANES, tq), I32))
        return jnp.sum(acc, axis=0, keepdims=True)

    def bit_body(it, p):
        cand_p = p | lax.shift_left(jnp.int32(1), 31 - it)
        cand = cand_p ^ INT_MIN
        cnt = count(lambda k, s: k >= cand)
        return jnp.where(cnt >= top_k, cand_p, p)

    p_fin = lax.fori_loop(0, 32, bit_body, jnp.zeros((1, tq), I32))
    v = p_fin ^ INT_MIN

    cnt_gt = count(lambda k, s: k > v)
    cnt_eq = count(lambda k, s: k == v)
    need = top_k - cnt_gt
    pidx_scr[...] = jnp.full((1, tq), INT_MAX, I32)
    pos_bits = (keys_scr.shape[0] - 1).bit_length()

    @pl.when(jnp.max(cnt_eq - need) > 0)
    def _():
        def ibit(it, p):
            cand = p | lax.shift_left(jnp.int32(1), pos_bits - 1 - it)
            cnt = count(lambda k, s: jnp.logical_and(k == v, s < cand))
            return jnp.where(cnt < need, cand, p)
        pidx_scr[...] = lax.fori_loop(0, pos_bits, ibit, jnp.zeros((1, tq), I32))

    pidx = pidx_scr[...]

    def mask_chunk(c, carry):
        ks = pl.multiple_of(c * kc, kc)
        k = keys_scr[pl.ds(ks, kc), :]
        spos = ks + srow
        sel = jnp.logical_or(k > v, jnp.logical_and(k == v, spos <= pidx))
        sel = jnp.logical_and(sel, spos <= tpos)
        negm_scr[pl.ds(ks, kc), :] = jnp.where(sel, 0.0, NEG_BIG)
        return carry

    lax.fori_loop(0, nch, mask_chunk, 0)

    acc_scr[...] = jnp.zeros(acc_scr.shape, F32)
    for n in range(N_KV_HEADS):
        r0 = n * N_REP * HEAD_DIM
        qall_scr[n] = jnp.concatenate(
            [qT_ref[0, r0 + g * HEAD_DIM:r0 + (g + 1) * HEAD_DIM, :] for g in range(N_REP)], axis=1)

    def qk_chunk(c, s_ref):
        ks = pl.multiple_of(c * kc, kc)
        for n in range(N_KV_HEADS):
            s_ref[n] = jnp.dot(k_ref[0, n, pl.ds(ks, kc), :], qall_scr[n], preferred_element_type=F32)

    def softmax_pv(c, s_ref, ms, far):
        ks = pl.multiple_of(c * kc, kc)
        negm = negm_scr[pl.ds(ks, kc), :]
        w0 = pl.multiple_of(jnp.clip(BIAS_C - (q0 - ks), 0, BIAS_C), LANES)
        new_ms = []
        for n in range(N_KV_HEADS):
            s = s_ref[n]
            if far:
                cvec = jnp.concatenate([pt_ref[n * N_REP + g, 0:1, :] for g in range(N_REP)], axis=1)
                lg = jnp.concatenate([s[:, g * tq:(g + 1) * tq] + negm for g in range(N_REP)], axis=1)
                m_new = jnp.maximum(ms[n], jnp.max(lg, axis=0, keepdims=True) + cvec)
                p = jnp.exp2(lg - (m_new - cvec))
            else:
                lg = jnp.concatenate(
                    [s[:, g * tq:(g + 1) * tq] + pt_ref[n * N_REP + g, pl.ds(w0, kc), :] + negm
                     for g in range(N_REP)], axis=1)
                m_new = jnp.maximum(ms[n], jnp.max(lg, axis=0, keepdims=True))
                p = jnp.exp2(lg - m_new)
            alpha = jnp.exp2(ms[n] - m_new)
            acc_scr[n] = alpha * acc_scr[n] + jnp.dot(vT_ref[0, c, n], p.astype(BF16),
                                                      preferred_element_type=F32)
            new_ms.append(m_new)
        return tuple(new_ms)

    last_chunk = k_ref.shape[2] // kc - 1

    def pair_step(i, ms, far):
        c0 = 2 * i
        qk_chunk(c0 + 1, sb_scr)
        ms = softmax_pv(c0, sa_scr, ms, far)
        qk_chunk(jnp.minimum(c0 + 2, last_chunk), sa_scr)
        return softmax_pv(c0 + 1, sb_scr, ms, far)

    n_far = jnp.clip((q0 - BIAS_C + kc) // kc, 0, nch)
    ms = tuple(jnp.full((1, N_REP * tq), NEG_BIG, F32) for _ in range(N_KV_HEADS))
    qk_chunk(0, sa_scr)
    ms = lax.fori_loop(0, n_far // 2, functools.partial(pair_step, far=True), ms)
    lax.fori_loop(n_far // 2, npair, functools.partial(pair_step, far=False), ms)
    for n in range(N_KV_HEADS):
        o_t = acc_scr[n, 0:HEAD_DIM, :] / acc_scr[n, HEAD_DIM:HEAD_DIM + 1, :]
        for g in range(N_REP):
            r0 = (n * N_REP + g) * HEAD_DIM
            oT_scr[r0:r0 + HEAD_DIM, :] = o_t[:, g * tq:(g + 1) * tq]

    o_ref[0] = oT_scr[...].T.astype(BF16)


def _attention(qT, qiT, wiT, k4, vT, ki, ptab, top_k):
    bsz, dq, seq = qT.shape
    tq = ATT_TQ
    assert seq % (2 * ATT_KC) == 0
    kern = functools.partial(_attn_kernel, top_k=top_k)
    return pl.pallas_call(
        kern,
        out_shape=jax.ShapeDtypeStruct((bsz, seq, dq), BF16),
        grid=(bsz, seq // tq),
        in_specs=[
            pl.BlockSpec((1, dq, tq), lambda b, i: (b, 0, i)),
            pl.BlockSpec((1, qiT.shape[1], tq), lambda b, i: (b, 0, i)),
            pl.BlockSpec((1, wiT.shape[1], tq), lambda b, i: (b, 0, i)),
            pl.BlockSpec((1,) + k4.shape[1:], lambda b, i: (b, 0, 0, 0)),
            pl.BlockSpec((1,) + vT.shape[1:], lambda b, i: (b, 0, 0, 0, 0)),
            pl.BlockSpec((1,) + ki.shape[1:], lambda b, i: (b, 0, 0)),
            pl.BlockSpec(ptab.shape, lambda b, i: (0, 0, 0)),
        ],
        out_specs=pl.BlockSpec((1, tq, dq), lambda b, i: (b, i, 0)),
        scratch_shapes=[
            pltpu.VMEM((seq, tq), I32),
            pltpu.VMEM((seq, tq), F32),
            pltpu.VMEM((1, tq), I32),
            pltpu.VMEM((dq, tq), F32),
            pltpu.VMEM((N_KV_HEADS, ATT_VROWS, N_REP * tq), F32),
            pltpu.VMEM((N_KV_HEADS, HEAD_DIM, N_REP * tq), BF16),
            pltpu.VMEM((N_KV_HEADS, ATT_KC, N_REP * tq), F32),
            pltpu.VMEM((N_KV_HEADS, ATT_KC, N_REP * tq), F32),
        ],
        compiler_params=_cparams("parallel", "arbitrary"),
        name="attn_core",
    )(qT, qiT, wiT, k4, vT, ki, ptab)


def _proj_res_kernel(a_ref, x_ref, gate_ref, w_ref, o_ref):
    y = jnp.dot(a_ref[0], w_ref[...], preferred_element_type=F32)
    o_ref[0] = x_ref[0] + gate_ref[0] * y


def _proj_residual(a, x, gate, w, tm=512):
    bsz, seq, d = x.shape
    tm = min(tm, seq)
    return pl.pallas_call(
        _proj_res_kernel,
        out_shape=jax.ShapeDtypeStruct((bsz, seq, d), F32),
        grid=(bsz, seq // tm),
        in_specs=[
            pl.BlockSpec((1, tm, a.shape[2]), lambda b, i: (b, i, 0)),
            pl.BlockSpec((1, tm, d), lambda b, i: (b, i, 0)),
            pl.BlockSpec((1, 1, d), lambda b, i: (b, 0, 0)),
            pl.BlockSpec(w.shape, lambda b, i: (0, 0)),
        ],
        out_specs=pl.BlockSpec((1, tm, d), lambda b, i: (b, i, 0)),
        compiler_params=_cparams("parallel", "parallel"),
        name="proj_residual",
    )(a, x, gate, w)


def _attn_layer(x, g, sh, sc, gate, w_in, q_gain, k_gain, w_out, rel_bias):
    bsz, seq, d = x.shape
    top_k = min(TOPK_MAX, seq // 4)
    dq = N_HEADS * HEAD_DIM
    dk = N_KV_HEADS * HEAD_DIM
    dqi = IDX_HEADS * IDX_DIM
    nmain = dq + 2 * dk + dqi
    w_main = w_in[:, :nmain].astype(BF16)
    w_small = jnp.pad(w_in[:, nmain:], ((0, 0), (0, LANES - (IDX_DIM + IDX_HEADS)))).astype(BF16)
    q, k, v, qi, kiwi = _qkv_proj(x, g, sh, sc, w_main, w_small, q_gain, k_gain)
    ki = kiwi[:, :, :IDX_DIM].astype(BF16)
    wi = kiwi[:, :, IDX_DIM:IDX_DIM + IDX_HEADS] * (IDX_HEADS ** -0.5 * IDX_DIM ** -0.5)
    qT = jnp.swapaxes(q, 1, 2)
    qiT = jnp.swapaxes(qi, 1, 2)
    wiT = jnp.swapaxes(wi, 1, 2)
    nck = seq // ATT_KC
    vT = jnp.swapaxes(v.reshape(bsz, nck, ATT_KC, dk), 2, 3).reshape(bsz, nck, N_KV_HEADS, HEAD_DIM, ATT_KC)
    vT = jnp.concatenate([
        vT, jnp.ones((bsz, nck, N_KV_HEADS, 1, ATT_KC), BF16),
        jnp.zeros((bsz, nck, N_KV_HEADS, ATT_VROWS - HEAD_DIM - 1, ATT_KC), BF16)], axis=3)
    k4 = jnp.swapaxes(k.reshape(bsz, seq, N_KV_HEADS, HEAD_DIM), 1, 2)
    ptab = _bias_table(rel_bias)
    attn = _attention(qT, qiT, wiT, k4, vT, ki, ptab, top_k)
    return _proj_residual(attn, x, gate, w_out.astype(BF16))


def _s5_prep_kernel(lre_ref, lim_ref, ls_ref, bre_ref, bim_ref, cre_ref, cim_ref,
                    bcw_ref, mtw_ref, ccw_ref, are_ref, aim_ref):
    t_len = SSM_CHUNK
    lre = jnp.minimum(lre_ref[0], -1e-4)
    lim = lim_ref[0]
    step = jnp.exp(ls_ref[0])
    ar = lre * step
    ai = lim * step

    def powers(jv):
        mag = jnp.exp(jv * ar)
        return mag * jnp.cos(jv * ai), mag * jnp.sin(jv * ai)

    lb_re, lb_im = powers(1.0)
    nr = lb_re - 1.0
    ni = lb_im
    den = lre * lre + lim * lim
    cf_re = (nr * lre + ni * lim) / den
    cf_im = (ni * lre - nr * lim) / den
    bre = bre_ref[0]
    bim = bim_ref[0]
    bb_re = cf_re * bre - cf_im * bim
    bb_im = cf_re * bim + cf_im * bre
    cre = cre_ref[0]
    cim = cim_ref[0]
    nst = lre.shape[-1]
    jv = lax.broadcasted_iota(I32, (t_len, 1, nst), 0).astype(F32)
    pj_re, pj_im = powers(jv)
    a_re = (cre[None] * pj_re - cim[None] * pj_im).reshape(t_len * SSM_GROUP, nst)
    a_im = (cre[None] * pj_im + cim[None] * pj_re).reshape(t_len * SSM_GROUP, nst)
    dn = (((1,), (1,)), ((), ()))
    cg = SSM_GROUP
    tc = t_len * cg
    width = t_len * LANES
    gq = pl.program_id(0) % (LANES // cg)
    nh = (LANES // cg) * nst
    cg_shift = cg.bit_length() - 1

    def place(nrows, target):
        r = lax.broadcasted_iota(I32, (nrows, width), 0)
        col = lax.broadcasted_iota(I32, (nrows, width), 1)
        return jnp.where(col == target(r), 1.0, 0.0).astype(BF16)

    pm = place(tc, lambda r: lax.shift_right_logical(r, cg_shift) * LANES + gq * cg + (r & (cg - 1)))
    k_t = (lax.dot_general(bb_re, a_re, dn, preferred_element_type=F32, precision=HIGHEST)
           - lax.dot_general(bb_im, a_im, dn, preferred_element_type=F32, precision=HIGHEST))
    lane = lax.broadcasted_iota(I32, (cg, tc), 1)
    mt_rows = [k_t] + [jnp.where(lane >= s * cg, pltpu.roll(k_t, s * cg, 1), 0.0) for s in range(1, t_len)]
    mt_t = jnp.concatenate(mt_rows, axis=0)
    mtw_ref[0] = jnp.dot(mt_t.astype(BF16), pm, preferred_element_type=F32).astype(BF16)
    pr_re, pr_im = powers((t_len - 1.0) - jv)
    bc_re = (bb_re[None] * pr_re - bb_im[None] * pr_im).reshape(tc, nst)
    bc_im = (bb_re[None] * pr_im + bb_im[None] * pr_re).reshape(tc, nst)
    pb_re = place(nst, lambda r: gq * nst + r)
    pb_im = place(nst, lambda r: nh + gq * nst + r)
    bcw_ref[0] = (jnp.dot(bc_re.astype(BF16), pb_re, preferred_element_type=F32)
                  + jnp.dot(bc_im.astype(BF16), pb_im, preferred_element_type=F32)).astype(BF16)
    pn_re, pn_im = powers(jv + 1.0)
    cc_re = (cre[None] * pn_re - cim[None] * pn_im).reshape(tc, nst)
    cc_im = (cre[None] * pn_im + cim[None] * pn_re).reshape(tc, nst)
    eye = jnp.where(lax.broadcasted_iota(I32, (nst, nst), 0) == lax.broadcasted_iota(I32, (nst, nst), 1),
                    1.0, 0.0)
    cct_re = lax.dot_general(eye, cc_re, dn, preferred_element_type=F32, precision=HIGHEST)
    cct_im = lax.dot_general(eye, cc_im, dn, preferred_element_type=F32, precision=HIGHEST)
    ccw_ref[0, 0:nst, :] = jnp.dot(cct_re.astype(BF16), pm, preferred_element_type=F32).astype(BF16)
    ccw_ref[0, nst:2 * nst, :] = (-jnp.dot(cct_im.astype(BF16), pm, preferred_element_type=F32)).astype(BF16)
    at_re, at_im = powers(float(t_len))
    rs = lax.broadcasted_iota(I32, (nst, nh), 0)
    cs = lax.broadcasted_iota(I32, (nst, nh), 1)
    pa = jnp.where(cs == gq * nst + rs, 1.0, 0.0)
    are_ref[0] = jnp.dot(at_re, pa, preferred_element_type=F32, precision=HIGHEST)
    aim_ref[0] = jnp.dot(at_im, pa, preferred_element_type=F32, precision=HIGHEST)


def _s5_prep(lam_re, lam_im, log_step, b_re, b_im, c_re, c_im):
    ng, nst = lam_re.shape
    tc = SSM_CHUNK * SSM_GROUP
    vec = pl.BlockSpec((1, 1, nst), lambda gi: (gi, 0, 0))
    mat = pl.BlockSpec((1, SSM_GROUP, nst), lambda gi: (gi, 0, 0))
    width = SSM_CHUNK * LANES
    assert tc == 2 * nst and 2 * (LANES // SSM_GROUP) * nst == width
    big = pl.BlockSpec((1, tc, width), lambda gi: (gi, 0, 0))
    wide = jax.ShapeDtypeStruct((ng, tc, width), BF16)
    return pl.pallas_call(
        _s5_prep_kernel,
        out_shape=[wide, wide, wide,
                   jax.ShapeDtypeStruct((ng, 1, width // 2), F32),
                   jax.ShapeDtypeStruct((ng, 1, width // 2), F32)],
        grid=(ng,),
        in_specs=[vec, vec, pl.BlockSpec((1, 1, 1), lambda gi: (gi, 0, 0)), mat, mat, mat, mat],
        out_specs=[big, big, big, pl.BlockSpec((1, 1, width // 2), lambda gi: (gi, 0, 0)),
                   pl.BlockSpec((1, 1, width // 2), lambda gi: (gi, 0, 0))],
        compiler_params=_cparams("parallel"),
        name="s5_prep",
    )(lam_re.reshape(ng, 1, nst), lam_im.reshape(ng, 1, nst), log_step.reshape(ng, 1, 1),
      jnp.swapaxes(b_re, 1, 2), jnp.swapaxes(b_im, 1, 2), c_re, c_im)


def _s5_pre_kernel(x_ref, g_ref, sh_ref, sc_ref, o_ref, h_scr):
    t_len = SSM_CHUNK
    h = _norm_mod(x_ref[0], g_ref[...], sh_ref[0], sc_ref[0])
    nj = o_ref.shape[1]
    for q in range(o_ref.shape[0]):
        h_scr[q] = h[:, q * LANES:(q + 1) * LANES]
        for t in range(t_len):
            o_ref[q, :, t * LANES:(t + 1) * LANES] = h_scr[q, pl.ds(t, nj, stride=t_len), :].astype(BF16)


def _s5_pre(x, g, sh, sc, tm=512):
    bsz, seq, d = x.shape
    tm = min(tm, seq)
    nt = seq // tm
    nq = d // LANES
    t_len = SSM_CHUNK
    mod_spec = pl.BlockSpec((1, 1, d), lambda b, i: (b, 0, 0))
    return pl.pallas_call(
        _s5_pre_kernel,
        out_shape=jax.ShapeDtypeStruct((nq, bsz * seq // t_len, t_len * LANES), BF16),
        grid=(bsz, nt),
        in_specs=[pl.BlockSpec((1, tm, d), lambda b, i: (b, i, 0)),
                  pl.BlockSpec((1, d), lambda b, i: (0, 0)), mod_spec, mod_spec],
        out_specs=pl.BlockSpec((nq, tm // t_len, t_len * LANES), lambda b, i: (0, b * nt + i, 0)),
        scratch_shapes=[pltpu.VMEM((nq, tm, LANES), F32)],
        compiler_params=_cparams("parallel", "parallel"),
        name="s5_pre",
    )(x, g, sh, sc)


def _s5_scan_kernel(x_ref, bc_ref, mt_ref, cc_ref, are_ref, aim_ref, y_ref, re_scr, im_scr):
    x = x_ref[0]
    nrow = x.shape[0]
    pad = re_scr.shape[0] - nrow
    v = jnp.dot(x, bc_ref[0], preferred_element_type=F32)
    nh = v.shape[1] // 2
    s_re = v[:, :nh]
    s_im = v[:, nh:]
    a_re = are_ref[0]
    a_im = aim_ref[0]
    re_scr[0:pad, :] = jnp.zeros((pad, nh), F32)
    im_scr[0:pad, :] = jnp.zeros((pad, nh), F32)

    def shifted(scr, val, dist):
        scr[pad:pad + nrow, :] = val
        return scr[pad - dist:pad - dist + nrow, :]

    dist = 1
    while dist < nrow:
        sh_re = shifted(re_scr, s_re, dist)
        sh_im = shifted(im_scr, s_im, dist)
        s_re, s_im = (s_re + a_re * sh_re - a_im * sh_im, s_im + a_re * sh_im + a_im * sh_re)
        a_re, a_im = (a_re * a_re - a_im * a_im, 2.0 * a_re * a_im)
        dist *= 2
    sp = jnp.concatenate([shifted(re_scr, s_re, 1), shifted(im_scr, s_im, 1)], axis=1).astype(BF16)
    y_ref[0] = (jnp.dot(x, mt_ref[0], preferred_element_type=F32)
                + jnp.dot(sp, cc_ref[0], preferred_element_type=F32))


def _s5_scan(hq, bcq, mtq, ccq, a_re, a_im, bsz):
    nq, nrows, width = hq.shape
    nj = nrows // bsz
    nh = a_re.shape[-1]
    pad = max(nj // 2, SUBLANES)
    tile = pl.BlockSpec((1, nj, width), lambda q, b: (q, b, 0))
    wspec = lambda a: pl.BlockSpec((1,) + a.shape[1:], lambda q, b: (q, 0, 0))
    return pl.pallas_call(
        _s5_scan_kernel,
        out_shape=jax.ShapeDtypeStruct((nq, nrows, width), F32),
        grid=(nq, bsz),
        in_specs=[tile, wspec(bcq), wspec(mtq), wspec(ccq), wspec(a_re), wspec(a_im)],
        out_specs=tile,
        scratch_shapes=[pltpu.VMEM((pad + nj, nh), F32), pltpu.VMEM((pad + nj, nh), F32)],
        compiler_params=_cparams("parallel", "parallel"),
        name="s5_scan",
    )(hq, bcq, mtq, ccq, a_re, a_im)


def _s5_post_kernel(x_ref, y_ref, g_ref, sh_ref, sc_ref, gate_ref, dsk_ref, w_ref, o_ref, y_scr, *, d):
    t_len = SSM_CHUNK
    x = x_ref[0]
    h = _norm_mod(x, g_ref[...], sh_ref[0], sc_ref[0])
    nj = y_ref.shape[1]
    for q in range(y_ref.shape[0]):
        for t in range(t_len):
            y_scr[q, pl.ds(t, nj, stride=t_len), :] = y_ref[q, :, t * LANES:(t + 1) * LANES]
    y = jnp.concatenate([y_scr[q] for q in range(y_ref.shape[0])], axis=1)
    yy = y + dsk_ref[...] * h
    gl = jax.nn.gelu(yy).astype(BF16)
    z = jnp.dot(gl, w_ref[...], preferred_element_type=F32)
    o_ref[0] = x + gate_ref[0] * (z[:, :d] * jax.nn.sigmoid(z[:, d:]))


def _s5_post(x, yq, g, sh, sc, gate, d_skip, w_glu, tm=512):
    bsz, seq, d = x.shape
    tm = min(tm, seq)
    nt = seq // tm
    nq = d // LANES
    kern = functools.partial(_s5_post_kernel, d=d)
    mod_spec = pl.BlockSpec((1, 1, d), lambda b, i: (b, 0, 0))
    tile = pl.BlockSpec((1, tm, d), lambda b, i: (b, i, 0))
    return pl.pallas_call(
        kern,
        out_shape=jax.ShapeDtypeStruct((bsz, seq, d), F32),
        grid=(bsz, nt),
        in_specs=[tile,
                  pl.BlockSpec((nq, tm // SSM_CHUNK, SSM_CHUNK * LANES), lambda b, i: (0, b * nt + i, 0)),
                  pl.BlockSpec((1, d), lambda b, i: (0, 0)), mod_spec, mod_spec, mod_spec,
                  pl.BlockSpec((1, d), lambda b, i: (0, 0)),
                  pl.BlockSpec(w_glu.shape, lambda b, i: (0, 0))],
        out_specs=tile,
        scratch_shapes=[pltpu.VMEM((nq, tm, LANES), F32)],
        compiler_params=_cparams("parallel", "parallel"),
        name="s5_post",
    )(x, yq, g, sh, sc, gate, d_skip, w_glu)


def _s5_layer(x, g, sh, sc, gate, lam_re, lam_im, log_step, b_re, b_im, c_re, c_im, d_skip, w_glu):
    bsz, seq, d = x.shape
    ng, nst = lam_re.shape
    t_len, cg = SSM_CHUNK, SSM_GROUP
    nq = d // LANES
    gq = LANES // cg
    nj = seq // t_len
    bcw, mtw, ccw, a_re, a_im = _s5_prep(lam_re, lam_im, log_step, b_re, b_im, c_re, c_im)
    width = t_len * LANES
    bcq = jnp.swapaxes(bcw.reshape(nq, gq, t_len, cg, width), 1, 2).reshape(nq, width, width)
    mtq = jnp.swapaxes(mtw.reshape(nq, gq, t_len, cg, width), 1, 2).reshape(nq, width, width)
    ccq = jnp.swapaxes(ccw.reshape(nq, gq, 2, nst, width), 1, 2).reshape(nq, 2 * gq * nst, width)
    aq_re = jnp.sum(a_re.reshape(nq, gq, 1, gq * nst), axis=1)
    aq_im = jnp.sum(a_im.reshape(nq, gq, 1, gq * nst), axis=1)

    hq = _s5_pre(x, g, sh, sc)
    yq = _s5_scan(hq, bcq, mtq, ccq, aq_re, aq_im, bsz)
    return _s5_post(x, yq, g, sh, sc, gate, d_skip.reshape(1, d).astype(F32), w_glu.astype(BF16))


def kernel(x, c, ada_w, ada_b, norm_g, conv_w_in, conv_w, conv_w_out, attn_w_in, attn_q_gain, attn_k_gain, attn_w_out, rel_bias, ssm_lambda_re, ssm_lambda_im, ssm_log_step, ssm_b_re, ssm_b_im, ssm_c_re, ssm_c_im, ssm_d, ssm_w_glu, ffn_w_gu, ffn_w_down, moe_router_w, moe_router_b, moe_w_gu, moe_w_down):
    bsz, seq, d = x.shape
    depth = ada_w.shape[0]
    mod = _ada_mod(c, ada_w, ada_b).reshape(depth, bsz, 6, 1, d)
    moe_gu = moe_w_gu.astype(BF16).reshape((-1,) + moe_w_gu.shape[2:])
    moe_down = moe_w_down.astype(BF16).reshape((-1,) + moe_w_down.shape[2:])
    for i in range(depth):
        sh1, sc1, g1, sh2, sc2, g2 = (mod[i, :, r] for r in range(6))
        gn1 = norm_g[i, 0].reshape(1, d)
        gn2 = norm_g[i, 1].reshape(1, d)
        j = i // N_MIXERS
        if i % N_MIXERS == 0:
            x = _conv_layer(x, gn1, sh1, sc1, g1, conv_w_in[j].astype(BF16), conv_w[j],
                            conv_w_out[j].astype(BF16))
        elif i % N_MIXERS == 1:
            x = _attn_layer(x, gn1, sh1, sc1, g1, attn_w_in[j], attn_q_gain[j], attn_k_gain[j],
                            attn_w_out[j], rel_bias)
        else:
            x = _s5_layer(x, gn1, sh1, sc1, g1, ssm_lambda_re[j], ssm_lambda_im[j], ssm_log_step[j],
                          ssm_b_re[j], ssm_b_im[j], ssm_c_re[j], ssm_c_im[j], ssm_d[j], ssm_w_glu[j])
        if i % 2 == 0:
            x = _ffn_layer(x, gn2, sh2, sc2, g2, ffn_w_gu[i // 2].astype(BF16), ffn_w_down[i // 2].astype(BF16))
        else:
            x = _moe_layer(x, gn2, sh2, sc2, g2, moe_router_w[i // 2], moe_router_b[i // 2],
                           moe_gu, moe_down, ebase=(i // 2) * moe_w_gu.shape[1])
    return x
```

```python
import functools
import math

import numpy as np
import jax
import jax.numpy as jnp
from jax import lax
from jax.experimental import pallas as pl
from jax.experimental.pallas import tpu as pltpu

F32 = jnp.float32
BF16 = jnp.bfloat16
I32 = jnp.int32
HIGHEST = lax.Precision.HIGHEST

DEPTH = 4
N_MIXERS = 3
EPS = 1e-6
CONV_WIDTH = 3
N_HEADS = 16
N_KV_HEADS = 4
N_REP = N_HEADS // N_KV_HEADS
HEAD_DIM = 64
IDX_HEADS = 8
IDX_DIM = 64
TOPK_MAX = 256
REL_BUCKETS = 32
REL_MAX_DIST = 128
SSM_GROUP = 16
SSM_STATE = 64
N_EXPERTS = 8
TOP_K_EXPERTS = 2

VMEM_LIMIT_BYTES = 56 * 1024 * 1024
LANES = 128
SUBLANES = 8

INT_MIN = -(2 ** 31)
INT_MAX = 2 ** 31 - 1
NEG_BIG = -1e30
LOG2E = 1.4426950408889634

ATT_TQ = 128
ATT_KC = 256
BIAS_C = 384
BIAS_W = BIAS_C + ATT_KC
ATT_VROWS = 80

SSM_CHUNK = 8


def _cparams(*sem):
    return pltpu.CompilerParams(dimension_semantics=sem, vmem_limit_bytes=VMEM_LIMIT_BYTES)


def _norm_mod(x, g, shift, scale):
    ms = jnp.mean(x * x, axis=-1, keepdims=True)
    y = x * lax.rsqrt(ms + EPS)
    return (y * g) * (1.0 + scale) + shift


def _silu(x):
    return x * jax.nn.sigmoid(x)


def _ada_kernel(c_ref, w_ref, b_ref, o_ref):
    c = c_ref[...]
    cond = _silu(c)
    o_ref[0] = jnp.dot(cond, w_ref[0], preferred_element_type=F32, precision=HIGHEST) + b_ref[0]


def _ada_mod(c, ada_w, ada_b):
    depth, d, d6 = ada_w.shape
    bsz = c.shape[0]
    tn = d
    return pl.pallas_call(
        _ada_kernel,
        out_shape=jax.ShapeDtypeStruct((depth, bsz, d6), F32),
        grid=(depth, d6 // tn),
        in_specs=[
            pl.BlockSpec((bsz, d), lambda i, j: (0, 0)),
            pl.BlockSpec((1, d, tn), lambda i, j: (i, 0, j)),
            pl.BlockSpec((1, 1, tn), lambda i, j: (i, 0, j)),
        ],
        out_specs=pl.BlockSpec((1, bsz, tn), lambda i, j: (i, 0, j)),
        compiler_params=_cparams("parallel", "parallel"),
        name="ada_mod",
    )(c, ada_w, ada_b.reshape(depth, 1, d6))


def _conv_kernel(x_ref, xh_ref, g_ref, sh_ref, sc_ref, gate_ref, win_ref, wc_ref, wout_ref,
                 o_ref, u_scr, *, tm, d):
    i = pl.program_id(1)
    g = g_ref[...]
    sh = sh_ref[0]
    sc = sc_ref[0]
    x = x_ref[0]
    halo = xh_ref.shape[1]
    xe = jnp.concatenate([xh_ref[0], x], axis=0)
    he = _norm_mod(xe, g, sh, sc).astype(BF16)
    z = jnp.dot(he, win_ref[...], preferred_element_type=F32)
    u_all = z[:, d:2 * d] * z[:, 2 * d:]
    row = lax.broadcasted_iota(I32, (halo + tm, 1), 0)
    u_scr[...] = jnp.where(jnp.logical_or(i > 0, row >= halo), u_all, 0.0)
    b_gate = z[halo:, :d]
    wc = wc_ref[...]
    conv = (wc[0:1, :] * u_scr[halo - 2:halo - 2 + tm, :]
            + wc[1:2, :] * u_scr[halo - 1:halo - 1 + tm, :]
            + wc[2:3, :] * u_scr[halo:halo + tm, :])
    y = jnp.dot((b_gate * conv).astype(BF16), wout_ref[...], preferred_element_type=F32)
    o_ref[0] = x + gate_ref[0] * y


def _conv_layer(x, g, sh, sc, gate, w_in, w_conv, w_out, tm=512):
    bsz, seq, d = x.shape
    tm = min(tm, seq)
    nt = seq // tm
    halo = 2 * SUBLANES
    hb = tm // halo
    kern = functools.partial(_conv_kernel, tm=tm, d=d)
    mod_spec = pl.BlockSpec((1, 1, d), lambda b, i: (b, 0, 0))
    return pl.pallas_call(
        kern,
        out_shape=jax.ShapeDtypeStruct((bsz, seq, d), F32),
        grid=(bsz, nt),
        in_specs=[
            pl.BlockSpec((1, tm, d), lambda b, i: (b, i, 0)),
            pl.BlockSpec((1, halo, d), lambda b, i: (b, jnp.maximum(i * hb - 1, 0), 0)),
            pl.BlockSpec((1, d), lambda b, i: (0, 0)),
            mod_spec, mod_spec, mod_spec,
            pl.BlockSpec((d, 3 * d), lambda b, i: (0, 0)),
            pl.BlockSpec((CONV_WIDTH, d), lambda b, i: (0, 0)),
            pl.BlockSpec((d, d), lambda b, i: (0, 0)),
        ],
        out_specs=pl.BlockSpec((1, tm, d), lambda b, i: (b, i, 0)),
        scratch_shapes=[pltpu.VMEM((tm + halo, d), F32)],
        compiler_params=_cparams("parallel", "parallel"),
        name="conv_mixer",
    )(x, x, g, sh, sc, gate, w_in, w_conv, w_out)


def _ffn_kernel(x_ref, g_ref, sh_ref, sc_ref, gate_ref, wgu_ref, wd_ref, o_ref, *, dff, nchunk):
    x = x_ref[0]
    h = _norm_mod(x, g_ref[...], sh_ref[0], sc_ref[0]).astype(BF16)
    cols = dff // nchunk
    acc = jnp.zeros(x.shape, F32)
    for c in range(nchunk):
        gg = jnp.dot(h, wgu_ref[:, c * cols:(c + 1) * cols], preferred_element_type=F32)
        uu = jnp.dot(h, wgu_ref[:, dff + c * cols:dff + (c + 1) * cols], preferred_element_type=F32)
        a = (_silu(gg) * uu).astype(BF16)
        acc = acc + jnp.dot(a, wd_ref[c * cols:(c + 1) * cols, :], preferred_element_type=F32)
    o_ref[0] = x + gate_ref[0] * acc


def _ffn_layer(x, g, sh, sc, gate, w_gu, w_down, tm=512):
    bsz, seq, d = x.shape
    dff = w_down.shape[0]
    tm = min(tm, seq)
    kern = functools.partial(_ffn_kernel, dff=dff, nchunk=2)
    mod_spec = pl.BlockSpec((1, 1, d), lambda b, i: (b, 0, 0))
    return pl.pallas_call(
        kern,
        out_shape=jax.ShapeDtypeStruct((bsz, seq, d), F32),
        grid=(bsz, seq // tm),
        in_specs=[
            pl.BlockSpec((1, tm, d), lambda b, i: (b, i, 0)),
            pl.BlockSpec((1, d), lambda b, i: (0, 0)),
            mod_spec, mod_spec, mod_spec,
            pl.BlockSpec((d, 2 * dff), lambda b, i: (0, 0), pipeline_mode=pl.Buffered(1)),
            pl.BlockSpec((dff, d), lambda b, i: (0, 0), pipeline_mode=pl.Buffered(1)),
        ],
        out_specs=pl.BlockSpec((1, tm, d), lambda b, i: (b, i, 0)),
        compiler_params=_cparams("parallel", "parallel"),
        name="ffn_dense",
    )(x, g, sh, sc, gate, w_gu, w_down)


MOE_SB = 512
MOE_RT = 512
MOE_ALIGN = 16
MOE_WIN = 256
MOE_CAP = MOE_WIN - MOE_ALIGN


def _moe_router_kernel(x_ref, g_ref, sh_ref, sc_ref, rw_ref, rb_ref, h_ref, meta_ref):
    hf = _norm_mod(x_ref[...], g_ref[...], sh_ref[0], sc_ref[0])
    h_ref[...] = hf.astype(BF16)
    logits = jnp.dot(hf, rw_ref[...], preferred_element_type=F32, precision=HIGHEST) + rb_ref[...]
    mx = jnp.max(logits, axis=-1, keepdims=True)
    ex = jnp.exp(logits - mx)
    probs = ex / jnp.sum(ex, axis=-1, keepdims=True)
    lane = lax.broadcasted_iota(I32, probs.shape, 1)
    m1 = jnp.max(probs, axis=-1, keepdims=True)
    i1 = jnp.min(jnp.where(probs == m1, lane, LANES), axis=-1, keepdims=True)
    rest = jnp.where(lane == i1, -1.0, probs)
    m2 = jnp.max(rest, axis=-1, keepdims=True)
    i2 = jnp.min(jnp.where(rest == m2, lane, LANES), axis=-1, keepdims=True)
    den = m1 + m2
    gates = jnp.where(lane == i1, m1 / den, 0.0) + jnp.where(lane == i2, m2 / den, 0.0)
    chosen = jnp.where(jnp.logical_or(lane == i1, lane == i2), 1.0, 0.0)
    nrow = chosen.shape[0]
    below = jnp.where(lax.broadcasted_iota(I32, (nrow, nrow), 1) < lax.broadcasted_iota(I32, (nrow, nrow), 0),
                      1.0, 0.0).astype(BF16)
    rank = jnp.dot(below, chosen.astype(BF16), preferred_element_type=F32)
    meta_ref[...] = gates + pltpu.roll(chosen, N_EXPERTS, 1) + pltpu.roll(rank, 2 * N_EXPERTS, 1)


def _moe_router(xt, g, sh, sc, rw_pad, rb_pad, tpb, tm):
    n, d = xt.shape
    mod_spec = pl.BlockSpec((1, 1, d), lambda i: (i // tpb, 0, 0))
    return pl.pallas_call(
        _moe_router_kernel,
        out_shape=[jax.ShapeDtypeStruct((n, d), BF16), jax.ShapeDtypeStruct((n, LANES), F32)],
        grid=(n // tm,),
        in_specs=[pl.BlockSpec((tm, d), lambda i: (i, 0)),
                  pl.BlockSpec((1, d), lambda i: (0, 0)), mod_spec, mod_spec,
                  pl.BlockSpec((d, LANES), lambda i: (0, 0)),
                  pl.BlockSpec((1, LANES), lambda i: (0, 0))],
        out_specs=[pl.BlockSpec((tm, d), lambda i: (i, 0)), pl.BlockSpec((tm, LANES), lambda i: (i, 0))],
        compiler_params=_cparams("parallel"),
        name="moe_router",
    )(xt, g, sh, sc, rw_pad, rb_pad)


def _window(start, count, w):
    s = start + w * MOE_CAP
    n = jnp.minimum(count - w * MOE_CAP, MOE_CAP)
    a = pl.multiple_of((s // MOE_ALIGN) * MOE_ALIGN, MOE_ALIGN)
    return s, n, a


def _moe_dispatch_kernel(start_ref, cnt_ref, h_ref, pos_ref, xs_init_ref, xs_ref,
                         buf, carry, sems, pending):
    del xs_init_ref
    b = pl.program_id(0)
    nb = pl.num_programs(0)
    ne = buf.shape[0]

    @pl.when(b == 0)
    def _():
        carry[...] = jnp.zeros(carry.shape, BF16)
        for e in range(ne):
            pending[e] = 0

    h = h_ref[...]
    riota = lax.broadcasted_iota(I32, (MOE_WIN, h.shape[0]), 0)

    def out_copy(e, a):
        return pltpu.make_async_copy(buf.at[e], xs_ref.at[pl.ds(a, MOE_WIN)], sems.at[e])

    for e in range(ne):
        posrow = pos_ref[0, e:e + 1, :]
        nwin = (cnt_ref[b, e] + MOE_CAP - 1) // MOE_CAP

        def wbody(w, carry_unused, e=e, posrow=posrow):
            s, n, a = _window(start_ref[b, e], cnt_ref[b, e], w)

            @pl.when(pending[e] == 1)
            def _():
                out_copy(e, 0).wait()

            rel = jnp.where(jnp.logical_and(posrow >= s, posrow < s + n), posrow - a, -1)
            onehot = jnp.where(rel == riota, 1.0, 0.0).astype(BF16)
            buf[e] = jnp.dot(onehot, h, preferred_element_type=F32).astype(BF16)
            buf[e, 0:MOE_ALIGN, :] = buf[e, 0:MOE_ALIGN, :] + carry[e]
            c0 = pl.multiple_of(((s + n) // MOE_ALIGN) * MOE_ALIGN - a, MOE_ALIGN)
            carry[e] = buf[e, pl.ds(c0, MOE_ALIGN), :]
            out_copy(e, a).start()
            pending[e] = 1
            return carry_unused

        lax.fori_loop(0, nwin, wbody, 0)

    @pl.when(b == nb - 1)
    def _():
        for e in range(ne):
            @pl.when(pending[e] == 1)
            def _():
                out_copy(e, 0).wait()


def _moe_dispatch(start, cnt, h, pos_t, ncap):
    n, d = h.shape
    nb, ne, sb = pos_t.shape
    xs_init = jnp.zeros((ncap, d), BF16)
    return pl.pallas_call(
        _moe_dispatch_kernel,
        out_shape=jax.ShapeDtypeStruct((ncap, d), BF16),
        grid_spec=pltpu.PrefetchScalarGridSpec(
            num_scalar_prefetch=2,
            grid=(nb,),
            in_specs=[pl.BlockSpec((sb, d), lambda b, *_: (b, 0)),
                      pl.BlockSpec((1, ne, sb), lambda b, *_: (b, 0, 0)),
                      pl.BlockSpec(memory_space=pl.ANY)],
            out_specs=pl.BlockSpec(memory_space=pl.ANY),
            scratch_shapes=[pltpu.VMEM((ne, MOE_WIN, d), BF16),
                            pltpu.VMEM((ne, MOE_ALIGN, d), BF16),
                            pltpu.SemaphoreType.DMA((ne,)),
                            pltpu.SMEM((ne,), I32)],
        ),
        input_output_aliases={4: 0},
        compiler_params=_cparams("arbitrary"),
        name="moe_dispatch",
    )(start, cnt, h, pos_t, xs_init)


MOE_WCHUNKS = 8


def _moe_ffn_kernel(blk_ref, exp_ref, nt_ref, xs_ref, wgu_hbm, wd_hbm, ys_init_ref, ys_ref,
                    wgu_scr, wd_scr, gu_stage, d_stage, sems, *, dff, nchunk, ebase):
    del blk_ref, ys_init_ref
    k = pl.program_id(0)
    d = xs_ref.shape[1]
    gu_rows = d // MOE_WCHUNKS
    d_rows = dff // MOE_WCHUNKS
    e = ebase + exp_ref[k]

    def gu_copy(c, slot):
        return pltpu.make_async_copy(wgu_hbm.at[e, pl.ds(c * gu_rows, gu_rows), :], gu_stage.at[slot],
                                     sems.at[0, slot])

    def d_copy(c, slot):
        return pltpu.make_async_copy(wd_hbm.at[e, pl.ds(c * d_rows, d_rows), :], d_stage.at[slot],
                                     sems.at[1, slot])

    new_expert = jnp.logical_or(k == 0, exp_ref[k] != exp_ref[jnp.maximum(k - 1, 0)])

    @pl.when(jnp.logical_and(k < nt_ref[0], new_expert))
    def _():
        gu_copy(0, 0).start()
        d_copy(0, 0).start()
        for c in range(MOE_WCHUNKS):
            slot = c % 2
            if c + 1 < MOE_WCHUNKS:
                gu_copy(c + 1, 1 - slot).start()
                d_copy(c + 1, 1 - slot).start()
            gu_copy(c, slot).wait()
            wgu_scr[c * gu_rows:(c + 1) * gu_rows, :] = gu_stage[slot].astype(BF16)
            d_copy(c, slot).wait()
            wd_scr[c * d_rows:(c + 1) * d_rows, :] = d_stage[slot].astype(BF16)

    @pl.when(k < nt_ref[0])
    def _():
        x = xs_ref[...]
        cols = dff // nchunk
        acc = jnp.zeros(x.shape, F32)
        for c in range(nchunk):
            gg = jnp.dot(x, wgu_scr[:, c * cols:(c + 1) * cols], preferred_element_type=F32)
            uu = jnp.dot(x, wgu_scr[:, dff + c * cols:dff + (c + 1) * cols], preferred_element_type=F32)
            a = (_silu(gg) * uu).astype(BF16)
            acc = acc + jnp.dot(a, wd_scr[c * cols:(c + 1) * cols, :], preferred_element_type=F32)
        ys_ref[...] = acc.astype(BF16)


def _moe_ffn(tile_blk, tile_exp, ntiles, xs, w_gu, w_down, ebase):
    ncap, d = xs.shape
    dff = w_down.shape[1]
    assert d % (MOE_WCHUNKS * 2 * SUBLANES) == 0 and dff % (MOE_WCHUNKS * 2 * SUBLANES) == 0
    kern = functools.partial(_moe_ffn_kernel, dff=dff, nchunk=2, ebase=ebase)
    ys_init = jnp.zeros((ncap, d), BF16)
    return pl.pallas_call(
        kern,
        out_shape=jax.ShapeDtypeStruct((ncap, d), BF16),
        grid_spec=pltpu.PrefetchScalarGridSpec(
            num_scalar_prefetch=3,
            grid=(ncap // MOE_RT,),
            in_specs=[pl.BlockSpec((MOE_RT, d), lambda k, blk, exp, nt: (blk[k], 0)),
                      pl.BlockSpec(memory_space=pl.ANY),
                      pl.BlockSpec(memory_space=pl.ANY),
                      pl.BlockSpec(memory_space=pl.ANY)],
            out_specs=pl.BlockSpec((MOE_RT, d), lambda k, blk, exp, nt: (blk[k], 0)),
            scratch_shapes=[pltpu.VMEM((d, 2 * dff), BF16),
                            pltpu.VMEM((dff, d), BF16),
                            pltpu.VMEM((2, d // MOE_WCHUNKS, 2 * dff), F32),
                            pltpu.VMEM((2, dff // MOE_WCHUNKS, d), F32),
                            pltpu.SemaphoreType.DMA((2, 2))],
        ),
        input_output_aliases={6: 0},
        compiler_params=_cparams("arbitrary"),
        name="moe_ffn",
    )(tile_blk, tile_exp, ntiles, xs, w_gu, w_down, ys_init)


def _moe_combine_kernel(start_ref, cnt_ref, x_ref, gate_ref, pos_ref, gts_ref, ys_ref, o_ref,
                        buf, sems, acc_scr):
    b = pl.program_id(0)
    nb = pl.num_programs(0)
    ne = buf.shape[1]
    sb = x_ref.shape[0]
    slot = b % 2

    def in_copy(sl, e, a):
        return pltpu.make_async_copy(ys_ref.at[pl.ds(a, MOE_WIN)], buf.at[sl, e], sems.at[sl, e])

    def start_first_windows(blk, sl):
        for e in range(ne):
            @pl.when(cnt_ref[blk, e] > 0)
            def _():
                _, _, a = _window(start_ref[blk, e], cnt_ref[blk, e], 0)
                in_copy(sl, e, a).start()

    @pl.when(b == 0)
    def _():
        start_first_windows(0, 0)

    @pl.when(b + 1 < nb)
    def _():
        start_first_windows(b + 1, 1 - slot)

    acc_scr[...] = jnp.zeros(acc_scr.shape, F32)
    liota = lax.broadcasted_iota(I32, (sb, MOE_WIN), 1)
    for e in range(ne):
        poscol = pos_ref[:, e:e + 1]
        gcol = gts_ref[:, e:e + 1]
        nwin = (cnt_ref[b, e] + MOE_CAP - 1) // MOE_CAP

        def wbody(w, carry_unused, e=e, poscol=poscol, gcol=gcol):
            s, n, a = _window(start_ref[b, e], cnt_ref[b, e], w)

            @pl.when(w > 0)
            def _():
                in_copy(slot, e, a).start()

            in_copy(slot, e, a).wait()
            rel = jnp.where(jnp.logical_and(poscol >= s, poscol < s + n), poscol - a, -1)
            onehot = jnp.where(rel == liota, 1.0, 0.0).astype(BF16)
            acc_scr[...] += gcol * jnp.dot(onehot, buf[slot, e], preferred_element_type=F32)
            return carry_unused

        lax.fori_loop(0, nwin, wbody, 0)

    o_ref[...] = x_ref[...] + gate_ref[0] * acc_scr[...]


def _moe_combine(start, cnt, xt, gate, pos_n, gates_n, ys, tpb):
    n, d = xt.shape
    nb, ne = cnt.shape
    sb = n // nb
    return pl.pallas_call(
        _moe_combine_kernel,
        out_shape=jax.ShapeDtypeStruct((n, d), F32),
        grid_spec=pltpu.PrefetchScalarGridSpec(
            num_scalar_prefetch=2,
            grid=(nb,),
            in_specs=[pl.BlockSpec((sb, d), lambda b, *_: (b, 0)),
                      pl.BlockSpec((1, 1, d), lambda b, *_: (b // tpb, 0, 0)),
                      pl.BlockSpec((sb, ne), lambda b, *_: (b, 0)),
                      pl.BlockSpec((sb, ne), lambda b, *_: (b, 0)),
                      pl.BlockSpec(memory_space=pl.ANY)],
            out_specs=pl.BlockSpec((sb, d), lambda b, *_: (b, 0)),
            scratch_shapes=[pltpu.VMEM((2, ne, MOE_WIN, d), BF16),
                            pltpu.SemaphoreType.DMA((2, ne)),
                            pltpu.VMEM((sb, d), F32)],
        ),
        compiler_params=_cparams("arbitrary"),
        name="moe_combine",
    )(start, cnt, xt, gate, pos_n, gates_n, ys)


def _moe_layer(x, g, sh, sc, gate, router_w, router_b, w_gu, w_down, ebase=0):
    bsz, seq, d = x.shape
    ne = router_w.shape[1]
    n = bsz * seq
    sb = min(MOE_SB, seq)
    tpb = seq // sb
    nb = n // sb
    rt = MOE_RT
    xt = x.reshape(n, d)
    rw_pad = jnp.pad(router_w.astype(F32), ((0, 0), (0, LANES - ne)))
    rb_pad = jnp.pad(router_b.astype(F32).reshape(1, ne), ((0, 0), (0, LANES - ne)), constant_values=NEG_BIG)
    h, meta = _moe_router(xt, g, sh, sc, rw_pad, rb_pad, tpb, sb)
    gates = meta[:, :ne]
    sel = (meta[:, ne:2 * ne] > 0.5).astype(I32)

    selb = sel.reshape(nb, sb, ne)
    cnt = jnp.sum(selb, axis=1)
    rank = meta[:, 2 * ne:3 * ne].astype(I32).reshape(nb, sb, ne)
    total = jnp.sum(cnt, axis=0)
    region = ((total + MOE_WIN + rt - 1) // rt) * rt
    off = jnp.cumsum(region) - region
    start = (off[None, :] + jnp.cumsum(cnt, axis=0) - cnt).astype(I32)
    pos = jnp.where(selb > 0, start[:, None, :] + rank, -1).astype(I32)
    ncap = 2 * n + ne * (MOE_WIN + rt)
    tiles_e = (total + rt - 1) // rt
    tcum = jnp.cumsum(tiles_e)
    ntiles = tcum[-1]
    kk = jnp.minimum(jnp.arange(ncap // rt), ntiles - 1)
    tile_exp = jnp.sum((kk[:, None] >= tcum[None, :]).astype(I32), axis=1)
    tile_blk = (off[tile_exp] // rt + kk - (tcum - tiles_e)[tile_exp]).astype(I32)

    xs = _moe_dispatch(start, cnt.astype(I32), h, jnp.swapaxes(pos, 1, 2), ncap)
    ys = _moe_ffn(tile_blk, tile_exp, ntiles.reshape(1).astype(I32), xs, w_gu, w_down, ebase)
    out = _moe_combine(start, cnt.astype(I32), xt, gate, pos.reshape(n, ne), gates, ys, tpb)
    return out.reshape(bsz, seq, d)


def _head_norm(q, hsum_ref, hexp_ref, gain, scale):
    ms = jnp.dot((q * q).astype(BF16), hsum_ref[...], preferred_element_type=F32)
    r = lax.rsqrt(ms + EPS)
    r_hi = r.astype(BF16)
    r_lo = (r - r_hi.astype(F32)).astype(BF16)
    rexp = (jnp.dot(r_hi, hexp_ref[...], preferred_element_type=F32)
            + jnp.dot(r_lo, hexp_ref[...], preferred_element_type=F32))
    return q * rexp * (gain * scale)


def _qkv_kernel(x_ref, g_ref, sh_ref, sc_ref, wm_ref, ws_ref, qg_ref, kg_ref,
                hsq_ref, heq_ref, hsk_ref, hek_ref,
                q_ref, k_ref, v_ref, qi_ref, kiwi_ref, *, dq, dk, dqi):
    x = x_ref[0]
    h = _norm_mod(x, g_ref[...], sh_ref[0], sc_ref[0]).astype(BF16)
    z = jnp.dot(h, wm_ref[...], preferred_element_type=F32)
    q = z[:, :dq]
    k = z[:, dq:dq + dk]
    v = z[:, dq + dk:dq + 2 * dk]
    qi = z[:, dq + 2 * dk:dq + 2 * dk + dqi]
    q_ref[0] = _head_norm(q, hsq_ref, heq_ref, qg_ref[...], HEAD_DIM ** -0.5 * LOG2E).astype(BF16)
    k_ref[0] = _head_norm(k, hsk_ref, hek_ref, kg_ref[...], 1.0).astype(BF16)
    v_ref[0] = v.astype(BF16)
    qi_ref[0] = qi.astype(BF16)
    kiwi_ref[0] = jnp.dot(h, ws_ref[...], preferred_element_type=F32)


def _head_indicators(nheads):
    hs = np.zeros((nheads * HEAD_DIM, LANES), np.float32)
    he = np.zeros((LANES, nheads * HEAD_DIM), np.float32)
    for hd in range(nheads):
        hs[hd * HEAD_DIM:(hd + 1) * HEAD_DIM, hd] = 1.0 / HEAD_DIM
        he[hd, hd * HEAD_DIM:(hd + 1) * HEAD_DIM] = 1.0
    return jnp.asarray(hs, BF16), jnp.asarray(he, BF16)


def _qkv_proj(x, g, sh, sc, w_main, w_small, q_gain, k_gain, tm=512):
    bsz, seq, d = x.shape
    tm = min(tm, seq)
    dq = N_HEADS * HEAD_DIM
    dk = N_KV_HEADS * HEAD_DIM
    dqi = IDX_HEADS * IDX_DIM
    hsq, heq = _head_indicators(N_HEADS)
    hsk, hek = _head_indicators(N_KV_HEADS)
    qg = jnp.tile(q_gain.reshape(1, HEAD_DIM), (1, N_HEADS)).astype(F32)
    kg = jnp.tile(k_gain.reshape(1, HEAD_DIM), (1, N_KV_HEADS)).astype(F32)
    kern = functools.partial(_qkv_kernel, dq=dq, dk=dk, dqi=dqi)
    mod_spec = pl.BlockSpec((1, 1, d), lambda b, i: (b, 0, 0))

    def full(a):
        return pl.BlockSpec(a.shape, lambda b, i: (0,) * a.ndim)

    def out(n):
        return pl.BlockSpec((1, tm, n), lambda b, i: (b, i, 0))

    return pl.pallas_call(
        kern,
        out_shape=[
            jax.ShapeDtypeStruct((bsz, seq, dq), BF16),
            jax.ShapeDtypeStruct((bsz, seq, dk), BF16),
            jax.ShapeDtypeStruct((bsz, seq, dk), BF16),
            jax.ShapeDtypeStruct((bsz, seq, dqi), BF16),
            jax.ShapeDtypeStruct((bsz, seq, LANES), F32),
        ],
        grid=(bsz, seq // tm),
        in_specs=[
            pl.BlockSpec((1, tm, d), lambda b, i: (b, i, 0)),
            pl.BlockSpec((1, d), lambda b, i: (0, 0)),
            mod_spec, mod_spec,
            full(w_main), full(w_small), full(qg), full(kg),
            full(hsq), full(heq), full(hsk), full(hek),
        ],
        out_specs=[out(dq), out(dk), out(dk), out(dqi), out(LANES)],
        compiler_params=_cparams("parallel", "parallel"),
        name="attn_qkv",
    )(x, g, sh, sc, w_main, w_small, qg, kg, hsq, heq, hsk, hek)


def _rel_bucket_np(dist):
    max_exact = REL_BUCKETS // 2
    d = np.maximum(dist, 1).astype(np.float64)
    large = max_exact + (np.log(d / max_exact) / math.log(REL_MAX_DIST / max_exact)
                         * (REL_BUCKETS - max_exact)).astype(np.int32)
    large = np.minimum(large, REL_BUCKETS - 1)
    return np.where(dist < max_exact, dist, large).astype(np.int32)


def _bias_table_kernel(bucket_ref, rb_ref, o_ref):
    hd = pl.program_id(0)
    bucket = bucket_ref[...]
    acc = jnp.zeros(bucket.shape, F32)
    for b in range(REL_BUCKETS):
        acc = jnp.where(bucket == b, rb_ref[b, hd] * LOG2E, acc)
    o_ref[0] = acc


def _bias_table(rel_bias):
    w = np.arange(BIAS_W)[:, None]
    i = np.arange(ATT_TQ)[None, :]
    bucket = jnp.asarray(_rel_bucket_np(np.maximum(i - w + BIAS_C, 0)))
    return pl.pallas_call(
        _bias_table_kernel,
        out_shape=jax.ShapeDtypeStruct((N_HEADS, BIAS_W, ATT_TQ), F32),
        grid=(N_HEADS,),
        in_specs=[
            pl.BlockSpec((BIAS_W, ATT_TQ), lambda hd: (0, 0)),
            pl.BlockSpec(memory_space=pltpu.SMEM),
        ],
        out_specs=pl.BlockSpec((1, BIAS_W, ATT_TQ), lambda hd: (hd, 0, 0)),
        compiler_params=_cparams("arbitrary"),
        name="attn_bias_table",
    )(bucket, rel_bias.astype(F32))


def _attn_kernel(qT_ref, qiT_ref, wiT_ref, k_ref, vT_ref, ki_ref, pt_ref, o_ref,
                 keys_scr, negm_scr, pidx_scr, oT_scr, acc_scr, qall_scr, sa_scr, sb_scr, *, top_k):
    tq, kc = ATT_TQ, ATT_KC
    qt = pl.program_id(1)
    q0 = qt * tq
    nch = (q0 + tq + kc - 1) // kc
    tpos = q0 + lax.broadcasted_iota(I32, (kc, tq), 1)
    srow = lax.broadcasted_iota(I32, (kc, tq), 0)

    qiT = qiT_ref[0]
    qi_all = jnp.concatenate([qiT[hd * IDX_DIM:(hd + 1) * IDX_DIM, :] for hd in range(IDX_HEADS)], axis=1)
    wiT = wiT_ref[0]

    def score_chunk(c):
        ks = pl.multiple_of(c * kc, kc)
        kic = ki_ref[0, pl.ds(ks, kc), :]
        dots = jnp.dot(kic, qi_all, preferred_element_type=F32)
        acc = jnp.zeros((kc, tq), F32)
        for hd in range(IDX_HEADS):
            acc = acc + jnp.maximum(dots[:, hd * tq:(hd + 1) * tq], 0.0) * wiT[hd:hd + 1, :]
        acc = jnp.where(acc == 0.0, 0.0, acc)
        bits = pltpu.bitcast(acc, I32)
        key = jnp.where(bits < 0, bits ^ INT_MAX, bits)
        key = jnp.where(ks + srow <= tpos, key, INT_MIN)
        keys_scr[pl.ds(ks, kc), :] = key

    npair = (nch + 1) // 2

    def score_pair(i, carry):
        score_chunk(2 * i)
        score_chunk(2 * i + 1)
        return carry

    lax.fori_loop(0, npair, score_pair, 0)

    @pl.when(nch % 2 == 1)
    def _():
        negm_scr[pl.ds(pl.multiple_of(nch * kc, kc), kc), :] = jnp.full((kc, tq), NEG_BIG, F32)

    srow2 = lax.broadcasted_iota(I32, (2 * kc, tq), 0)

    def count(pred):
        def body(c, acc):
            ks = pl.multiple_of(c * 2 * kc, 2 * kc)
            m = pred(keys_scr[pl.ds(ks, 2 * kc), :], ks + srow2).astype(I32)
            return acc + jnp.sum(m.reshape(2 * kc // SUBLANES, SUBLANES, tq), axis=0)
        acc = lax.fori_loop(0, npair, body, jnp.zeros((SUBLANES, tq), I32))
        return jnp.sum(acc, axis=0, keepdims=True)

    def bit_body(it, p):
        cand_p = p | lax.shift_left(jnp.int32(1), 31 - it)
        cand = cand_p ^ INT_MIN
        cnt = count(lambda k, s: k >= cand)
        return jnp.where(cnt >= top_k, cand_p, p)

    p_fin = lax.fori_loop(0, 32, bit_body, jnp.zeros((1, tq), I32))
    v = p_fin ^ INT_MIN

    cnt_gt = count(lambda k, s: k > v)
    cnt_eq = count(lambda k, s: k == v)
    need = top_k - cnt_gt
    pidx_scr[...] = jnp.full((1, tq), INT_MAX, I32)
    pos_bits = (keys_scr.shape[0] - 1).bit_length()

    @pl.when(jnp.max(cnt_eq - need) > 0)
    def _():
        def ibit(it, p):
            cand = p | lax.shift_left(jnp.int32(1), pos_bits - 1 - it)
            cnt = count(lambda k, s: jnp.logical_and(k == v, s < cand))
            return jnp.where(cnt < need, cand, p)
        pidx_scr[...] = lax.fori_loop(0, pos_bits, ibit, jnp.zeros((1, tq), I32))

    pidx = pidx_scr[...]

    def mask_chunk(c, carry):
        ks = pl.multiple_of(c * kc, kc)
        k = keys_scr[pl.ds(ks, kc), :]
        spos = ks + srow
        sel = jnp.logical_or(k > v, jnp.logical_and(k == v, spos <= pidx))
        sel = jnp.logical_and(sel, spos <= tpos)
        negm_scr[pl.ds(ks, kc), :] = jnp.where(sel, 0.0, NEG_BIG)
        return carry

    lax.fori_loop(0, nch, mask_chunk, 0)

    acc_scr[...] = jnp.zeros(acc_scr.shape, F32)
    for n in range(N_KV_HEADS):
        r0 = n * N_REP * HEAD_DIM
        qall_scr[n] = jnp.concatenate(
            [qT_ref[0, r0 + g * HEAD_DIM:r0 + (g + 1) * HEAD_DIM, :] for g in range(N_REP)], axis=1)

    def qk_chunk(c, s_ref):
        ks = pl.multiple_of(c * kc, kc)
        for n in range(N_KV_HEADS):
            s_ref[n] = jnp.dot(k_ref[0, n, pl.ds(ks, kc), :], qall_scr[n], preferred_element_type=F32)

    def softmax_pv(c, s_ref, ms, far):
        ks = pl.multiple_of(c * kc, kc)
        negm = negm_scr[pl.ds(ks, kc), :]
        w0 = pl.multiple_of(jnp.clip(BIAS_C - (q0 - ks), 0, BIAS_C), LANES)
        new_ms = []
        for n in range(N_KV_HEADS):
            s = s_ref[n]
            if far:
                cvec = jnp.concatenate([pt_ref[n * N_REP + g, 0:1, :] for g in range(N_REP)], axis=1)
                lg = jnp.concatenate([s[:, g * tq:(g + 1) * tq] + negm for g in range(N_REP)], axis=1)
                m_new = jnp.maximum(ms[n], jnp.max(lg, axis=0, keepdims=True) + cvec)
                p = jnp.exp2(lg - (m_new - cvec))
            else:
                lg = jnp.concatenate(
                    [s[:, g * tq:(g + 1) * tq] + pt_ref[n * N_REP + g, pl.ds(w0, kc), :] + negm
                     for g in range(N_REP)], axis=1)
                m_new = jnp.maximum(ms[n], jnp.max(lg, axis=0, keepdims=True))
                p = jnp.exp2(lg - m_new)
            alpha = jnp.exp2(ms[n] - m_new)
            acc_scr[n] = alpha * acc_scr[n] + jnp.dot(vT_ref[0, c, n], p.astype(BF16),
                                                      preferred_element_type=F32)
            new_ms.append(m_new)
        return tuple(new_ms)

    last_chunk = k_ref.shape[2] // kc - 1

    def pair_step(i, ms, far):
        c0 = 2 * i
        qk_chunk(c0 + 1, sb_scr)
        ms = softmax_pv(c0, sa_scr, ms, far)
        qk_chunk(jnp.minimum(c0 + 2, last_chunk), sa_scr)
        return softmax_pv(c0 + 1, sb_scr, ms, far)

    n_far = jnp.clip((q0 - BIAS_C + kc) // kc, 0, nch)
    ms = tuple(jnp.full((1, N_REP * tq), NEG_BIG, F32) for _ in range(N_KV_HEADS))
    qk_chunk(0, sa_scr)
    ms = lax.fori_loop(0, n_far // 2, functools.partial(pair_step, far=True), ms)
    lax.fori_loop(n_far // 2, npair, functools.partial(pair_step, far=False), ms)
    for n in range(N_KV_HEADS):
        o_t = acc_scr[n, 0:HEAD_DIM, :] / acc_scr[n, HEAD_DIM:HEAD_DIM + 1, :]
        for g in range(N_REP):
            r0 = (n * N_REP + g) * HEAD_DIM
            oT_scr[r0:r0 + HEAD_DIM, :] = o_t[:, g * tq:(g + 1) * tq]

    o_ref[0] = oT_scr[...].T.astype(BF16)


def _attention(qT, qiT, wiT, k4, vT, ki, ptab, top_k):
    bsz, dq, seq = qT.shape
    tq = ATT_TQ
    assert seq % (2 * ATT_KC) == 0
    kern = functools.partial(_attn_kernel, top_k=top_k)
    return pl.pallas_call(
        kern,
        out_shape=jax.ShapeDtypeStruct((bsz, seq, dq), BF16),
        grid=(bsz, seq // tq),
        in_specs=[
            pl.BlockSpec((1, dq, tq), lambda b, i: (b, 0, i)),
            pl.BlockSpec((1, qiT.shape[1], tq), lambda b, i: (b, 0, i)),
            pl.BlockSpec((1, wiT.shape[1], tq), lambda b, i: (b, 0, i)),
            pl.BlockSpec((1,) + k4.shape[1:], lambda b, i: (b, 0, 0, 0)),
            pl.BlockSpec((1,) + vT.shape[1:], lambda b, i: (b, 0, 0, 0, 0)),
            pl.BlockSpec((1,) + ki.shape[1:], lambda b, i: (b, 0, 0)),
            pl.BlockSpec(ptab.shape, lambda b, i: (0, 0, 0)),
        ],
        out_specs=pl.BlockSpec((1, tq, dq), lambda b, i: (b, i, 0)),
        scratch_shapes=[
            pltpu.VMEM((seq, tq), I32),
            pltpu.VMEM((seq, tq), F32),
            pltpu.VMEM((1, tq), I32),
            pltpu.VMEM((dq, tq), F32),
            pltpu.VMEM((N_KV_HEADS, ATT_VROWS, N_REP * tq), F32),
            pltpu.VMEM((N_KV_HEADS, HEAD_DIM, N_REP * tq), BF16),
            pltpu.VMEM((N_KV_HEADS, ATT_KC, N_REP * tq), F32),
            pltpu.VMEM((N_KV_HEADS, ATT_KC, N_REP * tq), F32),
        ],
        compiler_params=_cparams("parallel", "arbitrary"),
        name="attn_core",
    )(qT, qiT, wiT, k4, vT, ki, ptab)


def _proj_res_kernel(a_ref, x_ref, gate_ref, w_ref, o_ref):
    y = jnp.dot(a_ref[0], w_ref[...], preferred_element_type=F32)
    o_ref[0] = x_ref[0] + gate_ref[0] * y


def _proj_residual(a, x, gate, w, tm=512):
    bsz, seq, d = x.shape
    tm = min(tm, seq)
    return pl.pallas_call(
        _proj_res_kernel,
        out_shape=jax.ShapeDtypeStruct((bsz, seq, d), F32),
        grid=(bsz, seq // tm),
        in_specs=[
            pl.BlockSpec((1, tm, a.shape[2]), lambda b, i: (b, i, 0)),
            pl.BlockSpec((1, tm, d), lambda b, i: (b, i, 0)),
            pl.BlockSpec((1, 1, d), lambda b, i: (b, 0, 0)),
            pl.BlockSpec(w.shape, lambda b, i: (0, 0)),
        ],
        out_specs=pl.BlockSpec((1, tm, d), lambda b, i: (b, i, 0)),
        compiler_params=_cparams("parallel", "parallel"),
        name="proj_residual",
    )(a, x, gate, w)


def _attn_layer(x, g, sh, sc, gate, w_in, q_gain, k_gain, w_out, rel_bias):
    bsz, seq, d = x.shape
    top_k = min(TOPK_MAX, seq // 4)
    dq = N_HEADS * HEAD_DIM
    dk = N_KV_HEADS * HEAD_DIM
    dqi = IDX_HEADS * IDX_DIM
    nmain = dq + 2 * dk + dqi
    w_main = w_in[:, :nmain].astype(BF16)
    w_small = jnp.pad(w_in[:, nmain:], ((0, 0), (0, LANES - (IDX_DIM + IDX_HEADS)))).astype(BF16)
    q, k, v, qi, kiwi = _qkv_proj(x, g, sh, sc, w_main, w_small, q_gain, k_gain)
    ki = kiwi[:, :, :IDX_DIM].astype(BF16)
    wi = kiwi[:, :, IDX_DIM:IDX_DIM + IDX_HEADS] * (IDX_HEADS ** -0.5 * IDX_DIM ** -0.5)
    qT = jnp.swapaxes(q, 1, 2)
    qiT = jnp.swapaxes(qi, 1, 2)
    wiT = jnp.swapaxes(wi, 1, 2)
    nck = seq // ATT_KC
    vT = jnp.swapaxes(v.reshape(bsz, nck, ATT_KC, dk), 2, 3).reshape(bsz, nck, N_KV_HEADS, HEAD_DIM, ATT_KC)
    vT = jnp.concatenate([
        vT, jnp.ones((bsz, nck, N_KV_HEADS, 1, ATT_KC), BF16),
        jnp.zeros((bsz, nck, N_KV_HEADS, ATT_VROWS - HEAD_DIM - 1, ATT_KC), BF16)], axis=3)
    k4 = jnp.swapaxes(k.reshape(bsz, seq, N_KV_HEADS, HEAD_DIM), 1, 2)
    ptab = _bias_table(rel_bias)
    attn = _attention(qT, qiT, wiT, k4, vT, ki, ptab, top_k)
    return _proj_residual(attn, x, gate, w_out.astype(BF16))


def _s5_prep_kernel(lre_ref, lim_ref, ls_ref, bre_ref, bim_ref, cre_ref, cim_ref,
                    bcw_ref, mtw_ref, ccw_ref, are_ref, aim_ref):
    t_len = SSM_CHUNK
    lre = jnp.minimum(lre_ref[0], -1e-4)
    lim = lim_ref[0]
    step = jnp.exp(ls_ref[0])
    ar = lre * step
    ai = lim * step

    def powers(jv):
        mag = jnp.exp(jv * ar)
        return mag * jnp.cos(jv * ai), mag * jnp.sin(jv * ai)

    lb_re, lb_im = powers(1.0)
    nr = lb_re - 1.0
    ni = lb_im
    den = lre * lre + lim * lim
    cf_re = (nr * lre + ni * lim) / den
    cf_im = (ni * lre - nr * lim) / den
    bre = bre_ref[0]
    bim = bim_ref[0]
    bb_re = cf_re * bre - cf_im * bim
    bb_im = cf_re * bim + cf_im * bre
    cre = cre_ref[0]
    cim = cim_ref[0]
    nst = lre.shape[-1]
    jv = lax.broadcasted_iota(I32, (t_len, 1, nst), 0).astype(F32)
    pj_re, pj_im = powers(jv)
    a_re = (cre[None] * pj_re - cim[None] * pj_im).reshape(t_len * SSM_GROUP, nst)
    a_im = (cre[None] * pj_im + cim[None] * pj_re).reshape(t_len * SSM_GROUP, nst)
    dn = (((1,), (1,)), ((), ()))
    cg = SSM_GROUP
    tc = t_len * cg
    width = t_len * LANES
    gq = pl.program_id(0) % (LANES // cg)
    nh = (LANES // cg) * nst
    cg_shift = cg.bit_length() - 1

    def place(nrows, target):
        r = lax.broadcasted_iota(I32, (nrows, width), 0)
        col = lax.broadcasted_iota(I32, (nrows, width), 1)
        return jnp.where(col == target(r), 1.0, 0.0).astype(BF16)

    pm = place(tc, lambda r: lax.shift_right_logical(r, cg_shift) * LANES + gq * cg + (r & (cg - 1)))
    k_t = (lax.dot_general(bb_re, a_re, dn, preferred_element_type=F32, precision=HIGHEST)
           - lax.dot_general(bb_im, a_im, dn, preferred_element_type=F32, precision=HIGHEST))
    lane = lax.broadcasted_iota(I32, (cg, tc), 1)
    mt_rows = [k_t] + [jnp.where(lane >= s * cg, pltpu.roll(k_t, s * cg, 1), 0.0) for s in range(1, t_len)]
    mt_t = jnp.concatenate(mt_rows, axis=0)
    mtw_ref[0] = jnp.dot(mt_t.astype(BF16), pm, preferred_element_type=F32).astype(BF16)
    pr_re, pr_im = powers((t_len - 1.0) - jv)
    bc_re = (bb_re[None] * pr_re - bb_im[None] * pr_im).reshape(tc, nst)
    bc_im = (bb_re[None] * pr_im + bb_im[None] * pr_re).reshape(tc, nst)
    pb_re = place(nst, lambda r: gq * nst + r)
    pb_im = place(nst, lambda r: nh + gq * nst + r)
    bcw_ref[0] = (jnp.dot(bc_re.astype(BF16), pb_re, preferred_element_type=F32)
                  + jnp.dot(bc_im.astype(BF16), pb_im, preferred_element_type=F32)).astype(BF16)
    pn_re, pn_im = powers(jv + 1.0)
    cc_re = (cre[None] * pn_re - cim[None] * pn_im).reshape(tc, nst)
    cc_im = (cre[None] * pn_im + cim[None] * pn_re).reshape(tc, nst)
    eye = jnp.where(lax.broadcasted_iota(I32, (nst, nst), 0) == lax.broadcasted_iota(I32, (nst, nst), 1),
                    1.0, 0.0)
    cct_re = lax.dot_general(eye, cc_re, dn, preferred_element_type=F32, precision=HIGHEST)
    cct_im = lax.dot_general(eye, cc_im, dn, preferred_element_type=F32, precision=HIGHEST)
    ccw_ref[0, 0:nst, :] = jnp.dot(cct_re.astype(BF16), pm, preferred_element_type=F32).astype(BF16)
    ccw_ref[0, nst:2 * nst, :] = (-jnp.dot(cct_im.astype(BF16), pm, preferred_element_type=F32)).astype(BF16)
    at_re, at_im = powers(float(t_len))
    rs = lax.broadcasted_iota(I32, (nst, nh), 0)
    cs = lax.broadcasted_iota(I32, (nst, nh), 1)
    pa = jnp.where(cs == gq * nst + rs, 1.0, 0.0)
    are_ref[0] = jnp.dot(at_re, pa, preferred_element_type=F32, precision=HIGHEST)
    aim_ref[0] = jnp.dot(at_im, pa, preferred_element_type=F32, precision=HIGHEST)


def _s5_prep(lam_re, lam_im, log_step, b_re, b_im, c_re, c_im):
    ng, nst = lam_re.shape
    tc = SSM_CHUNK * SSM_GROUP
    vec = pl.BlockSpec((1, 1, nst), lambda gi: (gi, 0, 0))
    mat = pl.BlockSpec((1, SSM_GROUP, nst), lambda gi: (gi, 0, 0))
    width = SSM_CHUNK * LANES
    assert tc == 2 * nst and 2 * (LANES // SSM_GROUP) * nst == width
    big = pl.BlockSpec((1, tc, width), lambda gi: (gi, 0, 0))
    wide = jax.ShapeDtypeStruct((ng, tc, width), BF16)
    return pl.pallas_call(
        _s5_prep_kernel,
        out_shape=[wide, wide, wide,
                   jax.ShapeDtypeStruct((ng, 1, width // 2), F32),
                   jax.ShapeDtypeStruct((ng, 1, width // 2), F32)],
        grid=(ng,),
        in_specs=[vec, vec, pl.BlockSpec((1, 1, 1), lambda gi: (gi, 0, 0)), mat, mat, mat, mat],
        out_specs=[big, big, big, pl.BlockSpec((1, 1, width // 2), lambda gi: (gi, 0, 0)),
                   pl.BlockSpec((1, 1, width // 2), lambda gi: (gi, 0, 0))],
        compiler_params=_cparams("parallel"),
        name="s5_prep",
    )(lam_re.reshape(ng, 1, nst), lam_im.reshape(ng, 1, nst), log_step.reshape(ng, 1, 1),
      jnp.swapaxes(b_re, 1, 2), jnp.swapaxes(b_im, 1, 2), c_re, c_im)


def _s5_pre_kernel(x_ref, g_ref, sh_ref, sc_ref, o_ref, h_scr):
    t_len = SSM_CHUNK
    h = _norm_mod(x_ref[0], g_ref[...], sh_ref[0], sc_ref[0])
    nj = o_ref.shape[1]
    for q in range(o_ref.shape[0]):
        h_scr[q] = h[:, q * LANES:(q + 1) * LANES]
        for t in range(t_len):
            o_ref[q, :, t * LANES:(t + 1) * LANES] = h_scr[q, pl.ds(t, nj, stride=t_len), :].astype(BF16)


def _s5_pre(x, g, sh, sc, tm=512):
    bsz, seq, d = x.shape
    tm = min(tm, seq)
    nt = seq // tm
    nq = d // LANES
    t_len = SSM_CHUNK
    mod_spec = pl.BlockSpec((1, 1, d), lambda b, i: (b, 0, 0))
    return pl.pallas_call(
        _s5_pre_kernel,
        out_shape=jax.ShapeDtypeStruct((nq, bsz * seq // t_len, t_len * LANES), BF16),
        grid=(bsz, nt),
        in_specs=[pl.BlockSpec((1, tm, d), lambda b, i: (b, i, 0)),
                  pl.BlockSpec((1, d), lambda b, i: (0, 0)), mod_spec, mod_spec],
        out_specs=pl.BlockSpec((nq, tm // t_len, t_len * LANES), lambda b, i: (0, b * nt + i, 0)),
        scratch_shapes=[pltpu.VMEM((nq, tm, LANES), F32)],
        compiler_params=_cparams("parallel", "parallel"),
        name="s5_pre",
    )(x, g, sh, sc)


def _s5_scan_kernel(x_ref, bc_ref, mt_ref, cc_ref, are_ref, aim_ref, y_ref, re_scr, im_scr):
    x = x_ref[0]
    nrow = x.shape[0]
    pad = re_scr.shape[0] - nrow
    v = jnp.dot(x, bc_ref[0], preferred_element_type=F32)
    nh = v.shape[1] // 2
    s_re = v[:, :nh]
    s_im = v[:, nh:]
    a_re = are_ref[0]
    a_im = aim_ref[0]
    re_scr[0:pad, :] = jnp.zeros((pad, nh), F32)
    im_scr[0:pad, :] = jnp.zeros((pad, nh), F32)

    def shifted(scr, val, dist):
        scr[pad:pad + nrow, :] = val
        return scr[pad - dist:pad - dist + nrow, :]

    dist = 1
    while dist < nrow:
        sh_re = shifted(re_scr, s_re, dist)
        sh_im = shifted(im_scr, s_im, dist)
        s_re, s_im = (s_re + a_re * sh_re - a_im * sh_im, s_im + a_re * sh_im + a_im * sh_re)
        a_re, a_im = (a_re * a_re - a_im * a_im, 2.0 * a_re * a_im)
        dist *= 2
    sp = jnp.concatenate([shifted(re_scr, s_re, 1), shifted(im_scr, s_im, 1)], axis=1).astype(BF16)
    y_ref[0] = (jnp.dot(x, mt_ref[0], preferred_element_type=F32)
                + jnp.dot(sp, cc_ref[0], preferred_element_type=F32))


def _s5_scan(hq, bcq, mtq, ccq, a_re, a_im, bsz):
    nq, nrows, width = hq.shape
    nj = nrows // bsz
    nh = a_re.shape[-1]
    pad = max(nj // 2, SUBLANES)
    tile = pl.BlockSpec((1, nj, width), lambda q, b: (q, b, 0))
    wspec = lambda a: pl.BlockSpec((1,) + a.shape[1:], lambda q, b: (q, 0, 0))
    return pl.pallas_call(
        _s5_scan_kernel,
        out_shape=jax.ShapeDtypeStruct((nq, nrows, width), F32),
        grid=(nq, bsz),
        in_specs=[tile, wspec(bcq), wspec(mtq), wspec(ccq), wspec(a_re), wspec(a_im)],
        out_specs=tile,
        scratch_shapes=[pltpu.VMEM((pad + nj, nh), F32), pltpu.VMEM((pad + nj, nh), F32)],
        compiler_params=_cparams("parallel", "parallel"),
        name="s5_scan",
    )(hq, bcq, mtq, ccq, a_re, a_im)


def _s5_post_kernel(x_ref, y_ref, g_ref, sh_ref, sc_ref, gate_ref, dsk_ref, w_ref, o_ref, y_scr, *, d):
    t_len = SSM_CHUNK
    x = x_ref[0]
    h = _norm_mod(x, g_ref[...], sh_ref[0], sc_ref[0])
    nj = y_ref.shape[1]
    for q in range(y_ref.shape[0]):
        for t in range(t_len):
            y_scr[q, pl.ds(t, nj, stride=t_len), :] = y_ref[q, :, t * LANES:(t + 1) * LANES]
    y = jnp.concatenate([y_scr[q] for q in range(y_ref.shape[0])], axis=1)
    yy = y + dsk_ref[...] * h
    gl = jax.nn.gelu(yy).astype(BF16)
    z = jnp.dot(gl, w_ref[...], preferred_element_type=F32)
    o_ref[0] = x + gate_ref[0] * (z[:, :d] * jax.nn.sigmoid(z[:, d:]))


def _s5_post(x, yq, g, sh, sc, gate, d_skip, w_glu, tm=512):
    bsz, seq, d = x.shape
    tm = min(tm, seq)
    nt = seq // tm
    nq = d // LANES
    kern = functools.partial(_s5_post_kernel, d=d)
    mod_spec = pl.BlockSpec((1, 1, d), lambda b, i: (b, 0, 0))
    tile = pl.BlockSpec((1, tm, d), lambda b, i: (b, i, 0))
    return pl.pallas_call(
        kern,
        out_shape=jax.ShapeDtypeStruct((bsz, seq, d), F32),
        grid=(bsz, nt),
        in_specs=[tile,
                  pl.BlockSpec((nq, tm // SSM_CHUNK, SSM_CHUNK * LANES), lambda b, i: (0, b * nt + i, 0)),
                  pl.BlockSpec((1, d), lambda b, i: (0, 0)), mod_spec, mod_spec, mod_spec,
                  pl.BlockSpec((1, d), lambda b, i: (0, 0)),
                  pl.BlockSpec(w_glu.shape, lambda b, i: (0, 0))],
        out_specs=tile,
        scratch_shapes=[pltpu.VMEM((nq, tm, LANES), F32)],
        compiler_params=_cparams("parallel", "parallel"),
        name="s5_post",
    )(x, yq, g, sh, sc, gate, d_skip, w_glu)


def _s5_layer(x, g, sh, sc, gate, lam_re, lam_im, log_step, b_re, b_im, c_re, c_im, d_skip, w_glu):
    bsz, seq, d = x.shape
    ng, nst = lam_re.shape
    t_len, cg = SSM_CHUNK, SSM_GROUP
    nq = d // LANES
    gq = LANES // cg
    nj = seq // t_len
    bcw, mtw, ccw, a_re, a_im = _s5_prep(lam_re, lam_im, log_step, b_re, b_im, c_re, c_im)
    width = t_len * LANES
    bcq = jnp.swapaxes(bcw.reshape(nq, gq, t_len, cg, width), 1, 2).reshape(nq, width, width)
    mtq = jnp.swapaxes(mtw.reshape(nq, gq, t_len, cg, width), 1, 2).reshape(nq, width, width)
    ccq = jnp.swapaxes(ccw.reshape(nq, gq, 2, nst, width), 1, 2).reshape(nq, 2 * gq * nst, width)
    aq_re = jnp.sum(a_re.reshape(nq, gq, 1, gq * nst), axis=1)
    aq_im = jnp.sum(a_im.reshape(nq, gq, 1, gq * nst), axis=1)

    hq = _s5_pre(x, g, sh, sc)
    yq = _s5_scan(hq, bcq, mtq, ccq, aq_re, aq_im, bsz)
    return _s5_post(x, yq, g, sh, sc, gate, d_skip.reshape(1, d).astype(F32), w_glu.astype(BF16))


def kernel(x, c, ada_w, ada_b, norm_g, conv_w_in, conv_w, conv_w_out, attn_w_in, attn_q_gain, attn_k_gain, attn_w_out, rel_bias, ssm_lambda_re, ssm_lambda_im, ssm_log_step, ssm_b_re, ssm_b_im, ssm_c_re, ssm_c_im, ssm_d, ssm_w_glu, ffn_w_gu, ffn_w_down, moe_router_w, moe_router_b, moe_w_gu, moe_w_down):
    bsz, seq, d = x.shape
    depth = ada_w.shape[0]
    mod = _ada_mod(c, ada_w, ada_b).reshape(depth, bsz, 6, 1, d)
    moe_gu = moe_w_gu.reshape((-1,) + moe_w_gu.shape[2:])
    moe_down = moe_w_down.reshape((-1,) + moe_w_down.shape[2:])
    for i in range(depth):
        sh1, sc1, g1, sh2, sc2, g2 = (mod[i, :, r] for r in range(6))
        gn1 = norm_g[i, 0].reshape(1, d)
        gn2 = norm_g[i, 1].reshape(1, d)
        j = i // N_MIXERS
        if i % N_MIXERS == 0:
            x = _conv_layer(x, gn1, sh1, sc1, g1, conv_w_in[j].astype(BF16), conv_w[j],
                            conv_w_out[j].astype(BF16))
        elif i % N_MIXERS == 1:
            x = _attn_layer(x, gn1, sh1, sc1, g1, attn_w_in[j], attn_q_gain[j], attn_k_gain[j],
                            attn_w_out[j], rel_bias)
        else:
            x = _s5_layer(x, gn1, sh1, sc1, g1, ssm_lambda_re[j], ssm_lambda_im[j], ssm_log_step[j],
                          ssm_b_re[j], ssm_b_im[j], ssm_c_re[j], ssm_c_im[j], ssm_d[j], ssm_w_glu[j])
        if i % 2 == 0:
            x = _ffn_layer(x, gn2, sh2, sc2, g2, ffn_w_gu[i // 2].astype(BF16), ffn_w_down[i // 2].astype(BF16))
        else:
            x = _moe_layer(x, gn2, sh2, sc2, g2, moe_router_w[i // 2], moe_router_b[i // 2],
                           moe_gu, moe_down, ebase=(i // 2) * moe_w_gu.shape[1])
    return x
```

```python
import functools
import math

import numpy as np
import jax
import jax.numpy as jnp
from jax import lax
from jax.experimental import pallas as pl
from jax.experimental.pallas import tpu as pltpu

F32 = jnp.float32
BF16 = jnp.bfloat16
I32 = jnp.int32
HIGHEST = lax.Precision.HIGHEST

DEPTH = 4
N_MIXERS = 3
EPS = 1e-6
CONV_WIDTH = 3
N_HEADS = 16
N_KV_HEADS = 4
N_REP = N_HEADS // N_KV_HEADS
HEAD_DIM = 64
IDX_HEADS = 8
IDX_DIM = 64
TOPK_MAX = 256
REL_BUCKETS = 32
REL_MAX_DIST = 128
SSM_GROUP = 16
SSM_STATE = 64
N_EXPERTS = 8
TOP_K_EXPERTS = 2

VMEM_LIMIT_BYTES = 56 * 1024 * 1024
LANES = 128
SUBLANES = 8

INT_MIN = -(2 ** 31)
INT_MAX = 2 ** 31 - 1
NEG_BIG = -1e30
LOG2E = 1.4426950408889634

ATT_TQ = 128
ATT_KC = 256
BIAS_C = 384
BIAS_W = BIAS_C + ATT_KC
ATT_VROWS = 80

SSM_CHUNK = 8


def _cparams(*sem):
    return pltpu.CompilerParams(dimension_semantics=sem, vmem_limit_bytes=VMEM_LIMIT_BYTES)


def _norm_mod(x, g, shift, scale):
    ms = jnp.mean(x * x, axis=-1, keepdims=True)
    y = x * lax.rsqrt(ms + EPS)
    return (y * g) * (1.0 + scale) + shift


def _silu(x):
    return x * jax.nn.sigmoid(x)


def _ada_kernel(c_ref, w_ref, b_ref, o_ref):
    c = c_ref[...]
    cond = _silu(c)
    o_ref[0] = jnp.dot(cond, w_ref[0], preferred_element_type=F32, precision=HIGHEST) + b_ref[0]


def _ada_mod(c, ada_w, ada_b):
    depth, d, d6 = ada_w.shape
    bsz = c.shape[0]
    tn = d
    return pl.pallas_call(
        _ada_kernel,
        out_shape=jax.ShapeDtypeStruct((depth, bsz, d6), F32),
        grid=(depth, d6 // tn),
        in_specs=[
            pl.BlockSpec((bsz, d), lambda i, j: (0, 0)),
            pl.BlockSpec((1, d, tn), lambda i, j: (i, 0, j)),
            pl.BlockSpec((1, 1, tn), lambda i, j: (i, 0, j)),
        ],
        out_specs=pl.BlockSpec((1, bsz, tn), lambda i, j: (i, 0, j)),
        compiler_params=_cparams("parallel", "parallel"),
        name="ada_mod",
    )(c, ada_w, ada_b.reshape(depth, 1, d6))


def _conv_kernel(x_ref, xh_ref, g_ref, sh_ref, sc_ref, gate_ref, win_ref, wc_ref, wout_ref,
                 o_ref, u_scr, *, tm, d):
    i = pl.program_id(1)
    g = g_ref[...]
    sh = sh_ref[0]
    sc = sc_ref[0]
    x = x_ref[0]
    halo = xh_ref.shape[1]
    xe = jnp.concatenate([xh_ref[0], x], axis=0)
    he = _norm_mod(xe, g, sh, sc).astype(BF16)
    z = jnp.dot(he, win_ref[...], preferred_element_type=F32)
    u_all = z[:, d:2 * d] * z[:, 2 * d:]
    row = lax.broadcasted_iota(I32, (halo + tm, 1), 0)
    u_scr[...] = jnp.where(jnp.logical_or(i > 0, row >= halo), u_all, 0.0)
    b_gate = z[halo:, :d]
    wc = wc_ref[...]
    conv = (wc[0:1, :] * u_scr[halo - 2:halo - 2 + tm, :]
            + wc[1:2, :] * u_scr[halo - 1:halo - 1 + tm, :]
            + wc[2:3, :] * u_scr[halo:halo + tm, :])
    y = jnp.dot((b_gate * conv).astype(BF16), wout_ref[...], preferred_element_type=F32)
    o_ref[0] = x + gate_ref[0] * y


def _conv_layer(x, g, sh, sc, gate, w_in, w_conv, w_out, tm=512):
    bsz, seq, d = x.shape
    tm = min(tm, seq)
    nt = seq // tm
    halo = 2 * SUBLANES
    hb = tm // halo
    kern = functools.partial(_conv_kernel, tm=tm, d=d)
    mod_spec = pl.BlockSpec((1, 1, d), lambda b, i: (b, 0, 0))
    return pl.pallas_call(
        kern,
        out_shape=jax.ShapeDtypeStruct((bsz, seq, d), F32),
        grid=(bsz, nt),
        in_specs=[
            pl.BlockSpec((1, tm, d), lambda b, i: (b, i, 0)),
            pl.BlockSpec((1, halo, d), lambda b, i: (b, jnp.maximum(i * hb - 1, 0), 0)),
            pl.BlockSpec((1, d), lambda b, i: (0, 0)),
            mod_spec, mod_spec, mod_spec,
            pl.BlockSpec((d, 3 * d), lambda b, i: (0, 0)),
            pl.BlockSpec((CONV_WIDTH, d), lambda b, i: (0, 0)),
            pl.BlockSpec((d, d), lambda b, i: (0, 0)),
        ],
        out_specs=pl.BlockSpec((1, tm, d), lambda b, i: (b, i, 0)),
        scratch_shapes=[pltpu.VMEM((tm + halo, d), F32)],
        compiler_params=_cparams("parallel", "parallel"),
        name="conv_mixer",
    )(x, x, g, sh, sc, gate, w_in, w_conv, w_out)


def _ffn_kernel(x_ref, g_ref, sh_ref, sc_ref, gate_ref, wgu_ref, wd_ref, o_ref, *, dff, nchunk):
    x = x_ref[0]
    h = _norm_mod(x, g_ref[...], sh_ref[0], sc_ref[0]).astype(BF16)
    cols = dff // nchunk
    acc = jnp.zeros(x.shape, F32)
    for c in range(nchunk):
        gg = jnp.dot(h, wgu_ref[:, c * cols:(c + 1) * cols], preferred_element_type=F32)
        uu = jnp.dot(h, wgu_ref[:, dff + c * cols:dff + (c + 1) * cols], preferred_element_type=F32)
        a = (_silu(gg) * uu).astype(BF16)
        acc = acc + jnp.dot(a, wd_ref[c * cols:(c + 1) * cols, :], preferred_element_type=F32)
    o_ref[0] = x + gate_ref[0] * acc


def _ffn_layer(x, g, sh, sc, gate, w_gu, w_down, tm=512):
    bsz, seq, d = x.shape
    dff = w_down.shape[0]
    tm = min(tm, seq)
    kern = functools.partial(_ffn_kernel, dff=dff, nchunk=2)
    mod_spec = pl.BlockSpec((1, 1, d), lambda b, i: (b, 0, 0))
    return pl.pallas_call(
        kern,
        out_shape=jax.ShapeDtypeStruct((bsz, seq, d), F32),
        grid=(bsz, seq // tm),
        in_specs=[
            pl.BlockSpec((1, tm, d), lambda b, i: (b, i, 0)),
            pl.BlockSpec((1, d), lambda b, i: (0, 0)),
            mod_spec, mod_spec, mod_spec,
            pl.BlockSpec((d, 2 * dff), lambda b, i: (0, 0), pipeline_mode=pl.Buffered(1)),
            pl.BlockSpec((dff, d), lambda b, i: (0, 0), pipeline_mode=pl.Buffered(1)),
        ],
        out_specs=pl.BlockSpec((1, tm, d), lambda b, i: (b, i, 0)),
        compiler_params=_cparams("parallel", "parallel"),
        name="ffn_dense",
    )(x, g, sh, sc, gate, w_gu, w_down)


MOE_SB = 512
MOE_RT = 512
MOE_ALIGN = 16
MOE_WIN = 256
MOE_CAP = MOE_WIN - MOE_ALIGN


def _moe_router_kernel(x_ref, g_ref, sh_ref, sc_ref, rw_ref, rb_ref, h_ref, meta_ref):
    hf = _norm_mod(x_ref[...], g_ref[...], sh_ref[0], sc_ref[0])
    h_ref[...] = hf.astype(BF16)
    logits = jnp.dot(hf, rw_ref[...], preferred_element_type=F32, precision=HIGHEST) + rb_ref[...]
    mx = jnp.max(logits, axis=-1, keepdims=True)
    ex = jnp.exp(logits - mx)
    probs = ex / jnp.sum(ex, axis=-1, keepdims=True)
    lane = lax.broadcasted_iota(I32, probs.shape, 1)
    m1 = jnp.max(probs, axis=-1, keepdims=True)
    i1 = jnp.min(jnp.where(probs == m1, lane, LANES), axis=-1, keepdims=True)
    rest = jnp.where(lane == i1, -1.0, probs)
    m2 = jnp.max(rest, axis=-1, keepdims=True)
    i2 = jnp.min(jnp.where(rest == m2, lane, LANES), axis=-1, keepdims=True)
    den = m1 + m2
    gates = jnp.where(lane == i1, m1 / den, 0.0) + jnp.where(lane == i2, m2 / den, 0.0)
    chosen = jnp.where(jnp.logical_or(lane == i1, lane == i2), 1.0, 0.0)
    nrow = chosen.shape[0]
    below = jnp.where(lax.broadcasted_iota(I32, (nrow, nrow), 1) < lax.broadcasted_iota(I32, (nrow, nrow), 0),
                      1.0, 0.0).astype(BF16)
    rank = jnp.dot(below, chosen.astype(BF16), preferred_element_type=F32)
    meta_ref[...] = gates + pltpu.roll(chosen, N_EXPERTS, 1) + pltpu.roll(rank, 2 * N_EXPERTS, 1)


def _moe_router(xt, g, sh, sc, rw_pad, rb_pad, tpb, tm):
    n, d = xt.shape
    mod_spec = pl.BlockSpec((1, 1, d), lambda i: (i // tpb, 0, 0))
    return pl.pallas_call(
        _moe_router_kernel,
        out_shape=[jax.ShapeDtypeStruct((n, d), BF16), jax.ShapeDtypeStruct((n, LANES), F32)],
        grid=(n // tm,),
        in_specs=[pl.BlockSpec((tm, d), lambda i: (i, 0)),
                  pl.BlockSpec((1, d), lambda i: (0, 0)), mod_spec, mod_spec,
                  pl.BlockSpec((d, LANES), lambda i: (0, 0)),
                  pl.BlockSpec((1, LANES), lambda i: (0, 0))],
        out_specs=[pl.BlockSpec((tm, d), lambda i: (i, 0)), pl.BlockSpec((tm, LANES), lambda i: (i, 0))],
        compiler_params=_cparams("parallel"),
        name="moe_router",
    )(xt, g, sh, sc, rw_pad, rb_pad)


def _window(start, count, w):
    s = start + w * MOE_CAP
    n = jnp.minimum(count - w * MOE_CAP, MOE_CAP)
    a = pl.multiple_of((s // MOE_ALIGN) * MOE_ALIGN, MOE_ALIGN)
    return s, n, a


def _moe_dispatch_kernel(start_ref, cnt_ref, h_ref, pos_ref, xs_init_ref, xs_ref,
                         buf, carry, sems, pending):
    del xs_init_ref
    b = pl.program_id(0)
    nb = pl.num_programs(0)
    ne = buf.shape[0]

    @pl.when(b == 0)
    def _():
        carry[...] = jnp.zeros(carry.shape, BF16)
        for e in range(ne):
            pending[e] = 0

    h = h_ref[...]
    riota = lax.broadcasted_iota(I32, (MOE_WIN, h.shape[0]), 0)

    def out_copy(e, a):
        return pltpu.make_async_copy(buf.at[e], xs_ref.at[pl.ds(a, MOE_WIN)], sems.at[e])

    def wait_pending(e):
        @pl.when(pending[e] == 1)
        def _():
            out_copy(e, 0).wait()

    def onehot_of(e, s, n, a):
        posrow = pos_ref[0, e:e + 1, :]
        rel = jnp.where(jnp.logical_and(posrow >= s, posrow < s + n), posrow - a, -1)
        return jnp.where(rel == riota, 1.0, 0.0).astype(BF16)

    def send_window(e, s, n, a):
        buf[e, 0:MOE_ALIGN, :] = buf[e, 0:MOE_ALIGN, :] + carry[e]
        c0 = pl.multiple_of(((s + n) // MOE_ALIGN) * MOE_ALIGN - a, MOE_ALIGN)
        carry[e] = buf[e, pl.ds(c0, MOE_ALIGN), :]
        out_copy(e, a).start()
        pending[e] = 1

    firsts = [_window(start_ref[b, e], cnt_ref[b, e], 0) for e in range(ne)]
    for e in range(ne):
        wait_pending(e)
    onehots = jnp.concatenate([onehot_of(e, *firsts[e]) for e in range(ne)], axis=0)
    first_rows = jnp.dot(onehots, h, preferred_element_type=F32).astype(BF16)
    for e in range(ne):
        @pl.when(cnt_ref[b, e] > 0)
        def _(e=e):
            buf[e] = first_rows[e * MOE_WIN:(e + 1) * MOE_WIN, :]
            send_window(e, *firsts[e])

    for e in range(ne):
        nwin = (cnt_ref[b, e] + MOE_CAP - 1) // MOE_CAP

        def wbody(w, carry_unused, e=e):
            s, n, a = _window(start_ref[b, e], cnt_ref[b, e], w)
            wait_pending(e)
            buf[e] = jnp.dot(onehot_of(e, s, n, a), h, preferred_element_type=F32).astype(BF16)
            send_window(e, s, n, a)
            return carry_unused

        lax.fori_loop(1, nwin, wbody, 0)

    @pl.when(b == nb - 1)
    def _():
        for e in range(ne):
            @pl.when(pending[e] == 1)
            def _():
                out_copy(e, 0).wait()


def _moe_dispatch(start, cnt, h, pos_t, ncap):
    n, d = h.shape
    nb, ne, sb = pos_t.shape
    xs_init = jnp.zeros((ncap, d), BF16)
    return pl.pallas_call(
        _moe_dispatch_kernel,
        out_shape=jax.ShapeDtypeStruct((ncap, d), BF16),
        grid_spec=pltpu.PrefetchScalarGridSpec(
            num_scalar_prefetch=2,
            grid=(nb,),
            in_specs=[pl.BlockSpec((sb, d), lambda b, *_: (b, 0)),
                      pl.BlockSpec((1, ne, sb), lambda b, *_: (b, 0, 0)),
                      pl.BlockSpec(memory_space=pl.ANY)],
            out_specs=pl.BlockSpec(memory_space=pl.ANY),
            scratch_shapes=[pltpu.VMEM((ne, MOE_WIN, d), BF16),
                            pltpu.VMEM((ne, MOE_ALIGN, d), BF16),
                            pltpu.SemaphoreType.DMA((ne,)),
                            pltpu.SMEM((ne,), I32)],
        ),
        input_output_aliases={4: 0},
        compiler_params=_cparams("arbitrary"),
        name="moe_dispatch",
    )(start, cnt, h, pos_t, xs_init)


MOE_WCHUNKS = 8


def _moe_ffn_kernel(blk_ref, exp_ref, nt_ref, xs_ref, wgu_hbm, wd_hbm, ys_init_ref, ys_ref,
                    wgu_scr, wd_scr, gu_stage, d_stage, sems, *, dff, nchunk, ebase):
    del blk_ref, ys_init_ref
    k = pl.program_id(0)
    d = xs_ref.shape[1]
    gu_rows = d // MOE_WCHUNKS
    d_rows = dff // MOE_WCHUNKS
    e = ebase + exp_ref[k]

    def gu_copy(c, slot):
        return pltpu.make_async_copy(wgu_hbm.at[e, pl.ds(c * gu_rows, gu_rows), :], gu_stage.at[slot],
                                     sems.at[0, slot])

    def d_copy(c, slot):
        return pltpu.make_async_copy(wd_hbm.at[e, pl.ds(c * d_rows, d_rows), :], d_stage.at[slot],
                                     sems.at[1, slot])

    new_expert = jnp.logical_or(k == 0, exp_ref[k] != exp_ref[jnp.maximum(k - 1, 0)])

    @pl.when(jnp.logical_and(k < nt_ref[0], new_expert))
    def _():
        gu_copy(0, 0).start()
        d_copy(0, 0).start()
        for c in range(MOE_WCHUNKS):
            slot = c % 2
            if c + 1 < MOE_WCHUNKS:
                gu_copy(c + 1, 1 - slot).start()
                d_copy(c + 1, 1 - slot).start()
            gu_copy(c, slot).wait()
            wgu_scr[c * gu_rows:(c + 1) * gu_rows, :] = gu_stage[slot].astype(BF16)
            d_copy(c, slot).wait()
            wd_scr[c * d_rows:(c + 1) * d_rows, :] = d_stage[slot].astype(BF16)

    @pl.when(k < nt_ref[0])
    def _():
        x = xs_ref[...]
        cols = dff // nchunk
        acc = jnp.zeros(x.shape, F32)
        for c in range(nchunk):
            gg = jnp.dot(x, wgu_scr[:, c * cols:(c + 1) * cols], preferred_element_type=F32)
            uu = jnp.dot(x, wgu_scr[:, dff + c * cols:dff + (c + 1) * cols], preferred_element_type=F32)
            a = (_silu(gg) * uu).astype(BF16)
            acc = acc + jnp.dot(a, wd_scr[c * cols:(c + 1) * cols, :], preferred_element_type=F32)
        ys_ref[...] = acc.astype(BF16)


def _moe_ffn(tile_blk, tile_exp, ntiles, xs, w_gu, w_down, ebase):
    ncap, d = xs.shape
    dff = w_down.shape[1]
    assert d % (MOE_WCHUNKS * 2 * SUBLANES) == 0 and dff % (MOE_WCHUNKS * 2 * SUBLANES) == 0
    kern = functools.partial(_moe_ffn_kernel, dff=dff, nchunk=2, ebase=ebase)
    ys_init = jnp.zeros((ncap, d), BF16)
    return pl.pallas_call(
        kern,
        out_shape=jax.ShapeDtypeStruct((ncap, d), BF16),
        grid_spec=pltpu.PrefetchScalarGridSpec(
            num_scalar_prefetch=3,
            grid=(ncap // MOE_RT,),
            in_specs=[pl.BlockSpec((MOE_RT, d), lambda k, blk, exp, nt: (blk[k], 0)),
                      pl.BlockSpec(memory_space=pl.ANY),
                      pl.BlockSpec(memory_space=pl.ANY),
                      pl.BlockSpec(memory_space=pl.ANY)],
            out_specs=pl.BlockSpec((MOE_RT, d), lambda k, blk, exp, nt: (blk[k], 0)),
            scratch_shapes=[pltpu.VMEM((d, 2 * dff), BF16),
                            pltpu.VMEM((dff, d), BF16),
                            pltpu.VMEM((2, d // MOE_WCHUNKS, 2 * dff), F32),
                            pltpu.VMEM((2, dff // MOE_WCHUNKS, d), F32),
                            pltpu.SemaphoreType.DMA((2, 2))],
        ),
        input_output_aliases={6: 0},
        compiler_params=_cparams("arbitrary"),
        name="moe_ffn",
    )(tile_blk, tile_exp, ntiles, xs, w_gu, w_down, ys_init)


def _moe_combine_kernel(start_ref, cnt_ref, x_ref, gate_ref, pos_ref, gts_ref, ys_ref, o_ref,
                        buf, sems, acc_scr):
    b = pl.program_id(0)
    nb = pl.num_programs(0)
    ne = buf.shape[1]
    sb = x_ref.shape[0]
    slot = b % 2

    def in_copy(sl, e, a):
        return pltpu.make_async_copy(ys_ref.at[pl.ds(a, MOE_WIN)], buf.at[sl, e], sems.at[sl, e])

    def start_first_windows(blk, sl):
        for e in range(ne):
            @pl.when(cnt_ref[blk, e] > 0)
            def _():
                _, _, a = _window(start_ref[blk, e], cnt_ref[blk, e], 0)
                in_copy(sl, e, a).start()

    @pl.when(b == 0)
    def _():
        start_first_windows(0, 0)

    @pl.when(b + 1 < nb)
    def _():
        start_first_windows(b + 1, 1 - slot)

    acc_scr[...] = jnp.zeros(acc_scr.shape, F32)
    liota = lax.broadcasted_iota(I32, (sb, MOE_WIN), 1)
    for e in range(ne):
        poscol = pos_ref[:, e:e + 1]
        gcol = gts_ref[:, e:e + 1]
        nwin = (cnt_ref[b, e] + MOE_CAP - 1) // MOE_CAP

        def wbody(w, carry_unused, e=e, poscol=poscol, gcol=gcol):
            s, n, a = _window(start_ref[b, e], cnt_ref[b, e], w)

            @pl.when(w > 0)
            def _():
                in_copy(slot, e, a).start()

            in_copy(slot, e, a).wait()
            rel = jnp.where(jnp.logical_and(poscol >= s, poscol < s + n), poscol - a, -1)
            onehot = jnp.where(rel == liota, 1.0, 0.0).astype(BF16)
            acc_scr[...] += gcol * jnp.dot(onehot, buf[slot, e], preferred_element_type=F32)
            return carry_unused

        lax.fori_loop(0, nwin, wbody, 0)

    o_ref[...] = x_ref[...] + gate_ref[0] * acc_scr[...]


def _moe_combine(start, cnt, xt, gate, pos_n, gates_n, ys, tpb):
    n, d = xt.shape
    nb, ne = cnt.shape
    sb = n // nb
    return pl.pallas_call(
        _moe_combine_kernel,
        out_shape=jax.ShapeDtypeStruct((n, d), F32),
        grid_spec=pltpu.PrefetchScalarGridSpec(
            num_scalar_prefetch=2,
            grid=(nb,),
            in_specs=[pl.BlockSpec((sb, d), lambda b, *_: (b, 0)),
                      pl.BlockSpec((1, 1, d), lambda b, *_: (b // tpb, 0, 0)),
                      pl.BlockSpec((sb, ne), lambda b, *_: (b, 0)),
                      pl.BlockSpec((sb, ne), lambda b, *_: (b, 0)),
                      pl.BlockSpec(memory_space=pl.ANY)],
            out_specs=pl.BlockSpec((sb, d), lambda b, *_: (b, 0)),
            scratch_shapes=[pltpu.VMEM((2, ne, MOE_WIN, d), BF16),
                            pltpu.SemaphoreType.DMA((2, ne)),
                            pltpu.VMEM((sb, d), F32)],
        ),
        compiler_params=_cparams("arbitrary"),
        name="moe_combine",
    )(start, cnt, xt, gate, pos_n, gates_n, ys)


def _moe_layer(x, g, sh, sc, gate, router_w, router_b, w_gu, w_down, ebase=0):
    bsz, seq, d = x.shape
    ne = router_w.shape[1]
    n = bsz * seq
    sb = min(MOE_SB, seq)
    tpb = seq // sb
    nb = n // sb
    rt = MOE_RT
    xt = x.reshape(n, d)
    rw_pad = jnp.pad(router_w.astype(F32), ((0, 0), (0, LANES - ne)))
    rb_pad = jnp.pad(router_b.astype(F32).reshape(1, ne), ((0, 0), (0, LANES - ne)), constant_values=NEG_BIG)
    h, meta = _moe_router(xt, g, sh, sc, rw_pad, rb_pad, tpb, sb)
    gates = meta[:, :ne]
    sel = (meta[:, ne:2 * ne] > 0.5).astype(I32)

    selb = sel.reshape(nb, sb, ne)
    cnt = jnp.sum(selb, axis=1)
    rank = meta[:, 2 * ne:3 * ne].astype(I32).reshape(nb, sb, ne)
    total = jnp.sum(cnt, axis=0)
    region = ((total + MOE_WIN + rt - 1) // rt) * rt
    off = jnp.cumsum(region) - region
    start = (off[None, :] + jnp.cumsum(cnt, axis=0) - cnt).astype(I32)
    pos = jnp.where(selb > 0, start[:, None, :] + rank, -1).astype(I32)
    ncap = 2 * n + ne * (MOE_WIN + rt)
    tiles_e = (total + rt - 1) // rt
    tcum = jnp.cumsum(tiles_e)
    ntiles = tcum[-1]
    kk = jnp.minimum(jnp.arange(ncap // rt), ntiles - 1)
    tile_exp = jnp.sum((kk[:, None] >= tcum[None, :]).astype(I32), axis=1)
    tile_blk = (off[tile_exp] // rt + kk - (tcum - tiles_e)[tile_exp]).astype(I32)

    xs = _moe_dispatch(start, cnt.astype(I32), h, jnp.swapaxes(pos, 1, 2), ncap)
    ys = _moe_ffn(tile_blk, tile_exp, ntiles.reshape(1).astype(I32), xs, w_gu, w_down, ebase)
    out = _moe_combine(start, cnt.astype(I32), xt, gate, pos.reshape(n, ne), gates, ys, tpb)
    return out.reshape(bsz, seq, d)


def _head_norm(q, hsum_ref, hexp_ref, gain, scale):
    ms = jnp.dot((q * q).astype(BF16), hsum_ref[...], preferred_element_type=F32)
    r = lax.rsqrt(ms + EPS)
    r_hi = r.astype(BF16)
    r_lo = (r - r_hi.astype(F32)).astype(BF16)
    rexp = (jnp.dot(r_hi, hexp_ref[...], preferred_element_type=F32)
            + jnp.dot(r_lo, hexp_ref[...], preferred_element_type=F32))
    return q * rexp * (gain * scale)


def _qkv_kernel(x_ref, g_ref, sh_ref, sc_ref, wm_ref, ws_ref, qg_ref, kg_ref,
                hsq_ref, heq_ref, hsk_ref, hek_ref,
                q_ref, k_ref, v_ref, qi_ref, kiwi_ref, *, dq, dk, dqi):
    x = x_ref[0]
    h = _norm_mod(x, g_ref[...], sh_ref[0], sc_ref[0]).astype(BF16)
    z = jnp.dot(h, wm_ref[...], preferred_element_type=F32)
    q = z[:, :dq]
    k = z[:, dq:dq + dk]
    v = z[:, dq + dk:dq + 2 * dk]
    qi = z[:, dq + 2 * dk:dq + 2 * dk + dqi]
    q_ref[0] = _head_norm(q, hsq_ref, heq_ref, qg_ref[...], HEAD_DIM ** -0.5 * LOG2E).astype(BF16)
    k_ref[0] = _head_norm(k, hsk_ref, hek_ref, kg_ref[...], 1.0).astype(BF16)
    v_ref[0] = v.astype(BF16)
    qi_ref[0] = qi.astype(BF16)
    kiwi_ref[0] = jnp.dot(h, ws_ref[...], preferred_element_type=F32)


def _head_indicators(nheads):
    hs = np.zeros((nheads * HEAD_DIM, LANES), np.float32)
    he = np.zeros((LANES, nheads * HEAD_DIM), np.float32)
    for hd in range(nheads):
        hs[hd * HEAD_DIM:(hd + 1) * HEAD_DIM, hd] = 1.0 / HEAD_DIM
        he[hd, hd * HEAD_DIM:(hd + 1) * HEAD_DIM] = 1.0
    return jnp.asarray(hs, BF16), jnp.asarray(he, BF16)


def _qkv_proj(x, g, sh, sc, w_main, w_small, q_gain, k_gain, tm=512):
    bsz, seq, d = x.shape
    tm = min(tm, seq)
    dq = N_HEADS * HEAD_DIM
    dk = N_KV_HEADS * HEAD_DIM
    dqi = IDX_HEADS * IDX_DIM
    hsq, heq = _head_indicators(N_HEADS)
    hsk, hek = _head_indicators(N_KV_HEADS)
    qg = jnp.tile(q_gain.reshape(1, HEAD_DIM), (1, N_HEADS)).astype(F32)
    kg = jnp.tile(k_gain.reshape(1, HEAD_DIM), (1, N_KV_HEADS)).astype(F32)
    kern = functools.partial(_qkv_kernel, dq=dq, dk=dk, dqi=dqi)
    mod_spec = pl.BlockSpec((1, 1, d), lambda b, i: (b, 0, 0))

    def full(a):
        return pl.BlockSpec(a.shape, lambda b, i: (0,) * a.ndim)

    def out(n):
        return pl.BlockSpec((1, tm, n), lambda b, i: (b, i, 0))

    return pl.pallas_call(
        kern,
        out_shape=[
            jax.ShapeDtypeStruct((bsz, seq, dq), BF16),
            jax.ShapeDtypeStruct((bsz, seq, dk), BF16),
            jax.ShapeDtypeStruct((bsz, seq, dk), BF16),
            jax.ShapeDtypeStruct((bsz, seq, dqi), BF16),
            jax.ShapeDtypeStruct((bsz, seq, LANES), F32),
        ],
        grid=(bsz, seq // tm),
        in_specs=[
            pl.BlockSpec((1, tm, d), lambda b, i: (b, i, 0)),
            pl.BlockSpec((1, d), lambda b, i: (0, 0)),
            mod_spec, mod_spec,
            full(w_main), full(w_small), full(qg), full(kg),
            full(hsq), full(heq), full(hsk), full(hek),
        ],
        out_specs=[out(dq), out(dk), out(dk), out(dqi), out(LANES)],
        compiler_params=_cparams("parallel", "parallel"),
        name="attn_qkv",
    )(x, g, sh, sc, w_main, w_small, qg, kg, hsq, heq, hsk, hek)


def _rel_bucket_np(dist):
    max_exact = REL_BUCKETS // 2
    d = np.maximum(dist, 1).astype(np.float64)
    large = max_exact + (np.log(d / max_exact) / math.log(REL_MAX_DIST / max_exact)
                         * (REL_BUCKETS - max_exact)).astype(np.int32)
    large = np.minimum(large, REL_BUCKETS - 1)
    return np.where(dist < max_exact, dist, large).astype(np.int32)


def _bias_table_kernel(bucket_ref, rb_ref, o_ref):
    hd = pl.program_id(0)
    bucket = bucket_ref[...]
    acc = jnp.zeros(bucket.shape, F32)
    for b in range(REL_BUCKETS):
        acc = jnp.where(bucket == b, rb_ref[b, hd] * LOG2E, acc)
    o_ref[0] = acc


def _bias_table(rel_bias):
    w = np.arange(BIAS_W)[:, None]
    i = np.arange(ATT_TQ)[None, :]
    bucket = jnp.asarray(_rel_bucket_np(np.maximum(i - w + BIAS_C, 0)))
    return pl.pallas_call(
        _bias_table_kernel,
        out_shape=jax.ShapeDtypeStruct((N_HEADS, BIAS_W, ATT_TQ), F32),
        grid=(N_HEADS,),
        in_specs=[
            pl.BlockSpec((BIAS_W, ATT_TQ), lambda hd: (0, 0)),
            pl.BlockSpec(memory_space=pltpu.SMEM),
        ],
        out_specs=pl.BlockSpec((1, BIAS_W, ATT_TQ), lambda hd: (hd, 0, 0)),
        compiler_params=_cparams("arbitrary"),
        name="attn_bias_table",
    )(bucket, rel_bias.astype(F32))


def _attn_kernel(qT_ref, qiT_ref, wiT_ref, k_ref, vT_ref, ki_ref, pt_ref, o_ref,
                 keys_scr, negm_scr, pidx_scr, oT_scr, acc_scr, qall_scr, sa_scr, sb_scr, *, top_k):
    tq, kc = ATT_TQ, ATT_KC
    qt = pl.program_id(1)
    q0 = qt * tq
    nch = (q0 + tq + kc - 1) // kc
    tpos = q0 + lax.broadcasted_iota(I32, (kc, tq), 1)
    srow = lax.broadcasted_iota(I32, (kc, tq), 0)

    qiT = qiT_ref[0]
    qi_all = jnp.concatenate([qiT[hd * IDX_DIM:(hd + 1) * IDX_DIM, :] for hd in range(IDX_HEADS)], axis=1)
    wiT = wiT_ref[0]

    def score_chunk(c):
        ks = pl.multiple_of(c * kc, kc)
        kic = ki_ref[0, pl.ds(ks, kc), :]
        dots = jnp.dot(kic, qi_all, preferred_element_type=F32)
        acc = jnp.zeros((kc, tq), F32)
        for hd in range(IDX_HEADS):
            acc = acc + jnp.maximum(dots[:, hd * tq:(hd + 1) * tq], 0.0) * wiT[hd:hd + 1, :]
        acc = jnp.where(acc == 0.0, 0.0, acc)
        bits = pltpu.bitcast(acc, I32)
        key = jnp.where(bits < 0, bits ^ INT_MAX, bits)
        key = jnp.where(ks + srow <= tpos, key, INT_MIN)
        keys_scr[pl.ds(ks, kc), :] = key

    npair = (nch + 1) // 2

    def score_pair(i, carry):
        score_chunk(2 * i)
        score_chunk(2 * i + 1)
        return carry

    lax.fori_loop(0, npair, score_pair, 0)

    @pl.when(nch % 2 == 1)
    def _():
        negm_scr[pl.ds(pl.multiple_of(nch * kc, kc), kc), :] = jnp.full((kc, tq), NEG_BIG, F32)

    srow2 = lax.broadcasted_iota(I32, (2 * kc, tq), 0)

    def count(pred):
        def body(c, acc):
            ks = pl.multiple_of(c * 2 * kc, 2 * kc)
            m = pred(keys_scr[pl.ds(ks, 2 * kc), :], ks + srow2).astype(I32)
            return acc + jnp.sum(m.reshape(2 * kc // SUBLANES, SUBLANES, tq), axis=0)
        acc = lax.fori_loop(0, npair, body, jnp.zeros((SUBLANES, tq), I32))
        return jnp.sum(acc, axis=0, keepdims=True)

    def bit_body(it, p):
        cand_p = p | lax.shift_left(jnp.int32(1), 31 - it)
        cand = cand_p ^ INT_MIN
        cnt = count(lambda k, s: k >= cand)
        return jnp.where(cnt >= top_k, cand_p, p)

    p_fin = lax.fori_loop(0, 32, bit_body, jnp.zeros((1, tq), I32))
    v = p_fin ^ INT_MIN

    cnt_gt = count(lambda k, s: k > v)
    cnt_eq = count(lambda k, s: k == v)
    need = top_k - cnt_gt
    pidx_scr[...] = jnp.full((1, tq), INT_MAX, I32)
    pos_bits = (keys_scr.shape[0] - 1).bit_length()

    @pl.when(jnp.max(cnt_eq - need) > 0)
    def _():
        def ibit(it, p):
            cand = p | lax.shift_left(jnp.int32(1), pos_bits - 1 - it)
            cnt = count(lambda k, s: jnp.logical_and(k == v, s < cand))
            return jnp.where(cnt < need, cand, p)
        pidx_scr[...] = lax.fori_loop(0, pos_bits, ibit, jnp.zeros((1, tq), I32))

    pidx = pidx_scr[...]

    def mask_chunk(c, carry):
        ks = pl.multiple_of(c * kc, kc)
        k = keys_scr[pl.ds(ks, kc), :]
        spos = ks + srow
        sel = jnp.logical_or(k > v, jnp.logical_and(k == v, spos <= pidx))
        sel = jnp.logical_and(sel, spos <= tpos)
        negm_scr[pl.ds(ks, kc), :] = jnp.where(sel, 0.0, NEG_BIG)
        return carry

    lax.fori_loop(0, nch, mask_chunk, 0)

    acc_scr[...] = jnp.zeros(acc_scr.shape, F32)
    for n in range(N_KV_HEADS):
        r0 = n * N_REP * HEAD_DIM
        qall_scr[n] = jnp.concatenate(
            [qT_ref[0, r0 + g * HEAD_DIM:r0 + (g + 1) * HEAD_DIM, :] for g in range(N_REP)], axis=1)

    def qk_chunk(c, s_ref):
        ks = pl.multiple_of(c * kc, kc)
        for n in range(N_KV_HEADS):
            s_ref[n] = jnp.dot(k_ref[0, n, pl.ds(ks, kc), :], qall_scr[n], preferred_element_type=F32)

    def softmax_pv(c, s_ref, ms, far):
        ks = pl.multiple_of(c * kc, kc)
        negm = negm_scr[pl.ds(ks, kc), :]
        w0 = pl.multiple_of(jnp.clip(BIAS_C - (q0 - ks), 0, BIAS_C), LANES)
        new_ms = []
        for n in range(N_KV_HEADS):
            s = s_ref[n]
            if far:
                cvec = jnp.concatenate([pt_ref[n * N_REP + g, 0:1, :] for g in range(N_REP)], axis=1)
                lg = jnp.concatenate([s[:, g * tq:(g + 1) * tq] + negm for g in range(N_REP)], axis=1)
                m_new = jnp.maximum(ms[n], jnp.max(lg, axis=0, keepdims=True) + cvec)
                p = jnp.exp2(lg - (m_new - cvec))
            else:
                lg = jnp.concatenate(
                    [s[:, g * tq:(g + 1) * tq] + pt_ref[n * N_REP + g, pl.ds(w0, kc), :] + negm
                     for g in range(N_REP)], axis=1)
                m_new = jnp.maximum(ms[n], jnp.max(lg, axis=0, keepdims=True))
                p = jnp.exp2(lg - m_new)
            alpha = jnp.exp2(ms[n] - m_new)
            acc_scr[n] = alpha * acc_scr[n] + jnp.dot(vT_ref[0, c, n], p.astype(BF16),
                                                      preferred_element_type=F32)
            new_ms.append(m_new)
        return tuple(new_ms)

    last_chunk = k_ref.shape[2] // kc - 1

    def pair_step(i, ms, far):
        c0 = 2 * i
        qk_chunk(c0 + 1, sb_scr)
        ms = softmax_pv(c0, sa_scr, ms, far)
        qk_chunk(jnp.minimum(c0 + 2, last_chunk), sa_scr)
        return softmax_pv(c0 + 1, sb_scr, ms, far)

    n_far = jnp.clip((q0 - BIAS_C + kc) // kc, 0, nch)
    ms = tuple(jnp.full((1, N_REP * tq), NEG_BIG, F32) for _ in range(N_KV_HEADS))
    qk_chunk(0, sa_scr)
    ms = lax.fori_loop(0, n_far // 2, functools.partial(pair_step, far=True), ms)
    lax.fori_loop(n_far // 2, npair, functools.partial(pair_step, far=False), ms)
    for n in range(N_KV_HEADS):
        o_t = acc_scr[n, 0:HEAD_DIM, :] / acc_scr[n, HEAD_DIM:HEAD_DIM + 1, :]
        for g in range(N_REP):
            r0 = (n * N_REP + g) * HEAD_DIM
            oT_scr[r0:r0 + HEAD_DIM, :] = o_t[:, g * tq:(g + 1) * tq]

    o_ref[0] = oT_scr[...].T.astype(BF16)


def _attention(qT, qiT, wiT, k4, vT, ki, ptab, top_k):
    bsz, dq, seq = qT.shape
    tq = ATT_TQ
    assert seq % (2 * ATT_KC) == 0
    kern = functools.partial(_attn_kernel, top_k=top_k)
    return pl.pallas_call(
        kern,
        out_shape=jax.ShapeDtypeStruct((bsz, seq, dq), BF16),
        grid=(bsz, seq // tq),
        in_specs=[
            pl.BlockSpec((1, dq, tq), lambda b, i: (b, 0, i)),
            pl.BlockSpec((1, qiT.shape[1], tq), lambda b, i: (b, 0, i)),
            pl.BlockSpec((1, wiT.shape[1], tq), lambda b, i: (b, 0, i)),
            pl.BlockSpec((1,) + k4.shape[1:], lambda b, i: (b, 0, 0, 0)),
            pl.BlockSpec((1,) + vT.shape[1:], lambda b, i: (b, 0, 0, 0, 0)),
            pl.BlockSpec((1,) + ki.shape[1:], lambda b, i: (b, 0, 0)),
            pl.BlockSpec(ptab.shape, lambda b, i: (0, 0, 0)),
        ],
        out_specs=pl.BlockSpec((1, tq, dq), lambda b, i: (b, i, 0)),
        scratch_shapes=[
            pltpu.VMEM((seq, tq), I32),
            pltpu.VMEM((seq, tq), F32),
            pltpu.VMEM((1, tq), I32),
            pltpu.VMEM((dq, tq), F32),
            pltpu.VMEM((N_KV_HEADS, ATT_VROWS, N_REP * tq), F32),
            pltpu.VMEM((N_KV_HEADS, HEAD_DIM, N_REP * tq), BF16),
            pltpu.VMEM((N_KV_HEADS, ATT_KC, N_REP * tq), F32),
            pltpu.VMEM((N_KV_HEADS, ATT_KC, N_REP * tq), F32),
        ],
        compiler_params=_cparams("parallel", "arbitrary"),
        name="attn_core",
    )(qT, qiT, wiT, k4, vT, ki, ptab)


def _proj_res_kernel(a_ref, x_ref, gate_ref, w_ref, o_ref):
    y = jnp.dot(a_ref[0], w_ref[...], preferred_element_type=F32)
    o_ref[0] = x_ref[0] + gate_ref[0] * y


def _proj_residual(a, x, gate, w, tm=512):
    bsz, seq, d = x.shape
    tm = min(tm, seq)
    return pl.pallas_call(
        _proj_res_kernel,
        out_shape=jax.ShapeDtypeStruct((bsz, seq, d), F32),
        grid=(bsz, seq // tm),
        in_specs=[
            pl.BlockSpec((1, tm, a.shape[2]), lambda b, i: (b, i, 0)),
            pl.BlockSpec((1, tm, d), lambda b, i: (b, i, 0)),
            pl.BlockSpec((1, 1, d), lambda b, i: (b, 0, 0)),
            pl.BlockSpec(w.shape, lambda b, i: (0, 0)),
        ],
        out_specs=pl.BlockSpec((1, tm, d), lambda b, i: (b, i, 0)),
        compiler_params=_cparams("parallel", "parallel"),
        name="proj_residual",
    )(a, x, gate, w)


def _attn_layer(x, g, sh, sc, gate, w_in, q_gain, k_gain, w_out, rel_bias):
    bsz, seq, d = x.shape
    top_k = min(TOPK_MAX, seq // 4)
    dq = N_HEADS * HEAD_DIM
    dk = N_KV_HEADS * HEAD_DIM
    dqi = IDX_HEADS * IDX_DIM
    nmain = dq + 2 * dk + dqi
    w_main = w_in[:, :nmain].astype(BF16)
    w_small = jnp.pad(w_in[:, nmain:], ((0, 0), (0, LANES - (IDX_DIM + IDX_HEADS)))).astype(BF16)
    q, k, v, qi, kiwi = _qkv_proj(x, g, sh, sc, w_main, w_small, q_gain, k_gain)
    ki = kiwi[:, :, :IDX_DIM].astype(BF16)
    wi = kiwi[:, :, IDX_DIM:IDX_DIM + IDX_HEADS] * (IDX_HEADS ** -0.5 * IDX_DIM ** -0.5)
    qT = jnp.swapaxes(q, 1, 2)
    qiT = jnp.swapaxes(qi, 1, 2)
    wiT = jnp.swapaxes(wi, 1, 2)
    nck = seq // ATT_KC
    vT = jnp.swapaxes(v.reshape(bsz, nck, ATT_KC, dk), 2, 3).reshape(bsz, nck, N_KV_HEADS, HEAD_DIM, ATT_KC)
    vT = jnp.concatenate([
        vT, jnp.ones((bsz, nck, N_KV_HEADS, 1, ATT_KC), BF16),
        jnp.zeros((bsz, nck, N_KV_HEADS, ATT_VROWS - HEAD_DIM - 1, ATT_KC), BF16)], axis=3)
    k4 = jnp.swapaxes(k.reshape(bsz, seq, N_KV_HEADS, HEAD_DIM), 1, 2)
    ptab = _bias_table(rel_bias)
    attn = _attention(qT, qiT, wiT, k4, vT, ki, ptab, top_k)
    return _proj_residual(attn, x, gate, w_out.astype(BF16))


def _s5_prep_kernel(lre_ref, lim_ref, ls_ref, bre_ref, bim_ref, cre_ref, cim_ref,
                    bcw_ref, mtw_ref, ccw_ref, are_ref, aim_ref):
    t_len = SSM_CHUNK
    lre = jnp.minimum(lre_ref[0], -1e-4)
    lim = lim_ref[0]
    step = jnp.exp(ls_ref[0])
    ar = lre * step
    ai = lim * step

    def powers(jv):
        mag = jnp.exp(jv * ar)
        return mag * jnp.cos(jv * ai), mag * jnp.sin(jv * ai)

    lb_re, lb_im = powers(1.0)
    nr = lb_re - 1.0
    ni = lb_im
    den = lre * lre + lim * lim
    cf_re = (nr * lre + ni * lim) / den
    cf_im = (ni * lre - nr * lim) / den
    bre = bre_ref[0]
    bim = bim_ref[0]
    bb_re = cf_re * bre - cf_im * bim
    bb_im = cf_re * bim + cf_im * bre
    cre = cre_ref[0]
    cim = cim_ref[0]
    nst = lre.shape[-1]
    jv = lax.broadcasted_iota(I32, (t_len, 1, nst), 0).astype(F32)
    pj_re, pj_im = powers(jv)
    a_re = (cre[None] * pj_re - cim[None] * pj_im).reshape(t_len * SSM_GROUP, nst)
    a_im = (cre[None] * pj_im + cim[None] * pj_re).reshape(t_len * SSM_GROUP, nst)
    dn = (((1,), (1,)), ((), ()))
    cg = SSM_GROUP
    tc = t_len * cg
    width = t_len * LANES
    gq = pl.program_id(0) % (LANES // cg)
    nh = (LANES // cg) * nst
    cg_shift = cg.bit_length() - 1

    def place(nrows, target):
        r = lax.broadcasted_iota(I32, (nrows, width), 0)
        col = lax.broadcasted_iota(I32, (nrows, width), 1)
        return jnp.where(col == target(r), 1.0, 0.0).astype(BF16)

    pm = place(tc, lambda r: lax.shift_right_logical(r, cg_shift) * LANES + gq * cg + (r & (cg - 1)))
    k_t = (lax.dot_general(bb_re, a_re, dn, preferred_element_type=F32, precision=HIGHEST)
           - lax.dot_general(bb_im, a_im, dn, preferred_element_type=F32, precision=HIGHEST))
    lane = lax.broadcasted_iota(I32, (cg, tc), 1)
    mt_rows = [k_t] + [jnp.where(lane >= s * cg, pltpu.roll(k_t, s * cg, 1), 0.0) for s in range(1, t_len)]
    mt_t = jnp.concatenate(mt_rows, axis=0)
    mtw_ref[0] = jnp.dot(mt_t.astype(BF16), pm, preferred_element_type=F32).astype(BF16)
    pr_re, pr_im = powers((t_len - 1.0) - jv)
    bc_re = (bb_re[None] * pr_re - bb_im[None] * pr_im).reshape(tc, nst)
    bc_im = (bb_re[None] * pr_im + bb_im[None] * pr_re).reshape(tc, nst)
    pb_re = place(nst, lambda r: gq * nst + r)
    pb_im = place(nst, lambda r: nh + gq * nst + r)
    bcw_ref[0] = (jnp.dot(bc_re.astype(BF16), pb_re, preferred_element_type=F32)
                  + jnp.dot(bc_im.astype(BF16), pb_im, preferred_element_type=F32)).astype(BF16)
    pn_re, pn_im = powers(jv + 1.0)
    cc_re = (cre[None] * pn_re - cim[None] * pn_im).reshape(tc, nst)
    cc_im = (cre[None] * pn_im + cim[None] * pn_re).reshape(tc, nst)
    eye = jnp.where(lax.broadcasted_iota(I32, (nst, nst), 0) == lax.broadcasted_iota(I32, (nst, nst), 1),
                    1.0, 0.0)
    cct_re = lax.dot_general(eye, cc_re, dn, preferred_element_type=F32, precision=HIGHEST)
    cct_im = lax.dot_general(eye, cc_im, dn, preferred_element_type=F32, precision=HIGHEST)
    ccw_ref[0, 0:nst, :] = jnp.dot(cct_re.astype(BF16), pm, preferred_element_type=F32).astype(BF16)
    ccw_ref[0, nst:2 * nst, :] = (-jnp.dot(cct_im.astype(BF16), pm, preferred_element_type=F32)).astype(BF16)
    at_re, at_im = powers(float(t_len))
    rs = lax.broadcasted_iota(I32, (nst, nh), 0)
    cs = lax.broadcasted_iota(I32, (nst, nh), 1)
    pa = jnp.where(cs == gq * nst + rs, 1.0, 0.0)
    are_ref[0] = jnp.dot(at_re, pa, preferred_element_type=F32, precision=HIGHEST)
    aim_ref[0] = jnp.dot(at_im, pa, preferred_element_type=F32, precision=HIGHEST)


def _s5_prep(lam_re, lam_im, log_step, b_re, b_im, c_re, c_im):
    ng, nst = lam_re.shape
    tc = SSM_CHUNK * SSM_GROUP
    vec = pl.BlockSpec((1, 1, nst), lambda gi: (gi, 0, 0))
    mat = pl.BlockSpec((1, SSM_GROUP, nst), lambda gi: (gi, 0, 0))
    width = SSM_CHUNK * LANES
    assert tc == 2 * nst and 2 * (LANES // SSM_GROUP) * nst == width
    big = pl.BlockSpec((1, tc, width), lambda gi: (gi, 0, 0))
    wide = jax.ShapeDtypeStruct((ng, tc, width), BF16)
    return pl.pallas_call(
        _s5_prep_kernel,
        out_shape=[wide, wide, wide,
                   jax.ShapeDtypeStruct((ng, 1, width // 2), F32),
                   jax.ShapeDtypeStruct((ng, 1, width // 2), F32)],
        grid=(ng,),
        in_specs=[vec, vec, pl.BlockSpec((1, 1, 1), lambda gi: (gi, 0, 0)), mat, mat, mat, mat],
        out_specs=[big, big, big, pl.BlockSpec((1, 1, width // 2), lambda gi: (gi, 0, 0)),
                   pl.BlockSpec((1, 1, width // 2), lambda gi: (gi, 0, 0))],
        compiler_params=_cparams("parallel"),
        name="s5_prep",
    )(lam_re.reshape(ng, 1, nst), lam_im.reshape(ng, 1, nst), log_step.reshape(ng, 1, 1),
      jnp.swapaxes(b_re, 1, 2), jnp.swapaxes(b_im, 1, 2), c_re, c_im)


def _s5_pre_kernel(x_ref, g_ref, sh_ref, sc_ref, o_ref, h_scr):
    t_len = SSM_CHUNK
    h = _norm_mod(x_ref[0], g_ref[...], sh_ref[0], sc_ref[0])
    nj = o_ref.shape[1]
    for q in range(o_ref.shape[0]):
        h_scr[q] = h[:, q * LANES:(q + 1) * LANES]
        for t in range(t_len):
            o_ref[q, :, t * LANES:(t + 1) * LANES] = h_scr[q, pl.ds(t, nj, stride=t_len), :].astype(BF16)


def _s5_pre(x, g, sh, sc, tm=512):
    bsz, seq, d = x.shape
    tm = min(tm, seq)
    nt = seq // tm
    nq = d // LANES
    t_len = SSM_CHUNK
    mod_spec = pl.BlockSpec((1, 1, d), lambda b, i: (b, 0, 0))
    return pl.pallas_call(
        _s5_pre_kernel,
        out_shape=jax.ShapeDtypeStruct((nq, bsz * seq // t_len, t_len * LANES), BF16),
        grid=(bsz, nt),
        in_specs=[pl.BlockSpec((1, tm, d), lambda b, i: (b, i, 0)),
                  pl.BlockSpec((1, d), lambda b, i: (0, 0)), mod_spec, mod_spec],
        out_specs=pl.BlockSpec((nq, tm // t_len, t_len * LANES), lambda b, i: (0, b * nt + i, 0)),
        scratch_shapes=[pltpu.VMEM((nq, tm, LANES), F32)],
        compiler_params=_cparams("parallel", "parallel"),
        name="s5_pre",
    )(x, g, sh, sc)


def _s5_scan_kernel(x_ref, bc_ref, mt_ref, cc_ref, are_ref, aim_ref, y_ref, re_scr, im_scr):
    x = x_ref[0]
    nrow = x.shape[0]
    pad = re_scr.shape[0] - nrow
    v = jnp.dot(x, bc_ref[0], preferred_element_type=F32)
    nh = v.shape[1] // 2
    s_re = v[:, :nh]
    s_im = v[:, nh:]
    a_re = are_ref[0]
    a_im = aim_ref[0]
    re_scr[0:pad, :] = jnp.zeros((pad, nh), F32)
    im_scr[0:pad, :] = jnp.zeros((pad, nh), F32)

    def shifted(scr, val, dist):
        scr[pad:pad + nrow, :] = val
        return scr[pad - dist:pad - dist + nrow, :]

    dist = 1
    while dist < nrow:
        sh_re = shifted(re_scr, s_re, dist)
        sh_im = shifted(im_scr, s_im, dist)
        s_re, s_im = (s_re + a_re * sh_re - a_im * sh_im, s_im + a_re * sh_im + a_im * sh_re)
        a_re, a_im = (a_re * a_re - a_im * a_im, 2.0 * a_re * a_im)
        dist *= 2
    sp = jnp.concatenate([shifted(re_scr, s_re, 1), shifted(im_scr, s_im, 1)], axis=1).astype(BF16)
    y_ref[0] = (jnp.dot(x, mt_ref[0], preferred_element_type=F32)
                + jnp.dot(sp, cc_ref[0], preferred_element_type=F32))


def _s5_scan(hq, bcq, mtq, ccq, a_re, a_im, bsz):
    nq, nrows, width = hq.shape
    nj = nrows // bsz
    nh = a_re.shape[-1]
    pad = max(nj // 2, SUBLANES)
    tile = pl.BlockSpec((1, nj, width), lambda q, b: (q, b, 0))
    wspec = lambda a: pl.BlockSpec((1,) + a.shape[1:], lambda q, b: (q, 0, 0))
    return pl.pallas_call(
        _s5_scan_kernel,
        out_shape=jax.ShapeDtypeStruct((nq, nrows, width), F32),
        grid=(nq, bsz),
        in_specs=[tile, wspec(bcq), wspec(mtq), wspec(ccq), wspec(a_re), wspec(a_im)],
        out_specs=tile,
        scratch_shapes=[pltpu.VMEM((pad + nj, nh), F32), pltpu.VMEM((pad + nj, nh), F32)],
        compiler_params=_cparams("parallel", "parallel"),
        name="s5_scan",
    )(hq, bcq, mtq, ccq, a_re, a_im)


def _s5_post_kernel(x_ref, y_ref, g_ref, sh_ref, sc_ref, gate_ref, dsk_ref, w_ref, o_ref, y_scr, *, d):
    t_len = SSM_CHUNK
    x = x_ref[0]
    h = _norm_mod(x, g_ref[...], sh_ref[0], sc_ref[0])
    nj = y_ref.shape[1]
    for q in range(y_ref.shape[0]):
        for t in range(t_len):
            y_scr[q, pl.ds(t, nj, stride=t_len), :] = y_ref[q, :, t * LANES:(t + 1) * LANES]
    y = jnp.concatenate([y_scr[q] for q in range(y_ref.shape[0])], axis=1)
    yy = y + dsk_ref[...] * h
    gl = jax.nn.gelu(yy).astype(BF16)
    z = jnp.dot(gl, w_ref[...], preferred_element_type=F32)
    o_ref[0] = x + gate_ref[0] * (z[:, :d] * jax.nn.sigmoid(z[:, d:]))


def _s5_post(x, yq, g, sh, sc, gate, d_skip, w_glu, tm=512):
    bsz, seq, d = x.shape
    tm = min(tm, seq)
    nt = seq // tm
    nq = d // LANES
    kern = functools.partial(_s5_post_kernel, d=d)
    mod_spec = pl.BlockSpec((1, 1, d), lambda b, i: (b, 0, 0))
    tile = pl.BlockSpec((1, tm, d), lambda b, i: (b, i, 0))
    return pl.pallas_call(
        kern,
        out_shape=jax.ShapeDtypeStruct((bsz, seq, d), F32),
        grid=(bsz, nt),
        in_specs=[tile,
                  pl.BlockSpec((nq, tm // SSM_CHUNK, SSM_CHUNK * LANES), lambda b, i: (0, b * nt + i, 0)),
                  pl.BlockSpec((1, d), lambda b, i: (0, 0)), mod_spec, mod_spec, mod_spec,
                  pl.BlockSpec((1, d), lambda b, i: (0, 0)),
                  pl.BlockSpec(w_glu.shape, lambda b, i: (0, 0))],
        out_specs=tile,
        scratch_shapes=[pltpu.VMEM((nq, tm, LANES), F32)],
        compiler_params=_cparams("parallel", "parallel"),
        name="s5_post",
    )(x, yq, g, sh, sc, gate, d_skip, w_glu)


def _s5_layer(x, g, sh, sc, gate, lam_re, lam_im, log_step, b_re, b_im, c_re, c_im, d_skip, w_glu):
    bsz, seq, d = x.shape
    ng, nst = lam_re.shape
    t_len, cg = SSM_CHUNK, SSM_GROUP
    nq = d // LANES
    gq = LANES // cg
    nj = seq // t_len
    bcw, mtw, ccw, a_re, a_im = _s5_prep(lam_re, lam_im, log_step, b_re, b_im, c_re, c_im)
    width = t_len * LANES
    bcq = jnp.swapaxes(bcw.reshape(nq, gq, t_len, cg, width), 1, 2).reshape(nq, width, width)
    mtq = jnp.swapaxes(mtw.reshape(nq, gq, t_len, cg, width), 1, 2).reshape(nq, width, width)
    ccq = jnp.swapaxes(ccw.reshape(nq, gq, 2, nst, width), 1, 2).reshape(nq, 2 * gq * nst, width)
    aq_re = jnp.sum(a_re.reshape(nq, gq, 1, gq * nst), axis=1)
    aq_im = jnp.sum(a_im.reshape(nq, gq, 1, gq * nst), axis=1)

    hq = _s5_pre(x, g, sh, sc)
    yq = _s5_scan(hq, bcq, mtq, ccq, aq_re, aq_im, bsz)
    return _s5_post(x, yq, g, sh, sc, gate, d_skip.reshape(1, d).astype(F32), w_glu.astype(BF16))


def kernel(x, c, ada_w, ada_b, norm_g, conv_w_in, conv_w, conv_w_out, attn_w_in, attn_q_gain, attn_k_gain, attn_w_out, rel_bias, ssm_lambda_re, ssm_lambda_im, ssm_log_step, ssm_b_re, ssm_b_im, ssm_c_re, ssm_c_im, ssm_d, ssm_w_glu, ffn_w_gu, ffn_w_down, moe_router_w, moe_router_b, moe_w_gu, moe_w_down):
    bsz, seq, d = x.shape
    depth = ada_w.shape[0]
    mod = _ada_mod(c, ada_w, ada_b).reshape(depth, bsz, 6, 1, d)
    moe_gu = moe_w_gu.reshape((-1,) + moe_w_gu.shape[2:])
    moe_down = moe_w_down.reshape((-1,) + moe_w_down.shape[2:])
    for i in range(depth):
        sh1, sc1, g1, sh2, sc2, g2 = (mod[i, :, r] for r in range(6))
        gn1 = norm_g[i, 0].reshape(1, d)
        gn2 = norm_g[i, 1].reshape(1, d)
        j = i // N_MIXERS
        if i % N_MIXERS == 0:
            x = _conv_layer(x, gn1, sh1, sc1, g1, conv_w_in[j].astype(BF16), conv_w[j],
                            conv_w_out[j].astype(BF16))
        elif i % N_MIXERS == 1:
            x = _attn_layer(x, gn1, sh1, sc1, g1, attn_w_in[j], attn_q_gain[j], attn_k_gain[j],
                            attn_w_out[j], rel_bias)
        else:
            x = _s5_layer(x, gn1, sh1, sc1, g1, ssm_lambda_re[j], ssm_lambda_im[j], ssm_log_step[j],
                          ssm_b_re[j], ssm_b_im[j], ssm_c_re[j], ssm_c_im[j], ssm_d[j], ssm_w_glu[j])
        if i % 2 == 0:
            x = _ffn_layer(x, gn2, sh2, sc2, g2, ffn_w_gu[i // 2].astype(BF16), ffn_w_down[i // 2].astype(BF16))
        else:
            x = _moe_layer(x, gn2, sh2, sc2, g2, moe_router_w[i // 2], moe_router_b[i // 2],
                           moe_gu, moe_down, ebase=(i // 2) * moe_w_gu.shape[1])
    return x
```

```python
import functools
import math

import numpy as np
import jax
import jax.numpy as jnp
from jax import lax
from jax.experimental import pallas as pl
from jax.experimental.pallas import tpu as pltpu

F32 = jnp.float32
BF16 = jnp.bfloat16
I32 = jnp.int32
HIGHEST = lax.Precision.HIGHEST

DEPTH = 4
N_MIXERS = 3
EPS = 1e-6
CONV_WIDTH = 3
N_HEADS = 16
N_KV_HEADS = 4
N_REP = N_HEADS // N_KV_HEADS
HEAD_DIM = 64
IDX_HEADS = 8
IDX_DIM = 64
TOPK_MAX = 256
REL_BUCKETS = 32
REL_MAX_DIST = 128
SSM_GROUP = 16
SSM_STATE = 64
N_EXPERTS = 8
TOP_K_EXPERTS = 2

VMEM_LIMIT_BYTES = 56 * 1024 * 1024
LANES = 128
SUBLANES = 8

INT_MIN = -(2 ** 31)
INT_MAX = 2 ** 31 - 1
NEG_BIG = -1e30
LOG2E = 1.4426950408889634

ATT_TQ = 128
ATT_KC = 256
BIAS_C = 384
BIAS_W = BIAS_C + ATT_KC
ATT_VROWS = 80

SSM_CHUNK = 8


def _cparams(*sem):
    return pltpu.CompilerParams(dimension_semantics=sem, vmem_limit_bytes=VMEM_LIMIT_BYTES)


def _norm_mod(x, g, shift, scale):
    ms = jnp.mean(x * x, axis=-1, keepdims=True)
    y = x * lax.rsqrt(ms + EPS)
    return (y * g) * (1.0 + scale) + shift


def _silu(x):
    return x * jax.nn.sigmoid(x)


def _ada_kernel(c_ref, w_ref, b_ref, o_ref):
    c = c_ref[...]
    cond = _silu(c)
    o_ref[0] = jnp.dot(cond, w_ref[0], preferred_element_type=F32, precision=HIGHEST) + b_ref[0]


def _ada_mod(c, ada_w, ada_b):
    depth, d, d6 = ada_w.shape
    bsz = c.shape[0]
    tn = d
    return pl.pallas_call(
        _ada_kernel,
        out_shape=jax.ShapeDtypeStruct((depth, bsz, d6), F32),
        grid=(depth, d6 // tn),
        in_specs=[
            pl.BlockSpec((bsz, d), lambda i, j: (0, 0)),
            pl.BlockSpec((1, d, tn), lambda i, j: (i, 0, j)),
            pl.BlockSpec((1, 1, tn), lambda i, j: (i, 0, j)),
        ],
        out_specs=pl.BlockSpec((1, bsz, tn), lambda i, j: (i, 0, j)),
        compiler_params=_cparams("parallel", "parallel"),
        name="ada_mod",
    )(c, ada_w, ada_b.reshape(depth, 1, d6))


def _conv_kernel(x_ref, xh_ref, g_ref, sh_ref, sc_ref, gate_ref, win_ref, wc_ref, wout_ref,
                 o_ref, u_scr, *, tm, d):
    i = pl.program_id(1)
    g = g_ref[...]
    sh = sh_ref[0]
    sc = sc_ref[0]
    x = x_ref[0]
    halo = xh_ref.shape[1]
    xe = jnp.concatenate([xh_ref[0], x], axis=0)
    he = _norm_mod(xe, g, sh, sc).astype(BF16)
    z = jnp.dot(he, win_ref[...], preferred_element_type=F32)
    u_all = z[:, d:2 * d] * z[:, 2 * d:]
    row = lax.broadcasted_iota(I32, (halo + tm, 1), 0)
    u_scr[...] = jnp.where(jnp.logical_or(i > 0, row >= halo), u_all, 0.0)
    b_gate = z[halo:, :d]
    wc = wc_ref[...]
    conv = (wc[0:1, :] * u_scr[halo - 2:halo - 2 + tm, :]
            + wc[1:2, :] * u_scr[halo - 1:halo - 1 + tm, :]
            + wc[2:3, :] * u_scr[halo:halo + tm, :])
    y = jnp.dot((b_gate * conv).astype(BF16), wout_ref[...], preferred_element_type=F32)
    o_ref[0] = x + gate_ref[0] * y


def _conv_layer(x, g, sh, sc, gate, w_in, w_conv, w_out, tm=512):
    bsz, seq, d = x.shape
    tm = min(tm, seq)
    nt = seq // tm
    halo = 2 * SUBLANES
    hb = tm // halo
    kern = functools.partial(_conv_kernel, tm=tm, d=d)
    mod_spec = pl.BlockSpec((1, 1, d), lambda b, i: (b, 0, 0))
    return pl.pallas_call(
        kern,
        out_shape=jax.ShapeDtypeStruct((bsz, seq, d), F32),
        grid=(bsz, nt),
        in_specs=[
            pl.BlockSpec((1, tm, d), lambda b, i: (b, i, 0)),
            pl.BlockSpec((1, halo, d), lambda b, i: (b, jnp.maximum(i * hb - 1, 0), 0)),
            pl.BlockSpec((1, d), lambda b, i: (0, 0)),
            mod_spec, mod_spec, mod_spec,
            pl.BlockSpec((d, 3 * d), lambda b, i: (0, 0)),
            pl.BlockSpec((CONV_WIDTH, d), lambda b, i: (0, 0)),
            pl.BlockSpec((d, d), lambda b, i: (0, 0)),
        ],
        out_specs=pl.BlockSpec((1, tm, d), lambda b, i: (b, i, 0)),
        scratch_shapes=[pltpu.VMEM((tm + halo, d), F32)],
        compiler_params=_cparams("parallel", "parallel"),
        name="conv_mixer",
    )(x, x, g, sh, sc, gate, w_in, w_conv, w_out)


def _ffn_kernel(x_ref, g_ref, sh_ref, sc_ref, gate_ref, wgu_ref, wd_ref, o_ref, *, dff, nchunk):
    x = x_ref[0]
    h = _norm_mod(x, g_ref[...], sh_ref[0], sc_ref[0]).astype(BF16)
    cols = dff // nchunk
    acc = jnp.zeros(x.shape, F32)
    for c in range(nchunk):
        gg = jnp.dot(h, wgu_ref[:, c * cols:(c + 1) * cols], preferred_element_type=F32)
        uu = jnp.dot(h, wgu_ref[:, dff + c * cols:dff + (c + 1) * cols], preferred_element_type=F32)
        a = (_silu(gg) * uu).astype(BF16)
        acc = acc + jnp.dot(a, wd_ref[c * cols:(c + 1) * cols, :], preferred_element_type=F32)
    o_ref[0] = x + gate_ref[0] * acc


def _ffn_layer(x, g, sh, sc, gate, w_gu, w_down, tm=512):
    bsz, seq, d = x.shape
    dff = w_down.shape[0]
    tm = min(tm, seq)
    kern = functools.partial(_ffn_kernel, dff=dff, nchunk=2)
    mod_spec = pl.BlockSpec((1, 1, d), lambda b, i: (b, 0, 0))
    return pl.pallas_call(
        kern,
        out_shape=jax.ShapeDtypeStruct((bsz, seq, d), F32),
        grid=(bsz, seq // tm),
        in_specs=[
            pl.BlockSpec((1, tm, d), lambda b, i: (b, i, 0)),
            pl.BlockSpec((1, d), lambda b, i: (0, 0)),
            mod_spec, mod_spec, mod_spec,
            pl.BlockSpec((d, 2 * dff), lambda b, i: (0, 0), pipeline_mode=pl.Buffered(1)),
            pl.BlockSpec((dff, d), lambda b, i: (0, 0), pipeline_mode=pl.Buffered(1)),
        ],
        out_specs=pl.BlockSpec((1, tm, d), lambda b, i: (b, i, 0)),
        compiler_params=_cparams("parallel", "parallel"),
        name="ffn_dense",
    )(x, g, sh, sc, gate, w_gu, w_down)


MOE_SB = 512
MOE_RT = 512
MOE_ALIGN = 16
MOE_WIN = 256
MOE_CAP = MOE_WIN - MOE_ALIGN


def _moe_router_kernel(x_ref, g_ref, sh_ref, sc_ref, rw_ref, rb_ref, h_ref, meta_ref):
    hf = _norm_mod(x_ref[...], g_ref[...], sh_ref[0], sc_ref[0])
    h_hi = hf.astype(BF16)
    h_ref[...] = h_hi
    rw = rw_ref[...]
    w_hi = rw.astype(BF16)
    w_lo = (rw - w_hi.astype(F32)).astype(BF16)
    h_lo = (hf - h_hi.astype(F32)).astype(BF16)
    logits = (jnp.dot(h_hi, w_hi, preferred_element_type=F32) + jnp.dot(h_lo, w_hi, preferred_element_type=F32)
              + jnp.dot(h_hi, w_lo, preferred_element_type=F32) + rb_ref[...])
    mx = jnp.max(logits, axis=-1, keepdims=True)
    ex = jnp.exp(logits - mx)
    probs = ex / jnp.sum(ex, axis=-1, keepdims=True)
    lane = lax.broadcasted_iota(I32, probs.shape, 1)
    m1 = jnp.max(probs, axis=-1, keepdims=True)
    i1 = jnp.min(jnp.where(probs == m1, lane, LANES), axis=-1, keepdims=True)
    rest = jnp.where(lane == i1, -1.0, probs)
    m2 = jnp.max(rest, axis=-1, keepdims=True)
    i2 = jnp.min(jnp.where(rest == m2, lane, LANES), axis=-1, keepdims=True)
    den = m1 + m2
    gates = jnp.where(lane == i1, m1 / den, 0.0) + jnp.where(lane == i2, m2 / den, 0.0)
    chosen = jnp.where(jnp.logical_or(lane == i1, lane == i2), 1.0, 0.0)
    nrow = chosen.shape[0]
    below = jnp.where(lax.broadcasted_iota(I32, (nrow, nrow), 1) < lax.broadcasted_iota(I32, (nrow, nrow), 0),
                      1.0, 0.0).astype(BF16)
    rank = jnp.dot(below, chosen.astype(BF16), preferred_element_type=F32)
    meta_ref[...] = gates + pltpu.roll(chosen, N_EXPERTS, 1) + pltpu.roll(rank, 2 * N_EXPERTS, 1)


def _moe_router(xt, g, sh, sc, rw_pad, rb_pad, tpb, tm):
    n, d = xt.shape
    mod_spec = pl.BlockSpec((1, 1, d), lambda i: (i // tpb, 0, 0))
    return pl.pallas_call(
        _moe_router_kernel,
        out_shape=[jax.ShapeDtypeStruct((n, d), BF16), jax.ShapeDtypeStruct((n, LANES), F32)],
        grid=(n // tm,),
        in_specs=[pl.BlockSpec((tm, d), lambda i: (i, 0)),
                  pl.BlockSpec((1, d), lambda i: (0, 0)), mod_spec, mod_spec,
                  pl.BlockSpec((d, LANES), lambda i: (0, 0)),
                  pl.BlockSpec((1, LANES), lambda i: (0, 0))],
        out_specs=[pl.BlockSpec((tm, d), lambda i: (i, 0)), pl.BlockSpec((tm, LANES), lambda i: (i, 0))],
        compiler_params=_cparams("parallel"),
        name="moe_router",
    )(xt, g, sh, sc, rw_pad, rb_pad)


def _window(start, count, w):
    s = start + w * MOE_CAP
    n = jnp.minimum(count - w * MOE_CAP, MOE_CAP)
    a = pl.multiple_of((s // MOE_ALIGN) * MOE_ALIGN, MOE_ALIGN)
    return s, n, a


def _moe_dispatch_kernel(start_ref, cnt_ref, h_ref, pos_ref, xs_init_ref, xs_ref,
                         buf, carry, sems, pending):
    del xs_init_ref
    b = pl.program_id(0)
    nb = pl.num_programs(0)
    ne = buf.shape[0]

    @pl.when(b == 0)
    def _():
        carry[...] = jnp.zeros(carry.shape, BF16)
        for e in range(ne):
            pending[e] = 0

    h = h_ref[...]
    riota = lax.broadcasted_iota(I32, (MOE_WIN, h.shape[0]), 0)

    def out_copy(e, a):
        return pltpu.make_async_copy(buf.at[e], xs_ref.at[pl.ds(a, MOE_WIN)], sems.at[e])

    for e in range(ne):
        posrow = pos_ref[0, e:e + 1, :]
        nwin = (cnt_ref[b, e] + MOE_CAP - 1) // MOE_CAP

        def wbody(w, carry_unused, e=e, posrow=posrow):
            s, n, a = _window(start_ref[b, e], cnt_ref[b, e], w)

            @pl.when(pending[e] == 1)
            def _():
                out_copy(e, 0).wait()

            rel = jnp.where(jnp.logical_and(posrow >= s, posrow < s + n), posrow - a, -1)
            onehot = jnp.where(rel == riota, 1.0, 0.0).astype(BF16)
            buf[e] = jnp.dot(onehot, h, preferred_element_type=F32).astype(BF16)
            buf[e, 0:MOE_ALIGN, :] = buf[e, 0:MOE_ALIGN, :] + carry[e]
            c0 = pl.multiple_of(((s + n) // MOE_ALIGN) * MOE_ALIGN - a, MOE_ALIGN)
            carry[e] = buf[e, pl.ds(c0, MOE_ALIGN), :]
            out_copy(e, a).start()
            pending[e] = 1
            return carry_unused

        lax.fori_loop(0, nwin, wbody, 0)

    @pl.when(b == nb - 1)
    def _():
        for e in range(ne):
            @pl.when(pending[e] == 1)
            def _():
                out_copy(e, 0).wait()


def _moe_dispatch(start, cnt, h, pos_t, ncap):
    n, d = h.shape
    nb, ne, sb = pos_t.shape
    xs_init = jnp.zeros((ncap, d), BF16)
    return pl.pallas_call(
        _moe_dispatch_kernel,
        out_shape=jax.ShapeDtypeStruct((ncap, d), BF16),
        grid_spec=pltpu.PrefetchScalarGridSpec(
            num_scalar_prefetch=2,
            grid=(nb,),
            in_specs=[pl.BlockSpec((sb, d), lambda b, *_: (b, 0)),
                      pl.BlockSpec((1, ne, sb), lambda b, *_: (b, 0, 0)),
                      pl.BlockSpec(memory_space=pl.ANY)],
            out_specs=pl.BlockSpec(memory_space=pl.ANY),
            scratch_shapes=[pltpu.VMEM((ne, MOE_WIN, d), BF16),
                            pltpu.VMEM((ne, MOE_ALIGN, d), BF16),
                            pltpu.SemaphoreType.DMA((ne,)),
                            pltpu.SMEM((ne,), I32)],
        ),
        input_output_aliases={4: 0},
        compiler_params=_cparams("arbitrary"),
        name="moe_dispatch",
    )(start, cnt, h, pos_t, xs_init)


MOE_WCHUNKS = 8


def _moe_ffn_kernel(blk_ref, exp_ref, nt_ref, xs_ref, wgu_hbm, wd_hbm, ys_init_ref, ys_ref,
                    wgu_scr, wd_scr, gu_stage, d_stage, sems, *, dff, nchunk, ebase):
    del blk_ref, ys_init_ref
    k = pl.program_id(0)
    d = xs_ref.shape[1]
    gu_rows = d // MOE_WCHUNKS
    d_rows = dff // MOE_WCHUNKS
    e = ebase + exp_ref[k]

    def gu_copy(c, slot):
        return pltpu.make_async_copy(wgu_hbm.at[e, pl.ds(c * gu_rows, gu_rows), :], gu_stage.at[slot],
                                     sems.at[0, slot])

    def d_copy(c, slot):
        return pltpu.make_async_copy(wd_hbm.at[e, pl.ds(c * d_rows, d_rows), :], d_stage.at[slot],
                                     sems.at[1, slot])

    new_expert = jnp.logical_or(k == 0, exp_ref[k] != exp_ref[jnp.maximum(k - 1, 0)])

    @pl.when(jnp.logical_and(k < nt_ref[0], new_expert))
    def _():
        gu_copy(0, 0).start()
        d_copy(0, 0).start()
        for c in range(MOE_WCHUNKS):
            slot = c % 2
            if c + 1 < MOE_WCHUNKS:
                gu_copy(c + 1, 1 - slot).start()
                d_copy(c + 1, 1 - slot).start()
            gu_copy(c, slot).wait()
            wgu_scr[c * gu_rows:(c + 1) * gu_rows, :] = gu_stage[slot].astype(BF16)
            d_copy(c, slot).wait()
            wd_scr[c * d_rows:(c + 1) * d_rows, :] = d_stage[slot].astype(BF16)

    @pl.when(k < nt_ref[0])
    def _():
        x = xs_ref[...]
        cols = dff // nchunk
        acc = jnp.zeros(x.shape, F32)
        for c in range(nchunk):
            gg = jnp.dot(x, wgu_scr[:, c * cols:(c + 1) * cols], preferred_element_type=F32)
            uu = jnp.dot(x, wgu_scr[:, dff + c * cols:dff + (c + 1) * cols], preferred_element_type=F32)
            a = (_silu(gg) * uu).astype(BF16)
            acc = acc + jnp.dot(a, wd_scr[c * cols:(c + 1) * cols, :], preferred_element_type=F32)
        ys_ref[...] = acc.astype(BF16)


def _moe_ffn(tile_blk, tile_exp, ntiles, xs, w_gu, w_down, ebase):
    ncap, d = xs.shape
    dff = w_down.shape[1]
    assert d % (MOE_WCHUNKS * 2 * SUBLANES) == 0 and dff % (MOE_WCHUNKS * 2 * SUBLANES) == 0
    kern = functools.partial(_moe_ffn_kernel, dff=dff, nchunk=2, ebase=ebase)
    ys_init = jnp.zeros((ncap, d), BF16)
    return pl.pallas_call(
        kern,
        out_shape=jax.ShapeDtypeStruct((ncap, d), BF16),
        grid_spec=pltpu.PrefetchScalarGridSpec(
            num_scalar_prefetch=3,
            grid=(ncap // MOE_RT,),
            in_specs=[pl.BlockSpec((MOE_RT, d), lambda k, blk, exp, nt: (blk[k], 0)),
                      pl.BlockSpec(memory_space=pl.ANY),
                      pl.BlockSpec(memory_space=pl.ANY),
                      pl.BlockSpec(memory_space=pl.ANY)],
            out_specs=pl.BlockSpec((MOE_RT, d), lambda k, blk, exp, nt: (blk[k], 0)),
            scratch_shapes=[pltpu.VMEM((d, 2 * dff), BF16),
                            pltpu.VMEM((dff, d), BF16),
                            pltpu.VMEM((2, d // MOE_WCHUNKS, 2 * dff), F32),
                            pltpu.VMEM((2, dff // MOE_WCHUNKS, d), F32),
                            pltpu.SemaphoreType.DMA((2, 2))],
        ),
        input_output_aliases={6: 0},
        compiler_params=_cparams("arbitrary"),
        name="moe_ffn",
    )(tile_blk, tile_exp, ntiles, xs, w_gu, w_down, ys_init)


def _moe_combine_kernel(start_ref, cnt_ref, x_ref, gate_ref, pos_ref, gts_ref, ys_ref, o_ref,
                        buf, sems, acc_scr):
    b = pl.program_id(0)
    nb = pl.num_programs(0)
    ne = buf.shape[1]
    sb = x_ref.shape[0]
    slot = b % 2

    def in_copy(sl, e, a):
        return pltpu.make_async_copy(ys_ref.at[pl.ds(a, MOE_WIN)], buf.at[sl, e], sems.at[sl, e])

    def start_first_windows(blk, sl):
        for e in range(ne):
            @pl.when(cnt_ref[blk, e] > 0)
            def _():
                _, _, a = _window(start_ref[blk, e], cnt_ref[blk, e], 0)
                in_copy(sl, e, a).start()

    @pl.when(b == 0)
    def _():
        start_first_windows(0, 0)

    @pl.when(b + 1 < nb)
    def _():
        start_first_windows(b + 1, 1 - slot)

    acc_scr[...] = jnp.zeros(acc_scr.shape, F32)
    liota = lax.broadcasted_iota(I32, (sb, MOE_WIN), 1)
    for e in range(ne):
        poscol = pos_ref[:, e:e + 1]
        gcol = gts_ref[:, e:e + 1]
        nwin = (cnt_ref[b, e] + MOE_CAP - 1) // MOE_CAP

        def wbody(w, carry_unused, e=e, poscol=poscol, gcol=gcol):
            s, n, a = _window(start_ref[b, e], cnt_ref[b, e], w)

            @pl.when(w > 0)
            def _():
                in_copy(slot, e, a).start()

            in_copy(slot, e, a).wait()
            rel = jnp.where(jnp.logical_and(poscol >= s, poscol < s + n), poscol - a, -1)
            onehot = jnp.where(rel == liota, 1.0, 0.0).astype(BF16)
            acc_scr[...] += gcol * jnp.dot(onehot, buf[slot, e], preferred_element_type=F32)
            return carry_unused

        lax.fori_loop(0, nwin, wbody, 0)

    o_ref[...] = x_ref[...] + gate_ref[0] * acc_scr[...]


def _moe_combine(start, cnt, xt, gate, pos_n, gates_n, ys, tpb):
    n, d = xt.shape
    nb, ne = cnt.shape
    sb = n // nb
    return pl.pallas_call(
        _moe_combine_kernel,
        out_shape=jax.ShapeDtypeStruct((n, d), F32),
        grid_spec=pltpu.PrefetchScalarGridSpec(
            num_scalar_prefetch=2,
            grid=(nb,),
            in_specs=[pl.BlockSpec((sb, d), lambda b, *_: (b, 0)),
                      pl.BlockSpec((1, 1, d), lambda b, *_: (b // tpb, 0, 0)),
                      pl.BlockSpec((sb, ne), lambda b, *_: (b, 0)),
                      pl.BlockSpec((sb, ne), lambda b, *_: (b, 0)),
                      pl.BlockSpec(memory_space=pl.ANY)],
            out_specs=pl.BlockSpec((sb, d), lambda b, *_: (b, 0)),
            scratch_shapes=[pltpu.VMEM((2, ne, MOE_WIN, d), BF16),
                            pltpu.SemaphoreType.DMA((2, ne)),
                            pltpu.VMEM((sb, d), F32)],
        ),
        compiler_params=_cparams("arbitrary"),
        name="moe_combine",
    )(start, cnt, xt, gate, pos_n, gates_n, ys)


def _moe_layer(x, g, sh, sc, gate, router_w, router_b, w_gu, w_down, ebase=0):
    bsz, seq, d = x.shape
    ne = router_w.shape[1]
    n = bsz * seq
    sb = min(MOE_SB, seq)
    tpb = seq // sb
    nb = n // sb
    rt = MOE_RT
    xt = x.reshape(n, d)
    rw_pad = jnp.pad(router_w.astype(F32), ((0, 0), (0, LANES - ne)))
    rb_pad = jnp.pad(router_b.astype(F32).reshape(1, ne), ((0, 0), (0, LANES - ne)), constant_values=NEG_BIG)
    h, meta = _moe_router(xt, g, sh, sc, rw_pad, rb_pad, tpb, sb)
    gates = meta[:, :ne]
    sel = (meta[:, ne:2 * ne] > 0.5).astype(I32)

    selb = sel.reshape(nb, sb, ne)
    cnt = jnp.sum(selb, axis=1)
    rank = meta[:, 2 * ne:3 * ne].astype(I32).reshape(nb, sb, ne)
    total = jnp.sum(cnt, axis=0)
    region = ((total + MOE_WIN + rt - 1) // rt) * rt
    off = jnp.cumsum(region) - region
    start = (off[None, :] + jnp.cumsum(cnt, axis=0) - cnt).astype(I32)
    pos = jnp.where(selb > 0, start[:, None, :] + rank, -1).astype(I32)
    ncap = 2 * n + ne * (MOE_WIN + rt)
    tiles_e = (total + rt - 1) // rt
    tcum = jnp.cumsum(tiles_e)
    ntiles = tcum[-1]
    kk = jnp.minimum(jnp.arange(ncap // rt), ntiles - 1)
    tile_exp = jnp.sum((kk[:, None] >= tcum[None, :]).astype(I32), axis=1)
    tile_blk = (off[tile_exp] // rt + kk - (tcum - tiles_e)[tile_exp]).astype(I32)

    xs = _moe_dispatch(start, cnt.astype(I32), h, jnp.swapaxes(pos, 1, 2), ncap)
    ys = _moe_ffn(tile_blk, tile_exp, ntiles.reshape(1).astype(I32), xs, w_gu, w_down, ebase)
    out = _moe_combine(start, cnt.astype(I32), xt, gate, pos.reshape(n, ne), gates, ys, tpb)
    return out.reshape(bsz, seq, d)


def _head_norm(q, hsum_ref, hexp_ref, gain, scale):
    ms = jnp.dot((q * q).astype(BF16), hsum_ref[...], preferred_element_type=F32)
    r = lax.rsqrt(ms + EPS)
    r_hi = r.astype(BF16)
    r_lo = (r - r_hi.astype(F32)).astype(BF16)
    rexp = (jnp.dot(r_hi, hexp_ref[...], preferred_element_type=F32)
            + jnp.dot(r_lo, hexp_ref[...], preferred_element_type=F32))
    return q * rexp * (gain * scale)


def _qkv_kernel(x_ref, g_ref, sh_ref, sc_ref, wm_ref, ws_ref, qg_ref, kg_ref,
                hsq_ref, heq_ref, hsk_ref, hek_ref,
                q_ref, k_ref, v_ref, qi_ref, kiwi_ref, *, dq, dk, dqi):
    x = x_ref[0]
    h = _norm_mod(x, g_ref[...], sh_ref[0], sc_ref[0]).astype(BF16)
    z = jnp.dot(h, wm_ref[...], preferred_element_type=F32)
    q = z[:, :dq]
    k = z[:, dq:dq + dk]
    v = z[:, dq + dk:dq + 2 * dk]
    qi = z[:, dq + 2 * dk:dq + 2 * dk + dqi]
    q_ref[0] = _head_norm(q, hsq_ref, heq_ref, qg_ref[...], HEAD_DIM ** -0.5 * LOG2E).astype(BF16)
    k_ref[0] = _head_norm(k, hsk_ref, hek_ref, kg_ref[...], 1.0).astype(BF16)
    v_ref[0] = v.astype(BF16)
    qi_ref[0] = qi.astype(BF16)
    kiwi_ref[0] = jnp.dot(h, ws_ref[...], preferred_element_type=F32)


def _head_indicators(nheads):
    hs = np.zeros((nheads * HEAD_DIM, LANES), np.float32)
    he = np.zeros((LANES, nheads * HEAD_DIM), np.float32)
    for hd in range(nheads):
        hs[hd * HEAD_DIM:(hd + 1) * HEAD_DIM, hd] = 1.0 / HEAD_DIM
        he[hd, hd * HEAD_DIM:(hd + 1) * HEAD_DIM] = 1.0
    return jnp.asarray(hs, BF16), jnp.asarray(he, BF16)


def _qkv_proj(x, g, sh, sc, w_main, w_small, q_gain, k_gain, tm=512):
    bsz, seq, d = x.shape
    tm = min(tm, seq)
    dq = N_HEADS * HEAD_DIM
    dk = N_KV_HEADS * HEAD_DIM
    dqi = IDX_HEADS * IDX_DIM
    hsq, heq = _head_indicators(N_HEADS)
    hsk, hek = _head_indicators(N_KV_HEADS)
    qg = jnp.tile(q_gain.reshape(1, HEAD_DIM), (1, N_HEADS)).astype(F32)
    kg = jnp.tile(k_gain.reshape(1, HEAD_DIM), (1, N_KV_HEADS)).astype(F32)
    kern = functools.partial(_qkv_kernel, dq=dq, dk=dk, dqi=dqi)
    mod_spec = pl.BlockSpec((1, 1, d), lambda b, i: (b, 0, 0))

    def full(a):
        return pl.BlockSpec(a.shape, lambda b, i: (0,) * a.ndim)

    def out(n):
        return pl.BlockSpec((1, tm, n), lambda b, i: (b, i, 0))

    return pl.pallas_call(
        kern,
        out_shape=[
            jax.ShapeDtypeStruct((bsz, seq, dq), BF16),
            jax.ShapeDtypeStruct((bsz, seq, dk), BF16),
            jax.ShapeDtypeStruct((bsz, seq, dk), BF16),
            jax.ShapeDtypeStruct((bsz, seq, dqi), BF16),
            jax.ShapeDtypeStruct((bsz, seq, LANES), F32),
        ],
        grid=(bsz, seq // tm),
        in_specs=[
            pl.BlockSpec((1, tm, d), lambda b, i: (b, i, 0)),
            pl.BlockSpec((1, d), lambda b, i: (0, 0)),
            mod_spec, mod_spec,
            full(w_main), full(w_small), full(qg), full(kg),
            full(hsq), full(heq), full(hsk), full(hek),
        ],
        out_specs=[out(dq), out(dk), out(dk), out(dqi), out(LANES)],
        compiler_params=_cparams("parallel", "parallel"),
        name="attn_qkv",
    )(x, g, sh, sc, w_main, w_small, qg, kg, hsq, heq, hsk, hek)


def _rel_bucket_np(dist):
    max_exact = REL_BUCKETS // 2
    d = np.maximum(dist, 1).astype(np.float64)
    large = max_exact + (np.log(d / max_exact) / math.log(REL_MAX_DIST / max_exact)
                         * (REL_BUCKETS - max_exact)).astype(np.int32)
    large = np.minimum(large, REL_BUCKETS - 1)
    return np.where(dist < max_exact, dist, large).astype(np.int32)


def _bias_table_kernel(bucket_ref, rb_ref, o_ref):
    hd = pl.program_id(0)
    bucket = bucket_ref[...]
    acc = jnp.zeros(bucket.shape, F32)
    for b in range(REL_BUCKETS):
        acc = jnp.where(bucket == b, rb_ref[b, hd] * LOG2E, acc)
    o_ref[0] = acc


def _bias_table(rel_bias):
    w = np.arange(BIAS_W)[:, None]
    i = np.arange(ATT_TQ)[None, :]
    bucket = jnp.asarray(_rel_bucket_np(np.maximum(i - w + BIAS_C, 0)))
    return pl.pallas_call(
        _bias_table_kernel,
        out_shape=jax.ShapeDtypeStruct((N_HEADS, BIAS_W, ATT_TQ), F32),
        grid=(N_HEADS,),
        in_specs=[
            pl.BlockSpec((BIAS_W, ATT_TQ), lambda hd: (0, 0)),
            pl.BlockSpec(memory_space=pltpu.SMEM),
        ],
        out_specs=pl.BlockSpec((1, BIAS_W, ATT_TQ), lambda hd: (hd, 0, 0)),
        compiler_params=_cparams("arbitrary"),
        name="attn_bias_table",
    )(bucket, rel_bias.astype(F32))


def _attn_kernel(qT_ref, qiT_ref, wiT_ref, k_ref, vT_ref, ki_ref, pt_ref, o_ref,
                 keys_scr, negm_scr, pidx_scr, oT_scr, acc_scr, qall_scr, sa_scr, sb_scr, *, top_k):
    tq, kc = ATT_TQ, ATT_KC
    qt = pl.program_id(1)
    q0 = qt * tq
    nch = (q0 + tq + kc - 1) // kc
    tpos = q0 + lax.broadcasted_iota(I32, (kc, tq), 1)
    srow = lax.broadcasted_iota(I32, (kc, tq), 0)

    qiT = qiT_ref[0]
    qi_all = jnp.concatenate([qiT[hd * IDX_DIM:(hd + 1) * IDX_DIM, :] for hd in range(IDX_HEADS)], axis=1)
    wiT = wiT_ref[0]

    def score_chunk(c):
        ks = pl.multiple_of(c * kc, kc)
        kic = ki_ref[0, pl.ds(ks, kc), :]
        dots = jnp.dot(kic, qi_all, preferred_element_type=F32)
        acc = jnp.zeros((kc, tq), F32)
        for hd in range(IDX_HEADS):
            acc = acc + jnp.maximum(dots[:, hd * tq:(hd + 1) * tq], 0.0) * wiT[hd:hd + 1, :]
        acc = jnp.where(acc == 0.0, 0.0, acc)
        bits = pltpu.bitcast(acc, I32)
        key = jnp.where(bits < 0, bits ^ INT_MAX, bits)
        key = jnp.where(ks + srow <= tpos, key, INT_MIN)
        keys_scr[pl.ds(ks, kc), :] = key

    npair = (nch + 1) // 2

    def score_pair(i, carry):
        score_chunk(2 * i)
        score_chunk(2 * i + 1)
        return carry

    lax.fori_loop(0, npair, score_pair, 0)

    @pl.when(nch % 2 == 1)
    def _():
        negm_scr[pl.ds(pl.multiple_of(nch * kc, kc), kc), :] = jnp.full((kc, tq), NEG_BIG, F32)

    srow2 = lax.broadcasted_iota(I32, (2 * kc, tq), 0)

    def count(pred):
        def body(c, acc):
            ks = pl.multiple_of(c * 2 * kc, 2 * kc)
            m = pred(keys_scr[pl.ds(ks, 2 * kc), :], ks + srow2).astype(I32)
            return acc + jnp.sum(m.reshape(2 * kc // SUBLANES, SUBLANES, tq), axis=0)
        acc = lax.fori_loop(0, npair, body, jnp.zeros((SUBLANES, tq), I32))
        return jnp.sum(acc, axis=0, keepdims=True)

    def bit_body(it, p):
        cand_p = p | lax.shift_left(jnp.int32(1), 31 - it)
        cand = cand_p ^ INT_MIN
        cnt = count(lambda k, s: k >= cand)
        return jnp.where(cnt >= top_k, cand_p, p)

    p_fin = lax.fori_loop(0, 32, bit_body, jnp.zeros((1, tq), I32))
    v = p_fin ^ INT_MIN

    cnt_gt = count(lambda k, s: k > v)
    cnt_eq = count(lambda k, s: k == v)
    need = top_k - cnt_gt
    pidx_scr[...] = jnp.full((1, tq), INT_MAX, I32)
    pos_bits = (keys_scr.shape[0] - 1).bit_length()

    @pl.when(jnp.max(cnt_eq - need) > 0)
    def _():
        def ibit(it, p):
            cand = p | lax.shift_left(jnp.int32(1), pos_bits - 1 - it)
            cnt = count(lambda k, s: jnp.logical_and(k == v, s < cand))
            return jnp.where(cnt < need, cand, p)
        pidx_scr[...] = lax.fori_loop(0, pos_bits, ibit, jnp.zeros((1, tq), I32))

    pidx = pidx_scr[...]

    def mask_chunk(c, carry):
        ks = pl.multiple_of(c * kc, kc)
        k = keys_scr[pl.ds(ks, kc), :]
        spos = ks + srow
        sel = jnp.logical_or(k > v, jnp.logical_and(k == v, spos <= pidx))
        sel = jnp.logical_and(sel, spos <= tpos)
        negm_scr[pl.ds(ks, kc), :] = jnp.where(sel, 0.0, NEG_BIG)
        return carry

    lax.fori_loop(0, nch, mask_chunk, 0)

    acc_scr[...] = jnp.zeros(acc_scr.shape, F32)
    for n in range(N_KV_HEADS):
        r0 = n * N_REP * HEAD_DIM
        qall_scr[n] = jnp.concatenate(
            [qT_ref[0, r0 + g * HEAD_DIM:r0 + (g + 1) * HEAD_DIM, :] for g in range(N_REP)], axis=1)

    def qk_chunk(c, s_ref):
        ks = pl.multiple_of(c * kc, kc)
        for n in range(N_KV_HEADS):
            s_ref[n] = jnp.dot(k_ref[0, n, pl.ds(ks, kc), :], qall_scr[n], preferred_element_type=F32)

    def softmax_pv(c, s_ref, ms, far):
        ks = pl.multiple_of(c * kc, kc)
        negm = negm_scr[pl.ds(ks, kc), :]
        w0 = pl.multiple_of(jnp.clip(BIAS_C - (q0 - ks), 0, BIAS_C), LANES)
        new_ms = []
        for n in range(N_KV_HEADS):
            s = s_ref[n]
            if far:
                cvec = jnp.concatenate([pt_ref[n * N_REP + g, 0:1, :] for g in range(N_REP)], axis=1)
                lg = jnp.concatenate([s[:, g * tq:(g + 1) * tq] + negm for g in range(N_REP)], axis=1)
                m_new = jnp.maximum(ms[n], jnp.max(lg, axis=0, keepdims=True) + cvec)
                p = jnp.exp2(lg - (m_new - cvec))
            else:
                lg = jnp.concatenate(
                    [s[:, g * tq:(g + 1) * tq] + pt_ref[n * N_REP + g, pl.ds(w0, kc), :] + negm
                     for g in range(N_REP)], axis=1)
                m_new = jnp.maximum(ms[n], jnp.max(lg, axis=0, keepdims=True))
                p = jnp.exp2(lg - m_new)
            alpha = jnp.exp2(ms[n] - m_new)
            acc_scr[n] = alpha * acc_scr[n] + jnp.dot(vT_ref[0, c, n], p.astype(BF16),
                                                      preferred_element_type=F32)
            new_ms.append(m_new)
        return tuple(new_ms)

    last_chunk = k_ref.shape[2] // kc - 1

    def pair_step(i, ms, far):
        c0 = 2 * i
        qk_chunk(c0 + 1, sb_scr)
        ms = softmax_pv(c0, sa_scr, ms, far)
        qk_chunk(jnp.minimum(c0 + 2, last_chunk), sa_scr)
        return softmax_pv(c0 + 1, sb_scr, ms, far)

    n_far = jnp.clip((q0 - BIAS_C + kc) // kc, 0, nch)
    ms = tuple(jnp.full((1, N_REP * tq), NEG_BIG, F32) for _ in range(N_KV_HEADS))
    qk_chunk(0, sa_scr)
    ms = lax.fori_loop(0, n_far // 2, functools.partial(pair_step, far=True), ms)
    lax.fori_loop(n_far // 2, npair, functools.partial(pair_step, far=False), ms)
    for n in range(N_KV_HEADS):
        o_t = acc_scr[n, 0:HEAD_DIM, :] / acc_scr[n, HEAD_DIM:HEAD_DIM + 1, :]
        for g in range(N_REP):
            r0 = (n * N_REP + g) * HEAD_DIM
            oT_scr[r0:r0 + HEAD_DIM, :] = o_t[:, g * tq:(g + 1) * tq]

    o_ref[0] = oT_scr[...].T.astype(BF16)


def _attention(qT, qiT, wiT, k4, vT, ki, ptab, top_k):
    bsz, dq, seq = qT.shape
    tq = ATT_TQ
    assert seq % (2 * ATT_KC) == 0
    kern = functools.partial(_attn_kernel, top_k=top_k)
    return pl.pallas_call(
        kern,
        out_shape=jax.ShapeDtypeStruct((bsz, seq, dq), BF16),
        grid=(bsz, seq // tq),
        in_specs=[
            pl.BlockSpec((1, dq, tq), lambda b, i: (b, 0, i)),
            pl.BlockSpec((1, qiT.shape[1], tq), lambda b, i: (b, 0, i)),
            pl.BlockSpec((1, wiT.shape[1], tq), lambda b, i: (b, 0, i)),
            pl.BlockSpec((1,) + k4.shape[1:], lambda b, i: (b, 0, 0, 0)),
            pl.BlockSpec((1,) + vT.shape[1:], lambda b, i: (b, 0, 0, 0, 0)),
            pl.BlockSpec((1,) + ki.shape[1:], lambda b, i: (b, 0, 0)),
            pl.BlockSpec(ptab.shape, lambda b, i: (0, 0, 0)),
        ],
        out_specs=pl.BlockSpec((1, tq, dq), lambda b, i: (b, i, 0)),
        scratch_shapes=[
            pltpu.VMEM((seq, tq), I32),
            pltpu.VMEM((seq, tq), F32),
            pltpu.VMEM((1, tq), I32),
            pltpu.VMEM((dq, tq), F32),
            pltpu.VMEM((N_KV_HEADS, ATT_VROWS, N_REP * tq), F32),
            pltpu.VMEM((N_KV_HEADS, HEAD_DIM, N_REP * tq), BF16),
            pltpu.VMEM((N_KV_HEADS, ATT_KC, N_REP * tq), F32),
            pltpu.VMEM((N_KV_HEADS, ATT_KC, N_REP * tq), F32),
        ],
        compiler_params=_cparams("parallel", "arbitrary"),
        name="attn_core",
    )(qT, qiT, wiT, k4, vT, ki, ptab)


def _proj_res_kernel(a_ref, x_ref, gate_ref, w_ref, o_ref):
    y = jnp.dot(a_ref[0], w_ref[...], preferred_element_type=F32)
    o_ref[0] = x_ref[0] + gate_ref[0] * y


def _proj_residual(a, x, gate, w, tm=512):
    bsz, seq, d = x.shape
    tm = min(tm, seq)
    return pl.pallas_call(
        _proj_res_kernel,
        out_shape=jax.ShapeDtypeStruct((bsz, seq, d), F32),
        grid=(bsz, seq // tm),
        in_specs=[
            pl.BlockSpec((1, tm, a.shape[2]), lambda b, i: (b, i, 0)),
            pl.BlockSpec((1, tm, d), lambda b, i: (b, i, 0)),
            pl.BlockSpec((1, 1, d), lambda b, i: (b, 0, 0)),
            pl.BlockSpec(w.shape, lambda b, i: (0, 0)),
        ],
        out_specs=pl.BlockSpec((1, tm, d), lambda b, i: (b, i, 0)),
        compiler_params=_cparams("parallel", "parallel"),
        name="proj_residual",
    )(a, x, gate, w)


def _attn_layer(x, g, sh, sc, gate, w_in, q_gain, k_gain, w_out, rel_bias):
    bsz, seq, d = x.shape
    top_k = min(TOPK_MAX, seq // 4)
    dq = N_HEADS * HEAD_DIM
    dk = N_KV_HEADS * HEAD_DIM
    dqi = IDX_HEADS * IDX_DIM
    nmain = dq + 2 * dk + dqi
    w_main = w_in[:, :nmain].astype(BF16)
    w_small = jnp.pad(w_in[:, nmain:], ((0, 0), (0, LANES - (IDX_DIM + IDX_HEADS)))).astype(BF16)
    q, k, v, qi, kiwi = _qkv_proj(x, g, sh, sc, w_main, w_small, q_gain, k_gain)
    ki = kiwi[:, :, :IDX_DIM].astype(BF16)
    wi = kiwi[:, :, IDX_DIM:IDX_DIM + IDX_HEADS] * (IDX_HEADS ** -0.5 * IDX_DIM ** -0.5)
    qT = jnp.swapaxes(q, 1, 2)
    qiT = jnp.swapaxes(qi, 1, 2)
    wiT = jnp.swapaxes(wi, 1, 2)
    nck = seq // ATT_KC
    vT = jnp.swapaxes(v.reshape(bsz, nck, ATT_KC, dk), 2, 3).reshape(bsz, nck, N_KV_HEADS, HEAD_DIM, ATT_KC)
    vT = jnp.concatenate([
        vT, jnp.ones((bsz, nck, N_KV_HEADS, 1, ATT_KC), BF16),
        jnp.zeros((bsz, nck, N_KV_HEADS, ATT_VROWS - HEAD_DIM - 1, ATT_KC), BF16)], axis=3)
    k4 = jnp.swapaxes(k.reshape(bsz, seq, N_KV_HEADS, HEAD_DIM), 1, 2)
    ptab = _bias_table(rel_bias)
    attn = _attention(qT, qiT, wiT, k4, vT, ki, ptab, top_k)
    return _proj_residual(attn, x, gate, w_out.astype(BF16))


def _s5_prep_kernel(lre_ref, lim_ref, ls_ref, bre_ref, bim_ref, cre_ref, cim_ref,
                    bcw_ref, mtw_ref, ccw_ref, are_ref, aim_ref):
    t_len = SSM_CHUNK
    lre = jnp.minimum(lre_ref[0], -1e-4)
    lim = lim_ref[0]
    step = jnp.exp(ls_ref[0])
    ar = lre * step
    ai = lim * step

    def powers(jv):
        mag = jnp.exp(jv * ar)
        return mag * jnp.cos(jv * ai), mag * jnp.sin(jv * ai)

    lb_re, lb_im = powers(1.0)
    nr = lb_re - 1.0
    ni = lb_im
    den = lre * lre + lim * lim
    cf_re = (nr * lre + ni * lim) / den
    cf_im = (ni * lre - nr * lim) / den
    bre = bre_ref[0]
    bim = bim_ref[0]
    bb_re = cf_re * bre - cf_im * bim
    bb_im = cf_re * bim + cf_im * bre
    cre = cre_ref[0]
    cim = cim_ref[0]
    nst = lre.shape[-1]
    jv = lax.broadcasted_iota(I32, (t_len, 1, nst), 0).astype(F32)
    pj_re, pj_im = powers(jv)
    a_re = (cre[None] * pj_re - cim[None] * pj_im).reshape(t_len * SSM_GROUP, nst)
    a_im = (cre[None] * pj_im + cim[None] * pj_re).reshape(t_len * SSM_GROUP, nst)
    dn = (((1,), (1,)), ((), ()))
    cg = SSM_GROUP
    tc = t_len * cg
    width = t_len * LANES
    gq = pl.program_id(0) % (LANES // cg)
    nh = (LANES // cg) * nst
    cg_shift = cg.bit_length() - 1

    def place(nrows, target):
        r = lax.broadcasted_iota(I32, (nrows, width), 0)
        col = lax.broadcasted_iota(I32, (nrows, width), 1)
        return jnp.where(col == target(r), 1.0, 0.0).astype(BF16)

    pm = place(tc, lambda r: lax.shift_right_logical(r, cg_shift) * LANES + gq * cg + (r & (cg - 1)))
    k_t = (lax.dot_general(bb_re, a_re, dn, preferred_element_type=F32, precision=HIGHEST)
           - lax.dot_general(bb_im, a_im, dn, preferred_element_type=F32, precision=HIGHEST))
    lane = lax.broadcasted_iota(I32, (cg, tc), 1)
    mt_rows = [k_t] + [jnp.where(lane >= s * cg, pltpu.roll(k_t, s * cg, 1), 0.0) for s in range(1, t_len)]
    mt_t = jnp.concatenate(mt_rows, axis=0)
    mtw_ref[0] = jnp.dot(mt_t.astype(BF16), pm, preferred_element_type=F32).astype(BF16)
    pr_re, pr_im = powers((t_len - 1.0) - jv)
    bc_re = (bb_re[None] * pr_re - bb_im[None] * pr_im).reshape(tc, nst)
    bc_im = (bb_re[None] * pr_im + bb_im[None] * pr_re).reshape(tc, nst)
    pb_re = place(nst, lambda r: gq * nst + r)
    pb_im = place(nst, lambda r: nh + gq * nst + r)
    bcw_ref[0] = (jnp.dot(bc_re.astype(BF16), pb_re, preferred_element_type=F32)
                  + jnp.dot(bc_im.astype(BF16), pb_im, preferred_element_type=F32)).astype(BF16)
    pn_re, pn_im = powers(jv + 1.0)
    cc_re = (cre[None] * pn_re - cim[None] * pn_im).reshape(tc, nst)
    cc_im = (cre[None] * pn_im + cim[None] * pn_re).reshape(tc, nst)
    eye = jnp.where(lax.broadcasted_iota(I32, (nst, nst), 0) == lax.broadcasted_iota(I32, (nst, nst), 1),
                    1.0, 0.0)
    cct_re = lax.dot_general(eye, cc_re, dn, preferred_element_type=F32, precision=HIGHEST)
    cct_im = lax.dot_general(eye, cc_im, dn, preferred_element_type=F32, precision=HIGHEST)
    ccw_ref[0, 0:nst, :] = jnp.dot(cct_re.astype(BF16), pm, preferred_element_type=F32).astype(BF16)
    ccw_ref[0, nst:2 * nst, :] = (-jnp.dot(cct_im.astype(BF16), pm, preferred_element_type=F32)).astype(BF16)
    at_re, at_im = powers(float(t_len))
    rs = lax.broadcasted_iota(I32, (nst, nh), 0)
    cs = lax.broadcasted_iota(I32, (nst, nh), 1)
    pa = jnp.where(cs == gq * nst + rs, 1.0, 0.0)
    are_ref[0] = jnp.dot(at_re, pa, preferred_element_type=F32, precision=HIGHEST)
    aim_ref[0] = jnp.dot(at_im, pa, preferred_element_type=F32, precision=HIGHEST)


def _s5_prep(lam_re, lam_im, log_step, b_re, b_im, c_re, c_im):
    ng, nst = lam_re.shape
    tc = SSM_CHUNK * SSM_GROUP
    vec = pl.BlockSpec((1, 1, nst), lambda gi: (gi, 0, 0))
    mat = pl.BlockSpec((1, SSM_GROUP, nst), lambda gi: (gi, 0, 0))
    width = SSM_CHUNK * LANES
    assert tc == 2 * nst and 2 * (LANES // SSM_GROUP) * nst == width
    big = pl.BlockSpec((1, tc, width), lambda gi: (gi, 0, 0))
    wide = jax.ShapeDtypeStruct((ng, tc, width), BF16)
    return pl.pallas_call(
        _s5_prep_kernel,
        out_shape=[wide, wide, wide,
                   jax.ShapeDtypeStruct((ng, 1, width // 2), F32),
                   jax.ShapeDtypeStruct((ng, 1, width // 2), F32)],
        grid=(ng,),
        in_specs=[vec, vec, pl.BlockSpec((1, 1, 1), lambda gi: (gi, 0, 0)), mat, mat, mat, mat],
        out_specs=[big, big, big, pl.BlockSpec((1, 1, width // 2), lambda gi: (gi, 0, 0)),
                   pl.BlockSpec((1, 1, width // 2), lambda gi: (gi, 0, 0))],
        compiler_params=_cparams("parallel"),
        name="s5_prep",
    )(lam_re.reshape(ng, 1, nst), lam_im.reshape(ng, 1, nst), log_step.reshape(ng, 1, 1),
      jnp.swapaxes(b_re, 1, 2), jnp.swapaxes(b_im, 1, 2), c_re, c_im)


def _s5_pre_kernel(x_ref, g_ref, sh_ref, sc_ref, o_ref, h_scr):
    t_len = SSM_CHUNK
    h = _norm_mod(x_ref[0], g_ref[...], sh_ref[0], sc_ref[0])
    nj = o_ref.shape[1]
    for q in range(o_ref.shape[0]):
        h_scr[q] = h[:, q * LANES:(q + 1) * LANES]
        for t in range(t_len):
            o_ref[q, :, t * LANES:(t + 1) * LANES] = h_scr[q, pl.ds(t, nj, stride=t_len), :].astype(BF16)


def _s5_pre(x, g, sh, sc, tm=512):
    bsz, seq, d = x.shape
    tm = min(tm, seq)
    nt = seq // tm
    nq = d // LANES
    t_len = SSM_CHUNK
    mod_spec = pl.BlockSpec((1, 1, d), lambda b, i: (b, 0, 0))
    return pl.pallas_call(
        _s5_pre_kernel,
        out_shape=jax.ShapeDtypeStruct((nq, bsz * seq // t_len, t_len * LANES), BF16),
        grid=(bsz, nt),
        in_specs=[pl.BlockSpec((1, tm, d), lambda b, i: (b, i, 0)),
                  pl.BlockSpec((1, d), lambda b, i: (0, 0)), mod_spec, mod_spec],
        out_specs=pl.BlockSpec((nq, tm // t_len, t_len * LANES), lambda b, i: (0, b * nt + i, 0)),
        scratch_shapes=[pltpu.VMEM((nq, tm, LANES), F32)],
        compiler_params=_cparams("parallel", "parallel"),
        name="s5_pre",
    )(x, g, sh, sc)


def _s5_scan_kernel(x_ref, bc_ref, mt_ref, cc_ref, are_ref, aim_ref, y_ref, re_scr, im_scr):
    x = x_ref[0]
    nrow = x.shape[0]
    pad = re_scr.shape[0] - nrow
    v = jnp.dot(x, bc_ref[0], preferred_element_type=F32)
    nh = v.shape[1] // 2
    s_re = v[:, :nh]
    s_im = v[:, nh:]
    a_re = are_ref[0]
    a_im = aim_ref[0]
    re_scr[0:pad, :] = jnp.zeros((pad, nh), F32)
    im_scr[0:pad, :] = jnp.zeros((pad, nh), F32)

    def shifted(scr, val, dist):
        scr[pad:pad + nrow, :] = val
        return scr[pad - dist:pad - dist + nrow, :]

    dist = 1
    while dist < nrow:
        sh_re = shifted(re_scr, s_re, dist)
        sh_im = shifted(im_scr, s_im, dist)
        s_re, s_im = (s_re + a_re * sh_re - a_im * sh_im, s_im + a_re * sh_im + a_im * sh_re)
        a_re, a_im = (a_re * a_re - a_im * a_im, 2.0 * a_re * a_im)
        dist *= 2
    sp = jnp.concatenate([shifted(re_scr, s_re, 1), shifted(im_scr, s_im, 1)], axis=1).astype(BF16)
    y_ref[0] = (jnp.dot(x, mt_ref[0], preferred_element_type=F32)
                + jnp.dot(sp, cc_ref[0], preferred_element_type=F32))


def _s5_scan(hq, bcq, mtq, ccq, a_re, a_im, bsz):
    nq, nrows, width = hq.shape
    nj = nrows // bsz
    nh = a_re.shape[-1]
    pad = max(nj // 2, SUBLANES)
    tile = pl.BlockSpec((1, nj, width), lambda q, b: (q, b, 0))
    wspec = lambda a: pl.BlockSpec((1,) + a.shape[1:], lambda q, b: (q, 0, 0))
    return pl.pallas_call(
        _s5_scan_kernel,
        out_shape=jax.ShapeDtypeStruct((nq, nrows, width), F32),
        grid=(nq, bsz),
        in_specs=[tile, wspec(bcq), wspec(mtq), wspec(ccq), wspec(a_re), wspec(a_im)],
        out_specs=tile,
        scratch_shapes=[pltpu.VMEM((pad + nj, nh), F32), pltpu.VMEM((pad + nj, nh), F32)],
        compiler_params=_cparams("parallel", "parallel"),
        name="s5_scan",
    )(hq, bcq, mtq, ccq, a_re, a_im)


def _s5_post_kernel(x_ref, y_ref, g_ref, sh_ref, sc_ref, gate_ref, dsk_ref, w_ref, o_ref, y_scr, *, d):
    t_len = SSM_CHUNK
    x = x_ref[0]
    h = _norm_mod(x, g_ref[...], sh_ref[0], sc_ref[0])
    nj = y_ref.shape[1]
    for q in range(y_ref.shape[0]):
        for t in range(t_len):
            y_scr[q, pl.ds(t, nj, stride=t_len), :] = y_ref[q, :, t * LANES:(t + 1) * LANES]
    y = jnp.concatenate([y_scr[q] for q in range(y_ref.shape[0])], axis=1)
    yy = y + dsk_ref[...] * h
    gl = jax.nn.gelu(yy).astype(BF16)
    z = jnp.dot(gl, w_ref[...], preferred_element_type=F32)
    o_ref[0] = x + gate_ref[0] * (z[:, :d] * jax.nn.sigmoid(z[:, d:]))


def _s5_post(x, yq, g, sh, sc, gate, d_skip, w_glu, tm=512):
    bsz, seq, d = x.shape
    tm = min(tm, seq)
    nt = seq // tm
    nq = d // LANES
    kern = functools.partial(_s5_post_kernel, d=d)
    mod_spec = pl.BlockSpec((1, 1, d), lambda b, i: (b, 0, 0))
    tile = pl.BlockSpec((1, tm, d), lambda b, i: (b, i, 0))
    return pl.pallas_call(
        kern,
        out_shape=jax.ShapeDtypeStruct((bsz, seq, d), F32),
        grid=(bsz, nt),
        in_specs=[tile,
                  pl.BlockSpec((nq, tm // SSM_CHUNK, SSM_CHUNK * LANES), lambda b, i: (0, b * nt + i, 0)),
                  pl.BlockSpec((1, d), lambda b, i: (0, 0)), mod_spec, mod_spec, mod_spec,
                  pl.BlockSpec((1, d), lambda b, i: (0, 0)),
                  pl.BlockSpec(w_glu.shape, lambda b, i: (0, 0))],
        out_specs=tile,
        scratch_shapes=[pltpu.VMEM((nq, tm, LANES), F32)],
        compiler_params=_cparams("parallel", "parallel"),
        name="s5_post",
    )(x, yq, g, sh, sc, gate, d_skip, w_glu)


def _s5_layer(x, g, sh, sc, gate, lam_re, lam_im, log_step, b_re, b_im, c_re, c_im, d_skip, w_glu):
    bsz, seq, d = x.shape
    ng, nst = lam_re.shape
    t_len, cg = SSM_CHUNK, SSM_GROUP
    nq = d // LANES
    gq = LANES // cg
    nj = seq // t_len
    bcw, mtw, ccw, a_re, a_im = _s5_prep(lam_re, lam_im, log_step, b_re, b_im, c_re, c_im)
    width = t_len * LANES
    bcq = jnp.swapaxes(bcw.reshape(nq, gq, t_len, cg, width), 1, 2).reshape(nq, width, width)
    mtq = jnp.swapaxes(mtw.reshape(nq, gq, t_len, cg, width), 1, 2).reshape(nq, width, width)
    ccq = jnp.swapaxes(ccw.reshape(nq, gq, 2, nst, width), 1, 2).reshape(nq, 2 * gq * nst, width)
    aq_re = jnp.sum(a_re.reshape(nq, gq, 1, gq * nst), axis=1)
    aq_im = jnp.sum(a_im.reshape(nq, gq, 1, gq * nst), axis=1)

    hq = _s5_pre(x, g, sh, sc)
    yq = _s5_scan(hq, bcq, mtq, ccq, aq_re, aq_im, bsz)
    return _s5_post(x, yq, g, sh, sc, gate, d_skip.reshape(1, d).astype(F32), w_glu.astype(BF16))


def kernel(x, c, ada_w, ada_b, norm_g, conv_w_in, conv_w, conv_w_out, attn_w_in, attn_q_gain, attn_k_gain, attn_w_out, rel_bias, ssm_lambda_re, ssm_lambda_im, ssm_log_step, ssm_b_re, ssm_b_im, ssm_c_re, ssm_c_im, ssm_d, ssm_w_glu, ffn_w_gu, ffn_w_down, moe_router_w, moe_router_b, moe_w_gu, moe_w_down):
    bsz, seq, d = x.shape
    depth = ada_w.shape[0]
    mod = _ada_mod(c, ada_w, ada_b).reshape(depth, bsz, 6, 1, d)
    moe_gu = moe_w_gu.reshape((-1,) + moe_w_gu.shape[2:])
    moe_down = moe_w_down.reshape((-1,) + moe_w_down.shape[2:])
    for i in range(depth):
        sh1, sc1, g1, sh2, sc2, g2 = (mod[i, :, r] for r in range(6))
        gn1 = norm_g[i, 0].reshape(1, d)
        gn2 = norm_g[i, 1].reshape(1, d)
        j = i // N_MIXERS
        if i % N_MIXERS == 0:
            x = _conv_layer(x, gn1, sh1, sc1, g1, conv_w_in[j].astype(BF16), conv_w[j],
                            conv_w_out[j].astype(BF16))
        elif i % N_MIXERS == 1:
            x = _attn_layer(x, gn1, sh1, sc1, g1, attn_w_in[j], attn_q_gain[j], attn_k_gain[j],
                            attn_w_out[j], rel_bias)
        else:
            x = _s5_layer(x, gn1, sh1, sc1, g1, ssm_lambda_re[j], ssm_lambda_im[j], ssm_log_step[j],
                          ssm_b_re[j], ssm_b_im[j], ssm_c_re[j], ssm_c_im[j], ssm_d[j], ssm_w_glu[j])
        if i % 2 == 0:
            x = _ffn_layer(x, gn2, sh2, sc2, g2, ffn_w_gu[i // 2].astype(BF16), ffn_w_down[i // 2].astype(BF16))
        else:
            x = _moe_layer(x, gn2, sh2, sc2, g2, moe_router_w[i // 2], moe_router_b[i // 2],
                           moe_gu, moe_down, ebase=(i // 2) * moe_w_gu.shape[1])
    return x
```

```python
import functools
import math

import numpy as np
import jax
import jax.numpy as jnp
from jax import lax
from jax.experimental import pallas as pl
from jax.experimental.pallas import tpu as pltpu

F32 = jnp.float32
BF16 = jnp.bfloat16
I32 = jnp.int32
HIGHEST = lax.Precision.HIGHEST

DEPTH = 4
N_MIXERS = 3
EPS = 1e-6
CONV_WIDTH = 3
N_HEADS = 16
N_KV_HEADS = 4
N_REP = N_HEADS // N_KV_HEADS
HEAD_DIM = 64
IDX_HEADS = 8
IDX_DIM = 64
TOPK_MAX = 256
REL_BUCKETS = 32
REL_MAX_DIST = 128
SSM_GROUP = 16
SSM_STATE = 64
N_EXPERTS = 8
TOP_K_EXPERTS = 2

VMEM_LIMIT_BYTES = 56 * 1024 * 1024
LANES = 128
SUBLANES = 8

INT_MIN = -(2 ** 31)
INT_MAX = 2 ** 31 - 1
NEG_BIG = -1e30
LOG2E = 1.4426950408889634

ATT_TQ = 128
ATT_KC = 256
BIAS_C = 384
BIAS_W = BIAS_C + ATT_KC
ATT_VROWS = 80

SSM_CHUNK = 8


def _cparams(*sem):
    return pltpu.CompilerParams(dimension_semantics=sem, vmem_limit_bytes=VMEM_LIMIT_BYTES)


def _norm_mod(x, g, shift, scale):
    ms = jnp.mean(x * x, axis=-1, keepdims=True)
    y = x * lax.rsqrt(ms + EPS)
    return (y * g) * (1.0 + scale) + shift


def _silu(x):
    return x * jax.nn.sigmoid(x)


def _ada_kernel(c_ref, w_ref, b_ref, o_ref):
    c = c_ref[...]
    cond = _silu(c)
    o_ref[0] = jnp.dot(cond, w_ref[0], preferred_element_type=F32, precision=HIGHEST) + b_ref[0]


def _ada_mod(c, ada_w, ada_b):
    depth, d, d6 = ada_w.shape
    bsz = c.shape[0]
    tn = d
    return pl.pallas_call(
        _ada_kernel,
        out_shape=jax.ShapeDtypeStruct((depth, bsz, d6), F32),
        grid=(depth, d6 // tn),
        in_specs=[
            pl.BlockSpec((bsz, d), lambda i, j: (0, 0)),
            pl.BlockSpec((1, d, tn), lambda i, j: (i, 0, j)),
            pl.BlockSpec((1, 1, tn), lambda i, j: (i, 0, j)),
        ],
        out_specs=pl.BlockSpec((1, bsz, tn), lambda i, j: (i, 0, j)),
        compiler_params=_cparams("parallel", "parallel"),
        name="ada_mod",
    )(c, ada_w, ada_b.reshape(depth, 1, d6))


def _conv_kernel(x_ref, xh_ref, g_ref, sh_ref, sc_ref, gate_ref, win_ref, wc_ref, wout_ref,
                 o_ref, u_scr, *, tm, d):
    i = pl.program_id(1)
    g = g_ref[...]
    sh = sh_ref[0]
    sc = sc_ref[0]
    x = x_ref[0]
    halo = xh_ref.shape[1]
    xe = jnp.concatenate([xh_ref[0], x], axis=0)
    he = _norm_mod(xe, g, sh, sc).astype(BF16)
    z = jnp.dot(he, win_ref[...], preferred_element_type=F32)
    u_all = z[:, d:2 * d] * z[:, 2 * d:]
    row = lax.broadcasted_iota(I32, (halo + tm, 1), 0)
    u_scr[...] = jnp.where(jnp.logical_or(i > 0, row >= halo), u_all, 0.0)
    b_gate = z[halo:, :d]
    wc = wc_ref[...]
    conv = (wc[0:1, :] * u_scr[halo - 2:halo - 2 + tm, :]
            + wc[1:2, :] * u_scr[halo - 1:halo - 1 + tm, :]
            + wc[2:3, :] * u_scr[halo:halo + tm, :])
    y = jnp.dot((b_gate * conv).astype(BF16), wout_ref[...], preferred_element_type=F32)
    o_ref[0] = x + gate_ref[0] * y


def _conv_layer(x, g, sh, sc, gate, w_in, w_conv, w_out, tm=512):
    bsz, seq, d = x.shape
    tm = min(tm, seq)
    nt = seq // tm
    halo = 2 * SUBLANES
    hb = tm // halo
    kern = functools.partial(_conv_kernel, tm=tm, d=d)
    mod_spec = pl.BlockSpec((1, 1, d), lambda b, i: (b, 0, 0))
    return pl.pallas_call(
        kern,
        out_shape=jax.ShapeDtypeStruct((bsz, seq, d), F32),
        grid=(bsz, nt),
        in_specs=[
            pl.BlockSpec((1, tm, d), lambda b, i: (b, i, 0)),
            pl.BlockSpec((1, halo, d), lambda b, i: (b, jnp.maximum(i * hb - 1, 0), 0)),
            pl.BlockSpec((1, d), lambda b, i: (0, 0)),
            mod_spec, mod_spec, mod_spec,
            pl.BlockSpec((d, 3 * d), lambda b, i: (0, 0)),
            pl.BlockSpec((CONV_WIDTH, d), lambda b, i: (0, 0)),
            pl.BlockSpec((d, d), lambda b, i: (0, 0)),
        ],
        out_specs=pl.BlockSpec((1, tm, d), lambda b, i: (b, i, 0)),
        scratch_shapes=[pltpu.VMEM((tm + halo, d), F32)],
        compiler_params=_cparams("parallel", "parallel"),
        name="conv_mixer",
    )(x, x, g, sh, sc, gate, w_in, w_conv, w_out)


def _ffn_kernel(x_ref, g_ref, sh_ref, sc_ref, gate_ref, wgu_ref, wd_ref, o_ref, *, dff, nchunk):
    x = x_ref[0]
    h = _norm_mod(x, g_ref[...], sh_ref[0], sc_ref[0]).astype(BF16)
    cols = dff // nchunk
    acc = jnp.zeros(x.shape, F32)
    for c in range(nchunk):
        gg = jnp.dot(h, wgu_ref[:, c * cols:(c + 1) * cols], preferred_element_type=F32)
        uu = jnp.dot(h, wgu_ref[:, dff + c * cols:dff + (c + 1) * cols], preferred_element_type=F32)
        a = (_silu(gg) * uu).astype(BF16)
        acc = acc + jnp.dot(a, wd_ref[c * cols:(c + 1) * cols, :], preferred_element_type=F32)
    o_ref[0] = x + gate_ref[0] * acc


def _ffn_layer(x, g, sh, sc, gate, w_gu, w_down, tm=512):
    bsz, seq, d = x.shape
    dff = w_down.shape[0]
    tm = min(tm, seq)
    kern = functools.partial(_ffn_kernel, dff=dff, nchunk=2)
    mod_spec = pl.BlockSpec((1, 1, d), lambda b, i: (b, 0, 0))
    return pl.pallas_call(
        kern,
        out_shape=jax.ShapeDtypeStruct((bsz, seq, d), F32),
        grid=(bsz, seq // tm),
        in_specs=[
            pl.BlockSpec((1, tm, d), lambda b, i: (b, i, 0)),
            pl.BlockSpec((1, d), lambda b, i: (0, 0)),
            mod_spec, mod_spec, mod_spec,
            pl.BlockSpec((d, 2 * dff), lambda b, i: (0, 0), pipeline_mode=pl.Buffered(1)),
            pl.BlockSpec((dff, d), lambda b, i: (0, 0), pipeline_mode=pl.Buffered(1)),
        ],
        out_specs=pl.BlockSpec((1, tm, d), lambda b, i: (b, i, 0)),
        compiler_params=_cparams("parallel", "parallel"),
        name="ffn_dense",
    )(x, g, sh, sc, gate, w_gu, w_down)


MOE_SB = 512
MOE_RT = 512
MOE_ALIGN = 16
MOE_WIN = 256
MOE_CAP = MOE_WIN - MOE_ALIGN


def _moe_router_kernel(x_ref, g_ref, sh_ref, sc_ref, rw_ref, rb_ref, h_ref, meta_ref):
    hf = _norm_mod(x_ref[...], g_ref[...], sh_ref[0], sc_ref[0])
    h_hi = hf.astype(BF16)
    h_ref[...] = h_hi
    rw = rw_ref[...]
    w_hi = rw.astype(BF16)
    w_lo = (rw - w_hi.astype(F32)).astype(BF16)
    h_lo = (hf - h_hi.astype(F32)).astype(BF16)
    logits = (jnp.dot(h_hi, w_hi, preferred_element_type=F32) + jnp.dot(h_lo, w_hi, preferred_element_type=F32)
              + jnp.dot(h_hi, w_lo, preferred_element_type=F32) + rb_ref[...])
    mx = jnp.max(logits, axis=-1, keepdims=True)
    ex = jnp.exp(logits - mx)
    probs = ex / jnp.sum(ex, axis=-1, keepdims=True)
    lane = lax.broadcasted_iota(I32, probs.shape, 1)
    m1 = jnp.max(probs, axis=-1, keepdims=True)
    i1 = jnp.min(jnp.where(probs == m1, lane, LANES), axis=-1, keepdims=True)
    rest = jnp.where(lane == i1, -1.0, probs)
    m2 = jnp.max(rest, axis=-1, keepdims=True)
    i2 = jnp.min(jnp.where(rest == m2, lane, LANES), axis=-1, keepdims=True)
    den = m1 + m2
    gates = jnp.where(lane == i1, m1 / den, 0.0) + jnp.where(lane == i2, m2 / den, 0.0)
    chosen = jnp.where(jnp.logical_or(lane == i1, lane == i2), 1.0, 0.0)
    nrow = chosen.shape[0]
    below = jnp.where(lax.broadcasted_iota(I32, (nrow, nrow), 1) < lax.broadcasted_iota(I32, (nrow, nrow), 0),
                      1.0, 0.0).astype(BF16)
    rank = jnp.dot(below, chosen.astype(BF16), preferred_element_type=F32)
    meta_ref[...] = gates + pltpu.roll(chosen, N_EXPERTS, 1) + pltpu.roll(rank, 2 * N_EXPERTS, 1)


def _moe_router(xt, g, sh, sc, rw_pad, rb_pad, tpb, tm):
    n, d = xt.shape
    mod_spec = pl.BlockSpec((1, 1, d), lambda i: (i // tpb, 0, 0))
    return pl.pallas_call(
        _moe_router_kernel,
        out_shape=[jax.ShapeDtypeStruct((n, d), BF16), jax.ShapeDtypeStruct((n, LANES), F32)],
        grid=(n // tm,),
        in_specs=[pl.BlockSpec((tm, d), lambda i: (i, 0)),
                  pl.BlockSpec((1, d), lambda i: (0, 0)), mod_spec, mod_spec,
                  pl.BlockSpec((d, LANES), lambda i: (0, 0)),
                  pl.BlockSpec((1, LANES), lambda i: (0, 0))],
        out_specs=[pl.BlockSpec((tm, d), lambda i: (i, 0)), pl.BlockSpec((tm, LANES), lambda i: (i, 0))],
        compiler_params=_cparams("parallel"),
        name="moe_router",
    )(xt, g, sh, sc, rw_pad, rb_pad)


def _window(start, count, w):
    s = start + w * MOE_CAP
    n = jnp.minimum(count - w * MOE_CAP, MOE_CAP)
    a = pl.multiple_of((s // MOE_ALIGN) * MOE_ALIGN, MOE_ALIGN)
    return s, n, a


def _moe_dispatch_kernel(start_ref, cnt_ref, h_ref, pos_ref, xs_init_ref, xs_ref,
                         buf, carry, sems, pending):
    del xs_init_ref
    b = pl.program_id(0)
    nb = pl.num_programs(0)
    ne = buf.shape[0]

    @pl.when(b == 0)
    def _():
        carry[...] = jnp.zeros(carry.shape, BF16)
        for e in range(ne):
            pending[e] = 0

    h = h_ref[...]
    riota = lax.broadcasted_iota(I32, (MOE_WIN, h.shape[0]), 0)

    def out_copy(e, a):
        return pltpu.make_async_copy(buf.at[e], xs_ref.at[pl.ds(a, MOE_WIN)], sems.at[e])

    for e in range(ne):
        posrow = pos_ref[0, e:e + 1, :]
        nwin = (cnt_ref[b, e] + MOE_CAP - 1) // MOE_CAP

        def wbody(w, carry_unused, e=e, posrow=posrow):
            s, n, a = _window(start_ref[b, e], cnt_ref[b, e], w)

            @pl.when(pending[e] == 1)
            def _():
                out_copy(e, 0).wait()

            rel = jnp.where(jnp.logical_and(posrow >= s, posrow < s + n), posrow - a, -1)
            onehot = jnp.where(rel == riota, 1.0, 0.0).astype(BF16)
            buf[e] = jnp.dot(onehot, h, preferred_element_type=F32).astype(BF16)
            buf[e, 0:MOE_ALIGN, :] = buf[e, 0:MOE_ALIGN, :] + carry[e]
            c0 = pl.multiple_of(((s + n) // MOE_ALIGN) * MOE_ALIGN - a, MOE_ALIGN)
            carry[e] = buf[e, pl.ds(c0, MOE_ALIGN), :]
            out_copy(e, a).start()
            pending[e] = 1
            return carry_unused

        lax.fori_loop(0, nwin, wbody, 0)

    @pl.when(b == nb - 1)
    def _():
        for e in range(ne):
            @pl.when(pending[e] == 1)
            def _():
                out_copy(e, 0).wait()


def _moe_dispatch(start, cnt, h, pos_t, ncap):
    n, d = h.shape
    nb, ne, sb = pos_t.shape
    xs_init = jnp.zeros((ncap, d), BF16)
    return pl.pallas_call(
        _moe_dispatch_kernel,
        out_shape=jax.ShapeDtypeStruct((ncap, d), BF16),
        grid_spec=pltpu.PrefetchScalarGridSpec(
            num_scalar_prefetch=2,
            grid=(nb,),
            in_specs=[pl.BlockSpec((sb, d), lambda b, *_: (b, 0)),
                      pl.BlockSpec((1, ne, sb), lambda b, *_: (b, 0, 0)),
                      pl.BlockSpec(memory_space=pl.ANY)],
            out_specs=pl.BlockSpec(memory_space=pl.ANY),
            scratch_shapes=[pltpu.VMEM((ne, MOE_WIN, d), BF16),
                            pltpu.VMEM((ne, MOE_ALIGN, d), BF16),
                            pltpu.SemaphoreType.DMA((ne,)),
                            pltpu.SMEM((ne,), I32)],
        ),
        input_output_aliases={4: 0},
        compiler_params=_cparams("arbitrary"),
        name="moe_dispatch",
    )(start, cnt, h, pos_t, xs_init)


MOE_WCHUNKS = 8


def _moe_ffn_kernel(blk_ref, exp_ref, nt_ref, xs_ref, wgu_hbm, wd_hbm, ys_init_ref, ys_ref,
                    wgu_scr, wd_scr, gu_stage, d_stage, sems, ahead, *, dff, nchunk, ebase):
    del blk_ref, ys_init_ref
    k = pl.program_id(0)
    d = xs_ref.shape[1]
    gu_rows = d // MOE_WCHUNKS
    d_rows = dff // MOE_WCHUNKS
    e = ebase + exp_ref[k]
    k_next = jnp.minimum(k + 1, pl.num_programs(0) - 1)
    e_next = ebase + exp_ref[k_next]

    def gu_copy(ex, c, slot):
        return pltpu.make_async_copy(wgu_hbm.at[ex, pl.ds(c * gu_rows, gu_rows), :], gu_stage.at[slot],
                                     sems.at[0, slot])

    def d_copy(ex, c, slot):
        return pltpu.make_async_copy(wd_hbm.at[ex, pl.ds(c * d_rows, d_rows), :], d_stage.at[slot],
                                     sems.at[1, slot])

    def start_piece(ex, c):
        gu_copy(ex, c, c % 2).start()
        d_copy(ex, c, c % 2).start()

    @pl.when(k == 0)
    def _():
        ahead[0] = 0

    new_expert = jnp.logical_or(k == 0, exp_ref[k] != exp_ref[jnp.maximum(k - 1, 0)])

    @pl.when(jnp.logical_and(k < nt_ref[0], new_expert))
    def _():
        @pl.when(ahead[0] == 0)
        def _():
            start_piece(e, 0)
            start_piece(e, 1)

        ahead[0] = 0
        for c in range(MOE_WCHUNKS):
            slot = c % 2
            gu_copy(e, c, slot).wait()
            wgu_scr[c * gu_rows:(c + 1) * gu_rows, :] = gu_stage[slot].astype(BF16)
            d_copy(e, c, slot).wait()
            wd_scr[c * d_rows:(c + 1) * d_rows, :] = d_stage[slot].astype(BF16)
            if c + 2 < MOE_WCHUNKS:
                start_piece(e, c + 2)

    last_of_expert = jnp.logical_and(k + 1 < nt_ref[0], exp_ref[k_next] != exp_ref[k])

    @pl.when(jnp.logical_and(k < nt_ref[0], last_of_expert))
    def _():
        start_piece(e_next, 0)
        start_piece(e_next, 1)
        ahead[0] = 1

    @pl.when(k < nt_ref[0])
    def _():
        x = xs_ref[...]
        cols = dff // nchunk
        acc = jnp.zeros(x.shape, F32)
        for c in range(nchunk):
            gg = jnp.dot(x, wgu_scr[:, c * cols:(c + 1) * cols], preferred_element_type=F32)
            uu = jnp.dot(x, wgu_scr[:, dff + c * cols:dff + (c + 1) * cols], preferred_element_type=F32)
            a = (_silu(gg) * uu).astype(BF16)
            acc = acc + jnp.dot(a, wd_scr[c * cols:(c + 1) * cols, :], preferred_element_type=F32)
        ys_ref[...] = acc.astype(BF16)


def _moe_ffn(tile_blk, tile_exp, ntiles, xs, w_gu, w_down, ebase):
    ncap, d = xs.shape
    dff = w_down.shape[1]
    assert d % (MOE_WCHUNKS * 2 * SUBLANES) == 0 and dff % (MOE_WCHUNKS * 2 * SUBLANES) == 0
    kern = functools.partial(_moe_ffn_kernel, dff=dff, nchunk=2, ebase=ebase)
    ys_init = jnp.zeros((ncap, d), BF16)
    return pl.pallas_call(
        kern,
        out_shape=jax.ShapeDtypeStruct((ncap, d), BF16),
        grid_spec=pltpu.PrefetchScalarGridSpec(
            num_scalar_prefetch=3,
            grid=(ncap // MOE_RT,),
            in_specs=[pl.BlockSpec((MOE_RT, d), lambda k, blk, exp, nt: (blk[k], 0)),
                      pl.BlockSpec(memory_space=pl.ANY),
                      pl.BlockSpec(memory_space=pl.ANY),
                      pl.BlockSpec(memory_space=pl.ANY)],
            out_specs=pl.BlockSpec((MOE_RT, d), lambda k, blk, exp, nt: (blk[k], 0)),
            scratch_shapes=[pltpu.VMEM((d, 2 * dff), BF16),
                            pltpu.VMEM((dff, d), BF16),
                            pltpu.VMEM((2, d // MOE_WCHUNKS, 2 * dff), F32),
                            pltpu.VMEM((2, dff // MOE_WCHUNKS, d), F32),
                            pltpu.SemaphoreType.DMA((2, 2)),
                            pltpu.SMEM((1,), I32)],
        ),
        input_output_aliases={6: 0},
        compiler_params=_cparams("arbitrary"),
        name="moe_ffn",
    )(tile_blk, tile_exp, ntiles, xs, w_gu, w_down, ys_init)


def _moe_combine_kernel(start_ref, cnt_ref, x_ref, gate_ref, pos_ref, gts_ref, ys_ref, o_ref,
                        buf, sems, acc_scr):
    b = pl.program_id(0)
    nb = pl.num_programs(0)
    ne = buf.shape[1]
    sb = x_ref.shape[0]
    slot = b % 2

    def in_copy(sl, e, a):
        return pltpu.make_async_copy(ys_ref.at[pl.ds(a, MOE_WIN)], buf.at[sl, e], sems.at[sl, e])

    def start_first_windows(blk, sl):
        for e in range(ne):
            @pl.when(cnt_ref[blk, e] > 0)
            def _():
                _, _, a = _window(start_ref[blk, e], cnt_ref[blk, e], 0)
                in_copy(sl, e, a).start()

    @pl.when(b == 0)
    def _():
        start_first_windows(0, 0)

    @pl.when(b + 1 < nb)
    def _():
        start_first_windows(b + 1, 1 - slot)

    acc_scr[...] = jnp.zeros(acc_scr.shape, F32)
    liota = lax.broadcasted_iota(I32, (sb, MOE_WIN), 1)
    for e in range(ne):
        poscol = pos_ref[:, e:e + 1]
        gcol = gts_ref[:, e:e + 1]
        nwin = (cnt_ref[b, e] + MOE_CAP - 1) // MOE_CAP

        def wbody(w, carry_unused, e=e, poscol=poscol, gcol=gcol):
            s, n, a = _window(start_ref[b, e], cnt_ref[b, e], w)

            @pl.when(w > 0)
            def _():
                in_copy(slot, e, a).start()

            in_copy(slot, e, a).wait()
            rel = jnp.where(jnp.logical_and(poscol >= s, poscol < s + n), poscol - a, -1)
            onehot = jnp.where(rel == liota, 1.0, 0.0).astype(BF16)
            acc_scr[...] += gcol * jnp.dot(onehot, buf[slot, e], preferred_element_type=F32)
            return carry_unused

        lax.fori_loop(0, nwin, wbody, 0)

    o_ref[...] = x_ref[...] + gate_ref[0] * acc_scr[...]


def _moe_combine(start, cnt, xt, gate, pos_n, gates_n, ys, tpb):
    n, d = xt.shape
    nb, ne = cnt.shape
    sb = n // nb
    return pl.pallas_call(
        _moe_combine_kernel,
        out_shape=jax.ShapeDtypeStruct((n, d), F32),
        grid_spec=pltpu.PrefetchScalarGridSpec(
            num_scalar_prefetch=2,
            grid=(nb,),
            in_specs=[pl.BlockSpec((sb, d), lambda b, *_: (b, 0)),
                      pl.BlockSpec((1, 1, d), lambda b, *_: (b // tpb, 0, 0)),
                      pl.BlockSpec((sb, ne), lambda b, *_: (b, 0)),
                      pl.BlockSpec((sb, ne), lambda b, *_: (b, 0)),
                      pl.BlockSpec(memory_space=pl.ANY)],
            out_specs=pl.BlockSpec((sb, d), lambda b, *_: (b, 0)),
            scratch_shapes=[pltpu.VMEM((2, ne, MOE_WIN, d), BF16),
                            pltpu.SemaphoreType.DMA((2, ne)),
                            pltpu.VMEM((sb, d), F32)],
        ),
        compiler_params=_cparams("arbitrary"),
        name="moe_combine",
    )(start, cnt, xt, gate, pos_n, gates_n, ys)


def _moe_layer(x, g, sh, sc, gate, router_w, router_b, w_gu, w_down, ebase=0):
    bsz, seq, d = x.shape
    ne = router_w.shape[1]
    n = bsz * seq
    sb = min(MOE_SB, seq)
    tpb = seq // sb
    nb = n // sb
    rt = MOE_RT
    xt = x.reshape(n, d)
    rw_pad = jnp.pad(router_w.astype(F32), ((0, 0), (0, LANES - ne)))
    rb_pad = jnp.pad(router_b.astype(F32).reshape(1, ne), ((0, 0), (0, LANES - ne)), constant_values=NEG_BIG)
    h, meta = _moe_router(xt, g, sh, sc, rw_pad, rb_pad, tpb, sb)
    gates = meta[:, :ne]
    sel = (meta[:, ne:2 * ne] > 0.5).astype(I32)

    selb = sel.reshape(nb, sb, ne)
    cnt = jnp.sum(selb, axis=1)
    rank = meta[:, 2 * ne:3 * ne].astype(I32).reshape(nb, sb, ne)
    total = jnp.sum(cnt, axis=0)
    region = ((total + MOE_WIN + rt - 1) // rt) * rt
    off = jnp.cumsum(region) - region
    start = (off[None, :] + jnp.cumsum(cnt, axis=0) - cnt).astype(I32)
    pos = jnp.where(selb > 0, start[:, None, :] + rank, -1).astype(I32)
    ncap = 2 * n + ne * (MOE_WIN + rt)
    tiles_e = (total + rt - 1) // rt
    tcum = jnp.cumsum(tiles_e)
    ntiles = tcum[-1]
    kk = jnp.minimum(jnp.arange(ncap // rt), ntiles - 1)
    tile_exp = jnp.sum((kk[:, None] >= tcum[None, :]).astype(I32), axis=1)
    tile_blk = (off[tile_exp] // rt + kk - (tcum - tiles_e)[tile_exp]).astype(I32)

    xs = _moe_dispatch(start, cnt.astype(I32), h, jnp.swapaxes(pos, 1, 2), ncap)
    ys = _moe_ffn(tile_blk, tile_exp, ntiles.reshape(1).astype(I32), xs, w_gu, w_down, ebase)
    out = _moe_combine(start, cnt.astype(I32), xt, gate, pos.reshape(n, ne), gates, ys, tpb)
    return out.reshape(bsz, seq, d)


def _head_norm(q, hsum_ref, hexp_ref, gain, scale):
    ms = jnp.dot((q * q).astype(BF16), hsum_ref[...], preferred_element_type=F32)
    r = lax.rsqrt(ms + EPS)
    r_hi = r.astype(BF16)
    r_lo = (r - r_hi.astype(F32)).astype(BF16)
    rexp = (jnp.dot(r_hi, hexp_ref[...], preferred_element_type=F32)
            + jnp.dot(r_lo, hexp_ref[...], preferred_element_type=F32))
    return q * rexp * (gain * scale)


def _qkv_kernel(x_ref, g_ref, sh_ref, sc_ref, wm_ref, ws_ref, qg_ref, kg_ref,
                hsq_ref, heq_ref, hsk_ref, hek_ref,
                q_ref, k_ref, v_ref, qi_ref, kiwi_ref, *, dq, dk, dqi):
    x = x_ref[0]
    h = _norm_mod(x, g_ref[...], sh_ref[0], sc_ref[0]).astype(BF16)
    z = jnp.dot(h, wm_ref[...], preferred_element_type=F32)
    q = z[:, :dq]
    k = z[:, dq:dq + dk]
    v = z[:, dq + dk:dq + 2 * dk]
    qi = z[:, dq + 2 * dk:dq + 2 * dk + dqi]
    q_ref[0] = _head_norm(q, hsq_ref, heq_ref, qg_ref[...], HEAD_DIM ** -0.5 * LOG2E).astype(BF16)
    k_ref[0] = _head_norm(k, hsk_ref, hek_ref, kg_ref[...], 1.0).astype(BF16)
    v_ref[0] = v.astype(BF16)
    qi_ref[0] = qi.astype(BF16)
    kiwi_ref[0] = jnp.dot(h, ws_ref[...], preferred_element_type=F32)


def _head_indicators(nheads):
    hs = np.zeros((nheads * HEAD_DIM, LANES), np.float32)
    he = np.zeros((LANES, nheads * HEAD_DIM), np.float32)
    for hd in range(nheads):
        hs[hd * HEAD_DIM:(hd + 1) * HEAD_DIM, hd] = 1.0 / HEAD_DIM
        he[hd, hd * HEAD_DIM:(hd + 1) * HEAD_DIM] = 1.0
    return jnp.asarray(hs, BF16), jnp.asarray(he, BF16)


def _qkv_proj(x, g, sh, sc, w_main, w_small, q_gain, k_gain, tm=512):
    bsz, seq, d = x.shape
    tm = min(tm, seq)
    dq = N_HEADS * HEAD_DIM
    dk = N_KV_HEADS * HEAD_DIM
    dqi = IDX_HEADS * IDX_DIM
    hsq, heq = _head_indicators(N_HEADS)
    hsk, hek = _head_indicators(N_KV_HEADS)
    qg = jnp.tile(q_gain.reshape(1, HEAD_DIM), (1, N_HEADS)).astype(F32)
    kg = jnp.tile(k_gain.reshape(1, HEAD_DIM), (1, N_KV_HEADS)).astype(F32)
    kern = functools.partial(_qkv_kernel, dq=dq, dk=dk, dqi=dqi)
    mod_spec = pl.BlockSpec((1, 1, d), lambda b, i: (b, 0, 0))

    def full(a):
        return pl.BlockSpec(a.shape, lambda b, i: (0,) * a.ndim)

    def out(n):
        return pl.BlockSpec((1, tm, n), lambda b, i: (b, i, 0))

    return pl.pallas_call(
        kern,
        out_shape=[
            jax.ShapeDtypeStruct((bsz, seq, dq), BF16),
            jax.ShapeDtypeStruct((bsz, seq, dk), BF16),
            jax.ShapeDtypeStruct((bsz, seq, dk), BF16),
            jax.ShapeDtypeStruct((bsz, seq, dqi), BF16),
            jax.ShapeDtypeStruct((bsz, seq, LANES), F32),
        ],
        grid=(bsz, seq // tm),
        in_specs=[
            pl.BlockSpec((1, tm, d), lambda b, i: (b, i, 0)),
            pl.BlockSpec((1, d), lambda b, i: (0, 0)),
            mod_spec, mod_spec,
            full(w_main), full(w_small), full(qg), full(kg),
            full(hsq), full(heq), full(hsk), full(hek),
        ],
        out_specs=[out(dq), out(dk), out(dk), out(dqi), out(LANES)],
        compiler_params=_cparams("parallel", "parallel"),
        name="attn_qkv",
    )(x, g, sh, sc, w_main, w_small, qg, kg, hsq, heq, hsk, hek)


def _rel_bucket_np(dist):
    max_exact = REL_BUCKETS // 2
    d = np.maximum(dist, 1).astype(np.float64)
    large = max_exact + (np.log(d / max_exact) / math.log(REL_MAX_DIST / max_exact)
                         * (REL_BUCKETS - max_exact)).astype(np.int32)
    large = np.minimum(large, REL_BUCKETS - 1)
    return np.where(dist < max_exact, dist, large).astype(np.int32)


def _bias_table_kernel(bucket_ref, rb_ref, o_ref):
    hd = pl.program_id(0)
    bucket = bucket_ref[...]
    acc = jnp.zeros(bucket.shape, F32)
    for b in range(REL_BUCKETS):
        acc = jnp.where(bucket == b, rb_ref[b, hd] * LOG2E, acc)
    o_ref[0] = acc


def _bias_table(rel_bias):
    w = np.arange(BIAS_W)[:, None]
    i = np.arange(ATT_TQ)[None, :]
    bucket = jnp.asarray(_rel_bucket_np(np.maximum(i - w + BIAS_C, 0)))
    return pl.pallas_call(
        _bias_table_kernel,
        out_shape=jax.ShapeDtypeStruct((N_HEADS, BIAS_W, ATT_TQ), F32),
        grid=(N_HEADS,),
        in_specs=[
            pl.BlockSpec((BIAS_W, ATT_TQ), lambda hd: (0, 0)),
            pl.BlockSpec(memory_space=pltpu.SMEM),
        ],
        out_specs=pl.BlockSpec((1, BIAS_W, ATT_TQ), lambda hd: (hd, 0, 0)),
        compiler_params=_cparams("arbitrary"),
        name="attn_bias_table",
    )(bucket, rel_bias.astype(F32))


def _attn_kernel(qT_ref, qiT_ref, wiT_ref, k_ref, vT_ref, ki_ref, pt_ref, o_ref,
                 keys_scr, negm_scr, pidx_scr, oT_scr, acc_scr, qall_scr, sa_scr, sb_scr, *, top_k):
    tq, kc = ATT_TQ, ATT_KC
    qt = pl.program_id(1)
    q0 = qt * tq
    nch = (q0 + tq + kc - 1) // kc
    tpos = q0 + lax.broadcasted_iota(I32, (kc, tq), 1)
    srow = lax.broadcasted_iota(I32, (kc, tq), 0)

    qiT = qiT_ref[0]
    qi_all = jnp.concatenate([qiT[hd * IDX_DIM:(hd + 1) * IDX_DIM, :] for hd in range(IDX_HEADS)], axis=1)
    wiT = wiT_ref[0]

    def score_chunk(c):
        ks = pl.multiple_of(c * kc, kc)
        kic = ki_ref[0, pl.ds(ks, kc), :]
        dots = jnp.dot(kic, qi_all, preferred_element_type=F32)
        acc = jnp.zeros((kc, tq), F32)
        for hd in range(IDX_HEADS):
            acc = acc + jnp.maximum(dots[:, hd * tq:(hd + 1) * tq], 0.0) * wiT[hd:hd + 1, :]
        acc = jnp.where(acc == 0.0, 0.0, acc)
        bits = pltpu.bitcast(acc, I32)
        key = jnp.where(bits < 0, bits ^ INT_MAX, bits)
        key = jnp.where(ks + srow <= tpos, key, INT_MIN)
        keys_scr[pl.ds(ks, kc), :] = key

    npair = (nch + 1) // 2

    def score_pair(i, carry):
        score_chunk(2 * i)
        score_chunk(2 * i + 1)
        return carry

    lax.fori_loop(0, npair, score_pair, 0)

    @pl.when(nch % 2 == 1)
    def _():
        negm_scr[pl.ds(pl.multiple_of(nch * kc, kc), kc), :] = jnp.full((kc, tq), NEG_BIG, F32)

    srow2 = lax.broadcasted_iota(I32, (2 * kc, tq), 0)

    def count(pred):
        def body(c, acc):
            ks = pl.multiple_of(c * 2 * kc, 2 * kc)
            m = pred(keys_scr[pl.ds(ks, 2 * kc), :], ks + srow2).astype(I32)
            return acc + jnp.sum(m.reshape(2 * kc // SUBLANES, SUBLANES, tq), axis=0)
        acc = lax.fori_loop(0, npair, body, jnp.zeros((SUBLANES, tq), I32))
        return jnp.sum(acc, axis=0, keepdims=True)

    def bit_body(it, p):
        cand_p = p | lax.shift_left(jnp.int32(1), 31 - it)
        cand = cand_p ^ INT_MIN
        cnt = count(lambda k, s: k >= cand)
        return jnp.where(cnt >= top_k, cand_p, p)

    p_fin = lax.fori_loop(0, 32, bit_body, jnp.zeros((1, tq), I32))
    v = p_fin ^ INT_MIN

    cnt_gt = count(lambda k, s: k > v)
    cnt_eq = count(lambda k, s: k == v)
    need = top_k - cnt_gt
    pidx_scr[...] = jnp.full((1, tq), INT_MAX, I32)
    pos_bits = (keys_scr.shape[0] - 1).bit_length()

    @pl.when(jnp.max(cnt_eq - need) > 0)
    def _():
        def ibit(it, p):
            cand = p | lax.shift_left(jnp.int32(1), pos_bits - 1 - it)
            cnt = count(lambda k, s: jnp.logical_and(k == v, s < cand))
            return jnp.where(cnt < need, cand, p)
        pidx_scr[...] = lax.fori_loop(0, pos_bits, ibit, jnp.zeros((1, tq), I32))

    pidx = pidx_scr[...]

    def mask_chunk(c, carry):
        ks = pl.multiple_of(c * kc, kc)
        k = keys_scr[pl.ds(ks, kc), :]
        spos = ks + srow
        sel = jnp.logical_or(k > v, jnp.logical_and(k == v, spos <= pidx))
        sel = jnp.logical_and(sel, spos <= tpos)
        negm_scr[pl.ds(ks, kc), :] = jnp.where(sel, 0.0, NEG_BIG)
        return carry

    lax.fori_loop(0, nch, mask_chunk, 0)

    acc_scr[...] = jnp.zeros(acc_scr.shape, F32)
    for n in range(N_KV_HEADS):
        r0 = n * N_REP * HEAD_DIM
        qall_scr[n] = jnp.concatenate(
            [qT_ref[0, r0 + g * HEAD_DIM:r0 + (g + 1) * HEAD_DIM, :] for g in range(N_REP)], axis=1)

    def qk_chunk(c, s_ref):
        ks = pl.multiple_of(c * kc, kc)
        for n in range(N_KV_HEADS):
            s_ref[n] = jnp.dot(k_ref[0, n, pl.ds(ks, kc), :], qall_scr[n], preferred_element_type=F32)

    def softmax_pv(c, s_ref, ms, far):
        ks = pl.multiple_of(c * kc, kc)
        negm = negm_scr[pl.ds(ks, kc), :]
        w0 = pl.multiple_of(jnp.clip(BIAS_C - (q0 - ks), 0, BIAS_C), LANES)
        new_ms = []
        for n in range(N_KV_HEADS):
            s = s_ref[n]
            if far:
                cvec = jnp.concatenate([pt_ref[n * N_REP + g, 0:1, :] for g in range(N_REP)], axis=1)
                lg = jnp.concatenate([s[:, g * tq:(g + 1) * tq] + negm for g in range(N_REP)], axis=1)
                m_new = jnp.maximum(ms[n], jnp.max(lg, axis=0, keepdims=True) + cvec)
                p = jnp.exp2(lg - (m_new - cvec))
            else:
                lg = jnp.concatenate(
                    [s[:, g * tq:(g + 1) * tq] + pt_ref[n * N_REP + g, pl.ds(w0, kc), :] + negm
                     for g in range(N_REP)], axis=1)
                m_new = jnp.maximum(ms[n], jnp.max(lg, axis=0, keepdims=True))
                p = jnp.exp2(lg - m_new)
            alpha = jnp.exp2(ms[n] - m_new)
            acc_scr[n] = alpha * acc_scr[n] + jnp.dot(vT_ref[0, c, n], p.astype(BF16),
                                                      preferred_element_type=F32)
            new_ms.append(m_new)
        return tuple(new_ms)

    last_chunk = k_ref.shape[2] // kc - 1

    def pair_step(i, ms, far):
        c0 = 2 * i
        qk_chunk(c0 + 1, sb_scr)
        ms = softmax_pv(c0, sa_scr, ms, far)
        qk_chunk(jnp.minimum(c0 + 2, last_chunk), sa_scr)
        return softmax_pv(c0 + 1, sb_scr, ms, far)

    n_far = jnp.clip((q0 - BIAS_C + kc) // kc, 0, nch)
    ms = tuple(jnp.full((1, N_REP * tq), NEG_BIG, F32) for _ in range(N_KV_HEADS))
    qk_chunk(0, sa_scr)
    ms = lax.fori_loop(0, n_far // 2, functools.partial(pair_step, far=True), ms)
    lax.fori_loop(n_far // 2, npair, functools.partial(pair_step, far=False), ms)
    for n in range(N_KV_HEADS):
        o_t = acc_scr[n, 0:HEAD_DIM, :] / acc_scr[n, HEAD_DIM:HEAD_DIM + 1, :]
        for g in range(N_REP):
            r0 = (n * N_REP + g) * HEAD_DIM
            oT_scr[r0:r0 + HEAD_DIM, :] = o_t[:, g * tq:(g + 1) * tq]

    o_ref[0] = oT_scr[...].T.astype(BF16)


def _attention(qT, qiT, wiT, k4, vT, ki, ptab, top_k):
    bsz, dq, seq = qT.shape
    tq = ATT_TQ
    assert seq % (2 * ATT_KC) == 0
    kern = functools.partial(_attn_kernel, top_k=top_k)
    return pl.pallas_call(
        kern,
        out_shape=jax.ShapeDtypeStruct((bsz, seq, dq), BF16),
        grid=(bsz, seq // tq),
        in_specs=[
            pl.BlockSpec((1, dq, tq), lambda b, i: (b, 0, i)),
            pl.BlockSpec((1, qiT.shape[1], tq), lambda b, i: (b, 0, i)),
            pl.BlockSpec((1, wiT.shape[1], tq), lambda b, i: (b, 0, i)),
            pl.BlockSpec((1,) + k4.shape[1:], lambda b, i: (b, 0, 0, 0)),
            pl.BlockSpec((1,) + vT.shape[1:], lambda b, i: (b, 0, 0, 0, 0)),
            pl.BlockSpec((1,) + ki.shape[1:], lambda b, i: (b, 0, 0)),
            pl.BlockSpec(ptab.shape, lambda b, i: (0, 0, 0)),
        ],
        out_specs=pl.BlockSpec((1, tq, dq), lambda b, i: (b, i, 0)),
        scratch_shapes=[
            pltpu.VMEM((seq, tq), I32),
            pltpu.VMEM((seq, tq), F32),
            pltpu.VMEM((1, tq), I32),
            pltpu.VMEM((dq, tq), F32),
            pltpu.VMEM((N_KV_HEADS, ATT_VROWS, N_REP * tq), F32),
            pltpu.VMEM((N_KV_HEADS, HEAD_DIM, N_REP * tq), BF16),
            pltpu.VMEM((N_KV_HEADS, ATT_KC, N_REP * tq), F32),
            pltpu.VMEM((N_KV_HEADS, ATT_KC, N_REP * tq), F32),
        ],
        compiler_params=_cparams("parallel", "arbitrary"),
        name="attn_core",
    )(qT, qiT, wiT, k4, vT, ki, ptab)


def _proj_res_kernel(a_ref, x_ref, gate_ref, w_ref, o_ref):
    y = jnp.dot(a_ref[0], w_ref[...], preferred_element_type=F32)
    o_ref[0] = x_ref[0] + gate_ref[0] * y


def _proj_residual(a, x, gate, w, tm=512):
    bsz, seq, d = x.shape
    tm = min(tm, seq)
    return pl.pallas_call(
        _proj_res_kernel,
        out_shape=jax.ShapeDtypeStruct((bsz, seq, d), F32),
        grid=(bsz, seq // tm),
        in_specs=[
            pl.BlockSpec((1, tm, a.shape[2]), lambda b, i: (b, i, 0)),
            pl.BlockSpec((1, tm, d), lambda b, i: (b, i, 0)),
            pl.BlockSpec((1, 1, d), lambda b, i: (b, 0, 0)),
            pl.BlockSpec(w.shape, lambda b, i: (0, 0)),
        ],
        out_specs=pl.BlockSpec((1, tm, d), lambda b, i: (b, i, 0)),
        compiler_params=_cparams("parallel", "parallel"),
        name="proj_residual",
    )(a, x, gate, w)


def _attn_layer(x, g, sh, sc, gate, w_in, q_gain, k_gain, w_out, rel_bias):
    bsz, seq, d = x.shape
    top_k = min(TOPK_MAX, seq // 4)
    dq = N_HEADS * HEAD_DIM
    dk = N_KV_HEADS * HEAD_DIM
    dqi = IDX_HEADS * IDX_DIM
    nmain = dq + 2 * dk + dqi
    w_main = w_in[:, :nmain].astype(BF16)
    w_small = jnp.pad(w_in[:, nmain:], ((0, 0), (0, LANES - (IDX_DIM + IDX_HEADS)))).astype(BF16)
    q, k, v, qi, kiwi = _qkv_proj(x, g, sh, sc, w_main, w_small, q_gain, k_gain)
    ki = kiwi[:, :, :IDX_DIM].astype(BF16)
    wi = kiwi[:, :, IDX_DIM:IDX_DIM + IDX_HEADS] * (IDX_HEADS ** -0.5 * IDX_DIM ** -0.5)
    qT = jnp.swapaxes(q, 1, 2)
    qiT = jnp.swapaxes(qi, 1, 2)
    wiT = jnp.swapaxes(wi, 1, 2)
    nck = seq // ATT_KC
    vT = jnp.swapaxes(v.reshape(bsz, nck, ATT_KC, dk), 2, 3).reshape(bsz, nck, N_KV_HEADS, HEAD_DIM, ATT_KC)
    vT = jnp.concatenate([
        vT, jnp.ones((bsz, nck, N_KV_HEADS, 1, ATT_KC), BF16),
        jnp.zeros((bsz, nck, N_KV_HEADS, ATT_VROWS - HEAD_DIM - 1, ATT_KC), BF16)], axis=3)
    k4 = jnp.swapaxes(k.reshape(bsz, seq, N_KV_HEADS, HEAD_DIM), 1, 2)
    ptab = _bias_table(rel_bias)
    attn = _attention(qT, qiT, wiT, k4, vT, ki, ptab, top_k)
    return _proj_residual(attn, x, gate, w_out.astype(BF16))


def _s5_prep_kernel(lre_ref, lim_ref, ls_ref, bre_ref, bim_ref, cre_ref, cim_ref,
                    bcw_ref, mtw_ref, ccw_ref, are_ref, aim_ref):
    t_len = SSM_CHUNK
    lre = jnp.minimum(lre_ref[0], -1e-4)
    lim = lim_ref[0]
    step = jnp.exp(ls_ref[0])
    ar = lre * step
    ai = lim * step

    def powers(jv):
        mag = jnp.exp(jv * ar)
        return mag * jnp.cos(jv * ai), mag * jnp.sin(jv * ai)

    lb_re, lb_im = powers(1.0)
    nr = lb_re - 1.0
    ni = lb_im
    den = lre * lre + lim * lim
    cf_re = (nr * lre + ni * lim) / den
    cf_im = (ni * lre - nr * lim) / den
    bre = bre_ref[0]
    bim = bim_ref[0]
    bb_re = cf_re * bre - cf_im * bim
    bb_im = cf_re * bim + cf_im * bre
    cre = cre_ref[0]
    cim = cim_ref[0]
    nst = lre.shape[-1]
    jv = lax.broadcasted_iota(I32, (t_len, 1, nst), 0).astype(F32)
    pj_re, pj_im = powers(jv)
    a_re = (cre[None] * pj_re - cim[None] * pj_im).reshape(t_len * SSM_GROUP, nst)
    a_im = (cre[None] * pj_im + cim[None] * pj_re).reshape(t_len * SSM_GROUP, nst)
    dn = (((1,), (1,)), ((), ()))
    cg = SSM_GROUP
    tc = t_len * cg
    width = t_len * LANES
    gq = pl.program_id(0) % (LANES // cg)
    nh = (LANES // cg) * nst
    cg_shift = cg.bit_length() - 1

    def place(nrows, target):
        r = lax.broadcasted_iota(I32, (nrows, width), 0)
        col = lax.broadcasted_iota(I32, (nrows, width), 1)
        return jnp.where(col == target(r), 1.0, 0.0).astype(BF16)

    pm = place(tc, lambda r: lax.shift_right_logical(r, cg_shift) * LANES + gq * cg + (r & (cg - 1)))
    k_t = (lax.dot_general(bb_re, a_re, dn, preferred_element_type=F32, precision=HIGHEST)
           - lax.dot_general(bb_im, a_im, dn, preferred_element_type=F32, precision=HIGHEST))
    lane = lax.broadcasted_iota(I32, (cg, tc), 1)
    mt_rows = [k_t] + [jnp.where(lane >= s * cg, pltpu.roll(k_t, s * cg, 1), 0.0) for s in range(1, t_len)]
    mt_t = jnp.concatenate(mt_rows, axis=0)
    mtw_ref[0] = jnp.dot(mt_t.astype(BF16), pm, preferred_element_type=F32).astype(BF16)
    pr_re, pr_im = powers((t_len - 1.0) - jv)
    bc_re = (bb_re[None] * pr_re - bb_im[None] * pr_im).reshape(tc, nst)
    bc_im = (bb_re[None] * pr_im + bb_im[None] * pr_re).reshape(tc, nst)
    pb_re = place(nst, lambda r: gq * nst + r)
    pb_im = place(nst, lambda r: nh + gq * nst + r)
    bcw_ref[0] = (jnp.dot(bc_re.astype(BF16), pb_re, preferred_element_type=F32)
                  + jnp.dot(bc_im.astype(BF16), pb_im, preferred_element_type=F32)).astype(BF16)
    pn_re, pn_im = powers(jv + 1.0)
    cc_re = (cre[None] * pn_re - cim[None] * pn_im).reshape(tc, nst)
    cc_im = (cre[None] * pn_im + cim[None] * pn_re).reshape(tc, nst)
    eye = jnp.where(lax.broadcasted_iota(I32, (nst, nst), 0) == lax.broadcasted_iota(I32, (nst, nst), 1),
                    1.0, 0.0)
    cct_re = lax.dot_general(eye, cc_re, dn, preferred_element_type=F32, precision=HIGHEST)
    cct_im = lax.dot_general(eye, cc_im, dn, preferred_element_type=F32, precision=HIGHEST)
    ccw_ref[0, 0:nst, :] = jnp.dot(cct_re.astype(BF16), pm, preferred_element_type=F32).astype(BF16)
    ccw_ref[0, nst:2 * nst, :] = (-jnp.dot(cct_im.astype(BF16), pm, preferred_element_type=F32)).astype(BF16)
    at_re, at_im = powers(float(t_len))
    rs = lax.broadcasted_iota(I32, (nst, nh), 0)
    cs = lax.broadcasted_iota(I32, (nst, nh), 1)
    pa = jnp.where(cs == gq * nst + rs, 1.0, 0.0)
    are_ref[0] = jnp.dot(at_re, pa, preferred_element_type=F32, precision=HIGHEST)
    aim_ref[0] = jnp.dot(at_im, pa, preferred_element_type=F32, precision=HIGHEST)


def _s5_prep(lam_re, lam_im, log_step, b_re, b_im, c_re, c_im):
    ng, nst = lam_re.shape
    tc = SSM_CHUNK * SSM_GROUP
    vec = pl.BlockSpec((1, 1, nst), lambda gi: (gi, 0, 0))
    mat = pl.BlockSpec((1, SSM_GROUP, nst), lambda gi: (gi, 0, 0))
    width = SSM_CHUNK * LANES
    assert tc == 2 * nst and 2 * (LANES // SSM_GROUP) * nst == width
    big = pl.BlockSpec((1, tc, width), lambda gi: (gi, 0, 0))
    wide = jax.ShapeDtypeStruct((ng, tc, width), BF16)
    return pl.pallas_call(
        _s5_prep_kernel,
        out_shape=[wide, wide, wide,
                   jax.ShapeDtypeStruct((ng, 1, width // 2), F32),
                   jax.ShapeDtypeStruct((ng, 1, width // 2), F32)],
        grid=(ng,),
        in_specs=[vec, vec, pl.BlockSpec((1, 1, 1), lambda gi: (gi, 0, 0)), mat, mat, mat, mat],
        out_specs=[big, big, big, pl.BlockSpec((1, 1, width // 2), lambda gi: (gi, 0, 0)),
                   pl.BlockSpec((1, 1, width // 2), lambda gi: (gi, 0, 0))],
        compiler_params=_cparams("parallel"),
        name="s5_prep",
    )(lam_re.reshape(ng, 1, nst), lam_im.reshape(ng, 1, nst), log_step.reshape(ng, 1, 1),
      jnp.swapaxes(b_re, 1, 2), jnp.swapaxes(b_im, 1, 2), c_re, c_im)


def _s5_pre_kernel(x_ref, g_ref, sh_ref, sc_ref, o_ref, h_scr):
    t_len = SSM_CHUNK
    h = _norm_mod(x_ref[0], g_ref[...], sh_ref[0], sc_ref[0])
    nj = o_ref.shape[1]
    for q in range(o_ref.shape[0]):
        h_scr[q] = h[:, q * LANES:(q + 1) * LANES]
        for t in range(t_len):
            o_ref[q, :, t * LANES:(t + 1) * LANES] = h_scr[q, pl.ds(t, nj, stride=t_len), :].astype(BF16)


def _s5_pre(x, g, sh, sc, tm=512):
    bsz, seq, d = x.shape
    tm = min(tm, seq)
    nt = seq // tm
    nq = d // LANES
    t_len = SSM_CHUNK
    mod_spec = pl.BlockSpec((1, 1, d), lambda b, i: (b, 0, 0))
    return pl.pallas_call(
        _s5_pre_kernel,
        out_shape=jax.ShapeDtypeStruct((nq, bsz * seq // t_len, t_len * LANES), BF16),
        grid=(bsz, nt),
        in_specs=[pl.BlockSpec((1, tm, d), lambda b, i: (b, i, 0)),
                  pl.BlockSpec((1, d), lambda b, i: (0, 0)), mod_spec, mod_spec],
        out_specs=pl.BlockSpec((nq, tm // t_len, t_len * LANES), lambda b, i: (0, b * nt + i, 0)),
        scratch_shapes=[pltpu.VMEM((nq, tm, LANES), F32)],
        compiler_params=_cparams("parallel", "parallel"),
        name="s5_pre",
    )(x, g, sh, sc)


def _s5_scan_kernel(x_ref, bc_ref, mt_ref, cc_ref, are_ref, aim_ref, y_ref, re_scr, im_scr):
    x = x_ref[0]
    nrow = x.shape[0]
    pad = re_scr.shape[0] - nrow
    v = jnp.dot(x, bc_ref[0], preferred_element_type=F32)
    nh = v.shape[1] // 2
    s_re = v[:, :nh]
    s_im = v[:, nh:]
    a_re = are_ref[0]
    a_im = aim_ref[0]
    re_scr[0:pad, :] = jnp.zeros((pad, nh), F32)
    im_scr[0:pad, :] = jnp.zeros((pad, nh), F32)

    def shifted(scr, val, dist):
        scr[pad:pad + nrow, :] = val
        return scr[pad - dist:pad - dist + nrow, :]

    dist = 1
    while dist < nrow:
        sh_re = shifted(re_scr, s_re, dist)
        sh_im = shifted(im_scr, s_im, dist)
        s_re, s_im = (s_re + a_re * sh_re - a_im * sh_im, s_im + a_re * sh_im + a_im * sh_re)
        a_re, a_im = (a_re * a_re - a_im * a_im, 2.0 * a_re * a_im)
        dist *= 2
    sp = jnp.concatenate([shifted(re_scr, s_re, 1), shifted(im_scr, s_im, 1)], axis=1).astype(BF16)
    y_ref[0] = (jnp.dot(x, mt_ref[0], preferred_element_type=F32)
                + jnp.dot(sp, cc_ref[0], preferred_element_type=F32))


def _s5_scan(hq, bcq, mtq, ccq, a_re, a_im, bsz):
    nq, nrows, width = hq.shape
    nj = nrows // bsz
    nh = a_re.shape[-1]
    pad = max(nj // 2, SUBLANES)
    tile = pl.BlockSpec((1, nj, width), lambda q, b: (q, b, 0))
    wspec = lambda a: pl.BlockSpec((1,) + a.shape[1:], lambda q, b: (q, 0, 0))
    return pl.pallas_call(
        _s5_scan_kernel,
        out_shape=jax.ShapeDtypeStruct((nq, nrows, width), F32),
        grid=(nq, bsz),
        in_specs=[tile, wspec(bcq), wspec(mtq), wspec(ccq), wspec(a_re), wspec(a_im)],
        out_specs=tile,
        scratch_shapes=[pltpu.VMEM((pad + nj, nh), F32), pltpu.VMEM((pad + nj, nh), F32)],
        compiler_params=_cparams("parallel", "parallel"),
        name="s5_scan",
    )(hq, bcq, mtq, ccq, a_re, a_im)


def _s5_post_kernel(x_ref, y_ref, g_ref, sh_ref, sc_ref, gate_ref, dsk_ref, w_ref, o_ref, y_scr, *, d):
    t_len = SSM_CHUNK
    x = x_ref[0]
    h = _norm_mod(x, g_ref[...], sh_ref[0], sc_ref[0])
    nj = y_ref.shape[1]
    for q in range(y_ref.shape[0]):
        for t in range(t_len):
            y_scr[q, pl.ds(t, nj, stride=t_len), :] = y_ref[q, :, t * LANES:(t + 1) * LANES]
    y = jnp.concatenate([y_scr[q] for q in range(y_ref.shape[0])], axis=1)
    yy = y + dsk_ref[...] * h
    gl = jax.nn.gelu(yy).astype(BF16)
    z = jnp.dot(gl, w_ref[...], preferred_element_type=F32)
    o_ref[0] = x + gate_ref[0] * (z[:, :d] * jax.nn.sigmoid(z[:, d:]))


def _s5_post(x, yq, g, sh, sc, gate, d_skip, w_glu, tm=512):
    bsz, seq, d = x.shape
    tm = min(tm, seq)
    nt = seq // tm
    nq = d // LANES
    kern = functools.partial(_s5_post_kernel, d=d)
    mod_spec = pl.BlockSpec((1, 1, d), lambda b, i: (b, 0, 0))
    tile = pl.BlockSpec((1, tm, d), lambda b, i: (b, i, 0))
    return pl.pallas_call(
        kern,
        out_shape=jax.ShapeDtypeStruct((bsz, seq, d), F32),
        grid=(bsz, nt),
        in_specs=[tile,
                  pl.BlockSpec((nq, tm // SSM_CHUNK, SSM_CHUNK * LANES), lambda b, i: (0, b * nt + i, 0)),
                  pl.BlockSpec((1, d), lambda b, i: (0, 0)), mod_spec, mod_spec, mod_spec,
                  pl.BlockSpec((1, d), lambda b, i: (0, 0)),
                  pl.BlockSpec(w_glu.shape, lambda b, i: (0, 0))],
        out_specs=tile,
        scratch_shapes=[pltpu.VMEM((nq, tm, LANES), F32)],
        compiler_params=_cparams("parallel", "parallel"),
        name="s5_post",
    )(x, yq, g, sh, sc, gate, d_skip, w_glu)


def _s5_layer(x, g, sh, sc, gate, lam_re, lam_im, log_step, b_re, b_im, c_re, c_im, d_skip, w_glu):
    bsz, seq, d = x.shape
    ng, nst = lam_re.shape
    t_len, cg = SSM_CHUNK, SSM_GROUP
    nq = d // LANES
    gq = LANES // cg
    nj = seq // t_len
    bcw, mtw, ccw, a_re, a_im = _s5_prep(lam_re, lam_im, log_step, b_re, b_im, c_re, c_im)
    width = t_len * LANES
    bcq = jnp.swapaxes(bcw.reshape(nq, gq, t_len, cg, width), 1, 2).reshape(nq, width, width)
    mtq = jnp.swapaxes(mtw.reshape(nq, gq, t_len, cg, width), 1, 2).reshape(nq, width, width)
    ccq = jnp.swapaxes(ccw.reshape(nq, gq, 2, nst, width), 1, 2).reshape(nq, 2 * gq * nst, width)
    aq_re = jnp.sum(a_re.reshape(nq, gq, 1, gq * nst), axis=1)
    aq_im = jnp.sum(a_im.reshape(nq, gq, 1, gq * nst), axis=1)

    hq = _s5_pre(x, g, sh, sc)
    yq = _s5_scan(hq, bcq, mtq, ccq, aq_re, aq_im, bsz)
    return _s5_post(x, yq, g, sh, sc, gate, d_skip.reshape(1, d).astype(F32), w_glu.astype(BF16))


def kernel(x, c, ada_w, ada_b, norm_g, conv_w_in, conv_w, conv_w_out, attn_w_in, attn_q_gain, attn_k_gain, attn_w_out, rel_bias, ssm_lambda_re, ssm_lambda_im, ssm_log_step, ssm_b_re, ssm_b_im, ssm_c_re, ssm_c_im, ssm_d, ssm_w_glu, ffn_w_gu, ffn_w_down, moe_router_w, moe_router_b, moe_w_gu, moe_w_down):
    bsz, seq, d = x.shape
    depth = ada_w.shape[0]
    mod = _ada_mod(c, ada_w, ada_b).reshape(depth, bsz, 6, 1, d)
    moe_gu = moe_w_gu.reshape((-1,) + moe_w_gu.shape[2:])
    moe_down = moe_w_down.reshape((-1,) + moe_w_down.shape[2:])
    for i in range(depth):
        sh1, sc1, g1, sh2, sc2, g2 = (mod[i, :, r] for r in range(6))
        gn1 = norm_g[i, 0].reshape(1, d)
        gn2 = norm_g[i, 1].reshape(1, d)
        j = i // N_MIXERS
        if i % N_MIXERS == 0:
            x = _conv_layer(x, gn1, sh1, sc1, g1, conv_w_in[j].astype(BF16), conv_w[j],
                            conv_w_out[j].astype(BF16))
        elif i % N_MIXERS == 1:
            x = _attn_layer(x, gn1, sh1, sc1, g1, attn_w_in[j], attn_q_gain[j], attn_k_gain[j],
                            attn_w_out[j], rel_bias)
        else:
            x = _s5_layer(x, gn1, sh1, sc1, g1, ssm_lambda_re[j], ssm_lambda_im[j], ssm_log_step[j],
                          ssm_b_re[j], ssm_b_im[j], ssm_c_re[j], ssm_c_im[j], ssm_d[j], ssm_w_glu[j])
        if i % 2 == 0:
            x = _ffn_layer(x, gn2, sh2, sc2, g2, ffn_w_gu[i // 2].astype(BF16), ffn_w_down[i // 2].astype(BF16))
        else:
            x = _moe_layer(x, gn2, sh2, sc2, g2, moe_router_w[i // 2], moe_router_b[i // 2],
                           moe_gu, moe_down, ebase=(i // 2) * moe_w_gu.shape[1])
    return x
```

```python
import functools
import math

import numpy as np
import jax
import jax.numpy as jnp
from jax import lax
from jax.experimental import pallas as pl
from jax.experimental.pallas import tpu as pltpu

F32 = jnp.float32
BF16 = jnp.bfloat16
I32 = jnp.int32
HIGHEST = lax.Precision.HIGHEST

DEPTH = 4
N_MIXERS = 3
EPS = 1e-6
CONV_WIDTH = 3
N_HEADS = 16
N_KV_HEADS = 4
N_REP = N_HEADS // N_KV_HEADS
HEAD_DIM = 64
IDX_HEADS = 8
IDX_DIM = 64
TOPK_MAX = 256
REL_BUCKETS = 32
REL_MAX_DIST = 128
SSM_GROUP = 16
SSM_STATE = 64
N_EXPERTS = 8
TOP_K_EXPERTS = 2

VMEM_LIMIT_BYTES = 56 * 1024 * 1024
LANES = 128
SUBLANES = 8

INT_MIN = -(2 ** 31)
INT_MAX = 2 ** 31 - 1
NEG_BIG = -1e30
LOG2E = 1.4426950408889634

ATT_TQ = 128
ATT_KC = 256
BIAS_C = 384
BIAS_W = BIAS_C + ATT_KC
ATT_VROWS = 80

SSM_CHUNK = 8


def _cparams(*sem):
    return pltpu.CompilerParams(dimension_semantics=sem, vmem_limit_bytes=VMEM_LIMIT_BYTES)


def _norm_mod(x, g, shift, scale):
    ms = jnp.mean(x * x, axis=-1, keepdims=True)
    y = x * lax.rsqrt(ms + EPS)
    return (y * g) * (1.0 + scale) + shift


def _silu(x):
    return x * jax.nn.sigmoid(x)


def _ada_kernel(c_ref, w_ref, b_ref, o_ref):
    c = c_ref[...]
    cond = _silu(c)
    o_ref[0] = jnp.dot(cond, w_ref[0], preferred_element_type=F32, precision=HIGHEST) + b_ref[0]


def _ada_mod(c, ada_w, ada_b):
    depth, d, d6 = ada_w.shape
    bsz = c.shape[0]
    tn = d
    return pl.pallas_call(
        _ada_kernel,
        out_shape=jax.ShapeDtypeStruct((depth, bsz, d6), F32),
        grid=(depth, d6 // tn),
        in_specs=[
            pl.BlockSpec((bsz, d), lambda i, j: (0, 0)),
            pl.BlockSpec((1, d, tn), lambda i, j: (i, 0, j)),
            pl.BlockSpec((1, 1, tn), lambda i, j: (i, 0, j)),
        ],
        out_specs=pl.BlockSpec((1, bsz, tn), lambda i, j: (i, 0, j)),
        compiler_params=_cparams("parallel", "parallel"),
        name="ada_mod",
    )(c, ada_w, ada_b.reshape(depth, 1, d6))


def _conv_kernel(x_ref, xh_ref, g_ref, sh_ref, sc_ref, gate_ref, win_ref, wc_ref, wout_ref,
                 o_ref, u_scr, *, tm, d):
    i = pl.program_id(1)
    g = g_ref[...]
    sh = sh_ref[0]
    sc = sc_ref[0]
    x = x_ref[0]
    halo = xh_ref.shape[1]
    xe = jnp.concatenate([xh_ref[0], x], axis=0)
    he = _norm_mod(xe, g, sh, sc).astype(BF16)
    z = jnp.dot(he, win_ref[...], preferred_element_type=F32)
    u_all = z[:, d:2 * d] * z[:, 2 * d:]
    row = lax.broadcasted_iota(I32, (halo + tm, 1), 0)
    u_scr[...] = jnp.where(jnp.logical_or(i > 0, row >= halo), u_all, 0.0)
    b_gate = z[halo:, :d]
    wc = wc_ref[...]
    conv = (wc[0:1, :] * u_scr[halo - 2:halo - 2 + tm, :]
            + wc[1:2, :] * u_scr[halo - 1:halo - 1 + tm, :]
            + wc[2:3, :] * u_scr[halo:halo + tm, :])
    y = jnp.dot((b_gate * conv).astype(BF16), wout_ref[...], preferred_element_type=F32)
    o_ref[0] = x + gate_ref[0] * y


def _conv_layer(x, g, sh, sc, gate, w_in, w_conv, w_out, tm=512):
    bsz, seq, d = x.shape
    tm = min(tm, seq)
    nt = seq // tm
    halo = 2 * SUBLANES
    hb = tm // halo
    kern = functools.partial(_conv_kernel, tm=tm, d=d)
    mod_spec = pl.BlockSpec((1, 1, d), lambda b, i: (b, 0, 0))
    return pl.pallas_call(
        kern,
        out_shape=jax.ShapeDtypeStruct((bsz, seq, d), F32),
        grid=(bsz, nt),
        in_specs=[
            pl.BlockSpec((1, tm, d), lambda b, i: (b, i, 0)),
            pl.BlockSpec((1, halo, d), lambda b, i: (b, jnp.maximum(i * hb - 1, 0), 0)),
            pl.BlockSpec((1, d), lambda b, i: (0, 0)),
            mod_spec, mod_spec, mod_spec,
            pl.BlockSpec((d, 3 * d), lambda b, i: (0, 0)),
            pl.BlockSpec((CONV_WIDTH, d), lambda b, i: (0, 0)),
            pl.BlockSpec((d, d), lambda b, i: (0, 0)),
        ],
        out_specs=pl.BlockSpec((1, tm, d), lambda b, i: (b, i, 0)),
        scratch_shapes=[pltpu.VMEM((tm + halo, d), F32)],
        compiler_params=_cparams("parallel", "parallel"),
        name="conv_mixer",
    )(x, x, g, sh, sc, gate, w_in, w_conv, w_out)


def _ffn_kernel(x_ref, g_ref, sh_ref, sc_ref, gate_ref, wgu_ref, wd_ref, o_ref, *, dff, nchunk):
    x = x_ref[0]
    h = _norm_mod(x, g_ref[...], sh_ref[0], sc_ref[0]).astype(BF16)
    cols = dff // nchunk
    acc = jnp.zeros(x.shape, F32)
    for c in range(nchunk):
        gg = jnp.dot(h, wgu_ref[:, c * cols:(c + 1) * cols], preferred_element_type=F32)
        uu = jnp.dot(h, wgu_ref[:, dff + c * cols:dff + (c + 1) * cols], preferred_element_type=F32)
        a = (_silu(gg) * uu).astype(BF16)
        acc = acc + jnp.dot(a, wd_ref[c * cols:(c + 1) * cols, :], preferred_element_type=F32)
    o_ref[0] = x + gate_ref[0] * acc


def _ffn_layer(x, g, sh, sc, gate, w_gu, w_down, tm=512):
    bsz, seq, d = x.shape
    dff = w_down.shape[0]
    tm = min(tm, seq)
    kern = functools.partial(_ffn_kernel, dff=dff, nchunk=2)
    mod_spec = pl.BlockSpec((1, 1, d), lambda b, i: (b, 0, 0))
    return pl.pallas_call(
        kern,
        out_shape=jax.ShapeDtypeStruct((bsz, seq, d), F32),
        grid=(bsz, seq // tm),
        in_specs=[
            pl.BlockSpec((1, tm, d), lambda b, i: (b, i, 0)),
            pl.BlockSpec((1, d), lambda b, i: (0, 0)),
            mod_spec, mod_spec, mod_spec,
            pl.BlockSpec((d, 2 * dff), lambda b, i: (0, 0), pipeline_mode=pl.Buffered(1)),
            pl.BlockSpec((dff, d), lambda b, i: (0, 0), pipeline_mode=pl.Buffered(1)),
        ],
        out_specs=pl.BlockSpec((1, tm, d), lambda b, i: (b, i, 0)),
        compiler_params=_cparams("parallel", "parallel"),
        name="ffn_dense",
    )(x, g, sh, sc, gate, w_gu, w_down)


MOE_SB = 512
MOE_RT = 512
MOE_ALIGN = 16
MOE_WIN = 256
MOE_CAP = MOE_WIN - MOE_ALIGN


def _moe_router_kernel(x_ref, g_ref, sh_ref, sc_ref, rw_ref, rb_ref, h_ref, meta_ref):
    hf = _norm_mod(x_ref[...], g_ref[...], sh_ref[0], sc_ref[0])
    h_hi = hf.astype(BF16)
    h_ref[...] = h_hi
    rw = rw_ref[...]
    w_hi = rw.astype(BF16)
    w_lo = (rw - w_hi.astype(F32)).astype(BF16)
    h_lo = (hf - h_hi.astype(F32)).astype(BF16)
    logits = (jnp.dot(h_hi, w_hi, preferred_element_type=F32) + jnp.dot(h_lo, w_hi, preferred_element_type=F32)
              + jnp.dot(h_hi, w_lo, preferred_element_type=F32) + rb_ref[...])
    mx = jnp.max(logits, axis=-1, keepdims=True)
    ex = jnp.exp(logits - mx)
    probs = ex / jnp.sum(ex, axis=-1, keepdims=True)
    lane = lax.broadcasted_iota(I32, probs.shape, 1)
    m1 = jnp.max(probs, axis=-1, keepdims=True)
    i1 = jnp.min(jnp.where(probs == m1, lane, LANES), axis=-1, keepdims=True)
    rest = jnp.where(lane == i1, -1.0, probs)
    m2 = jnp.max(rest, axis=-1, keepdims=True)
    i2 = jnp.min(jnp.where(rest == m2, lane, LANES), axis=-1, keepdims=True)
    den = m1 + m2
    gates = jnp.where(lane == i1, m1 / den, 0.0) + jnp.where(lane == i2, m2 / den, 0.0)
    chosen = jnp.where(jnp.logical_or(lane == i1, lane == i2), 1.0, 0.0)
    nrow = chosen.shape[0]
    below = jnp.where(lax.broadcasted_iota(I32, (nrow, nrow), 1) < lax.broadcasted_iota(I32, (nrow, nrow), 0),
                      1.0, 0.0).astype(BF16)
    rank = jnp.dot(below, chosen.astype(BF16), preferred_element_type=F32)
    meta_ref[...] = gates + pltpu.roll(chosen, N_EXPERTS, 1) + pltpu.roll(rank, 2 * N_EXPERTS, 1)


def _moe_router(xt, g, sh, sc, rw_pad, rb_pad, tpb, tm):
    n, d = xt.shape
    mod_spec = pl.BlockSpec((1, 1, d), lambda i: (i // tpb, 0, 0))
    return pl.pallas_call(
        _moe_router_kernel,
        out_shape=[jax.ShapeDtypeStruct((n, d), BF16), jax.ShapeDtypeStruct((n, LANES), F32)],
        grid=(n // tm,),
        in_specs=[pl.BlockSpec((tm, d), lambda i: (i, 0)),
                  pl.BlockSpec((1, d), lambda i: (0, 0)), mod_spec, mod_spec,
                  pl.BlockSpec((d, LANES), lambda i: (0, 0)),
                  pl.BlockSpec((1, LANES), lambda i: (0, 0))],
        out_specs=[pl.BlockSpec((tm, d), lambda i: (i, 0)), pl.BlockSpec((tm, LANES), lambda i: (i, 0))],
        compiler_params=_cparams("parallel"),
        name="moe_router",
    )(xt, g, sh, sc, rw_pad, rb_pad)


def _window(start, count, w):
    s = start + w * MOE_CAP
    n = jnp.minimum(count - w * MOE_CAP, MOE_CAP)
    a = pl.multiple_of((s // MOE_ALIGN) * MOE_ALIGN, MOE_ALIGN)
    return s, n, a


def _moe_dispatch_kernel(start_ref, cnt_ref, h_ref, pos_ref, xs_init_ref, xs_ref,
                         buf, carry, sems, pending):
    del xs_init_ref
    b = pl.program_id(0)
    nb = pl.num_programs(0)
    ne = buf.shape[0]

    @pl.when(b == 0)
    def _():
        carry[...] = jnp.zeros(carry.shape, BF16)
        for e in range(ne):
            pending[e] = 0

    h = h_ref[...]
    riota = lax.broadcasted_iota(I32, (MOE_WIN, h.shape[0]), 0)

    def out_copy(e, a):
        return pltpu.make_async_copy(buf.at[e], xs_ref.at[pl.ds(a, MOE_WIN)], sems.at[e])

    for e in range(ne):
        posrow = pos_ref[0, e:e + 1, :]
        nwin = (cnt_ref[b, e] + MOE_CAP - 1) // MOE_CAP

        def wbody(w, carry_unused, e=e, posrow=posrow):
            s, n, a = _window(start_ref[b, e], cnt_ref[b, e], w)

            @pl.when(pending[e] == 1)
            def _():
                out_copy(e, 0).wait()

            rel = jnp.where(jnp.logical_and(posrow >= s, posrow < s + n), posrow - a, -1)
            onehot = jnp.where(rel == riota, 1.0, 0.0).astype(BF16)
            buf[e] = jnp.dot(onehot, h, preferred_element_type=F32).astype(BF16)
            buf[e, 0:MOE_ALIGN, :] = buf[e, 0:MOE_ALIGN, :] + carry[e]
            c0 = pl.multiple_of(((s + n) // MOE_ALIGN) * MOE_ALIGN - a, MOE_ALIGN)
            carry[e] = buf[e, pl.ds(c0, MOE_ALIGN), :]
            out_copy(e, a).start()
            pending[e] = 1
            return carry_unused

        lax.fori_loop(0, nwin, wbody, 0)

    @pl.when(b == nb - 1)
    def _():
        for e in range(ne):
            @pl.when(pending[e] == 1)
            def _():
                out_copy(e, 0).wait()


def _moe_dispatch(start, cnt, h, pos_t, ncap):
    n, d = h.shape
    nb, ne, sb = pos_t.shape
    xs_init = jnp.zeros((ncap, d), BF16)
    return pl.pallas_call(
        _moe_dispatch_kernel,
        out_shape=jax.ShapeDtypeStruct((ncap, d), BF16),
        grid_spec=pltpu.PrefetchScalarGridSpec(
            num_scalar_prefetch=2,
            grid=(nb,),
            in_specs=[pl.BlockSpec((sb, d), lambda b, *_: (b, 0)),
                      pl.BlockSpec((1, ne, sb), lambda b, *_: (b, 0, 0)),
                      pl.BlockSpec(memory_space=pl.ANY)],
            out_specs=pl.BlockSpec(memory_space=pl.ANY),
            scratch_shapes=[pltpu.VMEM((ne, MOE_WIN, d), BF16),
                            pltpu.VMEM((ne, MOE_ALIGN, d), BF16),
                            pltpu.SemaphoreType.DMA((ne,)),
                            pltpu.SMEM((ne,), I32)],
        ),
        input_output_aliases={4: 0},
        compiler_params=_cparams("arbitrary"),
        name="moe_dispatch",
    )(start, cnt, h, pos_t, xs_init)


MOE_WCHUNKS = 8
MOE_WSTAGE = 4


def _moe_ffn_kernel(blk_ref, exp_ref, nt_ref, xs_ref, wgu_hbm, wd_hbm, ys_init_ref, ys_ref,
                    wgu_scr, wd_scr, gu_stage, d_stage, sems, ahead, *, dff, nchunk, ebase):
    del blk_ref, ys_init_ref
    k = pl.program_id(0)
    d = xs_ref.shape[1]
    gu_rows = d // MOE_WCHUNKS
    d_rows = dff // MOE_WCHUNKS
    e = ebase + exp_ref[k]
    k_next = jnp.minimum(k + 1, pl.num_programs(0) - 1)
    e_next = ebase + exp_ref[k_next]

    def gu_copy(ex, c, slot):
        return pltpu.make_async_copy(wgu_hbm.at[ex, pl.ds(c * gu_rows, gu_rows), :], gu_stage.at[slot],
                                     sems.at[0, slot])

    def d_copy(ex, c, slot):
        return pltpu.make_async_copy(wd_hbm.at[ex, pl.ds(c * d_rows, d_rows), :], d_stage.at[slot],
                                     sems.at[1, slot])

    def start_piece(ex, c):
        gu_copy(ex, c, c % MOE_WSTAGE).start()
        d_copy(ex, c, c % MOE_WSTAGE).start()

    @pl.when(k == 0)
    def _():
        ahead[0] = 0

    new_expert = jnp.logical_or(k == 0, exp_ref[k] != exp_ref[jnp.maximum(k - 1, 0)])

    @pl.when(jnp.logical_and(k < nt_ref[0], new_expert))
    def _():
        @pl.when(ahead[0] == 0)
        def _():
            for c in range(MOE_WSTAGE):
                start_piece(e, c)

        ahead[0] = 0
        for c in range(MOE_WCHUNKS):
            slot = c % MOE_WSTAGE
            gu_copy(e, c, slot).wait()
            wgu_scr[c * gu_rows:(c + 1) * gu_rows, :] = gu_stage[slot].astype(BF16)
            d_copy(e, c, slot).wait()
            wd_scr[c * d_rows:(c + 1) * d_rows, :] = d_stage[slot].astype(BF16)
            if c + MOE_WSTAGE < MOE_WCHUNKS:
                start_piece(e, c + MOE_WSTAGE)

    last_of_expert = jnp.logical_and(k + 1 < nt_ref[0], exp_ref[k_next] != exp_ref[k])

    @pl.when(jnp.logical_and(k < nt_ref[0], last_of_expert))
    def _():
        for c in range(MOE_WSTAGE):
            start_piece(e_next, c)
        ahead[0] = 1

    @pl.when(k < nt_ref[0])
    def _():
        x = xs_ref[...]
        cols = dff // nchunk
        acc = jnp.zeros(x.shape, F32)
        for c in range(nchunk):
            gg = jnp.dot(x, wgu_scr[:, c * cols:(c + 1) * cols], preferred_element_type=F32)
            uu = jnp.dot(x, wgu_scr[:, dff + c * cols:dff + (c + 1) * cols], preferred_element_type=F32)
            a = (_silu(gg) * uu).astype(BF16)
            acc = acc + jnp.dot(a, wd_scr[c * cols:(c + 1) * cols, :], preferred_element_type=F32)
        ys_ref[...] = acc.astype(BF16)


def _moe_ffn(tile_blk, tile_exp, ntiles, xs, w_gu, w_down, ebase):
    ncap, d = xs.shape
    dff = w_down.shape[1]
    assert d % (MOE_WCHUNKS * 2 * SUBLANES) == 0 and dff % (MOE_WCHUNKS * 2 * SUBLANES) == 0
    kern = functools.partial(_moe_ffn_kernel, dff=dff, nchunk=2, ebase=ebase)
    ys_init = jnp.zeros((ncap, d), BF16)
    return pl.pallas_call(
        kern,
        out_shape=jax.ShapeDtypeStruct((ncap, d), BF16),
        grid_spec=pltpu.PrefetchScalarGridSpec(
            num_scalar_prefetch=3,
            grid=(ncap // MOE_RT,),
            in_specs=[pl.BlockSpec((MOE_RT, d), lambda k, blk, exp, nt: (blk[k], 0)),
                      pl.BlockSpec(memory_space=pl.ANY),
                      pl.BlockSpec(memory_space=pl.ANY),
                      pl.BlockSpec(memory_space=pl.ANY)],
            out_specs=pl.BlockSpec((MOE_RT, d), lambda k, blk, exp, nt: (blk[k], 0)),
            scratch_shapes=[pltpu.VMEM((d, 2 * dff), BF16),
                            pltpu.VMEM((dff, d), BF16),
                            pltpu.VMEM((MOE_WSTAGE, d // MOE_WCHUNKS, 2 * dff), F32),
                            pltpu.VMEM((MOE_WSTAGE, dff // MOE_WCHUNKS, d), F32),
                            pltpu.SemaphoreType.DMA((2, MOE_WSTAGE)),
                            pltpu.SMEM((1,), I32)],
        ),
        input_output_aliases={6: 0},
        compiler_params=_cparams("arbitrary"),
        name="moe_ffn",
    )(tile_blk, tile_exp, ntiles, xs, w_gu, w_down, ys_init)


def _moe_combine_kernel(start_ref, cnt_ref, x_ref, gate_ref, pos_ref, gts_ref, ys_ref, o_ref,
                        buf, sems, acc_scr):
    b = pl.program_id(0)
    nb = pl.num_programs(0)
    ne = buf.shape[1]
    sb = x_ref.shape[0]
    slot = b % 2

    def in_copy(sl, e, a):
        return pltpu.make_async_copy(ys_ref.at[pl.ds(a, MOE_WIN)], buf.at[sl, e], sems.at[sl, e])

    def start_first_windows(blk, sl):
        for e in range(ne):
            @pl.when(cnt_ref[blk, e] > 0)
            def _():
                _, _, a = _window(start_ref[blk, e], cnt_ref[blk, e], 0)
                in_copy(sl, e, a).start()

    @pl.when(b == 0)
    def _():
        start_first_windows(0, 0)

    @pl.when(b + 1 < nb)
    def _():
        start_first_windows(b + 1, 1 - slot)

    acc_scr[...] = jnp.zeros(acc_scr.shape, F32)
    liota = lax.broadcasted_iota(I32, (sb, MOE_WIN), 1)
    for e in range(ne):
        poscol = pos_ref[:, e:e + 1]
        gcol = gts_ref[:, e:e + 1]
        nwin = (cnt_ref[b, e] + MOE_CAP - 1) // MOE_CAP

        def wbody(w, carry_unused, e=e, poscol=poscol, gcol=gcol):
            s, n, a = _window(start_ref[b, e], cnt_ref[b, e], w)

            @pl.when(w > 0)
            def _():
                in_copy(slot, e, a).start()

            in_copy(slot, e, a).wait()
            rel = jnp.where(jnp.logical_and(poscol >= s, poscol < s + n), poscol - a, -1)
            onehot = jnp.where(rel == liota, 1.0, 0.0).astype(BF16)
            acc_scr[...] += gcol * jnp.dot(onehot, buf[slot, e], preferred_element_type=F32)
            return carry_unused

        lax.fori_loop(0, nwin, wbody, 0)

    o_ref[...] = x_ref[...] + gate_ref[0] * acc_scr[...]


def _moe_combine(start, cnt, xt, gate, pos_n, gates_n, ys, tpb):
    n, d = xt.shape
    nb, ne = cnt.shape
    sb = n // nb
    return pl.pallas_call(
        _moe_combine_kernel,
        out_shape=jax.ShapeDtypeStruct((n, d), F32),
        grid_spec=pltpu.PrefetchScalarGridSpec(
            num_scalar_prefetch=2,
            grid=(nb,),
            in_specs=[pl.BlockSpec((sb, d), lambda b, *_: (b, 0)),
                      pl.BlockSpec((1, 1, d), lambda b, *_: (b // tpb, 0, 0)),
                      pl.BlockSpec((sb, ne), lambda b, *_: (b, 0)),
                      pl.BlockSpec((sb, ne), lambda b, *_: (b, 0)),
                      pl.BlockSpec(memory_space=pl.ANY)],
            out_specs=pl.BlockSpec((sb, d), lambda b, *_: (b, 0)),
            scratch_shapes=[pltpu.VMEM((2, ne, MOE_WIN, d), BF16),
                            pltpu.SemaphoreType.DMA((2, ne)),
                            pltpu.VMEM((sb, d), F32)],
        ),
        compiler_params=_cparams("arbitrary"),
        name="moe_combine",
    )(start, cnt, xt, gate, pos_n, gates_n, ys)


def _moe_layer(x, g, sh, sc, gate, router_w, router_b, w_gu, w_down, ebase=0):
    bsz, seq, d = x.shape
    ne = router_w.shape[1]
    n = bsz * seq
    sb = min(MOE_SB, seq)
    tpb = seq // sb
    nb = n // sb
    rt = MOE_RT
    xt = x.reshape(n, d)
    rw_pad = jnp.pad(router_w.astype(F32), ((0, 0), (0, LANES - ne)))
    rb_pad = jnp.pad(router_b.astype(F32).reshape(1, ne), ((0, 0), (0, LANES - ne)), constant_values=NEG_BIG)
    h, meta = _moe_router(xt, g, sh, sc, rw_pad, rb_pad, tpb, sb)
    gates = meta[:, :ne]
    sel = (meta[:, ne:2 * ne] > 0.5).astype(I32)

    selb = sel.reshape(nb, sb, ne)
    cnt = jnp.sum(selb, axis=1)
    rank = meta[:, 2 * ne:3 * ne].astype(I32).reshape(nb, sb, ne)
    total = jnp.sum(cnt, axis=0)
    region = ((total + MOE_WIN + rt - 1) // rt) * rt
    off = jnp.cumsum(region) - region
    start = (off[None, :] + jnp.cumsum(cnt, axis=0) - cnt).astype(I32)
    pos = jnp.where(selb > 0, start[:, None, :] + rank, -1).astype(I32)
    ncap = 2 * n + ne * (MOE_WIN + rt)
    tiles_e = (total + rt - 1) // rt
    tcum = jnp.cumsum(tiles_e)
    ntiles = tcum[-1]
    kk = jnp.minimum(jnp.arange(ncap // rt), ntiles - 1)
    tile_exp = jnp.sum((kk[:, None] >= tcum[None, :]).astype(I32), axis=1)
    tile_blk = (off[tile_exp] // rt + kk - (tcum - tiles_e)[tile_exp]).astype(I32)

    xs = _moe_dispatch(start, cnt.astype(I32), h, jnp.swapaxes(pos, 1, 2), ncap)
    ys = _moe_ffn(tile_blk, tile_exp, ntiles.reshape(1).astype(I32), xs, w_gu, w_down, ebase)
    out = _moe_combine(start, cnt.astype(I32), xt, gate, pos.reshape(n, ne), gates, ys, tpb)
    return out.reshape(bsz, seq, d)


def _head_norm(q, hsum_ref, hexp_ref, gain, scale):
    ms = jnp.dot((q * q).astype(BF16), hsum_ref[...], preferred_element_type=F32)
    r = lax.rsqrt(ms + EPS)
    r_hi = r.astype(BF16)
    r_lo = (r - r_hi.astype(F32)).astype(BF16)
    rexp = (jnp.dot(r_hi, hexp_ref[...], preferred_element_type=F32)
            + jnp.dot(r_lo, hexp_ref[...], preferred_element_type=F32))
    return q * rexp * (gain * scale)


def _qkv_kernel(x_ref, g_ref, sh_ref, sc_ref, wm_ref, ws_ref, qg_ref, kg_ref,
                hsq_ref, heq_ref, hsk_ref, hek_ref,
                q_ref, k_ref, v_ref, qi_ref, kiwi_ref, *, dq, dk, dqi):
    x = x_ref[0]
    h = _norm_mod(x, g_ref[...], sh_ref[0], sc_ref[0]).astype(BF16)
    z = jnp.dot(h, wm_ref[...], preferred_element_type=F32)
    q = z[:, :dq]
    k = z[:, dq:dq + dk]
    v = z[:, dq + dk:dq + 2 * dk]
    qi = z[:, dq + 2 * dk:dq + 2 * dk + dqi]
    q_ref[0] = _head_norm(q, hsq_ref, heq_ref, qg_ref[...], HEAD_DIM ** -0.5 * LOG2E).astype(BF16)
    k_ref[0] = _head_norm(k, hsk_ref, hek_ref, kg_ref[...], 1.0).astype(BF16)
    v_ref[0] = v.astype(BF16)
    qi_ref[0] = qi.astype(BF16)
    kiwi_ref[0] = jnp.dot(h, ws_ref[...], preferred_element_type=F32)


def _head_indicators(nheads):
    hs = np.zeros((nheads * HEAD_DIM, LANES), np.float32)
    he = np.zeros((LANES, nheads * HEAD_DIM), np.float32)
    for hd in range(nheads):
        hs[hd * HEAD_DIM:(hd + 1) * HEAD_DIM, hd] = 1.0 / HEAD_DIM
        he[hd, hd * HEAD_DIM:(hd + 1) * HEAD_DIM] = 1.0
    return jnp.asarray(hs, BF16), jnp.asarray(he, BF16)


def _qkv_proj(x, g, sh, sc, w_main, w_small, q_gain, k_gain, tm=512):
    bsz, seq, d = x.shape
    tm = min(tm, seq)
    dq = N_HEADS * HEAD_DIM
    dk = N_KV_HEADS * HEAD_DIM
    dqi = IDX_HEADS * IDX_DIM
    hsq, heq = _head_indicators(N_HEADS)
    hsk, hek = _head_indicators(N_KV_HEADS)
    qg = jnp.tile(q_gain.reshape(1, HEAD_DIM), (1, N_HEADS)).astype(F32)
    kg = jnp.tile(k_gain.reshape(1, HEAD_DIM), (1, N_KV_HEADS)).astype(F32)
    kern = functools.partial(_qkv_kernel, dq=dq, dk=dk, dqi=dqi)
    mod_spec = pl.BlockSpec((1, 1, d), lambda b, i: (b, 0, 0))

    def full(a):
        return pl.BlockSpec(a.shape, lambda b, i: (0,) * a.ndim)

    def out(n):
        return pl.BlockSpec((1, tm, n), lambda b, i: (b, i, 0))

    return pl.pallas_call(
        kern,
        out_shape=[
            jax.ShapeDtypeStruct((bsz, seq, dq), BF16),
            jax.ShapeDtypeStruct((bsz, seq, dk), BF16),
            jax.ShapeDtypeStruct((bsz, seq, dk), BF16),
            jax.ShapeDtypeStruct((bsz, seq, dqi), BF16),
            jax.ShapeDtypeStruct((bsz, seq, LANES), F32),
        ],
        grid=(bsz, seq // tm),
        in_specs=[
            pl.BlockSpec((1, tm, d), lambda b, i: (b, i, 0)),
            pl.BlockSpec((1, d), lambda b, i: (0, 0)),
            mod_spec, mod_spec,
            full(w_main), full(w_small), full(qg), full(kg),
            full(hsq), full(heq), full(hsk), full(hek),
        ],
        out_specs=[out(dq), out(dk), out(dk), out(dqi), out(LANES)],
        compiler_params=_cparams("parallel", "parallel"),
        name="attn_qkv",
    )(x, g, sh, sc, w_main, w_small, qg, kg, hsq, heq, hsk, hek)


def _rel_bucket_np(dist):
    max_exact = REL_BUCKETS // 2
    d = np.maximum(dist, 1).astype(np.float64)
    large = max_exact + (np.log(d / max_exact) / math.log(REL_MAX_DIST / max_exact)
                         * (REL_BUCKETS - max_exact)).astype(np.int32)
    large = np.minimum(large, REL_BUCKETS - 1)
    return np.where(dist < max_exact, dist, large).astype(np.int32)


def _bias_table_kernel(bucket_ref, rb_ref, o_ref):
    hd = pl.program_id(0)
    bucket = bucket_ref[...]
    acc = jnp.zeros(bucket.shape, F32)
    for b in range(REL_BUCKETS):
        acc = jnp.where(bucket == b, rb_ref[b, hd] * LOG2E, acc)
    o_ref[0] = acc


def _bias_table(rel_bias):
    w = np.arange(BIAS_W)[:, None]
    i = np.arange(ATT_TQ)[None, :]
    bucket = jnp.asarray(_rel_bucket_np(np.maximum(i - w + BIAS_C, 0)))
    return pl.pallas_call(
        _bias_table_kernel,
        out_shape=jax.ShapeDtypeStruct((N_HEADS, BIAS_W, ATT_TQ), F32),
        grid=(N_HEADS,),
        in_specs=[
            pl.BlockSpec((BIAS_W, ATT_TQ), lambda hd: (0, 0)),
            pl.BlockSpec(memory_space=pltpu.SMEM),
        ],
        out_specs=pl.BlockSpec((1, BIAS_W, ATT_TQ), lambda hd: (hd, 0, 0)),
        compiler_params=_cparams("arbitrary"),
        name="attn_bias_table",
    )(bucket, rel_bias.astype(F32))


def _attn_kernel(qT_ref, qiT_ref, wiT_ref, k_ref, vT_ref, ki_ref, pt_ref, o_ref,
                 keys_scr, negm_scr, pidx_scr, oT_scr, acc_scr, qall_scr, sa_scr, sb_scr, *, top_k):
    tq, kc = ATT_TQ, ATT_KC
    qt = pl.program_id(1)
    q0 = qt * tq
    nch = (q0 + tq + kc - 1) // kc
    tpos = q0 + lax.broadcasted_iota(I32, (kc, tq), 1)
    srow = lax.broadcasted_iota(I32, (kc, tq), 0)

    qiT = qiT_ref[0]
    qi_all = jnp.concatenate([qiT[hd * IDX_DIM:(hd + 1) * IDX_DIM, :] for hd in range(IDX_HEADS)], axis=1)
    wiT = wiT_ref[0]

    def score_chunk(c):
        ks = pl.multiple_of(c * kc, kc)
        kic = ki_ref[0, pl.ds(ks, kc), :]
        dots = jnp.dot(kic, qi_all, preferred_element_type=F32)
        acc = jnp.zeros((kc, tq), F32)
        for hd in range(IDX_HEADS):
            acc = acc + jnp.maximum(dots[:, hd * tq:(hd + 1) * tq], 0.0) * wiT[hd:hd + 1, :]
        acc = jnp.where(acc == 0.0, 0.0, acc)
        bits = pltpu.bitcast(acc, I32)
        key = jnp.where(bits < 0, bits ^ INT_MAX, bits)
        key = jnp.where(ks + srow <= tpos, key, INT_MIN)
        keys_scr[pl.ds(ks, kc), :] = key

    npair = (nch + 1) // 2

    def score_pair(i, carry):
        score_chunk(2 * i)
        score_chunk(2 * i + 1)
        return carry

    lax.fori_loop(0, npair, score_pair, 0)

    @pl.when(nch % 2 == 1)
    def _():
        negm_scr[pl.ds(pl.multiple_of(nch * kc, kc), kc), :] = jnp.full((kc, tq), NEG_BIG, F32)

    srow2 = lax.broadcasted_iota(I32, (2 * kc, tq), 0)

    def count(pred):
        def body(c, acc):
            ks = pl.multiple_of(c * 2 * kc, 2 * kc)
            m = pred(keys_scr[pl.ds(ks, 2 * kc), :], ks + srow2).astype(I32)
            return acc + jnp.sum(m.reshape(2 * kc // SUBLANES, SUBLANES, tq), axis=0)
        acc = lax.fori_loop(0, npair, body, jnp.zeros((SUBLANES, tq), I32))
        return jnp.sum(acc, axis=0, keepdims=True)

    def bit_body(it, p):
        cand_p = p | lax.shift_left(jnp.int32(1), 31 - it)
        cand = cand_p ^ INT_MIN
        cnt = count(lambda k, s: k >= cand)
        return jnp.where(cnt >= top_k, cand_p, p)

    p_fin = lax.fori_loop(0, 32, bit_body, jnp.zeros((1, tq), I32))
    v = p_fin ^ INT_MIN

    cnt_gt = count(lambda k, s: k > v)
    cnt_eq = count(lambda k, s: k == v)
    need = top_k - cnt_gt
    pidx_scr[...] = jnp.full((1, tq), INT_MAX, I32)
    pos_bits = (keys_scr.shape[0] - 1).bit_length()

    @pl.when(jnp.max(cnt_eq - need) > 0)
    def _():
        def ibit(it, p):
            cand = p | lax.shift_left(jnp.int32(1), pos_bits - 1 - it)
            cnt = count(lambda k, s: jnp.logical_and(k == v, s < cand))
            return jnp.where(cnt < need, cand, p)
        pidx_scr[...] = lax.fori_loop(0, pos_bits, ibit, jnp.zeros((1, tq), I32))

    pidx = pidx_scr[...]

    def mask_chunk(c, carry):
        ks = pl.multiple_of(c * kc, kc)
        k = keys_scr[pl.ds(ks, kc), :]
        spos = ks + srow
        sel = jnp.logical_or(k > v, jnp.logical_and(k == v, spos <= pidx))
        sel = jnp.logical_and(sel, spos <= tpos)
        negm_scr[pl.ds(ks, kc), :] = jnp.where(sel, 0.0, NEG_BIG)
        return carry

    lax.fori_loop(0, nch, mask_chunk, 0)

    acc_scr[...] = jnp.zeros(acc_scr.shape, F32)
    for n in range(N_KV_HEADS):
        r0 = n * N_REP * HEAD_DIM
        qall_scr[n] = jnp.concatenate(
            [qT_ref[0, r0 + g * HEAD_DIM:r0 + (g + 1) * HEAD_DIM, :] for g in range(N_REP)], axis=1)

    def qk_chunk(c, s_ref):
        ks = pl.multiple_of(c * kc, kc)
        for n in range(N_KV_HEADS):
            s_ref[n] = jnp.dot(k_ref[0, n, pl.ds(ks, kc), :], qall_scr[n], preferred_element_type=F32)

    def softmax_pv(c, s_ref, ms, far):
        ks = pl.multiple_of(c * kc, kc)
        negm = negm_scr[pl.ds(ks, kc), :]
        w0 = pl.multiple_of(jnp.clip(BIAS_C - (q0 - ks), 0, BIAS_C), LANES)
        new_ms = []
        for n in range(N_KV_HEADS):
            s = s_ref[n]
            if far:
                cvec = jnp.concatenate([pt_ref[n * N_REP + g, 0:1, :] for g in range(N_REP)], axis=1)
                lg = jnp.concatenate([s[:, g * tq:(g + 1) * tq] + negm for g in range(N_REP)], axis=1)
                m_new = jnp.maximum(ms[n], jnp.max(lg, axis=0, keepdims=True) + cvec)
                p = jnp.exp2(lg - (m_new - cvec))
            else:
                lg = jnp.concatenate(
                    [s[:, g * tq:(g + 1) * tq] + pt_ref[n * N_REP + g, pl.ds(w0, kc), :] + negm
                     for g in range(N_REP)], axis=1)
                m_new = jnp.maximum(ms[n], jnp.max(lg, axis=0, keepdims=True))
                p = jnp.exp2(lg - m_new)
            alpha = jnp.exp2(ms[n] - m_new)
            acc_scr[n] = alpha * acc_scr[n] + jnp.dot(vT_ref[0, c, n], p.astype(BF16),
                                                      preferred_element_type=F32)
            new_ms.append(m_new)
        return tuple(new_ms)

    last_chunk = k_ref.shape[2] // kc - 1

    def pair_step(i, ms, far):
        c0 = 2 * i
        qk_chunk(c0 + 1, sb_scr)
        ms = softmax_pv(c0, sa_scr, ms, far)
        qk_chunk(jnp.minimum(c0 + 2, last_chunk), sa_scr)
        return softmax_pv(c0 + 1, sb_scr, ms, far)

    n_far = jnp.clip((q0 - BIAS_C + kc) // kc, 0, nch)
    ms = tuple(jnp.full((1, N_REP * tq), NEG_BIG, F32) for _ in range(N_KV_HEADS))
    qk_chunk(0, sa_scr)
    ms = lax.fori_loop(0, n_far // 2, functools.partial(pair_step, far=True), ms)
    lax.fori_loop(n_far // 2, npair, functools.partial(pair_step, far=False), ms)
    for n in range(N_KV_HEADS):
        o_t = acc_scr[n, 0:HEAD_DIM, :] / acc_scr[n, HEAD_DIM:HEAD_DIM + 1, :]
        for g in range(N_REP):
            r0 = (n * N_REP + g) * HEAD_DIM
            oT_scr[r0:r0 + HEAD_DIM, :] = o_t[:, g * tq:(g + 1) * tq]

    o_ref[0] = oT_scr[...].T.astype(BF16)


def _attention(qT, qiT, wiT, k4, vT, ki, ptab, top_k):
    bsz, dq, seq = qT.shape
    tq = ATT_TQ
    assert seq % (2 * ATT_KC) == 0
    kern = functools.partial(_attn_kernel, top_k=top_k)
    return pl.pallas_call(
        kern,
        out_shape=jax.ShapeDtypeStruct((bsz, seq, dq), BF16),
        grid=(bsz, seq // tq),
        in_specs=[
            pl.BlockSpec((1, dq, tq), lambda b, i: (b, 0, i)),
            pl.BlockSpec((1, qiT.shape[1], tq), lambda b, i: (b, 0, i)),
            pl.BlockSpec((1, wiT.shape[1], tq), lambda b, i: (b, 0, i)),
            pl.BlockSpec((1,) + k4.shape[1:], lambda b, i: (b, 0, 0, 0)),
            pl.BlockSpec((1,) + vT.shape[1:], lambda b, i: (b, 0, 0, 0, 0)),
            pl.BlockSpec((1,) + ki.shape[1:], lambda b, i: (b, 0, 0)),
            pl.BlockSpec(ptab.shape, lambda b, i: (0, 0, 0)),
        ],
        out_specs=pl.BlockSpec((1, tq, dq), lambda b, i: (b, i, 0)),
        scratch_shapes=[
            pltpu.VMEM((seq, tq), I32),
            pltpu.VMEM((seq, tq), F32),
            pltpu.VMEM((1, tq), I32),
            pltpu.VMEM((dq, tq), F32),
            pltpu.VMEM((N_KV_HEADS, ATT_VROWS, N_REP * tq), F32),
            pltpu.VMEM((N_KV_HEADS, HEAD_DIM, N_REP * tq), BF16),
            pltpu.VMEM((N_KV_HEADS, ATT_KC, N_REP * tq), F32),
            pltpu.VMEM((N_KV_HEADS, ATT_KC, N_REP * tq), F32),
        ],
        compiler_params=_cparams("parallel", "arbitrary"),
        name="attn_core",
    )(qT, qiT, wiT, k4, vT, ki, ptab)


def _proj_res_kernel(a_ref, x_ref, gate_ref, w_ref, o_ref):
    y = jnp.dot(a_ref[0], w_ref[...], preferred_element_type=F32)
    o_ref[0] = x_ref[0] + gate_ref[0] * y


def _proj_residual(a, x, gate, w, tm=512):
    bsz, seq, d = x.shape
    tm = min(tm, seq)
    return pl.pallas_call(
        _proj_res_kernel,
        out_shape=jax.ShapeDtypeStruct((bsz, seq, d), F32),
        grid=(bsz, seq // tm),
        in_specs=[
            pl.BlockSpec((1, tm, a.shape[2]), lambda b, i: (b, i, 0)),
            pl.BlockSpec((1, tm, d), lambda b, i: (b, i, 0)),
            pl.BlockSpec((1, 1, d), lambda b, i: (b, 0, 0)),
            pl.BlockSpec(w.shape, lambda b, i: (0, 0)),
        ],
        out_specs=pl.BlockSpec((1, tm, d), lambda b, i: (b, i, 0)),
        compiler_params=_cparams("parallel", "parallel"),
        name="proj_residual",
    )(a, x, gate, w)


def _attn_layer(x, g, sh, sc, gate, w_in, q_gain, k_gain, w_out, rel_bias):
    bsz, seq, d = x.shape
    top_k = min(TOPK_MAX, seq // 4)
    dq = N_HEADS * HEAD_DIM
    dk = N_KV_HEADS * HEAD_DIM
    dqi = IDX_HEADS * IDX_DIM
    nmain = dq + 2 * dk + dqi
    w_main = w_in[:, :nmain].astype(BF16)
    w_small = jnp.pad(w_in[:, nmain:], ((0, 0), (0, LANES - (IDX_DIM + IDX_HEADS)))).astype(BF16)
    q, k, v, qi, kiwi = _qkv_proj(x, g, sh, sc, w_main, w_small, q_gain, k_gain)
    ki = kiwi[:, :, :IDX_DIM].astype(BF16)
    wi = kiwi[:, :, IDX_DIM:IDX_DIM + IDX_HEADS] * (IDX_HEADS ** -0.5 * IDX_DIM ** -0.5)
    qT = jnp.swapaxes(q, 1, 2)
    qiT = jnp.swapaxes(qi, 1, 2)
    wiT = jnp.swapaxes(wi, 1, 2)
    nck = seq // ATT_KC
    vT = jnp.swapaxes(v.reshape(bsz, nck, ATT_KC, dk), 2, 3).reshape(bsz, nck, N_KV_HEADS, HEAD_DIM, ATT_KC)
    vT = jnp.concatenate([
        vT, jnp.ones((bsz, nck, N_KV_HEADS, 1, ATT_KC), BF16),
        jnp.zeros((bsz, nck, N_KV_HEADS, ATT_VROWS - HEAD_DIM - 1, ATT_KC), BF16)], axis=3)
    k4 = jnp.swapaxes(k.reshape(bsz, seq, N_KV_HEADS, HEAD_DIM), 1, 2)
    ptab = _bias_table(rel_bias)
    attn = _attention(qT, qiT, wiT, k4, vT, ki, ptab, top_k)
    return _proj_residual(attn, x, gate, w_out.astype(BF16))


def _s5_prep_kernel(lre_ref, lim_ref, ls_ref, bre_ref, bim_ref, cre_ref, cim_ref,
                    bcw_ref, mtw_ref, ccw_ref, are_ref, aim_ref):
    t_len = SSM_CHUNK
    lre = jnp.minimum(lre_ref[0], -1e-4)
    lim = lim_ref[0]
    step = jnp.exp(ls_ref[0])
    ar = lre * step
    ai = lim * step

    def powers(jv):
        mag = jnp.exp(jv * ar)
        return mag * jnp.cos(jv * ai), mag * jnp.sin(jv * ai)

    lb_re, lb_im = powers(1.0)
    nr = lb_re - 1.0
    ni = lb_im
    den = lre * lre + lim * lim
    cf_re = (nr * lre + ni * lim) / den
    cf_im = (ni * lre - nr * lim) / den
    bre = bre_ref[0]
    bim = bim_ref[0]
    bb_re = cf_re * bre - cf_im * bim
    bb_im = cf_re * bim + cf_im * bre
    cre = cre_ref[0]
    cim = cim_ref[0]
    nst = lre.shape[-1]
    jv = lax.broadcasted_iota(I32, (t_len, 1, nst), 0).astype(F32)
    pj_re, pj_im = powers(jv)
    a_re = (cre[None] * pj_re - cim[None] * pj_im).reshape(t_len * SSM_GROUP, nst)
    a_im = (cre[None] * pj_im + cim[None] * pj_re).reshape(t_len * SSM_GROUP, nst)
    dn = (((1,), (1,)), ((), ()))
    cg = SSM_GROUP
    tc = t_len * cg
    width = t_len * LANES
    gq = pl.program_id(0) % (LANES // cg)
    nh = (LANES // cg) * nst
    cg_shift = cg.bit_length() - 1

    def place(nrows, target):
        r = lax.broadcasted_iota(I32, (nrows, width), 0)
        col = lax.broadcasted_iota(I32, (nrows, width), 1)
        return jnp.where(col == target(r), 1.0, 0.0).astype(BF16)

    pm = place(tc, lambda r: lax.shift_right_logical(r, cg_shift) * LANES + gq * cg + (r & (cg - 1)))
    k_t = (lax.dot_general(bb_re, a_re, dn, preferred_element_type=F32, precision=HIGHEST)
           - lax.dot_general(bb_im, a_im, dn, preferred_element_type=F32, precision=HIGHEST))
    lane = lax.broadcasted_iota(I32, (cg, tc), 1)
    mt_rows = [k_t] + [jnp.where(lane >= s * cg, pltpu.roll(k_t, s * cg, 1), 0.0) for s in range(1, t_len)]
    mt_t = jnp.concatenate(mt_rows, axis=0)
    mtw_ref[0] = jnp.dot(mt_t.astype(BF16), pm, preferred_element_type=F32).astype(BF16)
    pr_re, pr_im = powers((t_len - 1.0) - jv)
    bc_re = (bb_re[None] * pr_re - bb_im[None] * pr_im).reshape(tc, nst)
    bc_im = (bb_re[None] * pr_im + bb_im[None] * pr_re).reshape(tc, nst)
    pb_re = place(nst, lambda r: gq * nst + r)
    pb_im = place(nst, lambda r: nh + gq * nst + r)
    bcw_ref[0] = (jnp.dot(bc_re.astype(BF16), pb_re, preferred_element_type=F32)
                  + jnp.dot(bc_im.astype(BF16), pb_im, preferred_element_type=F32)).astype(BF16)
    pn_re, pn_im = powers(jv + 1.0)
    cc_re = (cre[None] * pn_re - cim[None] * pn_im).reshape(tc, nst)
    cc_im = (cre[None] * pn_im + cim[None] * pn_re).reshape(tc, nst)
    eye = jnp.where(lax.broadcasted_iota(I32, (nst, nst), 0) == lax.broadcasted_iota(I32, (nst, nst), 1),
                    1.0, 0.0)
    cct_re = lax.dot_general(eye, cc_re, dn, preferred_element_type=F32, precision=HIGHEST)
    cct_im = lax.dot_general(eye, cc_im, dn, preferred_element_type=F32, precision=HIGHEST)
    ccw_ref[0, 0:nst, :] = jnp.dot(cct_re.astype(BF16), pm, preferred_element_type=F32).astype(BF16)
    ccw_ref[0, nst:2 * nst, :] = (-jnp.dot(cct_im.astype(BF16), pm, preferred_element_type=F32)).astype(BF16)
    at_re, at_im = powers(float(t_len))
    rs = lax.broadcasted_iota(I32, (nst, nh), 0)
    cs = lax.broadcasted_iota(I32, (nst, nh), 1)
    pa = jnp.where(cs == gq * nst + rs, 1.0, 0.0)
    are_ref[0] = jnp.dot(at_re, pa, preferred_element_type=F32, precision=HIGHEST)
    aim_ref[0] = jnp.dot(at_im, pa, preferred_element_type=F32, precision=HIGHEST)


def _s5_prep(lam_re, lam_im, log_step, b_re, b_im, c_re, c_im):
    ng, nst = lam_re.shape
    tc = SSM_CHUNK * SSM_GROUP
    vec = pl.BlockSpec((1, 1, nst), lambda gi: (gi, 0, 0))
    mat = pl.BlockSpec((1, SSM_GROUP, nst), lambda gi: (gi, 0, 0))
    width = SSM_CHUNK * LANES
    assert tc == 2 * nst and 2 * (LANES // SSM_GROUP) * nst == width
    big = pl.BlockSpec((1, tc, width), lambda gi: (gi, 0, 0))
    wide = jax.ShapeDtypeStruct((ng, tc, width), BF16)
    return pl.pallas_call(
        _s5_prep_kernel,
        out_shape=[wide, wide, wide,
                   jax.ShapeDtypeStruct((ng, 1, width // 2), F32),
                   jax.ShapeDtypeStruct((ng, 1, width // 2), F32)],
        grid=(ng,),
        in_specs=[vec, vec, pl.BlockSpec((1, 1, 1), lambda gi: (gi, 0, 0)), mat, mat, mat, mat],
        out_specs=[big, big, big, pl.BlockSpec((1, 1, width // 2), lambda gi: (gi, 0, 0)),
                   pl.BlockSpec((1, 1, width // 2), lambda gi: (gi, 0, 0))],
        compiler_params=_cparams("parallel"),
        name="s5_prep",
    )(lam_re.reshape(ng, 1, nst), lam_im.reshape(ng, 1, nst), log_step.reshape(ng, 1, 1),
      jnp.swapaxes(b_re, 1, 2), jnp.swapaxes(b_im, 1, 2), c_re, c_im)


def _s5_pre_kernel(x_ref, g_ref, sh_ref, sc_ref, o_ref, h_scr):
    t_len = SSM_CHUNK
    h = _norm_mod(x_ref[0], g_ref[...], sh_ref[0], sc_ref[0])
    nj = o_ref.shape[1]
    for q in range(o_ref.shape[0]):
        h_scr[q] = h[:, q * LANES:(q + 1) * LANES]
        for t in range(t_len):
            o_ref[q, :, t * LANES:(t + 1) * LANES] = h_scr[q, pl.ds(t, nj, stride=t_len), :].astype(BF16)


def _s5_pre(x, g, sh, sc, tm=512):
    bsz, seq, d = x.shape
    tm = min(tm, seq)
    nt = seq // tm
    nq = d // LANES
    t_len = SSM_CHUNK
    mod_spec = pl.BlockSpec((1, 1, d), lambda b, i: (b, 0, 0))
    return pl.pallas_call(
        _s5_pre_kernel,
        out_shape=jax.ShapeDtypeStruct((nq, bsz * seq // t_len, t_len * LANES), BF16),
        grid=(bsz, nt),
        in_specs=[pl.BlockSpec((1, tm, d), lambda b, i: (b, i, 0)),
                  pl.BlockSpec((1, d), lambda b, i: (0, 0)), mod_spec, mod_spec],
        out_specs=pl.BlockSpec((nq, tm // t_len, t_len * LANES), lambda b, i: (0, b * nt + i, 0)),
        scratch_shapes=[pltpu.VMEM((nq, tm, LANES), F32)],
        compiler_params=_cparams("parallel", "parallel"),
        name="s5_pre",
    )(x, g, sh, sc)


def _s5_scan_kernel(x_ref, bc_ref, mt_ref, cc_ref, are_ref, aim_ref, y_ref, re_scr, im_scr):
    x = x_ref[0]
    nrow = x.shape[0]
    pad = re_scr.shape[0] - nrow
    v = jnp.dot(x, bc_ref[0], preferred_element_type=F32)
    nh = v.shape[1] // 2
    s_re = v[:, :nh]
    s_im = v[:, nh:]
    a_re = are_ref[0]
    a_im = aim_ref[0]
    re_scr[0:pad, :] = jnp.zeros((pad, nh), F32)
    im_scr[0:pad, :] = jnp.zeros((pad, nh), F32)

    def shifted(scr, val, dist):
        scr[pad:pad + nrow, :] = val
        return scr[pad - dist:pad - dist + nrow, :]

    dist = 1
    while dist < nrow:
        sh_re = shifted(re_scr, s_re, dist)
        sh_im = shifted(im_scr, s_im, dist)
        s_re, s_im = (s_re + a_re * sh_re - a_im * sh_im, s_im + a_re * sh_im + a_im * sh_re)
        a_re, a_im = (a_re * a_re - a_im * a_im, 2.0 * a_re * a_im)
        dist *= 2
    sp = jnp.concatenate([shifted(re_scr, s_re, 1), shifted(im_scr, s_im, 1)], axis=1).astype(BF16)
    y_ref[0] = (jnp.dot(x, mt_ref[0], preferred_element_type=F32)
                + jnp.dot(sp, cc_ref[0], preferred_element_type=F32))


def _s5_scan(hq, bcq, mtq, ccq, a_re, a_im, bsz):
    nq, nrows, width = hq.shape
    nj = nrows // bsz
    nh = a_re.shape[-1]
    pad = max(nj // 2, SUBLANES)
    tile = pl.BlockSpec((1, nj, width), lambda q, b: (q, b, 0))
    wspec = lambda a: pl.BlockSpec((1,) + a.shape[1:], lambda q, b: (q, 0, 0))
    return pl.pallas_call(
        _s5_scan_kernel,
        out_shape=jax.ShapeDtypeStruct((nq, nrows, width), F32),
        grid=(nq, bsz),
        in_specs=[tile, wspec(bcq), wspec(mtq), wspec(ccq), wspec(a_re), wspec(a_im)],
        out_specs=tile,
        scratch_shapes=[pltpu.VMEM((pad + nj, nh), F32), pltpu.VMEM((pad + nj, nh), F32)],
        compiler_params=_cparams("parallel", "parallel"),
        name="s5_scan",
    )(hq, bcq, mtq, ccq, a_re, a_im)


def _s5_post_kernel(x_ref, y_ref, g_ref, sh_ref, sc_ref, gate_ref, dsk_ref, w_ref, o_ref, y_scr, *, d):
    t_len = SSM_CHUNK
    x = x_ref[0]
    h = _norm_mod(x, g_ref[...], sh_ref[0], sc_ref[0])
    nj = y_ref.shape[1]
    for q in range(y_ref.shape[0]):
        for t in range(t_len):
            y_scr[q, pl.ds(t, nj, stride=t_len), :] = y_ref[q, :, t * LANES:(t + 1) * LANES]
    y = jnp.concatenate([y_scr[q] for q in range(y_ref.shape[0])], axis=1)
    yy = y + dsk_ref[...] * h
    gl = jax.nn.gelu(yy).astype(BF16)
    z = jnp.dot(gl, w_ref[...], preferred_element_type=F32)
    o_ref[0] = x + gate_ref[0] * (z[:, :d] * jax.nn.sigmoid(z[:, d:]))


def _s5_post(x, yq, g, sh, sc, gate, d_skip, w_glu, tm=512):
    bsz, seq, d = x.shape
    tm = min(tm, seq)
    nt = seq // tm
    nq = d // LANES
    kern = functools.partial(_s5_post_kernel, d=d)
    mod_spec = pl.BlockSpec((1, 1, d), lambda b, i: (b, 0, 0))
    tile = pl.BlockSpec((1, tm, d), lambda b, i: (b, i, 0))
    return pl.pallas_call(
        kern,
        out_shape=jax.ShapeDtypeStruct((bsz, seq, d), F32),
        grid=(bsz, nt),
        in_specs=[tile,
                  pl.BlockSpec((nq, tm // SSM_CHUNK, SSM_CHUNK * LANES), lambda b, i: (0, b * nt + i, 0)),
                  pl.BlockSpec((1, d), lambda b, i: (0, 0)), mod_spec, mod_spec, mod_spec,
                  pl.BlockSpec((1, d), lambda b, i: (0, 0)),
                  pl.BlockSpec(w_glu.shape, lambda b, i: (0, 0))],
        out_specs=tile,
        scratch_shapes=[pltpu.VMEM((nq, tm, LANES), F32)],
        compiler_params=_cparams("parallel", "parallel"),
        name="s5_post",
    )(x, yq, g, sh, sc, gate, d_skip, w_glu)


def _s5_layer(x, g, sh, sc, gate, lam_re, lam_im, log_step, b_re, b_im, c_re, c_im, d_skip, w_glu):
    bsz, seq, d = x.shape
    ng, nst = lam_re.shape
    t_len, cg = SSM_CHUNK, SSM_GROUP
    nq = d // LANES
    gq = LANES // cg
    nj = seq // t_len
    bcw, mtw, ccw, a_re, a_im = _s5_prep(lam_re, lam_im, log_step, b_re, b_im, c_re, c_im)
    width = t_len * LANES
    bcq = jnp.swapaxes(bcw.reshape(nq, gq, t_len, cg, width), 1, 2).reshape(nq, width, width)
    mtq = jnp.swapaxes(mtw.reshape(nq, gq, t_len, cg, width), 1, 2).reshape(nq, width, width)
    ccq = jnp.swapaxes(ccw.reshape(nq, gq, 2, nst, width), 1, 2).reshape(nq, 2 * gq * nst, width)
    aq_re = jnp.sum(a_re.reshape(nq, gq, 1, gq * nst), axis=1)
    aq_im = jnp.sum(a_im.reshape(nq, gq, 1, gq * nst), axis=1)

    hq = _s5_pre(x, g, sh, sc)
    yq = _s5_scan(hq, bcq, mtq, ccq, aq_re, aq_im, bsz)
    return _s5_post(x, yq, g, sh, sc, gate, d_skip.reshape(1, d).astype(F32), w_glu.astype(BF16))


def kernel(x, c, ada_w, ada_b, norm_g, conv_w_in, conv_w, conv_w_out, attn_w_in, attn_q_gain, attn_k_gain, attn_w_out, rel_bias, ssm_lambda_re, ssm_lambda_im, ssm_log_step, ssm_b_re, ssm_b_im, ssm_c_re, ssm_c_im, ssm_d, ssm_w_glu, ffn_w_gu, ffn_w_down, moe_router_w, moe_router_b, moe_w_gu, moe_w_down):
    bsz, seq, d = x.shape
    depth = ada_w.shape[0]
    mod = _ada_mod(c, ada_w, ada_b).reshape(depth, bsz, 6, 1, d)
    moe_gu = moe_w_gu.reshape((-1,) + moe_w_gu.shape[2:])
    moe_down = moe_w_down.reshape((-1,) + moe_w_down.shape[2:])
    for i in range(depth):
        sh1, sc1, g1, sh2, sc2, g2 = (mod[i, :, r] for r in range(6))
        gn1 = norm_g[i, 0].reshape(1, d)
        gn2 = norm_g[i, 1].reshape(1, d)
        j = i // N_MIXERS
        if i % N_MIXERS == 0:
            x = _conv_layer(x, gn1, sh1, sc1, g1, conv_w_in[j].astype(BF16), conv_w[j],
                            conv_w_out[j].astype(BF16))
        elif i % N_MIXERS == 1:
            x = _attn_layer(x, gn1, sh1, sc1, g1, attn_w_in[j], attn_q_gain[j], attn_k_gain[j],
                            attn_w_out[j], rel_bias)
        else:
            x = _s5_layer(x, gn1, sh1, sc1, g1, ssm_lambda_re[j], ssm_lambda_im[j], ssm_log_step[j],
                          ssm_b_re[j], ssm_b_im[j], ssm_c_re[j], ssm_c_im[j], ssm_d[j], ssm_w_glu[j])
        if i % 2 == 0:
            x = _ffn_layer(x, gn2, sh2, sc2, g2, ffn_w_gu[i // 2].astype(BF16), ffn_w_down[i // 2].astype(BF16))
        else:
            x = _moe_layer(x, gn2, sh2, sc2, g2, moe_router_w[i // 2], moe_router_b[i // 2],
                           moe_gu, moe_down, ebase=(i // 2) * moe_w_gu.shape[1])
    return x
```

```python
import functools
import math

import numpy as np
import jax
import jax.numpy as jnp
from jax import lax
from jax.experimental import pallas as pl
from jax.experimental.pallas import tpu as pltpu

F32 = jnp.float32
BF16 = jnp.bfloat16
I32 = jnp.int32
HIGHEST = lax.Precision.HIGHEST

DEPTH = 4
N_MIXERS = 3
EPS = 1e-6
CONV_WIDTH = 3
N_HEADS = 16
N_KV_HEADS = 4
N_REP = N_HEADS // N_KV_HEADS
HEAD_DIM = 64
IDX_HEADS = 8
IDX_DIM = 64
TOPK_MAX = 256
REL_BUCKETS = 32
REL_MAX_DIST = 128
SSM_GROUP = 16
SSM_STATE = 64
N_EXPERTS = 8
TOP_K_EXPERTS = 2

VMEM_LIMIT_BYTES = 56 * 1024 * 1024
LANES = 128
SUBLANES = 8

INT_MIN = -(2 ** 31)
INT_MAX = 2 ** 31 - 1
NEG_BIG = -1e30
LOG2E = 1.4426950408889634

ATT_TQ = 128
ATT_KC = 256
BIAS_C = 384
BIAS_W = BIAS_C + ATT_KC
ATT_VROWS = 80

SSM_CHUNK = 8


def _cparams(*sem):
    return pltpu.CompilerParams(dimension_semantics=sem, vmem_limit_bytes=VMEM_LIMIT_BYTES)


def _norm_mod(x, g, shift, scale):
    ms = jnp.mean(x * x, axis=-1, keepdims=True)
    y = x * lax.rsqrt(ms + EPS)
    return (y * g) * (1.0 + scale) + shift


def _silu(x):
    return x * jax.nn.sigmoid(x)


def _ada_kernel(c_ref, w_ref, b_ref, o_ref):
    c = c_ref[...]
    cond = _silu(c)
    o_ref[0] = jnp.dot(cond, w_ref[0], preferred_element_type=F32, precision=HIGHEST) + b_ref[0]


def _ada_mod(c, ada_w, ada_b):
    depth, d, d6 = ada_w.shape
    bsz = c.shape[0]
    tn = d
    return pl.pallas_call(
        _ada_kernel,
        out_shape=jax.ShapeDtypeStruct((depth, bsz, d6), F32),
        grid=(depth, d6 // tn),
        in_specs=[
            pl.BlockSpec((bsz, d), lambda i, j: (0, 0)),
            pl.BlockSpec((1, d, tn), lambda i, j: (i, 0, j)),
            pl.BlockSpec((1, 1, tn), lambda i, j: (i, 0, j)),
        ],
        out_specs=pl.BlockSpec((1, bsz, tn), lambda i, j: (i, 0, j)),
        compiler_params=_cparams("parallel", "parallel"),
        name="ada_mod",
    )(c, ada_w, ada_b.reshape(depth, 1, d6))


def _conv_kernel(x_ref, xh_ref, g_ref, sh_ref, sc_ref, gate_ref, win_ref, wc_ref, wout_ref,
                 o_ref, u_scr, *, tm, d):
    i = pl.program_id(1)
    g = g_ref[...]
    sh = sh_ref[0]
    sc = sc_ref[0]
    x = x_ref[0]
    halo = xh_ref.shape[1]
    xe = jnp.concatenate([xh_ref[0], x], axis=0)
    he = _norm_mod(xe, g, sh, sc).astype(BF16)
    z = jnp.dot(he, win_ref[...], preferred_element_type=F32)
    u_all = z[:, d:2 * d] * z[:, 2 * d:]
    row = lax.broadcasted_iota(I32, (halo + tm, 1), 0)
    u_scr[...] = jnp.where(jnp.logical_or(i > 0, row >= halo), u_all, 0.0)
    b_gate = z[halo:, :d]
    wc = wc_ref[...]
    conv = (wc[0:1, :] * u_scr[halo - 2:halo - 2 + tm, :]
            + wc[1:2, :] * u_scr[halo - 1:halo - 1 + tm, :]
            + wc[2:3, :] * u_scr[halo:halo + tm, :])
    y = jnp.dot((b_gate * conv).astype(BF16), wout_ref[...], preferred_element_type=F32)
    o_ref[0] = x + gate_ref[0] * y


def _conv_layer(x, g, sh, sc, gate, w_in, w_conv, w_out, tm=512):
    bsz, seq, d = x.shape
    tm = min(tm, seq)
    nt = seq // tm
    halo = 2 * SUBLANES
    hb = tm // halo
    kern = functools.partial(_conv_kernel, tm=tm, d=d)
    mod_spec = pl.BlockSpec((1, 1, d), lambda b, i: (b, 0, 0))
    return pl.pallas_call(
        kern,
        out_shape=jax.ShapeDtypeStruct((bsz, seq, d), F32),
        grid=(bsz, nt),
        in_specs=[
            pl.BlockSpec((1, tm, d), lambda b, i: (b, i, 0)),
            pl.BlockSpec((1, halo, d), lambda b, i: (b, jnp.maximum(i * hb - 1, 0), 0)),
            pl.BlockSpec((1, d), lambda b, i: (0, 0)),
            mod_spec, mod_spec, mod_spec,
            pl.BlockSpec((d, 3 * d), lambda b, i: (0, 0)),
            pl.BlockSpec((CONV_WIDTH, d), lambda b, i: (0, 0)),
            pl.BlockSpec((d, d), lambda b, i: (0, 0)),
        ],
        out_specs=pl.BlockSpec((1, tm, d), lambda b, i: (b, i, 0)),
        scratch_shapes=[pltpu.VMEM((tm + halo, d), F32)],
        compiler_params=_cparams("parallel", "parallel"),
        name="conv_mixer",
    )(x, x, g, sh, sc, gate, w_in, w_conv, w_out)


FFN_WCHUNKS = 8


def _ffn_kernel(x_ref, g_ref, sh_ref, sc_ref, gate_ref, wgu_hbm, wd_hbm, o_ref,
                wgu_scr, wd_scr, gu_stage, d_stage, sems, *, dff, nchunk, layer):
    d = x_ref.shape[2]
    gu_rows = d // FFN_WCHUNKS
    d_rows = dff // FFN_WCHUNKS

    def gu_copy(c):
        return pltpu.make_async_copy(wgu_hbm.at[layer, pl.ds(c * gu_rows, gu_rows), :], gu_stage.at[c % 2],
                                     sems.at[0, c % 2])

    def d_copy(c):
        return pltpu.make_async_copy(wd_hbm.at[layer, pl.ds(c * d_rows, d_rows), :], d_stage.at[c % 2],
                                     sems.at[1, c % 2])

    @pl.when(jnp.logical_and(pl.program_id(0) == 0, pl.program_id(1) == 0))
    def _():
        for c in range(2):
            gu_copy(c).start()
            d_copy(c).start()
        for c in range(FFN_WCHUNKS):
            gu_copy(c).wait()
            wgu_scr[c * gu_rows:(c + 1) * gu_rows, :] = gu_stage[c % 2].astype(BF16)
            d_copy(c).wait()
            wd_scr[c * d_rows:(c + 1) * d_rows, :] = d_stage[c % 2].astype(BF16)
            if c + 2 < FFN_WCHUNKS:
                gu_copy(c + 2).start()
                d_copy(c + 2).start()

    x = x_ref[0]
    h = _norm_mod(x, g_ref[...], sh_ref[0], sc_ref[0]).astype(BF16)
    cols = dff // nchunk
    acc = jnp.zeros(x.shape, F32)
    for c in range(nchunk):
        gg = jnp.dot(h, wgu_scr[:, c * cols:(c + 1) * cols], preferred_element_type=F32)
        uu = jnp.dot(h, wgu_scr[:, dff + c * cols:dff + (c + 1) * cols], preferred_element_type=F32)
        a = (_silu(gg) * uu).astype(BF16)
        acc = acc + jnp.dot(a, wd_scr[c * cols:(c + 1) * cols, :], preferred_element_type=F32)
    o_ref[0] = x + gate_ref[0] * acc


def _ffn_layer(x, g, sh, sc, gate, w_gu, w_down, layer, tm=512):
    bsz, seq, d = x.shape
    dff = w_down.shape[1]
    tm = min(tm, seq)
    assert d % (FFN_WCHUNKS * 2 * SUBLANES) == 0 and dff % (FFN_WCHUNKS * 2 * SUBLANES) == 0
    kern = functools.partial(_ffn_kernel, dff=dff, nchunk=2, layer=layer)
    mod_spec = pl.BlockSpec((1, 1, d), lambda b, i: (b, 0, 0))
    return pl.pallas_call(
        kern,
        out_shape=jax.ShapeDtypeStruct((bsz, seq, d), F32),
        grid=(bsz, seq // tm),
        in_specs=[
            pl.BlockSpec((1, tm, d), lambda b, i: (b, i, 0)),
            pl.BlockSpec((1, d), lambda b, i: (0, 0)),
            mod_spec, mod_spec, mod_spec,
            pl.BlockSpec(memory_space=pl.ANY),
            pl.BlockSpec(memory_space=pl.ANY),
        ],
        out_specs=pl.BlockSpec((1, tm, d), lambda b, i: (b, i, 0)),
        scratch_shapes=[pltpu.VMEM((d, 2 * dff), BF16),
                        pltpu.VMEM((dff, d), BF16),
                        pltpu.VMEM((2, d // FFN_WCHUNKS, 2 * dff), F32),
                        pltpu.VMEM((2, dff // FFN_WCHUNKS, d), F32),
                        pltpu.SemaphoreType.DMA((2, 2))],
        compiler_params=_cparams("arbitrary", "arbitrary"),
        name="ffn_dense",
    )(x, g, sh, sc, gate, w_gu, w_down)


MOE_SB = 512
MOE_RT = 512
MOE_ALIGN = 16
MOE_WIN = 256
MOE_CAP = MOE_WIN - MOE_ALIGN


def _moe_router_kernel(x_ref, g_ref, sh_ref, sc_ref, rw_ref, rb_ref, h_ref, meta_ref):
    hf = _norm_mod(x_ref[...], g_ref[...], sh_ref[0], sc_ref[0])
    h_hi = hf.astype(BF16)
    h_ref[...] = h_hi
    rw = rw_ref[...]
    w_hi = rw.astype(BF16)
    w_lo = (rw - w_hi.astype(F32)).astype(BF16)
    h_lo = (hf - h_hi.astype(F32)).astype(BF16)
    logits = (jnp.dot(h_hi, w_hi, preferred_element_type=F32) + jnp.dot(h_lo, w_hi, preferred_element_type=F32)
              + jnp.dot(h_hi, w_lo, preferred_element_type=F32) + rb_ref[...])
    mx = jnp.max(logits, axis=-1, keepdims=True)
    ex = jnp.exp(logits - mx)
    probs = ex / jnp.sum(ex, axis=-1, keepdims=True)
    lane = lax.broadcasted_iota(I32, probs.shape, 1)
    m1 = jnp.max(probs, axis=-1, keepdims=True)
    i1 = jnp.min(jnp.where(probs == m1, lane, LANES), axis=-1, keepdims=True)
    rest = jnp.where(lane == i1, -1.0, probs)
    m2 = jnp.max(rest, axis=-1, keepdims=True)
    i2 = jnp.min(jnp.where(rest == m2, lane, LANES), axis=-1, keepdims=True)
    den = m1 + m2
    gates = jnp.where(lane == i1, m1 / den, 0.0) + jnp.where(lane == i2, m2 / den, 0.0)
    chosen = jnp.where(jnp.logical_or(lane == i1, lane == i2), 1.0, 0.0)
    nrow = chosen.shape[0]
    below = jnp.where(lax.broadcasted_iota(I32, (nrow, nrow), 1) < lax.broadcasted_iota(I32, (nrow, nrow), 0),
                      1.0, 0.0).astype(BF16)
    rank = jnp.dot(below, chosen.astype(BF16), preferred_element_type=F32)
    meta_ref[...] = gates + pltpu.roll(chosen, N_EXPERTS, 1) + pltpu.roll(rank, 2 * N_EXPERTS, 1)


def _moe_router(xt, g, sh, sc, rw_pad, rb_pad, tpb, tm):
    n, d = xt.shape
    mod_spec = pl.BlockSpec((1, 1, d), lambda i: (i // tpb, 0, 0))
    return pl.pallas_call(
        _moe_router_kernel,
        out_shape=[jax.ShapeDtypeStruct((n, d), BF16), jax.ShapeDtypeStruct((n, LANES), F32)],
        grid=(n // tm,),
        in_specs=[pl.BlockSpec((tm, d), lambda i: (i, 0)),
                  pl.BlockSpec((1, d), lambda i: (0, 0)), mod_spec, mod_spec,
                  pl.BlockSpec((d, LANES), lambda i: (0, 0)),
                  pl.BlockSpec((1, LANES), lambda i: (0, 0))],
        out_specs=[pl.BlockSpec((tm, d), lambda i: (i, 0)), pl.BlockSpec((tm, LANES), lambda i: (i, 0))],
        compiler_params=_cparams("parallel"),
        name="moe_router",
    )(xt, g, sh, sc, rw_pad, rb_pad)


def _window(start, count, w):
    s = start + w * MOE_CAP
    n = jnp.minimum(count - w * MOE_CAP, MOE_CAP)
    a = pl.multiple_of((s // MOE_ALIGN) * MOE_ALIGN, MOE_ALIGN)
    return s, n, a


def _moe_dispatch_kernel(start_ref, cnt_ref, h_ref, pos_ref, xs_init_ref, xs_ref,
                         buf, carry, sems, pending):
    del xs_init_ref
    b = pl.program_id(0)
    nb = pl.num_programs(0)
    ne = buf.shape[0]

    @pl.when(b == 0)
    def _():
        carry[...] = jnp.zeros(carry.shape, BF16)
        for e in range(ne):
            pending[e] = 0

    h = h_ref[...]
    riota = lax.broadcasted_iota(I32, (MOE_WIN, h.shape[0]), 0)

    def out_copy(e, a):
        return pltpu.make_async_copy(buf.at[e], xs_ref.at[pl.ds(a, MOE_WIN)], sems.at[e])

    for e in range(ne):
        posrow = pos_ref[0, e:e + 1, :]
        nwin = (cnt_ref[b, e] + MOE_CAP - 1) // MOE_CAP

        def wbody(w, carry_unused, e=e, posrow=posrow):
            s, n, a = _window(start_ref[b, e], cnt_ref[b, e], w)

            @pl.when(pending[e] == 1)
            def _():
                out_copy(e, 0).wait()

            rel = jnp.where(jnp.logical_and(posrow >= s, posrow < s + n), posrow - a, -1)
            onehot = jnp.where(rel == riota, 1.0, 0.0).astype(BF16)
            buf[e] = jnp.dot(onehot, h, preferred_element_type=F32).astype(BF16)
            buf[e, 0:MOE_ALIGN, :] = buf[e, 0:MOE_ALIGN, :] + carry[e]
            c0 = pl.multiple_of(((s + n) // MOE_ALIGN) * MOE_ALIGN - a, MOE_ALIGN)
            carry[e] = buf[e, pl.ds(c0, MOE_ALIGN), :]
            out_copy(e, a).start()
            pending[e] = 1
            return carry_unused

        lax.fori_loop(0, nwin, wbody, 0)

    @pl.when(b == nb - 1)
    def _():
        for e in range(ne):
            @pl.when(pending[e] == 1)
            def _():
                out_copy(e, 0).wait()


def _moe_dispatch(start, cnt, h, pos_t, ncap):
    n, d = h.shape
    nb, ne, sb = pos_t.shape
    xs_init = jnp.zeros((ncap, d), BF16)
    return pl.pallas_call(
        _moe_dispatch_kernel,
        out_shape=jax.ShapeDtypeStruct((ncap, d), BF16),
        grid_spec=pltpu.PrefetchScalarGridSpec(
            num_scalar_prefetch=2,
            grid=(nb,),
            in_specs=[pl.BlockSpec((sb, d), lambda b, *_: (b, 0)),
                      pl.BlockSpec((1, ne, sb), lambda b, *_: (b, 0, 0)),
                      pl.BlockSpec(memory_space=pl.ANY)],
            out_specs=pl.BlockSpec(memory_space=pl.ANY),
            scratch_shapes=[pltpu.VMEM((ne, MOE_WIN, d), BF16),
                            pltpu.VMEM((ne, MOE_ALIGN, d), BF16),
                            pltpu.SemaphoreType.DMA((ne,)),
                            pltpu.SMEM((ne,), I32)],
        ),
        input_output_aliases={4: 0},
        compiler_params=_cparams("arbitrary"),
        name="moe_dispatch",
    )(start, cnt, h, pos_t, xs_init)


MOE_WCHUNKS = 8
MOE_WSTAGE = 4


def _moe_ffn_kernel(blk_ref, exp_ref, nt_ref, xs_ref, wgu_hbm, wd_hbm, ys_init_ref, ys_ref,
                    wgu_scr, wd_scr, gu_stage, d_stage, sems, ahead, *, dff, nchunk, ebase):
    del blk_ref, ys_init_ref
    k = pl.program_id(0)
    d = xs_ref.shape[1]
    gu_rows = d // MOE_WCHUNKS
    d_rows = dff // MOE_WCHUNKS
    e = ebase + exp_ref[k]
    k_next = jnp.minimum(k + 1, pl.num_programs(0) - 1)
    e_next = ebase + exp_ref[k_next]

    def gu_copy(ex, c, slot):
        return pltpu.make_async_copy(wgu_hbm.at[ex, pl.ds(c * gu_rows, gu_rows), :], gu_stage.at[slot],
                                     sems.at[0, slot])

    def d_copy(ex, c, slot):
        return pltpu.make_async_copy(wd_hbm.at[ex, pl.ds(c * d_rows, d_rows), :], d_stage.at[slot],
                                     sems.at[1, slot])

    def start_piece(ex, c):
        gu_copy(ex, c, c % MOE_WSTAGE).start()
        d_copy(ex, c, c % MOE_WSTAGE).start()

    @pl.when(k == 0)
    def _():
        ahead[0] = 0

    new_expert = jnp.logical_or(k == 0, exp_ref[k] != exp_ref[jnp.maximum(k - 1, 0)])

    @pl.when(jnp.logical_and(k < nt_ref[0], new_expert))
    def _():
        @pl.when(ahead[0] == 0)
        def _():
            for c in range(MOE_WSTAGE):
                start_piece(e, c)

        ahead[0] = 0
        for c in range(MOE_WCHUNKS):
            slot = c % MOE_WSTAGE
            gu_copy(e, c, slot).wait()
            wgu_scr[c * gu_rows:(c + 1) * gu_rows, :] = gu_stage[slot].astype(BF16)
            d_copy(e, c, slot).wait()
            wd_scr[c * d_rows:(c + 1) * d_rows, :] = d_stage[slot].astype(BF16)
            if c + MOE_WSTAGE < MOE_WCHUNKS:
                start_piece(e, c + MOE_WSTAGE)

    last_of_expert = jnp.logical_and(k + 1 < nt_ref[0], exp_ref[k_next] != exp_ref[k])

    @pl.when(jnp.logical_and(k < nt_ref[0], last_of_expert))
    def _():
        for c in range(MOE_WSTAGE):
            start_piece(e_next, c)
        ahead[0] = 1

    @pl.when(k < nt_ref[0])
    def _():
        x = xs_ref[...]
        cols = dff // nchunk
        acc = jnp.zeros(x.shape, F32)
        for c in range(nchunk):
            gg = jnp.dot(x, wgu_scr[:, c * cols:(c + 1) * cols], preferred_element_type=F32)
            uu = jnp.dot(x, wgu_scr[:, dff + c * cols:dff + (c + 1) * cols], preferred_element_type=F32)
            a = (_silu(gg) * uu).astype(BF16)
            acc = acc + jnp.dot(a, wd_scr[c * cols:(c + 1) * cols, :], preferred_element_type=F32)
        ys_ref[...] = acc.astype(BF16)


def _moe_ffn(tile_blk, tile_exp, ntiles, xs, w_gu, w_down, ebase):
    ncap, d = xs.shape
    dff = w_down.shape[1]
    assert d % (MOE_WCHUNKS * 2 * SUBLANES) == 0 and dff % (MOE_WCHUNKS * 2 * SUBLANES) == 0
    kern = functools.partial(_moe_ffn_kernel, dff=dff, nchunk=2, ebase=ebase)
    ys_init = jnp.zeros((ncap, d), BF16)
    return pl.pallas_call(
        kern,
        out_shape=jax.ShapeDtypeStruct((ncap, d), BF16),
        grid_spec=pltpu.PrefetchScalarGridSpec(
            num_scalar_prefetch=3,
            grid=(ncap // MOE_RT,),
            in_specs=[pl.BlockSpec((MOE_RT, d), lambda k, blk, exp, nt: (blk[k], 0)),
                      pl.BlockSpec(memory_space=pl.ANY),
                      pl.BlockSpec(memory_space=pl.ANY),
                      pl.BlockSpec(memory_space=pl.ANY)],
            out_specs=pl.BlockSpec((MOE_RT, d), lambda k, blk, exp, nt: (blk[k], 0)),
            scratch_shapes=[pltpu.VMEM((d, 2 * dff), BF16),
                            pltpu.VMEM((dff, d), BF16),
                            pltpu.VMEM((MOE_WSTAGE, d // MOE_WCHUNKS, 2 * dff), F32),
                            pltpu.VMEM((MOE_WSTAGE, dff // MOE_WCHUNKS, d), F32),
                            pltpu.SemaphoreType.DMA((2, MOE_WSTAGE)),
                            pltpu.SMEM((1,), I32)],
        ),
        input_output_aliases={6: 0},
        compiler_params=_cparams("arbitrary"),
        name="moe_ffn",
    )(tile_blk, tile_exp, ntiles, xs, w_gu, w_down, ys_init)


def _moe_combine_kernel(start_ref, cnt_ref, x_ref, gate_ref, pos_ref, gts_ref, ys_ref, o_ref,
                        buf, sems, acc_scr):
    b = pl.program_id(0)
    nb = pl.num_programs(0)
    ne = buf.shape[1]
    sb = x_ref.shape[0]
    slot = b % 2

    def in_copy(sl, e, a):
        return pltpu.make_async_copy(ys_ref.at[pl.ds(a, MOE_WIN)], buf.at[sl, e], sems.at[sl, e])

    def start_first_windows(blk, sl):
        for e in range(ne):
            @pl.when(cnt_ref[blk, e] > 0)
            def _():
                _, _, a = _window(start_ref[blk, e], cnt_ref[blk, e], 0)
                in_copy(sl, e, a).start()

    @pl.when(b == 0)
    def _():
        start_first_windows(0, 0)

    @pl.when(b + 1 < nb)
    def _():
        start_first_windows(b + 1, 1 - slot)

    acc_scr[...] = jnp.zeros(acc_scr.shape, F32)
    liota = lax.broadcasted_iota(I32, (sb, MOE_WIN), 1)
    for e in range(ne):
        poscol = pos_ref[:, e:e + 1]
        gcol = gts_ref[:, e:e + 1]
        nwin = (cnt_ref[b, e] + MOE_CAP - 1) // MOE_CAP

        def wbody(w, carry_unused, e=e, poscol=poscol, gcol=gcol):
            s, n, a = _window(start_ref[b, e], cnt_ref[b, e], w)

            @pl.when(w > 0)
            def _():
                in_copy(slot, e, a).start()

            in_copy(slot, e, a).wait()
            rel = jnp.where(jnp.logical_and(poscol >= s, poscol < s + n), poscol - a, -1)
            onehot = jnp.where(rel == liota, 1.0, 0.0).astype(BF16)
            acc_scr[...] += gcol * jnp.dot(onehot, buf[slot, e], preferred_element_type=F32)
            return carry_unused

        lax.fori_loop(0, nwin, wbody, 0)

    o_ref[...] = x_ref[...] + gate_ref[0] * acc_scr[...]


def _moe_combine(start, cnt, xt, gate, pos_n, gates_n, ys, tpb):
    n, d = xt.shape
    nb, ne = cnt.shape
    sb = n // nb
    return pl.pallas_call(
        _moe_combine_kernel,
        out_shape=jax.ShapeDtypeStruct((n, d), F32),
        grid_spec=pltpu.PrefetchScalarGridSpec(
            num_scalar_prefetch=2,
            grid=(nb,),
            in_specs=[pl.BlockSpec((sb, d), lambda b, *_: (b, 0)),
                      pl.BlockSpec((1, 1, d), lambda b, *_: (b // tpb, 0, 0)),
                      pl.BlockSpec((sb, ne), lambda b, *_: (b, 0)),
                      pl.BlockSpec((sb, ne), lambda b, *_: (b, 0)),
                      pl.BlockSpec(memory_space=pl.ANY)],
            out_specs=pl.BlockSpec((sb, d), lambda b, *_: (b, 0)),
            scratch_shapes=[pltpu.VMEM((2, ne, MOE_WIN, d), BF16),
                            pltpu.SemaphoreType.DMA((2, ne)),
                            pltpu.VMEM((sb, d), F32)],
        ),
        compiler_params=_cparams("arbitrary"),
        name="moe_combine",
    )(start, cnt, xt, gate, pos_n, gates_n, ys)


def _moe_layer(x, g, sh, sc, gate, router_w, router_b, w_gu, w_down, ebase=0):
    bsz, seq, d = x.shape
    ne = router_w.shape[1]
    n = bsz * seq
    sb = min(MOE_SB, seq)
    tpb = seq // sb
    nb = n // sb
    rt = MOE_RT
    xt = x.reshape(n, d)
    rw_pad = jnp.pad(router_w.astype(F32), ((0, 0), (0, LANES - ne)))
    rb_pad = jnp.pad(router_b.astype(F32).reshape(1, ne), ((0, 0), (0, LANES - ne)), constant_values=NEG_BIG)
    h, meta = _moe_router(xt, g, sh, sc, rw_pad, rb_pad, tpb, sb)
    gates = meta[:, :ne]
    sel = (meta[:, ne:2 * ne] > 0.5).astype(I32)

    selb = sel.reshape(nb, sb, ne)
    cnt = jnp.sum(selb, axis=1)
    rank = meta[:, 2 * ne:3 * ne].astype(I32).reshape(nb, sb, ne)
    total = jnp.sum(cnt, axis=0)
    region = ((total + MOE_WIN + rt - 1) // rt) * rt
    off = jnp.cumsum(region) - region
    start = (off[None, :] + jnp.cumsum(cnt, axis=0) - cnt).astype(I32)
    pos = jnp.where(selb > 0, start[:, None, :] + rank, -1).astype(I32)
    ncap = 2 * n + ne * (MOE_WIN + rt)
    tiles_e = (total + rt - 1) // rt
    tcum = jnp.cumsum(tiles_e)
    ntiles = tcum[-1]
    kk = jnp.minimum(jnp.arange(ncap // rt), ntiles - 1)
    tile_exp = jnp.sum((kk[:, None] >= tcum[None, :]).astype(I32), axis=1)
    tile_blk = (off[tile_exp] // rt + kk - (tcum - tiles_e)[tile_exp]).astype(I32)

    xs = _moe_dispatch(start, cnt.astype(I32), h, jnp.swapaxes(pos, 1, 2), ncap)
    ys = _moe_ffn(tile_blk, tile_exp, ntiles.reshape(1).astype(I32), xs, w_gu, w_down, ebase)
    out = _moe_combine(start, cnt.astype(I32), xt, gate, pos.reshape(n, ne), gates, ys, tpb)
    return out.reshape(bsz, seq, d)


def _head_norm(q, hsum_ref, hexp_ref, gain, scale):
    ms = jnp.dot((q * q).astype(BF16), hsum_ref[...], preferred_element_type=F32)
    r = lax.rsqrt(ms + EPS)
    r_hi = r.astype(BF16)
    r_lo = (r - r_hi.astype(F32)).astype(BF16)
    rexp = (jnp.dot(r_hi, hexp_ref[...], preferred_element_type=F32)
            + jnp.dot(r_lo, hexp_ref[...], preferred_element_type=F32))
    return q * rexp * (gain * scale)


def _qkv_kernel(x_ref, g_ref, sh_ref, sc_ref, wm_ref, ws_ref, qg_ref, kg_ref,
                hsq_ref, heq_ref, hsk_ref, hek_ref,
                q_ref, k_ref, v_ref, qi_ref, kiwi_ref, *, dq, dk, dqi):
    x = x_ref[0]
    h = _norm_mod(x, g_ref[...], sh_ref[0], sc_ref[0]).astype(BF16)
    z = jnp.dot(h, wm_ref[...], preferred_element_type=F32)
    q = z[:, :dq]
    k = z[:, dq:dq + dk]
    v = z[:, dq + dk:dq + 2 * dk]
    qi = z[:, dq + 2 * dk:dq + 2 * dk + dqi]
    q_ref[0] = _head_norm(q, hsq_ref, heq_ref, qg_ref[...], HEAD_DIM ** -0.5 * LOG2E).astype(BF16)
    k_ref[0] = _head_norm(k, hsk_ref, hek_ref, kg_ref[...], 1.0).astype(BF16)
    v_ref[0] = v.astype(BF16)
    qi_ref[0] = qi.astype(BF16)
    kiwi_ref[0] = jnp.dot(h, ws_ref[...], preferred_element_type=F32)


def _head_indicators(nheads):
    hs = np.zeros((nheads * HEAD_DIM, LANES), np.float32)
    he = np.zeros((LANES, nheads * HEAD_DIM), np.float32)
    for hd in range(nheads):
        hs[hd * HEAD_DIM:(hd + 1) * HEAD_DIM, hd] = 1.0 / HEAD_DIM
        he[hd, hd * HEAD_DIM:(hd + 1) * HEAD_DIM] = 1.0
    return jnp.asarray(hs, BF16), jnp.asarray(he, BF16)


def _qkv_proj(x, g, sh, sc, w_main, w_small, q_gain, k_gain, tm=512):
    bsz, seq, d = x.shape
    tm = min(tm, seq)
    dq = N_HEADS * HEAD_DIM
    dk = N_KV_HEADS * HEAD_DIM
    dqi = IDX_HEADS * IDX_DIM
    hsq, heq = _head_indicators(N_HEADS)
    hsk, hek = _head_indicators(N_KV_HEADS)
    qg = jnp.tile(q_gain.reshape(1, HEAD_DIM), (1, N_HEADS)).astype(F32)
    kg = jnp.tile(k_gain.reshape(1, HEAD_DIM), (1, N_KV_HEADS)).astype(F32)
    kern = functools.partial(_qkv_kernel, dq=dq, dk=dk, dqi=dqi)
    mod_spec = pl.BlockSpec((1, 1, d), lambda b, i: (b, 0, 0))

    def full(a):
        return pl.BlockSpec(a.shape, lambda b, i: (0,) * a.ndim)

    def out(n):
        return pl.BlockSpec((1, tm, n), lambda b, i: (b, i, 0))

    return pl.pallas_call(
        kern,
        out_shape=[
            jax.ShapeDtypeStruct((bsz, seq, dq), BF16),
            jax.ShapeDtypeStruct((bsz, seq, dk), BF16),
            jax.ShapeDtypeStruct((bsz, seq, dk), BF16),
            jax.ShapeDtypeStruct((bsz, seq, dqi), BF16),
            jax.ShapeDtypeStruct((bsz, seq, LANES), F32),
        ],
        grid=(bsz, seq // tm),
        in_specs=[
            pl.BlockSpec((1, tm, d), lambda b, i: (b, i, 0)),
            pl.BlockSpec((1, d), lambda b, i: (0, 0)),
            mod_spec, mod_spec,
            full(w_main), full(w_small), full(qg), full(kg),
            full(hsq), full(heq), full(hsk), full(hek),
        ],
        out_specs=[out(dq), out(dk), out(dk), out(dqi), out(LANES)],
        compiler_params=_cparams("parallel", "parallel"),
        name="attn_qkv",
    )(x, g, sh, sc, w_main, w_small, qg, kg, hsq, heq, hsk, hek)


def _rel_bucket_np(dist):
    max_exact = REL_BUCKETS // 2
    d = np.maximum(dist, 1).astype(np.float64)
    large = max_exact + (np.log(d / max_exact) / math.log(REL_MAX_DIST / max_exact)
                         * (REL_BUCKETS - max_exact)).astype(np.int32)
    large = np.minimum(large, REL_BUCKETS - 1)
    return np.where(dist < max_exact, dist, large).astype(np.int32)


def _bias_table_kernel(bucket_ref, rb_ref, o_ref):
    hd = pl.program_id(0)
    bucket = bucket_ref[...]
    acc = jnp.zeros(bucket.shape, F32)
    for b in range(REL_BUCKETS):
        acc = jnp.where(bucket == b, rb_ref[b, hd] * LOG2E, acc)
    o_ref[0] = acc


def _bias_table(rel_bias):
    w = np.arange(BIAS_W)[:, None]
    i = np.arange(ATT_TQ)[None, :]
    bucket = jnp.asarray(_rel_bucket_np(np.maximum(i - w + BIAS_C, 0)))
    return pl.pallas_call(
        _bias_table_kernel,
        out_shape=jax.ShapeDtypeStruct((N_HEADS, BIAS_W, ATT_TQ), F32),
        grid=(N_HEADS,),
        in_specs=[
            pl.BlockSpec((BIAS_W, ATT_TQ), lambda hd: (0, 0)),
            pl.BlockSpec(memory_space=pltpu.SMEM),
        ],
        out_specs=pl.BlockSpec((1, BIAS_W, ATT_TQ), lambda hd: (hd, 0, 0)),
        compiler_params=_cparams("arbitrary"),
        name="attn_bias_table",
    )(bucket, rel_bias.astype(F32))


def _attn_kernel(qT_ref, qiT_ref, wiT_ref, k_ref, vT_ref, ki_ref, pt_ref, o_ref,
                 keys_scr, negm_scr, pidx_scr, oT_scr, acc_scr, qall_scr, sa_scr, sb_scr, *, top_k):
    tq, kc = ATT_TQ, ATT_KC
    qt = pl.program_id(1)
    q0 = qt * tq
    nch = (q0 + tq + kc - 1) // kc
    tpos = q0 + lax.broadcasted_iota(I32, (kc, tq), 1)
    srow = lax.broadcasted_iota(I32, (kc, tq), 0)

    qiT = qiT_ref[0]
    qi_all = jnp.concatenate([qiT[hd * IDX_DIM:(hd + 1) * IDX_DIM, :] for hd in range(IDX_HEADS)], axis=1)
    wiT = wiT_ref[0]

    def score_chunk(c):
        ks = pl.multiple_of(c * kc, kc)
        kic = ki_ref[0, pl.ds(ks, kc), :]
        dots = jnp.dot(kic, qi_all, preferred_element_type=F32)
        acc = jnp.zeros((kc, tq), F32)
        for hd in range(IDX_HEADS):
            acc = acc + jnp.maximum(dots[:, hd * tq:(hd + 1) * tq], 0.0) * wiT[hd:hd + 1, :]
        acc = jnp.where(acc == 0.0, 0.0, acc)
        bits = pltpu.bitcast(acc, I32)
        key = jnp.where(bits < 0, bits ^ INT_MAX, bits)
        key = jnp.where(ks + srow <= tpos, key, INT_MIN)
        keys_scr[pl.ds(ks, kc), :] = key

    npair = (nch + 1) // 2

    def score_pair(i, carry):
        score_chunk(2 * i)
        score_chunk(2 * i + 1)
        return carry

    lax.fori_loop(0, npair, score_pair, 0)

    @pl.when(nch % 2 == 1)
    def _():
        negm_scr[pl.ds(pl.multiple_of(nch * kc, kc), kc), :] = jnp.full((kc, tq), NEG_BIG, F32)

    srow2 = lax.broadcasted_iota(I32, (2 * kc, tq), 0)

    def count(pred):
        def body(c, acc):
            ks = pl.multiple_of(c * 2 * kc, 2 * kc)
            m = pred(keys_scr[pl.ds(ks, 2 * kc), :], ks + srow2).astype(I32)
            return acc + jnp.sum(m.reshape(2 * kc // SUBLANES, SUBLANES, tq), axis=0)
        acc = lax.fori_loop(0, npair, body, jnp.zeros((SUBLANES, tq), I32))
        return jnp.sum(acc, axis=0, keepdims=True)

    def bit_body(it, p):
        cand_p = p | lax.shift_left(jnp.int32(1), 31 - it)
        cand = cand_p ^ INT_MIN
        cnt = count(lambda k, s: k >= cand)
        return jnp.where(cnt >= top_k, cand_p, p)

    p_fin = lax.fori_loop(0, 32, bit_body, jnp.zeros((1, tq), I32))
    v = p_fin ^ INT_MIN

    cnt_gt = count(lambda k, s: k > v)
    cnt_eq = count(lambda k, s: k == v)
    need = top_k - cnt_gt
    pidx_scr[...] = jnp.full((1, tq), INT_MAX, I32)
    pos_bits = (keys_scr.shape[0] - 1).bit_length()

    @pl.when(jnp.max(cnt_eq - need) > 0)
    def _():
        def ibit(it, p):
            cand = p | lax.shift_left(jnp.int32(1), pos_bits - 1 - it)
            cnt = count(lambda k, s: jnp.logical_and(k == v, s < cand))
            return jnp.where(cnt < need, cand, p)
        pidx_scr[...] = lax.fori_loop(0, pos_bits, ibit, jnp.zeros((1, tq), I32))

    pidx = pidx_scr[...]

    def mask_chunk(c, carry):
        ks = pl.multiple_of(c * kc, kc)
        k = keys_scr[pl.ds(ks, kc), :]
        spos = ks + srow
        sel = jnp.logical_or(k > v, jnp.logical_and(k == v, spos <= pidx))
        sel = jnp.logical_and(sel, spos <= tpos)
        negm_scr[pl.ds(ks, kc), :] = jnp.where(sel, 0.0, NEG_BIG)
        return carry

    lax.fori_loop(0, nch, mask_chunk, 0)

    acc_scr[...] = jnp.zeros(acc_scr.shape, F32)
    for n in range(N_KV_HEADS):
        r0 = n * N_REP * HEAD_DIM
        qall_scr[n] = jnp.concatenate(
            [qT_ref[0, r0 + g * HEAD_DIM:r0 + (g + 1) * HEAD_DIM, :] for g in range(N_REP)], axis=1)

    def qk_chunk(c, s_ref):
        ks = pl.multiple_of(c * kc, kc)
        for n in range(N_KV_HEADS):
            s_ref[n] = jnp.dot(k_ref[0, n, pl.ds(ks, kc), :], qall_scr[n], preferred_element_type=F32)

    def softmax_pv(c, s_ref, ms, far):
        ks = pl.multiple_of(c * kc, kc)
        negm = negm_scr[pl.ds(ks, kc), :]
        w0 = pl.multiple_of(jnp.clip(BIAS_C - (q0 - ks), 0, BIAS_C), LANES)
        new_ms = []
        for n in range(N_KV_HEADS):
            s = s_ref[n]
            if far:
                cvec = jnp.concatenate([pt_ref[n * N_REP + g, 0:1, :] for g in range(N_REP)], axis=1)
                lg = jnp.concatenate([s[:, g * tq:(g + 1) * tq] + negm for g in range(N_REP)], axis=1)
                m_new = jnp.maximum(ms[n], jnp.max(lg, axis=0, keepdims=True) + cvec)
                p = jnp.exp2(lg - (m_new - cvec))
            else:
                lg = jnp.concatenate(
                    [s[:, g * tq:(g + 1) * tq] + pt_ref[n * N_REP + g, pl.ds(w0, kc), :] + negm
                     for g in range(N_REP)], axis=1)
                m_new = jnp.maximum(ms[n], jnp.max(lg, axis=0, keepdims=True))
                p = jnp.exp2(lg - m_new)
            alpha = jnp.exp2(ms[n] - m_new)
            acc_scr[n] = alpha * acc_scr[n] + jnp.dot(vT_ref[0, c, n], p.astype(BF16),
                                                      preferred_element_type=F32)
            new_ms.append(m_new)
        return tuple(new_ms)

    last_chunk = k_ref.shape[2] // kc - 1

    def pair_step(i, ms, far):
        c0 = 2 * i
        qk_chunk(c0 + 1, sb_scr)
        ms = softmax_pv(c0, sa_scr, ms, far)
        qk_chunk(jnp.minimum(c0 + 2, last_chunk), sa_scr)
        return softmax_pv(c0 + 1, sb_scr, ms, far)

    n_far = jnp.clip((q0 - BIAS_C + kc) // kc, 0, nch)
    ms = tuple(jnp.full((1, N_REP * tq), NEG_BIG, F32) for _ in range(N_KV_HEADS))
    qk_chunk(0, sa_scr)
    ms = lax.fori_loop(0, n_far // 2, functools.partial(pair_step, far=True), ms)
    lax.fori_loop(n_far // 2, npair, functools.partial(pair_step, far=False), ms)
    for n in range(N_KV_HEADS):
        o_t = acc_scr[n, 0:HEAD_DIM, :] / acc_scr[n, HEAD_DIM:HEAD_DIM + 1, :]
        for g in range(N_REP):
            r0 = (n * N_REP + g) * HEAD_DIM
            oT_scr[r0:r0 + HEAD_DIM, :] = o_t[:, g * tq:(g + 1) * tq]

    o_ref[0] = oT_scr[...].T.astype(BF16)


def _attention(qT, qiT, wiT, k4, vT, ki, ptab, top_k):
    bsz, dq, seq = qT.shape
    tq = ATT_TQ
    assert seq % (2 * ATT_KC) == 0
    kern = functools.partial(_attn_kernel, top_k=top_k)
    return pl.pallas_call(
        kern,
        out_shape=jax.ShapeDtypeStruct((bsz, seq, dq), BF16),
        grid=(bsz, seq // tq),
        in_specs=[
            pl.BlockSpec((1, dq, tq), lambda b, i: (b, 0, i)),
            pl.BlockSpec((1, qiT.shape[1], tq), lambda b, i: (b, 0, i)),
            pl.BlockSpec((1, wiT.shape[1], tq), lambda b, i: (b, 0, i)),
            pl.BlockSpec((1,) + k4.shape[1:], lambda b, i: (b, 0, 0, 0)),
            pl.BlockSpec((1,) + vT.shape[1:], lambda b, i: (b, 0, 0, 0, 0)),
            pl.BlockSpec((1,) + ki.shape[1:], lambda b, i: (b, 0, 0)),
            pl.BlockSpec(ptab.shape, lambda b, i: (0, 0, 0)),
        ],
        out_specs=pl.BlockSpec((1, tq, dq), lambda b, i: (b, i, 0)),
        scratch_shapes=[
            pltpu.VMEM((seq, tq), I32),
            pltpu.VMEM((seq, tq), F32),
            pltpu.VMEM((1, tq), I32),
            pltpu.VMEM((dq, tq), F32),
            pltpu.VMEM((N_KV_HEADS, ATT_VROWS, N_REP * tq), F32),
            pltpu.VMEM((N_KV_HEADS, HEAD_DIM, N_REP * tq), BF16),
            pltpu.VMEM((N_KV_HEADS, ATT_KC, N_REP * tq), F32),
            pltpu.VMEM((N_KV_HEADS, ATT_KC, N_REP * tq), F32),
        ],
        compiler_params=_cparams("parallel", "arbitrary"),
        name="attn_core",
    )(qT, qiT, wiT, k4, vT, ki, ptab)


def _proj_res_kernel(a_ref, x_ref, gate_ref, w_ref, o_ref):
    y = jnp.dot(a_ref[0], w_ref[...], preferred_element_type=F32)
    o_ref[0] = x_ref[0] + gate_ref[0] * y


def _proj_residual(a, x, gate, w, tm=512):
    bsz, seq, d = x.shape
    tm = min(tm, seq)
    return pl.pallas_call(
        _proj_res_kernel,
        out_shape=jax.ShapeDtypeStruct((bsz, seq, d), F32),
        grid=(bsz, seq // tm),
        in_specs=[
            pl.BlockSpec((1, tm, a.shape[2]), lambda b, i: (b, i, 0)),
            pl.BlockSpec((1, tm, d), lambda b, i: (b, i, 0)),
            pl.BlockSpec((1, 1, d), lambda b, i: (b, 0, 0)),
            pl.BlockSpec(w.shape, lambda b, i: (0, 0)),
        ],
        out_specs=pl.BlockSpec((1, tm, d), lambda b, i: (b, i, 0)),
        compiler_params=_cparams("parallel", "parallel"),
        name="proj_residual",
    )(a, x, gate, w)


def _attn_layer(x, g, sh, sc, gate, w_in, q_gain, k_gain, w_out, rel_bias):
    bsz, seq, d = x.shape
    top_k = min(TOPK_MAX, seq // 4)
    dq = N_HEADS * HEAD_DIM
    dk = N_KV_HEADS * HEAD_DIM
    dqi = IDX_HEADS * IDX_DIM
    nmain = dq + 2 * dk + dqi
    w_main = w_in[:, :nmain].astype(BF16)
    w_small = jnp.pad(w_in[:, nmain:], ((0, 0), (0, LANES - (IDX_DIM + IDX_HEADS)))).astype(BF16)
    q, k, v, qi, kiwi = _qkv_proj(x, g, sh, sc, w_main, w_small, q_gain, k_gain)
    ki = kiwi[:, :, :IDX_DIM].astype(BF16)
    wi = kiwi[:, :, IDX_DIM:IDX_DIM + IDX_HEADS] * (IDX_HEADS ** -0.5 * IDX_DIM ** -0.5)
    qT = jnp.swapaxes(q, 1, 2)
    qiT = jnp.swapaxes(qi, 1, 2)
    wiT = jnp.swapaxes(wi, 1, 2)
    nck = seq // ATT_KC
    vT = jnp.swapaxes(v.reshape(bsz, nck, ATT_KC, dk), 2, 3).reshape(bsz, nck, N_KV_HEADS, HEAD_DIM, ATT_KC)
    vT = jnp.concatenate([
        vT, jnp.ones((bsz, nck, N_KV_HEADS, 1, ATT_KC), BF16),
        jnp.zeros((bsz, nck, N_KV_HEADS, ATT_VROWS - HEAD_DIM - 1, ATT_KC), BF16)], axis=3)
    k4 = jnp.swapaxes(k.reshape(bsz, seq, N_KV_HEADS, HEAD_DIM), 1, 2)
    ptab = _bias_table(rel_bias)
    attn = _attention(qT, qiT, wiT, k4, vT, ki, ptab, top_k)
    return _proj_residual(attn, x, gate, w_out.astype(BF16))


def _s5_prep_kernel(lre_ref, lim_ref, ls_ref, bre_ref, bim_ref, cre_ref, cim_ref,
                    bcw_ref, mtw_ref, ccw_ref, are_ref, aim_ref):
    t_len = SSM_CHUNK
    lre = jnp.minimum(lre_ref[0], -1e-4)
    lim = lim_ref[0]
    step = jnp.exp(ls_ref[0])
    ar = lre * step
    ai = lim * step

    def powers(jv):
        mag = jnp.exp(jv * ar)
        return mag * jnp.cos(jv * ai), mag * jnp.sin(jv * ai)

    lb_re, lb_im = powers(1.0)
    nr = lb_re - 1.0
    ni = lb_im
    den = lre * lre + lim * lim
    cf_re = (nr * lre + ni * lim) / den
    cf_im = (ni * lre - nr * lim) / den
    bre = bre_ref[0]
    bim = bim_ref[0]
    bb_re = cf_re * bre - cf_im * bim
    bb_im = cf_re * bim + cf_im * bre
    cre = cre_ref[0]
    cim = cim_ref[0]
    nst = lre.shape[-1]
    jv = lax.broadcasted_iota(I32, (t_len, 1, nst), 0).astype(F32)
    pj_re, pj_im = powers(jv)
    a_re = (cre[None] * pj_re - cim[None] * pj_im).reshape(t_len * SSM_GROUP, nst)
    a_im = (cre[None] * pj_im + cim[None] * pj_re).reshape(t_len * SSM_GROUP, nst)
    dn = (((1,), (1,)), ((), ()))
    cg = SSM_GROUP
    tc = t_len * cg
    width = t_len * LANES
    gq = pl.program_id(0) % (LANES // cg)
    nh = (LANES // cg) * nst
    cg_shift = cg.bit_length() - 1

    def place(nrows, target):
        r = lax.broadcasted_iota(I32, (nrows, width), 0)
        col = lax.broadcasted_iota(I32, (nrows, width), 1)
        return jnp.where(col == target(r), 1.0, 0.0).astype(BF16)

    pm = place(tc, lambda r: lax.shift_right_logical(r, cg_shift) * LANES + gq * cg + (r & (cg - 1)))
    k_t = (lax.dot_general(bb_re, a_re, dn, preferred_element_type=F32, precision=HIGHEST)
           - lax.dot_general(bb_im, a_im, dn, preferred_element_type=F32, precision=HIGHEST))
    lane = lax.broadcasted_iota(I32, (cg, tc), 1)
    mt_rows = [k_t] + [jnp.where(lane >= s * cg, pltpu.roll(k_t, s * cg, 1), 0.0) for s in range(1, t_len)]
    mt_t = jnp.concatenate(mt_rows, axis=0)
    mtw_ref[0] = jnp.dot(mt_t.astype(BF16), pm, preferred_element_type=F32).astype(BF16)
    pr_re, pr_im = powers((t_len - 1.0) - jv)
    bc_re = (bb_re[None] * pr_re - bb_im[None] * pr_im).reshape(tc, nst)
    bc_im = (bb_re[None] * pr_im + bb_im[None] * pr_re).reshape(tc, nst)
    pb_re = place(nst, lambda r: gq * nst + r)
    pb_im = place(nst, lambda r: nh + gq * nst + r)
    bcw_ref[0] = (jnp.dot(bc_re.astype(BF16), pb_re, preferred_element_type=F32)
                  + jnp.dot(bc_im.astype(BF16), pb_im, preferred_element_type=F32)).astype(BF16)
    pn_re, pn_im = powers(jv + 1.0)
    cc_re = (cre[None] * pn_re - cim[None] * pn_im).reshape(tc, nst)
    cc_im = (cre[None] * pn_im + cim[None] * pn_re).reshape(tc, nst)
    eye = jnp.where(lax.broadcasted_iota(I32, (nst, nst), 0) == lax.broadcasted_iota(I32, (nst, nst), 1),
                    1.0, 0.0)
    cct_re = lax.dot_general(eye, cc_re, dn, preferred_element_type=F32, precision=HIGHEST)
    cct_im = lax.dot_general(eye, cc_im, dn, preferred_element_type=F32, precision=HIGHEST)
    ccw_ref[0, 0:nst, :] = jnp.dot(cct_re.astype(BF16), pm, preferred_element_type=F32).astype(BF16)
    ccw_ref[0, nst:2 * nst, :] = (-jnp.dot(cct_im.astype(BF16), pm, preferred_element_type=F32)).astype(BF16)
    at_re, at_im = powers(float(t_len))
    rs = lax.broadcasted_iota(I32, (nst, nh), 0)
    cs = lax.broadcasted_iota(I32, (nst, nh), 1)
    pa = jnp.where(cs == gq * nst + rs, 1.0, 0.0)
    are_ref[0] = jnp.dot(at_re, pa, preferred_element_type=F32, precision=HIGHEST)
    aim_ref[0] = jnp.dot(at_im, pa, preferred_element_type=F32, precision=HIGHEST)


def _s5_prep(lam_re, lam_im, log_step, b_re, b_im, c_re, c_im):
    ng, nst = lam_re.shape
    tc = SSM_CHUNK * SSM_GROUP
    vec = pl.BlockSpec((1, 1, nst), lambda gi: (gi, 0, 0))
    mat = pl.BlockSpec((1, SSM_GROUP, nst), lambda gi: (gi, 0, 0))
    width = SSM_CHUNK * LANES
    assert tc == 2 * nst and 2 * (LANES // SSM_GROUP) * nst == width
    big = pl.BlockSpec((1, tc, width), lambda gi: (gi, 0, 0))
    wide = jax.ShapeDtypeStruct((ng, tc, width), BF16)
    return pl.pallas_call(
        _s5_prep_kernel,
        out_shape=[wide, wide, wide,
                   jax.ShapeDtypeStruct((ng, 1, width // 2), F32),
                   jax.ShapeDtypeStruct((ng, 1, width // 2), F32)],
        grid=(ng,),
        in_specs=[vec, vec, pl.BlockSpec((1, 1, 1), lambda gi: (gi, 0, 0)), mat, mat, mat, mat],
        out_specs=[big, big, big, pl.BlockSpec((1, 1, width // 2), lambda gi: (gi, 0, 0)),
                   pl.BlockSpec((1, 1, width // 2), lambda gi: (gi, 0, 0))],
        compiler_params=_cparams("parallel"),
        name="s5_prep",
    )(lam_re.reshape(ng, 1, nst), lam_im.reshape(ng, 1, nst), log_step.reshape(ng, 1, 1),
      jnp.swapaxes(b_re, 1, 2), jnp.swapaxes(b_im, 1, 2), c_re, c_im)


def _s5_pre_kernel(x_ref, g_ref, sh_ref, sc_ref, o_ref, h_scr):
    t_len = SSM_CHUNK
    h = _norm_mod(x_ref[0], g_ref[...], sh_ref[0], sc_ref[0])
    nj = o_ref.shape[1]
    for q in range(o_ref.shape[0]):
        h_scr[q] = h[:, q * LANES:(q + 1) * LANES]
        for t in range(t_len):
            o_ref[q, :, t * LANES:(t + 1) * LANES] = h_scr[q, pl.ds(t, nj, stride=t_len), :].astype(BF16)


def _s5_pre(x, g, sh, sc, tm=512):
    bsz, seq, d = x.shape
    tm = min(tm, seq)
    nt = seq // tm
    nq = d // LANES
    t_len = SSM_CHUNK
    mod_spec = pl.BlockSpec((1, 1, d), lambda b, i: (b, 0, 0))
    return pl.pallas_call(
        _s5_pre_kernel,
        out_shape=jax.ShapeDtypeStruct((nq, bsz * seq // t_len, t_len * LANES), BF16),
        grid=(bsz, nt),
        in_specs=[pl.BlockSpec((1, tm, d), lambda b, i: (b, i, 0)),
                  pl.BlockSpec((1, d), lambda b, i: (0, 0)), mod_spec, mod_spec],
        out_specs=pl.BlockSpec((nq, tm // t_len, t_len * LANES), lambda b, i: (0, b * nt + i, 0)),
        scratch_shapes=[pltpu.VMEM((nq, tm, LANES), F32)],
        compiler_params=_cparams("parallel", "parallel"),
        name="s5_pre",
    )(x, g, sh, sc)


def _s5_scan_kernel(x_ref, bc_ref, mt_ref, cc_ref, are_ref, aim_ref, y_ref, re_scr, im_scr):
    x = x_ref[0]
    nrow = x.shape[0]
    pad = re_scr.shape[0] - nrow
    v = jnp.dot(x, bc_ref[0], preferred_element_type=F32)
    nh = v.shape[1] // 2
    s_re = v[:, :nh]
    s_im = v[:, nh:]
    a_re = are_ref[0]
    a_im = aim_ref[0]
    re_scr[0:pad, :] = jnp.zeros((pad, nh), F32)
    im_scr[0:pad, :] = jnp.zeros((pad, nh), F32)

    def shifted(scr, val, dist):
        scr[pad:pad + nrow, :] = val
        return scr[pad - dist:pad - dist + nrow, :]

    dist = 1
    while dist < nrow:
        sh_re = shifted(re_scr, s_re, dist)
        sh_im = shifted(im_scr, s_im, dist)
        s_re, s_im = (s_re + a_re * sh_re - a_im * sh_im, s_im + a_re * sh_im + a_im * sh_re)
        a_re, a_im = (a_re * a_re - a_im * a_im, 2.0 * a_re * a_im)
        dist *= 2
    sp = jnp.concatenate([shifted(re_scr, s_re, 1), shifted(im_scr, s_im, 1)], axis=1).astype(BF16)
    y_ref[0] = (jnp.dot(x, mt_ref[0], preferred_element_type=F32)
                + jnp.dot(sp, cc_ref[0], preferred_element_type=F32))


def _s5_scan(hq, bcq, mtq, ccq, a_re, a_im, bsz):
    nq, nrows, width = hq.shape
    nj = nrows // bsz
    nh = a_re.shape[-1]
    pad = max(nj // 2, SUBLANES)
    tile = pl.BlockSpec((1, nj, width), lambda q, b: (q, b, 0))
    wspec = lambda a: pl.BlockSpec((1,) + a.shape[1:], lambda q, b: (q, 0, 0))
    return pl.pallas_call(
        _s5_scan_kernel,
        out_shape=jax.ShapeDtypeStruct((nq, nrows, width), F32),
        grid=(nq, bsz),
        in_specs=[tile, wspec(bcq), wspec(mtq), wspec(ccq), wspec(a_re), wspec(a_im)],
        out_specs=tile,
        scratch_shapes=[pltpu.VMEM((pad + nj, nh), F32), pltpu.VMEM((pad + nj, nh), F32)],
        compiler_params=_cparams("parallel", "parallel"),
        name="s5_scan",
    )(hq, bcq, mtq, ccq, a_re, a_im)


def _s5_post_kernel(x_ref, y_ref, g_ref, sh_ref, sc_ref, gate_ref, dsk_ref, w_ref, o_ref, y_scr, *, d):
    t_len = SSM_CHUNK
    x = x_ref[0]
    h = _norm_mod(x, g_ref[...], sh_ref[0], sc_ref[0])
    nj = y_ref.shape[1]
    for q in range(y_ref.shape[0]):
        for t in range(t_len):
            y_scr[q, pl.ds(t, nj, stride=t_len), :] = y_ref[q, :, t * LANES:(t + 1) * LANES]
    y = jnp.concatenate([y_scr[q] for q in range(y_ref.shape[0])], axis=1)
    yy = y + dsk_ref[...] * h
    gl = jax.nn.gelu(yy).astype(BF16)
    z = jnp.dot(gl, w_ref[...], preferred_element_type=F32)
    o_ref[0] = x + gate_ref[0] * (z[:, :d] * jax.nn.sigmoid(z[:, d:]))


def _s5_post(x, yq, g, sh, sc, gate, d_skip, w_glu, tm=512):
    bsz, seq, d = x.shape
    tm = min(tm, seq)
    nt = seq // tm
    nq = d // LANES
    kern = functools.partial(_s5_post_kernel, d=d)
    mod_spec = pl.BlockSpec((1, 1, d), lambda b, i: (b, 0, 0))
    tile = pl.BlockSpec((1, tm, d), lambda b, i: (b, i, 0))
    return pl.pallas_call(
        kern,
        out_shape=jax.ShapeDtypeStruct((bsz, seq, d), F32),
        grid=(bsz, nt),
        in_specs=[tile,
                  pl.BlockSpec((nq, tm // SSM_CHUNK, SSM_CHUNK * LANES), lambda b, i: (0, b * nt + i, 0)),
                  pl.BlockSpec((1, d), lambda b, i: (0, 0)), mod_spec, mod_spec, mod_spec,
                  pl.BlockSpec((1, d), lambda b, i: (0, 0)),
                  pl.BlockSpec(w_glu.shape, lambda b, i: (0, 0))],
        out_specs=tile,
        scratch_shapes=[pltpu.VMEM((nq, tm, LANES), F32)],
        compiler_params=_cparams("parallel", "parallel"),
        name="s5_post",
    )(x, yq, g, sh, sc, gate, d_skip, w_glu)


def _s5_layer(x, g, sh, sc, gate, lam_re, lam_im, log_step, b_re, b_im, c_re, c_im, d_skip, w_glu):
    bsz, seq, d = x.shape
    ng, nst = lam_re.shape
    t_len, cg = SSM_CHUNK, SSM_GROUP
    nq = d // LANES
    gq = LANES // cg
    nj = seq // t_len
    bcw, mtw, ccw, a_re, a_im = _s5_prep(lam_re, lam_im, log_step, b_re, b_im, c_re, c_im)
    width = t_len * LANES
    bcq = jnp.swapaxes(bcw.reshape(nq, gq, t_len, cg, width), 1, 2).reshape(nq, width, width)
    mtq = jnp.swapaxes(mtw.reshape(nq, gq, t_len, cg, width), 1, 2).reshape(nq, width, width)
    ccq = jnp.swapaxes(ccw.reshape(nq, gq, 2, nst, width), 1, 2).reshape(nq, 2 * gq * nst, width)
    aq_re = jnp.sum(a_re.reshape(nq, gq, 1, gq * nst), axis=1)
    aq_im = jnp.sum(a_im.reshape(nq, gq, 1, gq * nst), axis=1)

    hq = _s5_pre(x, g, sh, sc)
    yq = _s5_scan(hq, bcq, mtq, ccq, aq_re, aq_im, bsz)
    return _s5_post(x, yq, g, sh, sc, gate, d_skip.reshape(1, d).astype(F32), w_glu.astype(BF16))


def kernel(x, c, ada_w, ada_b, norm_g, conv_w_in, conv_w, conv_w_out, attn_w_in, attn_q_gain, attn_k_gain, attn_w_out, rel_bias, ssm_lambda_re, ssm_lambda_im, ssm_log_step, ssm_b_re, ssm_b_im, ssm_c_re, ssm_c_im, ssm_d, ssm_w_glu, ffn_w_gu, ffn_w_down, moe_router_w, moe_router_b, moe_w_gu, moe_w_down):
    bsz, seq, d = x.shape
    depth = ada_w.shape[0]
    mod = _ada_mod(c, ada_w, ada_b).reshape(depth, bsz, 6, 1, d)
    moe_gu = moe_w_gu.reshape((-1,) + moe_w_gu.shape[2:])
    moe_down = moe_w_down.reshape((-1,) + moe_w_down.shape[2:])
    for i in range(depth):
        sh1, sc1, g1, sh2, sc2, g2 = (mod[i, :, r] for r in range(6))
        gn1 = norm_g[i, 0].reshape(1, d)
        gn2 = norm_g[i, 1].reshape(1, d)
        j = i // N_MIXERS
        if i % N_MIXERS == 0:
            x = _conv_layer(x, gn1, sh1, sc1, g1, conv_w_in[j].astype(BF16), conv_w[j],
                            conv_w_out[j].astype(BF16))
        elif i % N_MIXERS == 1:
            x = _attn_layer(x, gn1, sh1, sc1, g1, attn_w_in[j], attn_q_gain[j], attn_k_gain[j],
                            attn_w_out[j], rel_bias)
        else:
            x = _s5_layer(x, gn1, sh1, sc1, g1, ssm_lambda_re[j], ssm_lambda_im[j], ssm_log_step[j],
                          ssm_b_re[j], ssm_b_im[j], ssm_c_re[j], ssm_c_im[j], ssm_d[j], ssm_w_glu[j])
        if i % 2 == 0:
            x = _ffn_layer(x, gn2, sh2, sc2, g2, ffn_w_gu, ffn_w_down, i // 2)
        else:
            x = _moe_layer(x, gn2, sh2, sc2, g2, moe_router_w[i // 2], moe_router_b[i // 2],
                           moe_gu, moe_down, ebase=(i // 2) * moe_w_gu.shape[1])
    return x
```
